```python
import math
import jax, jax.numpy as jnp
from jax import lax
import numpy as np

D_MODEL = 1024
BATCH = 16
SEQ = 2048
DEPTH = 1

D_RNN = 1024
LRU_BLOCKS = 8
LRU_BW = D_RNN // LRU_BLOCKS
CONV_W = 4
LRU_C = 8.0
N_HEADS = 8
HEAD_DIM = 64
ATT_QK = N_HEADS * 2 * HEAD_DIM
ATT_V = N_HEADS * 2 * HEAD_DIM
Q_BLOCK = 128
N_EXPERTS = 64
TOP_K = 8
N_GROUPS = 8
TOPK_GROUPS = 4
D_EXPERT = 256
D_SHARED = 256
ROUTED_SCALE = 2.5
MOE_BLOCK = 128
EPS = 1e-6
W_IN = 2 * D_RNN + 2 * ATT_QK + ATT_V + 2 * D_MODEL

kernel_name = "hybrid_rglru_diffattn_moe_block"


def rms_norm(x, g):
    xf = x.astype(jnp.float32)
    y = xf * lax.rsqrt(jnp.mean(xf * xf, axis=-1, keepdims=True) + EPS)
    return (y * g).astype(x.dtype)


def modulate(h, shift, scale):
    return h * (1.0 + scale) + shift


def lambda_init(layer):
    return 0.8 - 0.6 * math.exp(-0.3 * layer)


def causal_dwconv(x, w, b):
    y = lax.conv_general_dilated(
        x, w[:, None, :].astype(x.dtype), window_strides=(1,),
        padding=[(CONV_W - 1, 0)], dimension_numbers=("NWC", "WIO", "NWC"),
        feature_group_count=x.shape[-1])
    return y + b


def rg_lru(x, wa, ba, wx, bx, lam):
    B, S, C = x.shape
    xb = x.reshape(B, S, LRU_BLOCKS, LRU_BW)
    r = jax.nn.sigmoid(jnp.einsum("bsnc,ncd->bsnd", xb, wa) + ba).reshape(B, S, C)
    i = jax.nn.sigmoid(jnp.einsum("bsnc,ncd->bsnd", xb, wx) + bx).reshape(B, S, C)
    log_a = (-LRU_C * jax.nn.softplus(-lam) * r).astype(jnp.float32)
    a = jnp.exp(log_a)
    mult = jnp.sqrt(-jnp.expm1(2.0 * log_a))
    first = (jnp.arange(S) == 0)[None, :, None]
    mult = jnp.where(first, 1.0, mult)
    u = mult * (i * x).astype(jnp.float32)

    def combine(lhs, rhs):
        a1, b1 = lhs
        a2, b2 = rhs
        return a1 * a2, a2 * b1 + b2

    _, h = lax.associative_scan(combine, (a, u), axis=1)
    return h.astype(x.dtype)


def diff_attention(q, k, v, lam):
    B, S = q.shape[0], q.shape[1]
    n_blk = S // Q_BLOCK
    scale = HEAD_DIM ** -0.5
    slopes = jnp.exp2(-8.0 * jnp.arange(1, N_HEADS + 1, dtype=jnp.float32) / N_HEADS)
    qb = q.reshape(B, n_blk, Q_BLOCK, N_HEADS, 2, HEAD_DIM).transpose(1, 0, 2, 3, 4, 5)
    pos_k = jnp.arange(S)

    def one_block(args):
        qi, bi = args
        pos_q = bi * Q_BLOCK + jnp.arange(Q_BLOCK)
        s = jnp.einsum("bqhmd,bkhmd->bhmqk", qi, k).astype(jnp.float32) * scale
        dist = (pos_q[:, None] - pos_k[None, :]).astype(jnp.float32)
        bias = -slopes[:, None, None, None] * dist
        s = jnp.where(dist >= 0, s + bias, -jnp.inf)
        p = jax.nn.softmax(s, axis=-1)
        attn = p[:, :, 0] - lam * p[:, :, 1]
        return jnp.einsum("bhqk,bkhe->bqhe", attn.astype(v.dtype), v)

    o = lax.map(one_block, (qb, jnp.arange(n_blk)))
    return o.transpose(1, 0, 2, 3, 4).reshape(B, S, N_HEADS, 2 * HEAD_DIM)


def hybrid_mixer(h, w_in, conv_w, conv_b, lru_wa, lru_ba, lru_wx, lru_bx, lru_lambda,
                 lam_q1, lam_k1, lam_q2, lam_k2, g_subln, w_proj_rnn, w_proj_att, w_out,
                 lam_init):
    B, S, _ = h.shape
    proj = h @ w_in
    cuts = [int(s) for s in np.cumsum([D_RNN, D_RNN, ATT_QK, ATT_QK, ATT_V, D_MODEL])]
    xr, gr, q, k, v, ga, gb = jnp.split(proj, cuts, axis=-1)
    xr = causal_dwconv(xr, conv_w, conv_b)
    yr = rg_lru(xr, lru_wa, lru_ba, lru_wx, lru_bx, lru_lambda) * jax.nn.gelu(gr)
    ya = yr @ w_proj_rnn
    q = q.reshape(B, S, N_HEADS, 2, HEAD_DIM)
    k = k.reshape(B, S, N_HEADS, 2, HEAD_DIM)
    v = v.reshape(B, S, N_HEADS, 2 * HEAD_DIM)
    lam = (jnp.exp(jnp.sum(lam_q1 * lam_k1)) - jnp.exp(jnp.sum(lam_q2 * lam_k2)) + lam_init)
    o = diff_attention(q, k, v, lam.astype(jnp.float32))
    o = rms_norm(o, g_subln) * (1.0 - lam_init)
    yb = o.reshape(B, S, ATT_V) @ w_proj_att
    merged = jax.nn.sigmoid(ga) * ya + jax.nn.sigmoid(gb) * yb
    return merged @ w_out


def moe_ffn(xf, w_router, e_bias, w1_e, w3_e, w2_e, w1_s, w3_s, w2_s):
    T, D = xf.shape
    scores = jax.nn.sigmoid((xf @ w_router).astype(jnp.float32))
    choice = scores + e_bias.astype(jnp.float32)
    per_g = N_EXPERTS // N_GROUPS
    grp_score = lax.top_k(choice.reshape(T, N_GROUPS, per_g), 2)[0].sum(-1)
    _, top_g = lax.top_k(grp_score, TOPK_GROUPS)
    gmask = jax.nn.one_hot(top_g, N_GROUPS, dtype=jnp.float32).sum(1) > 0
    choice = jnp.where(jnp.repeat(gmask, per_g, axis=1), choice, -jnp.inf)
    _, top_e = lax.top_k(choice, TOP_K)
    wts = jnp.take_along_axis(scores, top_e, axis=1)
    wts = wts / (wts.sum(-1, keepdims=True) + 1e-20) * ROUTED_SCALE

    A = T * TOP_K
    flat_e = top_e.reshape(A)
    order = jnp.argsort(flat_e)
    sorted_e = flat_e[order]
    counts = jnp.bincount(flat_e, length=N_EXPERTS)
    padded = (counts + MOE_BLOCK - 1) // MOE_BLOCK * MOE_BLOCK
    p_end = jnp.cumsum(padded)
    p_start = p_end - padded
    s_start = jnp.cumsum(counts) - counts
    dest = p_start[sorted_e] + (jnp.arange(A) - s_start[sorted_e])
    P = A + N_EXPERTS * MOE_BLOCK
    n_blocks = P // MOE_BLOCK
    slot_tok = jnp.full((P,), T, dtype=jnp.int32).at[dest].set((order // TOP_K).astype(jnp.int32))
    slot_w = jnp.zeros((P,), jnp.float32).at[dest].set(wts.reshape(A)[order])
    block_e = jnp.minimum(jnp.searchsorted(p_end, jnp.arange(n_blocks) * MOE_BLOCK, side="right"),
                          N_EXPERTS - 1)
    x_pad = jnp.concatenate([xf, jnp.zeros((1, D), xf.dtype)], axis=0)

    def body(acc, blk):
        tok, wt, e = blk
        xb = x_pad[tok]
        hb = jax.nn.silu(xb @ w1_e[e]) * (xb @ w3_e[e])
        yb = (hb @ w2_e[e]) * wt[:, None].astype(xb.dtype)
        return acc.at[tok].add(yb.astype(acc.dtype)), None

    acc0 = jnp.zeros((T + 1, D), xf.dtype)
    acc, _ = lax.scan(body, acc0, (slot_tok.reshape(n_blocks, MOE_BLOCK),
                                   slot_w.reshape(n_blocks, MOE_BLOCK), block_e))
    shared = (jax.nn.silu(xf @ w1_s) * (xf @ w3_s)) @ w2_s
    return acc[:T] + shared


def setup_inputs(seed: int = 0) -> dict:
    key = jax.random.key(seed)
    ks = iter(jax.random.split(key, 48))
    L, D = DEPTH, D_MODEL

    def nrm(shape, fan_in):
        return jax.random.normal(next(ks), shape, jnp.float32) * fan_in ** -0.5

    def gain(shape):
        return 1.0 + 0.05 * jax.random.normal(next(ks), shape, jnp.float32)

    def small(shape, s=0.02):
        return s * jax.random.normal(next(ks), shape, jnp.float32)

    x = jax.random.normal(next(ks), (BATCH, SEQ, D), jnp.float32)
    c = jax.random.normal(next(ks), (BATCH, D), jnp.float32)
    w_ada = nrm((L, D, 6 * D), D)
    b_ada = small((L, 6 * D))
    g_pre_mix = gain((L, D))
    w_in = nrm((L, D, W_IN), D)
    conv_w = nrm((L, CONV_W, D_RNN), CONV_W)
    conv_b = small((L, D_RNN))
    lru_wa = nrm((L, LRU_BLOCKS, LRU_BW, LRU_BW), LRU_BW)
    lru_ba = small((L, LRU_BLOCKS, LRU_BW))
    lru_wx = nrm((L, LRU_BLOCKS, LRU_BW, LRU_BW), LRU_BW)
    lru_bx = small((L, LRU_BLOCKS, LRU_BW))
    a0 = jax.random.uniform(next(ks), (L, D_RNN), jnp.float32, minval=0.9, maxval=0.999)
    lru_lambda = jnp.log(a0) - jnp.log1p(-a0)
    lam_q1 = small((L, HEAD_DIM), 0.1)
    lam_k1 = small((L, HEAD_DIM), 0.1)
    lam_q2 = small((L, HEAD_DIM), 0.1)
    lam_k2 = small((L, HEAD_DIM), 0.1)
    g_subln = gain((L, 2 * HEAD_DIM))
    w_proj_rnn = nrm((L, D_RNN, D), D_RNN)
    w_proj_att = nrm((L, ATT_V, D), ATT_V)
    w_out = nrm((L, D, D), D)
    g_post_mix = gain((L, D))
    g_pre_ffn = gain((L, D))
    w_router = nrm((L, D, N_EXPERTS), D)
    e_bias = small((L, N_EXPERTS), 0.01)
    w1_e = nrm((L, N_EXPERTS, D, D_EXPERT), D)
    w3_e = nrm((L, N_EXPERTS, D, D_EXPERT), D)
    w2_e = nrm((L, N_EXPERTS, D_EXPERT, D), D_EXPERT)
    w1_s = nrm((L, D, D_SHARED), D)
    w3_s = nrm((L, D, D_SHARED), D)
    w2_s = nrm((L, D_SHARED, D), D_SHARED)
    g_post_ffn = gain((L, D))
    return {"x": x, "c": c, "w_ada": w_ada, "b_ada": b_ada, "g_pre_mix": g_pre_mix,
            "w_in": w_in, "conv_w": conv_w, "conv_b": conv_b, "lru_wa": lru_wa,
            "lru_ba": lru_ba, "lru_wx": lru_wx, "lru_bx": lru_bx, "lru_lambda": lru_lambda,
            "lam_q1": lam_q1, "lam_k1": lam_k1, "lam_q2": lam_q2, "lam_k2": lam_k2,
            "g_subln": g_subln, "w_proj_rnn": w_proj_rnn, "w_proj_att": w_proj_att,
            "w_out": w_out, "g_post_mix": g_post_mix, "g_pre_ffn": g_pre_ffn,
            "w_router": w_router, "e_bias": e_bias, "w1_e": w1_e, "w3_e": w3_e,
            "w2_e": w2_e, "w1_s": w1_s, "w3_s": w3_s, "w2_s": w2_s, "g_post_ffn": g_post_ffn}


def reference(x, c, w_ada, b_ada, g_pre_mix, w_in, conv_w, conv_b, lru_wa, lru_ba, lru_wx,
              lru_bx, lru_lambda, lam_q1, lam_k1, lam_q2, lam_k2, g_subln, w_proj_rnn,
              w_proj_att, w_out, g_post_mix, g_pre_ffn, w_router, e_bias, w1_e, w3_e, w2_e,
              w1_s, w3_s, w2_s, g_post_ffn):
    B, S, D = x.shape
    cond = jax.nn.silu(c)
    for l in range(DEPTH):
        mod = cond @ w_ada[l] + b_ada[l]
        sh1, sc1, gt1, sh2, sc2, gt2 = [m[:, None, :] for m in jnp.split(mod, 6, axis=-1)]
        h = modulate(rms_norm(x, g_pre_mix[l]), sh1, sc1)
        y = hybrid_mixer(h, w_in[l], conv_w[l], conv_b[l], lru_wa[l], lru_ba[l], lru_wx[l],
                         lru_bx[l], lru_lambda[l], lam_q1[l], lam_k1[l], lam_q2[l], lam_k2[l],
                         g_subln[l], w_proj_rnn[l], w_proj_att[l], w_out[l], lambda_init(l))
        x = x + gt1 * rms_norm(y, g_post_mix[l])
        h = modulate(rms_norm(x, g_pre_ffn[l]), sh2, sc2)
        y = moe_ffn(h.reshape(B * S, D), w_router[l], e_bias[l], w1_e[l], w3_e[l], w2_e[l],
                    w1_s[l], w3_s[l], w2_s[l]).reshape(B, S, D)
        x = x + gt2 * rms_norm(y, g_post_ffn[l])
    return x
```

```python
import functools
import math

import jax
import jax.numpy as jnp
from jax import lax
from jax.experimental import pallas as pl
from jax.experimental.pallas import tpu as pltpu

F32 = jnp.float32
BF16 = jnp.bfloat16

EPS = 1e-6
N_HEADS = 8
HEAD_DIM = 64
V_DIM = 2 * HEAD_DIM
LRU_BLOCKS = 8
CONV_W = 4
LRU_C = 8.0
N_EXPERTS = 64
TOP_K = 8
N_GROUPS = 8
GROUP_SIZE = N_EXPERTS // N_GROUPS
TOPK_GROUPS = 4
ROUTED_SCALE = 2.5

LANES = 128
SUBLANES = 8
VMEM_LIMIT = 48 * 1024 * 1024


def _cparams(sem):
    return pltpu.CompilerParams(dimension_semantics=sem, vmem_limit_bytes=VMEM_LIMIT)


def _tile(n, pref):
    t = min(n, pref)
    while n % t:
        t //= 2
    return t


def _rms(x, g):
    return x * lax.rsqrt(jnp.mean(x * x, axis=-1, keepdims=True) + EPS) * g


def _ada_kernel(c_ref, w_ref, b_ref, o_ref):
    c = c_ref[...]
    cond = c * jax.nn.sigmoid(c)
    o_ref[...] = jnp.dot(cond, w_ref[...], preferred_element_type=F32) + b_ref[...]


def _ada(c, w, b):
    B, D = c.shape
    N = w.shape[1]
    tn = _tile(N, 1024)
    return pl.pallas_call(
        _ada_kernel,
        grid=(N // tn,),
        in_specs=[pl.BlockSpec((B, D), lambda j: (0, 0)),
                  pl.BlockSpec((D, tn), lambda j: (0, j)),
                  pl.BlockSpec((1, tn), lambda j: (0, j))],
        out_specs=pl.BlockSpec((B, tn), lambda j: (0, j)),
        out_shape=jax.ShapeDtypeStruct((B, N), F32),
        compiler_params=_cparams(("parallel",)),
        name="ada",
    )(c, w, b.reshape(1, N))


def _inproj_kernel(x_ref, g_ref, sh_ref, sc_ref, w_ref, o_ref, h_scr):
    @pl.when(pl.program_id(1) == 0)
    def _():
        h = _rms(x_ref[...], g_ref[...]) * (1.0 + sc_ref[...]) + sh_ref[...]
        h_scr[...] = h.astype(BF16)

    o_ref[...] = jnp.dot(h_scr[...], w_ref[...], preferred_element_type=F32).astype(o_ref.dtype)


def _inproj(x2, g, mod4, w_bf, S):
    T, D = x2.shape
    N = w_bf.shape[1]
    tm = _tile(S, 1024)
    tn = 1024
    per_b = S // tm
    return pl.pallas_call(
        _inproj_kernel,
        grid=(T // tm, N // tn),
        in_specs=[pl.BlockSpec((tm, D), lambda i, j: (i, 0)),
                  pl.BlockSpec((1, D), lambda i, j: (0, 0)),
                  pl.BlockSpec((None, None, 1, D), lambda i, j: (i // per_b, 0, 0, 0)),
                  pl.BlockSpec((None, None, 1, D), lambda i, j: (i // per_b, 1, 0, 0)),
                  pl.BlockSpec((D, tn), lambda i, j: (0, j))],
        out_specs=pl.BlockSpec((tm, tn), lambda i, j: (i, j)),
        out_shape=jax.ShapeDtypeStruct((T, N), BF16),
        scratch_shapes=[pltpu.VMEM((tm, D), BF16)],
        compiler_params=_cparams(("parallel", "arbitrary")),
        name="inproj",
    )(x2, g.reshape(1, D), mod4, mod4, w_bf)


def _lru_kernel(xr_ref, gr_ref, cw_ref, cb_ref, wa_ref, ba_ref, wx_ref, bx_ref, lam_ref,
                o_ref, xc_scr, prev_scr, h_scr, *, ts):
    s = pl.program_id(1)

    @pl.when(s == 0)
    def _():
        prev_scr[...] = jnp.zeros_like(prev_scr)
        h_scr[...] = jnp.zeros_like(h_scr)

    x = xr_ref[...].astype(F32)
    prev = prev_scr[...]
    row8 = lax.broadcasted_iota(jnp.int32, (SUBLANES, 1), 0)
    acc = cb_ref[...] + cw_ref[CONV_W - 1:CONV_W, :] * x
    xc_scr[...] = acc
    top = cb_ref[...] + cw_ref[CONV_W - 1:CONV_W, :] * x[0:SUBLANES, :]
    for j in range(1, CONV_W):
        wj = cw_ref[CONV_W - 1 - j:CONV_W - j, :]
        rj = pltpu.roll(x, j, axis=0)
        xc_scr[...] += wj * rj
        pj = pltpu.roll(prev, j, axis=0)
        top += wj * jnp.where(row8 < j, pj, rj[0:SUBLANES, :])
    xc_scr[0:SUBLANES, :] = top
    prev_scr[...] = x[ts - SUBLANES:ts, :]

    row = lax.broadcasted_iota(jnp.int32, (ts, 1), 0)
    is_first = jnp.logical_and(row == 0, s == 0)
    for n in range(LRU_BLOCKS):
        cols = slice(n * LANES, (n + 1) * LANES)
        xc = xc_scr[:, cols]
        xb = xc.astype(BF16)
        r = jax.nn.sigmoid(jnp.dot(xb, wa_ref[n], preferred_element_type=F32) + ba_ref[:, cols])
        i = jax.nn.sigmoid(jnp.dot(xb, wx_ref[n], preferred_element_type=F32) + bx_ref[:, cols])
        lam = lam_ref[:, cols]
        softplus_neg = jnp.maximum(-lam, 0.0) + jnp.log1p(jnp.exp(-jnp.abs(lam)))
        log_a = (-LRU_C * softplus_neg) * r
        a = jnp.exp(log_a)
        mult = jnp.sqrt(1.0 - a * a)
        mult = jnp.where(is_first, 1.0, mult)
        u = mult * (i * xc)
        d = 1
        while d < ts:
            keep = row >= d
            a_sh = jnp.where(keep, pltpu.roll(a, d, axis=0), 1.0)
            u_sh = jnp.where(keep, pltpu.roll(u, d, axis=0), 0.0)
            u = u + a * u_sh
            a = a * a_sh
            d *= 2
        h = u + a * h_scr[:, cols]
        h_scr[:, cols] = h[ts - 1:ts, :]
        gate = jax.nn.gelu(gr_ref[:, cols].astype(F32))
        o_ref[:, cols] = (h * gate).astype(o_ref.dtype)


def _lru(proj, conv_w, conv_b, wa_bf, ba, wx_bf, bx, lam, B, S):
    C = conv_w.shape[1]
    ts = _tile(S, 256)
    ns = S // ts
    vec = lambda: pl.BlockSpec((1, C), lambda b, s: (0, 0))
    blk = lambda: pl.BlockSpec((LRU_BLOCKS, LANES, LANES), lambda b, s: (0, 0, 0))
    return pl.pallas_call(
        functools.partial(_lru_kernel, ts=ts),
        grid=(B, ns),
        in_specs=[pl.BlockSpec((ts, C), lambda b, s: (b * ns + s, 0)),
                  pl.BlockSpec((ts, C), lambda b, s: (b * ns + s, 1)),
                  pl.BlockSpec((CONV_W, C), lambda b, s: (0, 0)),
                  vec(), blk(), vec(), blk(), vec(), vec()],
        out_specs=pl.BlockSpec((ts, C), lambda b, s: (b * ns + s, 0)),
        out_shape=jax.ShapeDtypeStruct((B * S, C), BF16),
        scratch_shapes=[pltpu.VMEM((ts, C), F32), pltpu.VMEM((SUBLANES, C), F32),
                        pltpu.VMEM((1, C), F32)],
        compiler_params=_cparams(("parallel", "arbitrary")),
        name="lru",
    )(proj, proj, conv_w, conv_b.reshape(1, C), wa_bf, ba.reshape(1, C), wx_bf, bx.reshape(1, C),
      lam.reshape(1, C))


def _attn_kernel(slope_ref, lamv_ref, gsub_ref, q_ref, k_ref, v_ref, o_ref,
                 m_scr, l_scr, acc_scr, *, tq, lam_init):
    hd = pl.program_id(1)
    qi = pl.program_id(2)
    ki = pl.program_id(3)

    @pl.when(ki == 0)
    def _():
        m_scr[...] = jnp.full_like(m_scr, -jnp.inf)
        l_scr[...] = jnp.zeros_like(l_scr)
        acc_scr[...] = jnp.zeros_like(acc_scr)

    @pl.when(ki <= qi)
    def _():
        q = q_ref[...]
        k = k_ref[...]
        v = v_ref[...]
        lane = lax.broadcasted_iota(jnp.int32, (1, V_DIM), 1)
        zero = jnp.zeros_like(q)
        slope = slope_ref[hd]
        pos_q = qi * tq + lax.broadcasted_iota(jnp.int32, (tq, 1), 0)
        pos_k = ki * tq + lax.broadcasted_iota(jnp.int32, (1, tq), 1)
        dist = (pos_q - pos_k).astype(F32)
        bias = -slope * dist
        visible = dist >= 0
        scale = HEAD_DIM ** -0.5
        for mp in range(2):
            qm = jnp.where((lane < HEAD_DIM) if mp == 0 else (lane >= HEAD_DIM), q, zero)
            s = lax.dot_general(qm, k, (((1,), (1,)), ((), ())), preferred_element_type=F32)
            s = jnp.where(visible, s * scale + bias, -jnp.inf)
            m_old = m_scr[mp]
            m_new = jnp.maximum(m_old, jnp.max(s, axis=-1, keepdims=True))
            alpha = jnp.exp(m_old - m_new)
            p = jnp.exp(s - m_new)
            l_scr[mp] = alpha * l_scr[mp] + jnp.sum(p, axis=-1, keepdims=True)
            acc_scr[mp] = alpha * acc_scr[mp] + jnp.dot(p.astype(BF16), v,
                                                        preferred_element_type=F32)
            m_scr[mp] = m_new

    @pl.when(ki == qi)
    def _():
        lv = lamv_ref[...]
        lam = (jnp.exp(jnp.sum(lv[0:1, :] * lv[1:2, :], axis=-1, keepdims=True))
               - jnp.exp(jnp.sum(lv[2:3, :] * lv[3:4, :], axis=-1, keepdims=True)) + lam_init)
        o = acc_scr[0] / l_scr[0] - lam * (acc_scr[1] / l_scr[1])
        o = _rms(o, gsub_ref[...]) * (1.0 - lam_init)
        o_ref[...] = o.astype(o_ref.dtype)


def _attn(proj, slopes, lamv, g_subln, B, S, lam_init):
    tq = _tile(S, 512)
    nq = S // tq
    qc, kc, vc = 2 * 8, 3 * 8, 4 * 8
    grid_spec = pltpu.PrefetchScalarGridSpec(
        num_scalar_prefetch=1,
        grid=(B, N_HEADS, nq, nq),
        in_specs=[pl.BlockSpec((4, HEAD_DIM), lambda b, h, qi, ki, sl: (0, 0)),
                  pl.BlockSpec((1, V_DIM), lambda b, h, qi, ki, sl: (0, 0)),
                  pl.BlockSpec((tq, V_DIM), lambda b, h, qi, ki, sl: (b * nq + qi, qc + h)),
                  pl.BlockSpec((tq, V_DIM),
                               lambda b, h, qi, ki, sl: (b * nq + jnp.minimum(ki, qi), kc + h)),
                  pl.BlockSpec((tq, V_DIM),
                               lambda b, h, qi, ki, sl: (b * nq + jnp.minimum(ki, qi), vc + h))],
        out_specs=pl.BlockSpec((tq, V_DIM), lambda b, h, qi, ki, sl: (b * nq + qi, h)),
        scratch_shapes=[pltpu.VMEM((2, tq, 1), F32), pltpu.VMEM((2, tq, 1), F32),
                        pltpu.VMEM((2, tq, V_DIM), F32)],
    )
    return pl.pallas_call(
        functools.partial(_attn_kernel, tq=tq, lam_init=lam_init),
        grid_spec=grid_spec,
        out_shape=jax.ShapeDtypeStruct((B * S, N_HEADS * V_DIM), BF16),
        compiler_params=_cparams(("parallel", "parallel", "parallel", "arbitrary")),
        name="attn",
    )(slopes, lamv, g_subln.reshape(1, V_DIM), proj, proj, proj)


def _mixout_kernel(x_ref, yr_ref, ao_ref, ga_ref, gb_ref, wr_ref, wa_ref, wo_ref,
                   gpost_ref, gt1_ref, gpre_ref, sh2_ref, sc2_ref, x1_ref, h2_ref):
    ya = jnp.dot(yr_ref[...], wr_ref[...], preferred_element_type=F32)
    yb = jnp.dot(ao_ref[...], wa_ref[...], preferred_element_type=F32)
    merged = (jax.nn.sigmoid(ga_ref[...].astype(F32)) * ya
              + jax.nn.sigmoid(gb_ref[...].astype(F32)) * yb)
    y = jnp.dot(merged.astype(BF16), wo_ref[...], preferred_element_type=F32)
    x1 = x_ref[...] + gt1_ref[...] * _rms(y, gpost_ref[...])
    x1_ref[...] = x1
    h2_ref[...] = _rms(x1, gpre_ref[...]) * (1.0 + sc2_ref[...]) + sh2_ref[...]


def _mixout(x2, yr, ao, proj, wr_bf, wa_bf, wo_bf, g_post, g_pre, mod4, S):
    T, D = x2.shape
    tm = _tile(S, 512)
    per_b = S // tm
    gac, gbc = 5, 6
    row = lambda: pl.BlockSpec((tm, D), lambda i: (i, 0))
    wsp = lambda: pl.BlockSpec((D, D), lambda i: (0, 0))
    vec = lambda: pl.BlockSpec((1, D), lambda i: (0, 0))
    modv = lambda j: pl.BlockSpec((None, None, 1, D), lambda i: (i // per_b, j, 0, 0))
    return pl.pallas_call(
        _mixout_kernel,
        grid=(T // tm,),
        in_specs=[row(), row(), row(),
                  pl.BlockSpec((tm, D), lambda i: (i, gac)),
                  pl.BlockSpec((tm, D), lambda i: (i, gbc)),
                  wsp(), wsp(), wsp(), vec(), modv(2), vec(), modv(3), modv(4)],
        out_specs=[row(), row()],
        out_shape=[jax.ShapeDtypeStruct((T, D), F32), jax.ShapeDtypeStruct((T, D), F32)],
        compiler_params=_cparams(("parallel",)),
        name="mixout",
    )(x2, yr, ao, proj, proj, wr_bf, wa_bf, wo_bf, g_post.reshape(1, D), mod4,
      g_pre.reshape(1, D), mod4, mod4)


def _first_argmax(vals, ids, sentinel):
    m = jnp.max(vals, axis=0, keepdims=True)
    idx = jnp.min(jnp.where(vals == m, ids, sentinel), axis=0, keepdims=True)
    return m, idx


def _router_kernel(h_ref, wr_ref, eb_ref, e_ref, w_ref, r_ref, cnt_ref, cnt_scr, *, tm):
    step = pl.program_id(0)

    @pl.when(step == 0)
    def _():
        cnt_scr[...] = jnp.zeros_like(cnt_scr)

    logits = lax.dot_general(wr_ref[...], h_ref[...], (((1,), (1,)), ((), ())),
                             preferred_element_type=F32, precision=lax.Precision.HIGHEST)
    scores = jax.nn.sigmoid(logits)
    choice = scores + eb_ref[...]
    i8 = lax.broadcasted_iota(jnp.int32, (GROUP_SIZE, tm), 0)
    neg_inf = jnp.float32(-jnp.inf)

    slabs = [choice[g * GROUP_SIZE:(g + 1) * GROUP_SIZE, :] for g in range(N_GROUPS)]
    sc_slabs = [scores[g * GROUP_SIZE:(g + 1) * GROUP_SIZE, :] for g in range(N_GROUPS)]

    gs = jnp.zeros((N_GROUPS, tm), F32)
    for g in range(N_GROUPS):
        m1, idx1 = _first_argmax(slabs[g], i8, GROUP_SIZE)
        m2 = jnp.max(jnp.where(i8 == idx1, neg_inf, slabs[g]), axis=0, keepdims=True)
        gs = jnp.where(i8 == g, m1 + m2, gs)

    sel = jnp.zeros((N_GROUPS, tm), jnp.int32)
    cur = gs
    for _ in range(TOPK_GROUPS):
        _, idx = _first_argmax(cur, i8, N_GROUPS)
        hit = i8 == idx
        sel = jnp.where(hit, 1, sel)
        cur = jnp.where(hit, neg_inf, cur)

    masked = [jnp.where(sel[g:g + 1, :] > 0, slabs[g], neg_inf) for g in range(N_GROUPS)]
    ids = [i8 + g * GROUP_SIZE for g in range(N_GROUPS)]
    onehot = [jnp.zeros((GROUP_SIZE, tm), F32) for _ in range(N_GROUPS)]
    picks = []
    wts = []
    for _ in range(TOP_K):
        m = functools.reduce(jnp.maximum,
                             [jnp.max(c, axis=0, keepdims=True) for c in masked])
        idx = functools.reduce(
            jnp.minimum,
            [jnp.min(jnp.where(c == m, i, N_EXPERTS), axis=0, keepdims=True)
             for c, i in zip(masked, ids)])
        w = jnp.zeros((1, tm), F32)
        for g in range(N_GROUPS):
            hit = ids[g] == idx
            w = w + jnp.sum(jnp.where(hit, sc_slabs[g], 0.0), axis=0, keepdims=True)
            masked[g] = jnp.where(hit, neg_inf, masked[g])
            onehot[g] = jnp.where(hit, 1.0, onehot[g])
        picks.append(idx)
        wts.append(w)

    wsum = functools.reduce(lambda a, b: a + b, wts)
    norm = ROUTED_SCALE / (wsum + 1e-20)

    t_row = lax.broadcasted_iota(jnp.int32, (tm, tm), 0)
    t_col = lax.broadcasted_iota(jnp.int32, (tm, tm), 1)
    before = jnp.where(t_row < t_col, 1.0, 0.0).astype(BF16)
    cum = [jnp.dot(onehot[g].astype(BF16), before, preferred_element_type=F32)
           + cnt_scr[g * GROUP_SIZE:(g + 1) * GROUP_SIZE, :] for g in range(N_GROUPS)]

    for kk in range(TOP_K):
        rank = jnp.zeros((1, tm), F32)
        for g in range(N_GROUPS):
            rank = rank + jnp.sum(jnp.where(ids[g] == picks[kk], cum[g], 0.0),
                                  axis=0, keepdims=True)
        e_ref[kk:kk + 1, :] = picks[kk]
        w_ref[kk:kk + 1, :] = wts[kk] * norm
        r_ref[kk:kk + 1, :] = rank.astype(jnp.int32)

    for g in range(N_GROUPS):
        rows = slice(g * GROUP_SIZE, (g + 1) * GROUP_SIZE)
        cnt_scr[rows, :] = cnt_scr[rows, :] + jnp.sum(onehot[g], axis=1, keepdims=True)
    cnt_ref[...] = jnp.broadcast_to(cnt_scr[...], cnt_ref.shape)


def _router(h2, w_router_t, e_bias):
    T, D = h2.shape
    tm = _tile(T, 512)
    return pl.pallas_call(
        functools.partial(_router_kernel, tm=tm),
        grid=(T // tm,),
        in_specs=[pl.BlockSpec((tm, D), lambda i: (i, 0)),
                  pl.BlockSpec((N_EXPERTS, D), lambda i: (0, 0)),
                  pl.BlockSpec((N_EXPERTS, 1), lambda i: (0, 0))],
        out_specs=[pl.BlockSpec((TOP_K, tm), lambda i: (0, i)),
                   pl.BlockSpec((TOP_K, tm), lambda i: (0, i)),
                   pl.BlockSpec((TOP_K, tm), lambda i: (0, i)),
                   pl.BlockSpec((N_EXPERTS, LANES), lambda i: (0, 0))],
        out_shape=[jax.ShapeDtypeStruct((TOP_K, T), jnp.int32),
                   jax.ShapeDtypeStruct((TOP_K, T), F32),
                   jax.ShapeDtypeStruct((TOP_K, T), jnp.int32),
                   jax.ShapeDtypeStruct((N_EXPERTS, LANES), F32)],
        scratch_shapes=[pltpu.VMEM((N_EXPERTS, 1), F32)],
        compiler_params=_cparams(("arbitrary",)),
        name="router",
    )(h2, w_router_t, e_bias.reshape(N_EXPERTS, 1))


def _row_copy(src, src_row, dst, dst_row, sem):
    return pltpu.make_async_copy(src.at[pl.ds(src_row, 1), :], dst.at[pl.ds(dst_row, 1), :], sem)


def _dispatch_kernel(dest_ref, h_ref, xs_ref, sem, *, td):
    def issue(j, carry):
        for kk in range(TOP_K):
            _row_copy(h_ref, j, xs_ref, dest_ref[0, kk * td + j], sem).start()
        return carry

    lax.fori_loop(0, td, issue, 0)
    for _ in range(TOP_K):
        pltpu.make_async_copy(h_ref, xs_ref.at[pl.ds(0, td), :], sem).wait()


def _dispatch(h2, dest_tiles, td):
    T, D = h2.shape
    nt = T // td
    return pl.pallas_call(
        functools.partial(_dispatch_kernel, td=td),
        grid=(nt,),
        in_specs=[pl.BlockSpec((None, 1, TOP_K * td), lambda i: (i, 0, 0),
                               memory_space=pltpu.SMEM),
                  pl.BlockSpec((td, D), lambda i: (i, 0))],
        out_specs=pl.BlockSpec(memory_space=pl.ANY),
        out_shape=jax.ShapeDtypeStruct((T * TOP_K, D), h2.dtype),
        scratch_shapes=[pltpu.SemaphoreType.DMA],
        compiler_params=_cparams(("arbitrary",)),
        name="dispatch",
    )(dest_tiles, h2)


def _gmm_kernel(blk_ref, exp_ref, lo_ref, hi_ref, first_ref, x_ref, w1_ref, w3_ref, w2_ref,
                o_ref, *, bm):
    it = pl.program_id(0)
    rows = lax.broadcasted_iota(jnp.int32, (bm, 1), 0)
    valid = jnp.logical_and(rows >= lo_ref[it], rows < hi_ref[it])
    x = jnp.where(valid, x_ref[...], 0.0).astype(BF16)
    h1 = jnp.dot(x, w1_ref[...], preferred_element_type=F32)
    h3 = jnp.dot(x, w3_ref[...], preferred_element_type=F32)
    hb = (h1 * jax.nn.sigmoid(h1) * h3).astype(BF16)
    y = jnp.dot(hb, w2_ref[...], preferred_element_type=F32)

    @pl.when(first_ref[it] == 1)
    def _():
        o_ref[...] = y

    @pl.when(first_ref[it] == 0)
    def _():
        o_ref[...] += y


def _gmm(xs, items, w1_bf, w3_bf, w2_bf, bm):
    A, D = xs.shape
    F = w1_bf.shape[2]
    n_items = items[0].shape[0]
    grid_spec = pltpu.PrefetchScalarGridSpec(
        num_scalar_prefetch=5,
        grid=(n_items,),
        in_specs=[pl.BlockSpec((bm, D), lambda i, blk, ex, lo, hi, fi: (blk[i], 0)),
                  pl.BlockSpec((None, D, F), lambda i, blk, ex, lo, hi, fi: (ex[i], 0, 0)),
                  pl.BlockSpec((None, D, F), lambda i, blk, ex, lo, hi, fi: (ex[i], 0, 0)),
                  pl.BlockSpec((None, F, D), lambda i, blk, ex, lo, hi, fi: (ex[i], 0, 0))],
        out_specs=pl.BlockSpec((bm, D), lambda i, blk, ex, lo, hi, fi: (blk[i], 0)),
    )
    return pl.pallas_call(
        functools.partial(_gmm_kernel, bm=bm),
        grid_spec=grid_spec,
        out_shape=jax.ShapeDtypeStruct((A, D), F32),
        compiler_params=_cparams(("arbitrary",)),
        name="gmm",
    )(*items, xs, w1_bf, w3_bf, w2_bf)


def _work_items(counts, bm, n_blocks):
    n_items = n_blocks + N_EXPERTS - 1
    ends = jnp.cumsum(counts)
    starts = ends - counts
    nb = jnp.where(counts > 0, (ends - 1) // bm - starts // bm + 1, 0)
    item_end = jnp.cumsum(nb)
    item_start = item_end - nb
    n_real = item_end[-1]
    i = jnp.arange(n_items, dtype=jnp.int32)
    e = jnp.minimum(jnp.searchsorted(item_end, i, side="right"), N_EXPERTS - 1).astype(jnp.int32)
    blk = starts[e] // bm + (i - item_start[e])
    lo = jnp.clip(starts[e] - blk * bm, 0, bm)
    hi = jnp.clip(ends[e] - blk * bm, 0, bm)
    real = i < n_real
    blk = jnp.where(real, blk, n_blocks - 1).astype(jnp.int32)
    lo = jnp.where(real, lo, 0).astype(jnp.int32)
    hi = jnp.where(real, hi, 0).astype(jnp.int32)
    first = jnp.concatenate([jnp.ones((1,), jnp.int32),
                             (blk[1:] != blk[:-1]).astype(jnp.int32)])
    return blk, e, lo, hi, first


def _combine_kernel(dest_ref, ys_ref, w_ref, x1_ref, h_ref, w1_ref, w3_ref, w2_ref,
                    gt2_ref, g_ref, o_ref, buf, sem, *, tc):
    def issue(j, carry):
        for kk in range(TOP_K):
            _row_copy(ys_ref, dest_ref[0, kk * tc + j], buf.at[kk], j, sem).start()
        return carry

    lax.fori_loop(0, tc, issue, 0)

    hb = h_ref[...].astype(BF16)
    h1 = jnp.dot(hb, w1_ref[...], preferred_element_type=F32)
    h3 = jnp.dot(hb, w3_ref[...], preferred_element_type=F32)
    y = jnp.dot((h1 * jax.nn.sigmoid(h1) * h3).astype(BF16), w2_ref[...],
                preferred_element_type=F32)

    for kk in range(TOP_K):
        pltpu.make_async_copy(ys_ref.at[pl.ds(0, tc), :], buf.at[kk], sem).wait()
    moe = w_ref[:, 0:1] * buf[0]
    for kk in range(1, TOP_K):
        moe = moe + w_ref[:, kk:kk + 1] * buf[kk]
    o_ref[...] = x1_ref[...] + gt2_ref[...] * _rms(moe + y, g_ref[...])


def _combine(ys, dest_tiles, w_tok, x1, h2, w1s, w3s, w2s, mod4, g_post, S, tc):
    T, D = x1.shape
    F = w1s.shape[1]
    per_b = S // tc
    row = lambda: pl.BlockSpec((tc, D), lambda i: (i, 0))
    return pl.pallas_call(
        functools.partial(_combine_kernel, tc=tc),
        grid=(T // tc,),
        in_specs=[pl.BlockSpec((None, 1, TOP_K * tc), lambda i: (i, 0, 0),
                               memory_space=pltpu.SMEM),
                  pl.BlockSpec(memory_space=pl.ANY),
                  pl.BlockSpec((tc, TOP_K), lambda i: (i, 0)),
                  row(), row(),
                  pl.BlockSpec((D, F), lambda i: (0, 0)),
                  pl.BlockSpec((D, F), lambda i: (0, 0)),
                  pl.BlockSpec((F, D), lambda i: (0, 0)),
                  pl.BlockSpec((None, None, 1, D), lambda i: (i // per_b, 5, 0, 0)),
                  pl.BlockSpec((1, D), lambda i: (0, 0))],
        out_specs=row(),
        out_shape=jax.ShapeDtypeStruct((T, D), F32),
        scratch_shapes=[pltpu.VMEM((TOP_K, tc, D), F32), pltpu.SemaphoreType.DMA],
        compiler_params=_cparams(("arbitrary",)),
        name="combine",
    )(dest_tiles, ys, w_tok, x1, h2, w1s, w3s, w2s, mod4, g_post.reshape(1, D))


def _lambda_init(layer):
    return 0.8 - 0.6 * math.exp(-0.3 * layer)


def kernel(x, c, w_ada, b_ada, g_pre_mix, w_in, conv_w, conv_b, lru_wa, lru_ba, lru_wx, lru_bx,
           lru_lambda, lam_q1, lam_k1, lam_q2, lam_k2, g_subln, w_proj_rnn, w_proj_att, w_out,
           g_post_mix, g_pre_ffn, w_router, e_bias, w1_e, w3_e, w2_e, w1_s, w3_s, w2_s,
           g_post_ffn):
    B, S, D = x.shape
    T = B * S
    depth = w_ada.shape[0]
    slopes = jnp.exp2(-8.0 * jnp.arange(1, N_HEADS + 1, dtype=F32) / N_HEADS)
    tt = _tile(S, 256)
    bm = _tile(T * TOP_K, 256)
    n_blocks = T * TOP_K // bm

    x2 = x.reshape(T, D)
    for l in range(depth):
        lam_init = _lambda_init(l)
        mod4 = _ada(c, w_ada[l], b_ada[l]).reshape(B, 6, 1, D)

        proj = _inproj(x2, g_pre_mix[l], mod4, w_in[l].astype(BF16), S)
        yr = _lru(proj, conv_w[l], conv_b[l], lru_wa[l].astype(BF16), lru_ba[l],
                  lru_wx[l].astype(BF16), lru_bx[l], lru_lambda[l], B, S)
        lamv = jnp.stack([lam_q1[l], lam_k1[l], lam_q2[l], lam_k2[l]])
        ao = _attn(proj, slopes, lamv, g_subln[l], B, S, lam_init)
        x1, h2 = _mixout(x2, yr, ao, proj, w_proj_rnn[l].astype(BF16),
                         w_proj_att[l].astype(BF16), w_out[l].astype(BF16),
                         g_post_mix[l], g_pre_ffn[l], mod4, S)

        top_e, top_w, rank, cnt = _router(h2, w_router[l].T, e_bias[l])
        counts = cnt[:, 0].astype(jnp.int32)
        offs = jnp.cumsum(counts) - counts
        dest = offs[top_e] + rank
        dest_tiles = dest.reshape(TOP_K, T // tt, tt).transpose(1, 0, 2).reshape(T // tt, 1,
                                                                                  TOP_K * tt)
        xs = _dispatch(h2, dest_tiles, tt)
        items = _work_items(counts, bm, n_blocks)
        ys = _gmm(xs, items, w1_e[l].astype(BF16), w3_e[l].astype(BF16), w2_e[l].astype(BF16), bm)
        x2 = _combine(ys, dest_tiles, top_w.T, x1, h2, w1_s[l].astype(BF16),
                      w3_s[l].astype(BF16), w2_s[l].astype(BF16), mod4, g_post_ffn[l], S, tt)
    return x2.reshape(B, S, D)
```

```python
import functools
import math

import jax
import jax.numpy as jnp
from jax import lax
from jax.experimental import pallas as pl
from jax.experimental.pallas import tpu as pltpu

F32 = jnp.float32
BF16 = jnp.bfloat16

EPS = 1e-6
N_HEADS = 8
HEAD_DIM = 64
V_DIM = 2 * HEAD_DIM
LRU_BLOCKS = 8
CONV_W = 4
LRU_C = 8.0
N_EXPERTS = 64
TOP_K = 8
N_GROUPS = 8
GROUP_SIZE = N_EXPERTS // N_GROUPS
TOPK_GROUPS = 4
ROUTED_SCALE = 2.5

LANES = 128
SUBLANES = 8
VMEM_LIMIT = 48 * 1024 * 1024


def _cparams(sem):
    return pltpu.CompilerParams(dimension_semantics=sem, vmem_limit_bytes=VMEM_LIMIT)


def _tile(n, pref):
    t = min(n, pref)
    while n % t:
        t //= 2
    return t


def _rms(x, g):
    return x * lax.rsqrt(jnp.mean(x * x, axis=-1, keepdims=True) + EPS) * g


def _ada_kernel(c_ref, w_ref, b_ref, o_ref):
    c = c_ref[...]
    cond = c * jax.nn.sigmoid(c)
    o_ref[...] = jnp.dot(cond, w_ref[...], preferred_element_type=F32) + b_ref[...]


def _ada(c, w, b):
    B, D = c.shape
    N = w.shape[1]
    tn = _tile(N, 1024)
    return pl.pallas_call(
        _ada_kernel,
        grid=(N // tn,),
        in_specs=[pl.BlockSpec((B, D), lambda j: (0, 0)),
                  pl.BlockSpec((D, tn), lambda j: (0, j)),
                  pl.BlockSpec((1, tn), lambda j: (0, j))],
        out_specs=pl.BlockSpec((B, tn), lambda j: (0, j)),
        out_shape=jax.ShapeDtypeStruct((B, N), F32),
        compiler_params=_cparams(("parallel",)),
        name="ada",
    )(c, w, b.reshape(1, N))


LOG2E = 1.4426950408889634
Q_COL_BLOCK = 2
Q_PRESCALE = HEAD_DIM ** -0.5 * LOG2E


def _inproj_kernel(x_ref, g_ref, sh_ref, sc_ref, w_ref, o_ref, h_scr):
    @pl.when(pl.program_id(1) == 0)
    def _():
        h = _rms(x_ref[...], g_ref[...]) * (1.0 + sc_ref[...]) + sh_ref[...]
        h_scr[...] = h.astype(BF16)

    r = jnp.dot(h_scr[...], w_ref[...], preferred_element_type=F32)
    r = r * jnp.where(pl.program_id(1) == Q_COL_BLOCK, Q_PRESCALE, 1.0)
    o_ref[...] = r.astype(o_ref.dtype)


def _inproj(x2, g, mod4, w_bf, S):
    T, D = x2.shape
    N = w_bf.shape[1]
    tm = _tile(S, 1024)
    tn = 1024
    per_b = S // tm
    return pl.pallas_call(
        _inproj_kernel,
        grid=(T // tm, N // tn),
        in_specs=[pl.BlockSpec((tm, D), lambda i, j: (i, 0)),
                  pl.BlockSpec((1, D), lambda i, j: (0, 0)),
                  pl.BlockSpec((None, None, 1, D), lambda i, j: (i // per_b, 0, 0, 0)),
                  pl.BlockSpec((None, None, 1, D), lambda i, j: (i // per_b, 1, 0, 0)),
                  pl.BlockSpec((D, tn), lambda i, j: (0, j))],
        out_specs=pl.BlockSpec((tm, tn), lambda i, j: (i, j)),
        out_shape=jax.ShapeDtypeStruct((T, N), BF16),
        scratch_shapes=[pltpu.VMEM((tm, D), BF16)],
        compiler_params=_cparams(("parallel", "arbitrary")),
        name="inproj",
    )(x2, g.reshape(1, D), mod4, mod4, w_bf)


def _lru_kernel(xr_ref, gr_ref, cw_ref, cb_ref, wa_ref, ba_ref, wx_ref, bx_ref, lam_ref,
                o_ref, xc_scr, prev_scr, h_scr, *, ts):
    s = pl.program_id(1)

    @pl.when(s == 0)
    def _():
        prev_scr[...] = jnp.zeros_like(prev_scr)
        h_scr[...] = jnp.zeros_like(h_scr)

    x = xr_ref[...].astype(F32)
    prev = prev_scr[...]
    row8 = lax.broadcasted_iota(jnp.int32, (SUBLANES, 1), 0)
    acc = cb_ref[...] + cw_ref[CONV_W - 1:CONV_W, :] * x
    xc_scr[...] = acc
    top = cb_ref[...] + cw_ref[CONV_W - 1:CONV_W, :] * x[0:SUBLANES, :]
    for j in range(1, CONV_W):
        wj = cw_ref[CONV_W - 1 - j:CONV_W - j, :]
        rj = pltpu.roll(x, j, axis=0)
        xc_scr[...] += wj * rj
        pj = pltpu.roll(prev, j, axis=0)
        top += wj * jnp.where(row8 < j, pj, rj[0:SUBLANES, :])
    xc_scr[0:SUBLANES, :] = top
    prev_scr[...] = x[ts - SUBLANES:ts, :]

    row = lax.broadcasted_iota(jnp.int32, (ts, 1), 0)
    is_first = jnp.logical_and(row == 0, s == 0)
    for n in range(LRU_BLOCKS):
        cols = slice(n * LANES, (n + 1) * LANES)
        xc = xc_scr[:, cols]
        xb = xc.astype(BF16)
        r = jax.nn.sigmoid(jnp.dot(xb, wa_ref[n], preferred_element_type=F32) + ba_ref[:, cols])
        i = jax.nn.sigmoid(jnp.dot(xb, wx_ref[n], preferred_element_type=F32) + bx_ref[:, cols])
        lam = lam_ref[:, cols]
        softplus_neg = jnp.maximum(-lam, 0.0) + jnp.log1p(jnp.exp(-jnp.abs(lam)))
        log_a = (-LRU_C * softplus_neg) * r
        a = jnp.exp(log_a)
        mult = jnp.sqrt(1.0 - a * a)
        mult = jnp.where(is_first, 1.0, mult)
        u = mult * (i * xc)
        d = 1
        while d < ts:
            keep = row >= d
            a_sh = jnp.where(keep, pltpu.roll(a, d, axis=0), 1.0)
            u_sh = jnp.where(keep, pltpu.roll(u, d, axis=0), 0.0)
            u = u + a * u_sh
            a = a * a_sh
            d *= 2
        h = u + a * h_scr[:, cols]
        h_scr[:, cols] = h[ts - 1:ts, :]
        gate = jax.nn.gelu(gr_ref[:, cols].astype(F32))
        o_ref[:, cols] = (h * gate).astype(o_ref.dtype)


def _lru(proj, conv_w, conv_b, wa_bf, ba, wx_bf, bx, lam, B, S):
    C = conv_w.shape[1]
    ts = _tile(S, 256)
    ns = S // ts
    vec = lambda: pl.BlockSpec((1, C), lambda b, s: (0, 0))
    blk = lambda: pl.BlockSpec((LRU_BLOCKS, LANES, LANES), lambda b, s: (0, 0, 0))
    return pl.pallas_call(
        functools.partial(_lru_kernel, ts=ts),
        grid=(B, ns),
        in_specs=[pl.BlockSpec((ts, C), lambda b, s: (b * ns + s, 0)),
                  pl.BlockSpec((ts, C), lambda b, s: (b * ns + s, 1)),
                  pl.BlockSpec((CONV_W, C), lambda b, s: (0, 0)),
                  vec(), blk(), vec(), blk(), vec(), vec()],
        out_specs=pl.BlockSpec((ts, C), lambda b, s: (b * ns + s, 0)),
        out_shape=jax.ShapeDtypeStruct((B * S, C), BF16),
        scratch_shapes=[pltpu.VMEM((ts, C), F32), pltpu.VMEM((SUBLANES, C), F32),
                        pltpu.VMEM((1, C), F32)],
        compiler_params=_cparams(("parallel", "arbitrary")),
        name="lru",
    )(proj, proj, conv_w, conv_b.reshape(1, C), wa_bf, ba.reshape(1, C), wx_bf, bx.reshape(1, C),
      lam.reshape(1, C))


def _attn_kernel(slope_ref, lamv_ref, gsub_ref, q_ref, k_ref, v_ref, o_ref,
                 vt_scr, s_scr, p_scr, b_scr, a_scr, m_scr, l_scr, acc_scr, *, tq, lam_init):
    hd = pl.program_id(1)
    qi = pl.program_id(2)
    tk = tq
    nk = vt_scr.shape[0]

    @pl.when(qi == 0)
    def _():
        for j in range(nk):
            vt_scr[j] = v_ref[j * tk:(j + 1) * tk, :].astype(F32).T.astype(BF16)

    slope2 = slope_ref[hd] * LOG2E
    lane = lax.broadcasted_iota(jnp.int32, (1, V_DIM), 1)
    q = q_ref[...]
    zero = jnp.zeros_like(q)
    qm = (jnp.where(lane < HEAD_DIM, q, zero), jnp.where(lane >= HEAD_DIM, q, zero))
    m_scr[...] = jnp.full_like(m_scr, -jnp.inf)
    l_scr[...] = jnp.zeros_like(l_scr)
    acc_scr[...] = jnp.zeros_like(acc_scr)
    key_iota = lax.broadcasted_iota(jnp.int32, (tk, LANES), 0)
    row8 = lax.broadcasted_iota(jnp.int32, (SUBLANES, LANES), 0)
    row16 = lax.broadcasted_iota(jnp.int32, (2 * SUBLANES, LANES), 0)
    col = lax.broadcasted_iota(jnp.int32, (1, LANES), 1)
    neg_inf = jnp.float32(-jnp.inf)

    def tree(op, parts):
        parts = [p for p in parts if p is not None]
        while len(parts) > 1:
            parts = [op(parts[i], parts[i + 1]) if i + 1 < len(parts) else parts[i]
                     for i in range(0, len(parts), 2)]
        return parts[0]

    def block(ki, masked):
        start = pl.multiple_of(ki * tk, tk)
        k = k_ref[pl.ds(start, tk), :]
        b_scr[...] = slope2 * (start + key_iota).astype(F32)
        for mp in range(2):
            s_scr[mp] = lax.dot_general(k, qm[mp], (((1,), (1,)), ((), ())),
                                        preferred_element_type=F32)
        for mp in range(2):
            for c in range(tq // LANES):
                cols = slice(c * LANES, (c + 1) * LANES)
                lo_col, hi_col = c * LANES, (c + 1) * LANES - 1

                def scores(r0, n, riota):
                    t = s_scr[mp, r0:r0 + n, cols] + b_scr[r0:r0 + n, :]
                    if masked and r0 + n - 1 > lo_col:
                        t = jnp.where(riota + r0 <= col + lo_col, t, neg_inf)
                    return t

                accs = [None] * 4
                for i in range(tk // SUBLANES):
                    r0 = i * SUBLANES
                    if masked and r0 > hi_col:
                        continue
                    t = scores(r0, SUBLANES, row8)
                    accs[i % 4] = t if accs[i % 4] is None else jnp.maximum(accs[i % 4], t)
                mx = jnp.max(tree(jnp.maximum, accs), axis=0, keepdims=True)
                m_old = m_scr[mp, :, cols]
                m_new = jnp.maximum(m_old, mx)
                alpha = jnp.exp2(m_old - m_new)
                sums = [None] * 4
                for i in range(tk // (2 * SUBLANES)):
                    r0 = i * 2 * SUBLANES
                    if masked and r0 > hi_col:
                        p_scr[mp, r0:r0 + 2 * SUBLANES, cols] = jnp.zeros((2 * SUBLANES, LANES), BF16)
                        continue
                    p = jnp.exp2(scores(r0, 2 * SUBLANES, row16) - m_new)
                    sums[i % 4] = p if sums[i % 4] is None else sums[i % 4] + p
                    p_scr[mp, r0:r0 + 2 * SUBLANES, cols] = p.astype(BF16)
                psum = jnp.sum(tree(lambda a, b: a + b, sums), axis=0, keepdims=True)
                l_scr[mp, :, cols] = alpha * l_scr[mp, :, cols] + psum
                m_scr[mp, :, cols] = m_new
                a_scr[mp, :, cols] = alpha
        for mp in range(2):
            pv = jnp.dot(vt_scr[ki], p_scr[mp], preferred_element_type=F32)
            acc_scr[mp] = a_scr[mp] * acc_scr[mp] + pv

    def full_block(ki, carry):
        block(ki, False)
        return carry

    lax.fori_loop(0, qi, full_block, 0)
    block(qi, True)

    lv = lamv_ref[...]
    lam = (jnp.exp(jnp.sum(lv[0:1, :] * lv[1:2, :], axis=-1, keepdims=True))
           - jnp.exp(jnp.sum(lv[2:3, :] * lv[3:4, :], axis=-1, keepdims=True)) + lam_init)
    o_t = acc_scr[0] * (1.0 / l_scr[0]) - lam * (acc_scr[1] * (1.0 / l_scr[1]))
    o_t = o_t * lax.rsqrt(jnp.mean(o_t * o_t, axis=0, keepdims=True) + EPS) * gsub_ref[...]
    o_ref[...] = (o_t * (1.0 - lam_init)).T.astype(o_ref.dtype)


def _attn(proj, slopes, lamv, g_subln, B, S, lam_init):
    tq = _tile(S, 512)
    nq = S // tq
    qc, kc, vc = 2 * 8, 3 * 8, 4 * 8
    grid_spec = pltpu.PrefetchScalarGridSpec(
        num_scalar_prefetch=1,
        grid=(B, N_HEADS, nq),
        in_specs=[pl.BlockSpec((4, HEAD_DIM), lambda b, h, qi, sl: (0, 0)),
                  pl.BlockSpec((V_DIM, 1), lambda b, h, qi, sl: (0, 0)),
                  pl.BlockSpec((tq, V_DIM), lambda b, h, qi, sl: (b * nq + qi, qc + h)),
                  pl.BlockSpec((S, V_DIM), lambda b, h, qi, sl: (b, kc + h)),
                  pl.BlockSpec((S, V_DIM), lambda b, h, qi, sl: (b, vc + h))],
        out_specs=pl.BlockSpec((tq, V_DIM), lambda b, h, qi, sl: (b * nq + qi, h)),
        scratch_shapes=[pltpu.VMEM((nq, V_DIM, tq), BF16),
                        pltpu.VMEM((2, tq, tq), F32), pltpu.VMEM((2, tq, tq), BF16),
                        pltpu.VMEM((tq, LANES), F32),
                        pltpu.VMEM((2, 1, tq), F32), pltpu.VMEM((2, 1, tq), F32),
                        pltpu.VMEM((2, 1, tq), F32), pltpu.VMEM((2, V_DIM, tq), F32)],
    )
    return pl.pallas_call(
        functools.partial(_attn_kernel, tq=tq, lam_init=lam_init),
        grid_spec=grid_spec,
        out_shape=jax.ShapeDtypeStruct((B * S, N_HEADS * V_DIM), BF16),
        compiler_params=_cparams(("parallel", "parallel", "arbitrary")),
        name="attn",
    )(slopes, lamv, g_subln.reshape(V_DIM, 1), proj, proj, proj)


def _mixout_kernel(x_ref, yr_ref, ao_ref, ga_ref, gb_ref, wr_ref, wa_ref, wo_ref,
                   gpost_ref, gt1_ref, gpre_ref, sh2_ref, sc2_ref, x1_ref, h2_ref):
    ya = jnp.dot(yr_ref[...], wr_ref[...], preferred_element_type=F32)
    yb = jnp.dot(ao_ref[...], wa_ref[...], preferred_element_type=F32)
    merged = (jax.nn.sigmoid(ga_ref[...].astype(F32)) * ya
              + jax.nn.sigmoid(gb_ref[...].astype(F32)) * yb)
    y = jnp.dot(merged.astype(BF16), wo_ref[...], preferred_element_type=F32)
    x1 = x_ref[...] + gt1_ref[...] * _rms(y, gpost_ref[...])
    x1_ref[...] = x1
    h2_ref[...] = _rms(x1, gpre_ref[...]) * (1.0 + sc2_ref[...]) + sh2_ref[...]


def _mixout(x2, yr, ao, proj, wr_bf, wa_bf, wo_bf, g_post, g_pre, mod4, S):
    T, D = x2.shape
    tm = _tile(S, 512)
    per_b = S // tm
    gac, gbc = 5, 6
    row = lambda: pl.BlockSpec((tm, D), lambda i: (i, 0))
    wsp = lambda: pl.BlockSpec((D, D), lambda i: (0, 0))
    vec = lambda: pl.BlockSpec((1, D), lambda i: (0, 0))
    modv = lambda j: pl.BlockSpec((None, None, 1, D), lambda i: (i // per_b, j, 0, 0))
    return pl.pallas_call(
        _mixout_kernel,
        grid=(T // tm,),
        in_specs=[row(), row(), row(),
                  pl.BlockSpec((tm, D), lambda i: (i, gac)),
                  pl.BlockSpec((tm, D), lambda i: (i, gbc)),
                  wsp(), wsp(), wsp(), vec(), modv(2), vec(), modv(3), modv(4)],
        out_specs=[row(), row()],
        out_shape=[jax.ShapeDtypeStruct((T, D), F32), jax.ShapeDtypeStruct((T, D), F32)],
        compiler_params=_cparams(("parallel",)),
        name="mixout",
    )(x2, yr, ao, proj, proj, wr_bf, wa_bf, wo_bf, g_post.reshape(1, D), mod4,
      g_pre.reshape(1, D), mod4, mod4)


def _first_argmax(vals, ids, sentinel):
    m = jnp.max(vals, axis=0, keepdims=True)
    idx = jnp.min(jnp.where(vals == m, ids, sentinel), axis=0, keepdims=True)
    return m, idx


def _router_kernel(h_ref, wr_ref, eb_ref, e_ref, w_ref, r_ref, cnt_ref, cnt_scr, *, tm):
    step = pl.program_id(0)

    @pl.when(step == 0)
    def _():
        cnt_scr[...] = jnp.zeros_like(cnt_scr)

    logits = lax.dot_general(wr_ref[...], h_ref[...], (((1,), (1,)), ((), ())),
                             preferred_element_type=F32, precision=lax.Precision.HIGHEST)
    scores = jax.nn.sigmoid(logits)
    choice = scores + eb_ref[...]
    i8 = lax.broadcasted_iota(jnp.int32, (GROUP_SIZE, tm), 0)
    neg_inf = jnp.float32(-jnp.inf)

    slabs = [choice[g * GROUP_SIZE:(g + 1) * GROUP_SIZE, :] for g in range(N_GROUPS)]
    sc_slabs = [scores[g * GROUP_SIZE:(g + 1) * GROUP_SIZE, :] for g in range(N_GROUPS)]

    gs = jnp.zeros((N_GROUPS, tm), F32)
    for g in range(N_GROUPS):
        m1, idx1 = _first_argmax(slabs[g], i8, GROUP_SIZE)
        m2 = jnp.max(jnp.where(i8 == idx1, neg_inf, slabs[g]), axis=0, keepdims=True)
        gs = jnp.where(i8 == g, m1 + m2, gs)

    sel = jnp.zeros((N_GROUPS, tm), jnp.int32)
    cur = gs
    for _ in range(TOPK_GROUPS):
        _, idx = _first_argmax(cur, i8, N_GROUPS)
        hit = i8 == idx
        sel = jnp.where(hit, 1, sel)
        cur = jnp.where(hit, neg_inf, cur)

    masked = [jnp.where(sel[g:g + 1, :] > 0, slabs[g], neg_inf) for g in range(N_GROUPS)]
    ids = [i8 + g * GROUP_SIZE for g in range(N_GROUPS)]
    onehot = [jnp.zeros((GROUP_SIZE, tm), F32) for _ in range(N_GROUPS)]
    picks = []
    wts = []
    for _ in range(TOP_K):
        m = functools.reduce(jnp.maximum,
                             [jnp.max(c, axis=0, keepdims=True) for c in masked])
        idx = functools.reduce(
            jnp.minimum,
            [jnp.min(jnp.where(c == m, i, N_EXPERTS), axis=0, keepdims=True)
             for c, i in zip(masked, ids)])
        w = jnp.zeros((1, tm), F32)
        for g in range(N_GROUPS):
            hit = ids[g] == idx
            w = w + jnp.sum(jnp.where(hit, sc_slabs[g], 0.0), axis=0, keepdims=True)
            masked[g] = jnp.where(hit, neg_inf, masked[g])
            onehot[g] = jnp.where(hit, 1.0, onehot[g])
        picks.append(idx)
        wts.append(w)

    wsum = functools.reduce(lambda a, b: a + b, wts)
    norm = ROUTED_SCALE / (wsum + 1e-20)

    t_row = lax.broadcasted_iota(jnp.int32, (tm, tm), 0)
    t_col = lax.broadcasted_iota(jnp.int32, (tm, tm), 1)
    before = jnp.where(t_row < t_col, 1.0, 0.0).astype(BF16)
    cum = [jnp.dot(onehot[g].astype(BF16), before, preferred_element_type=F32)
           + cnt_scr[g * GROUP_SIZE:(g + 1) * GROUP_SIZE, :] for g in range(N_GROUPS)]

    for kk in range(TOP_K):
        rank = jnp.zeros((1, tm), F32)
        for g in range(N_GROUPS):
            rank = rank + jnp.sum(jnp.where(ids[g] == picks[kk], cum[g], 0.0),
                                  axis=0, keepdims=True)
        e_ref[kk:kk + 1, :] = picks[kk]
        w_ref[kk:kk + 1, :] = wts[kk] * norm
        r_ref[kk:kk + 1, :] = rank.astype(jnp.int32)

    for g in range(N_GROUPS):
        rows = slice(g * GROUP_SIZE, (g + 1) * GROUP_SIZE)
        cnt_scr[rows, :] = cnt_scr[rows, :] + jnp.sum(onehot[g], axis=1, keepdims=True)
    cnt_ref[...] = jnp.broadcast_to(cnt_scr[...], cnt_ref.shape)


def _router(h2, w_router_t, e_bias):
    T, D = h2.shape
    tm = _tile(T, 512)
    return pl.pallas_call(
        functools.partial(_router_kernel, tm=tm),
        grid=(T // tm,),
        in_specs=[pl.BlockSpec((tm, D), lambda i: (i, 0)),
                  pl.BlockSpec((N_EXPERTS, D), lambda i: (0, 0)),
                  pl.BlockSpec((N_EXPERTS, 1), lambda i: (0, 0))],
        out_specs=[pl.BlockSpec((TOP_K, tm), lambda i: (0, i)),
                   pl.BlockSpec((TOP_K, tm), lambda i: (0, i)),
                   pl.BlockSpec((TOP_K, tm), lambda i: (0, i)),
                   pl.BlockSpec((N_EXPERTS, LANES), lambda i: (0, 0))],
        out_shape=[jax.ShapeDtypeStruct((TOP_K, T), jnp.int32),
                   jax.ShapeDtypeStruct((TOP_K, T), F32),
                   jax.ShapeDtypeStruct((TOP_K, T), jnp.int32),
                   jax.ShapeDtypeStruct((N_EXPERTS, LANES), F32)],
        scratch_shapes=[pltpu.VMEM((N_EXPERTS, 1), F32)],
        compiler_params=_cparams(("arbitrary",)),
        name="router",
    )(h2, w_router_t, e_bias.reshape(N_EXPERTS, 1))


def _row_copy(src, src_row, dst, dst_row, sem):
    return pltpu.make_async_copy(src.at[pl.ds(src_row, 1), :], dst.at[pl.ds(dst_row, 1), :], sem)


def _dispatch_kernel(dest_ref, h_ref, xs_ref, sem, *, td):
    def issue(j, carry):
        for kk in range(TOP_K):
            _row_copy(h_ref, j, xs_ref, dest_ref[0, kk * td + j], sem).start()
        return carry

    lax.fori_loop(0, td, issue, 0)
    for _ in range(TOP_K):
        pltpu.make_async_copy(h_ref, xs_ref.at[pl.ds(0, td), :], sem).wait()


def _dispatch(h2, dest_tiles, td):
    T, D = h2.shape
    nt = T // td
    return pl.pallas_call(
        functools.partial(_dispatch_kernel, td=td),
        grid=(nt,),
        in_specs=[pl.BlockSpec((None, 1, TOP_K * td), lambda i: (i, 0, 0),
                               memory_space=pltpu.SMEM),
                  pl.BlockSpec((td, D), lambda i: (i, 0))],
        out_specs=pl.BlockSpec(memory_space=pl.ANY),
        out_shape=jax.ShapeDtypeStruct((T * TOP_K, D), h2.dtype),
        scratch_shapes=[pltpu.SemaphoreType.DMA],
        compiler_params=_cparams(("arbitrary",)),
        name="dispatch",
    )(dest_tiles, h2)


def _gmm_kernel(blk_ref, exp_ref, lo_ref, hi_ref, first_ref, x_ref, w1_ref, w3_ref, w2_ref,
                o_ref, *, bm):
    it = pl.program_id(0)
    rows = lax.broadcasted_iota(jnp.int32, (bm, 1), 0)
    valid = jnp.logical_and(rows >= lo_ref[it], rows < hi_ref[it])
    x = jnp.where(valid, x_ref[...], 0.0).astype(BF16)
    h1 = jnp.dot(x, w1_ref[...], preferred_element_type=F32)
    h3 = jnp.dot(x, w3_ref[...], preferred_element_type=F32)
    hb = (h1 * jax.nn.sigmoid(h1) * h3).astype(BF16)
    y = jnp.dot(hb, w2_ref[...], preferred_element_type=F32)

    @pl.when(first_ref[it] == 1)
    def _():
        o_ref[...] = y

    @pl.when(first_ref[it] == 0)
    def _():
        o_ref[...] += y


def _gmm(xs, items, w1_bf, w3_bf, w2_bf, bm):
    A, D = xs.shape
    F = w1_bf.shape[2]
    n_items = items[0].shape[0]
    grid_spec = pltpu.PrefetchScalarGridSpec(
        num_scalar_prefetch=5,
        grid=(n_items,),
        in_specs=[pl.BlockSpec((bm, D), lambda i, blk, ex, lo, hi, fi: (blk[i], 0)),
                  pl.BlockSpec((None, D, F), lambda i, blk, ex, lo, hi, fi: (ex[i], 0, 0)),
                  pl.BlockSpec((None, D, F), lambda i, blk, ex, lo, hi, fi: (ex[i], 0, 0)),
                  pl.BlockSpec((None, F, D), lambda i, blk, ex, lo, hi, fi: (ex[i], 0, 0))],
        out_specs=pl.BlockSpec((bm, D), lambda i, blk, ex, lo, hi, fi: (blk[i], 0)),
    )
    return pl.pallas_call(
        functools.partial(_gmm_kernel, bm=bm),
        grid_spec=grid_spec,
        out_shape=jax.ShapeDtypeStruct((A, D), F32),
        compiler_params=_cparams(("arbitrary",)),
        name="gmm",
    )(*items, xs, w1_bf, w3_bf, w2_bf)


def _work_items(counts, bm, n_blocks):
    n_items = n_blocks + N_EXPERTS - 1
    ends = jnp.cumsum(counts)
    starts = ends - counts
    nb = jnp.where(counts > 0, (ends - 1) // bm - starts // bm + 1, 0)
    item_end = jnp.cumsum(nb)
    item_start = item_end - nb
    n_real = item_end[-1]
    i = jnp.arange(n_items, dtype=jnp.int32)
    e = jnp.minimum(jnp.sum(item_end[None, :] <= i[:, None], axis=1), N_EXPERTS - 1).astype(jnp.int32)
    blk = starts[e] // bm + (i - item_start[e])
    lo = jnp.clip(starts[e] - blk * bm, 0, bm)
    hi = jnp.clip(ends[e] - blk * bm, 0, bm)
    real = i < n_real
    blk = jnp.where(real, blk, n_blocks - 1).astype(jnp.int32)
    lo = jnp.where(real, lo, 0).astype(jnp.int32)
    hi = jnp.where(real, hi, 0).astype(jnp.int32)
    first = jnp.concatenate([jnp.ones((1,), jnp.int32),
                             (blk[1:] != blk[:-1]).astype(jnp.int32)])
    return blk, e, lo, hi, first


def _combine_kernel(dest_ref, ys_ref, w_ref, x1_ref, h_ref, w1_ref, w3_ref, w2_ref,
                    gt2_ref, g_ref, o_ref, buf, sem, *, tc):
    def issue(j, carry):
        for kk in range(TOP_K):
            _row_copy(ys_ref, dest_ref[0, kk * tc + j], buf.at[kk], j, sem).start()
        return carry

    lax.fori_loop(0, tc, issue, 0)

    hb = h_ref[...].astype(BF16)
    h1 = jnp.dot(hb, w1_ref[...], preferred_element_type=F32)
    h3 = jnp.dot(hb, w3_ref[...], preferred_element_type=F32)
    y = jnp.dot((h1 * jax.nn.sigmoid(h1) * h3).astype(BF16), w2_ref[...],
                preferred_element_type=F32)

    for kk in range(TOP_K):
        pltpu.make_async_copy(ys_ref.at[pl.ds(0, tc), :], buf.at[kk], sem).wait()
    moe = w_ref[:, 0:1] * buf[0]
    for kk in range(1, TOP_K):
        moe = moe + w_ref[:, kk:kk + 1] * buf[kk]
    o_ref[...] = x1_ref[...] + gt2_ref[...] * _rms(moe + y, g_ref[...])


def _combine(ys, dest_tiles, w_tok, x1, h2, w1s, w3s, w2s, mod4, g_post, S, tc):
    T, D = x1.shape
    F = w1s.shape[1]
    per_b = S // tc
    row = lambda: pl.BlockSpec((tc, D), lambda i: (i, 0))
    return pl.pallas_call(
        functools.partial(_combine_kernel, tc=tc),
        grid=(T // tc,),
        in_specs=[pl.BlockSpec((None, 1, TOP_K * tc), lambda i: (i, 0, 0),
                               memory_space=pltpu.SMEM),
                  pl.BlockSpec(memory_space=pl.ANY),
                  pl.BlockSpec((tc, TOP_K), lambda i: (i, 0)),
                  row(), row(),
                  pl.BlockSpec((D, F), lambda i: (0, 0)),
                  pl.BlockSpec((D, F), lambda i: (0, 0)),
                  pl.BlockSpec((F, D), lambda i: (0, 0)),
                  pl.BlockSpec((None, None, 1, D), lambda i: (i // per_b, 5, 0, 0)),
                  pl.BlockSpec((1, D), lambda i: (0, 0))],
        out_specs=row(),
        out_shape=jax.ShapeDtypeStruct((T, D), F32),
        scratch_shapes=[pltpu.VMEM((TOP_K, tc, D), F32), pltpu.SemaphoreType.DMA],
        compiler_params=_cparams(("arbitrary",)),
        name="combine",
    )(dest_tiles, ys, w_tok, x1, h2, w1s, w3s, w2s, mod4, g_post.reshape(1, D))


def _lambda_init(layer):
    return 0.8 - 0.6 * math.exp(-0.3 * layer)


def kernel(x, c, w_ada, b_ada, g_pre_mix, w_in, conv_w, conv_b, lru_wa, lru_ba, lru_wx, lru_bx,
           lru_lambda, lam_q1, lam_k1, lam_q2, lam_k2, g_subln, w_proj_rnn, w_proj_att, w_out,
           g_post_mix, g_pre_ffn, w_router, e_bias, w1_e, w3_e, w2_e, w1_s, w3_s, w2_s,
           g_post_ffn):
    B, S, D = x.shape
    T = B * S
    depth = w_ada.shape[0]
    slopes = jnp.exp2(-8.0 * jnp.arange(1, N_HEADS + 1, dtype=F32) / N_HEADS)
    tt = _tile(S, 256)
    bm = _tile(T * TOP_K, 256)
    n_blocks = T * TOP_K // bm

    x2 = x.reshape(T, D)
    for l in range(depth):
        lam_init = _lambda_init(l)
        mod4 = _ada(c, w_ada[l], b_ada[l]).reshape(B, 6, 1, D)

        proj = _inproj(x2, g_pre_mix[l], mod4, w_in[l].astype(BF16), S)
        yr = _lru(proj, conv_w[l], conv_b[l], lru_wa[l].astype(BF16), lru_ba[l],
                  lru_wx[l].astype(BF16), lru_bx[l], lru_lambda[l], B, S)
        lamv = jnp.stack([lam_q1[l], lam_k1[l], lam_q2[l], lam_k2[l]])
        ao = _attn(proj, slopes, lamv, g_subln[l], B, S, lam_init)
        x1, h2 = _mixout(x2, yr, ao, proj, w_proj_rnn[l].astype(BF16),
                         w_proj_att[l].astype(BF16), w_out[l].astype(BF16),
                         g_post_mix[l], g_pre_ffn[l], mod4, S)

        top_e, top_w, rank, cnt = _router(h2, w_router[l].T, e_bias[l])
        counts = cnt[:, 0].astype(jnp.int32)
        offs = jnp.cumsum(counts) - counts
        eid = jnp.arange(N_EXPERTS, dtype=jnp.int32)
        dest = rank + jnp.sum(jnp.where(top_e[..., None] == eid, offs, 0), axis=-1)
        dest_tiles = dest.reshape(TOP_K, T // tt, tt).transpose(1, 0, 2).reshape(T // tt, 1,
                                                                                  TOP_K * tt)
        xs = _dispatch(h2, dest_tiles, tt)
        items = _work_items(counts, bm, n_blocks)
        ys = _gmm(xs, items, w1_e[l].astype(BF16), w3_e[l].astype(BF16), w2_e[l].astype(BF16), bm)
        x2 = _combine(ys, dest_tiles, top_w.T, x1, h2, w1_s[l].astype(BF16),
                      w3_s[l].astype(BF16), w2_s[l].astype(BF16), mod4, g_post_ffn[l], S, tt)
    return x2.reshape(B, S, D)
```

```python
import functools
import math

import jax
import jax.numpy as jnp
from jax import lax
from jax.experimental import pallas as pl
from jax.experimental.pallas import tpu as pltpu

F32 = jnp.float32
BF16 = jnp.bfloat16

EPS = 1e-6
N_HEADS = 8
HEAD_DIM = 64
V_DIM = 2 * HEAD_DIM
LRU_BLOCKS = 8
CONV_W = 4
LRU_C = 8.0
N_EXPERTS = 64
TOP_K = 8
N_GROUPS = 8
GROUP_SIZE = N_EXPERTS // N_GROUPS
TOPK_GROUPS = 4
ROUTED_SCALE = 2.5

LANES = 128
SUBLANES = 8
VMEM_LIMIT = 48 * 1024 * 1024


def _cparams(sem):
    return pltpu.CompilerParams(dimension_semantics=sem, vmem_limit_bytes=VMEM_LIMIT)


def _tile(n, pref):
    t = min(n, pref)
    while n % t:
        t //= 2
    return t


def _rms(x, g):
    return x * lax.rsqrt(jnp.mean(x * x, axis=-1, keepdims=True) + EPS) * g


def _ada_kernel(c_ref, w_ref, b_ref, o_ref):
    c = c_ref[...]
    cond = c * jax.nn.sigmoid(c)
    o_ref[...] = jnp.dot(cond, w_ref[...], preferred_element_type=F32) + b_ref[...]


def _ada(c, w, b):
    B, D = c.shape
    N = w.shape[1]
    tn = _tile(N, 1024)
    return pl.pallas_call(
        _ada_kernel,
        grid=(N // tn,),
        in_specs=[pl.BlockSpec((B, D), lambda j: (0, 0)),
                  pl.BlockSpec((D, tn), lambda j: (0, j)),
                  pl.BlockSpec((1, tn), lambda j: (0, j))],
        out_specs=pl.BlockSpec((B, tn), lambda j: (0, j)),
        out_shape=jax.ShapeDtypeStruct((B, N), F32),
        compiler_params=_cparams(("parallel",)),
        name="ada",
    )(c, w, b.reshape(1, N))


LOG2E = 1.4426950408889634
Q_COL_BLOCK = 2
Q_PRESCALE = HEAD_DIM ** -0.5 * LOG2E


def _inproj_kernel(x_ref, g_ref, sh_ref, sc_ref, w_ref, o_ref, h_scr):
    @pl.when(pl.program_id(1) == 0)
    def _():
        h = _rms(x_ref[...], g_ref[...]) * (1.0 + sc_ref[...]) + sh_ref[...]
        h_scr[...] = h.astype(BF16)

    r = jnp.dot(h_scr[...], w_ref[...], preferred_element_type=F32)
    r = r * jnp.where(pl.program_id(1) == Q_COL_BLOCK, Q_PRESCALE, 1.0)
    o_ref[...] = r.astype(o_ref.dtype)


def _inproj(x2, g, mod4, w_bf, S):
    T, D = x2.shape
    N = w_bf.shape[1]
    tm = _tile(S, 1024)
    tn = 1024
    per_b = S // tm
    return pl.pallas_call(
        _inproj_kernel,
        grid=(T // tm, N // tn),
        in_specs=[pl.BlockSpec((tm, D), lambda i, j: (i, 0)),
                  pl.BlockSpec((1, D), lambda i, j: (0, 0)),
                  pl.BlockSpec((None, None, 1, D), lambda i, j: (i // per_b, 0, 0, 0)),
                  pl.BlockSpec((None, None, 1, D), lambda i, j: (i // per_b, 1, 0, 0)),
                  pl.BlockSpec((D, tn), lambda i, j: (0, j))],
        out_specs=pl.BlockSpec((tm, tn), lambda i, j: (i, j)),
        out_shape=jax.ShapeDtypeStruct((T, N), BF16),
        scratch_shapes=[pltpu.VMEM((tm, D), BF16)],
        compiler_params=_cparams(("parallel", "arbitrary")),
        name="inproj",
    )(x2, g.reshape(1, D), mod4, mod4, w_bf)


def _lru_kernel(xr_ref, gr_ref, cw_ref, cb_ref, wa_ref, ba_ref, wx_ref, bx_ref, lam_ref,
                o_ref, xc_scr, prev_scr, h_scr, *, ts):
    s = pl.program_id(1)

    @pl.when(s == 0)
    def _():
        prev_scr[...] = jnp.zeros_like(prev_scr)
        h_scr[...] = jnp.zeros_like(h_scr)

    x = xr_ref[...].astype(F32)
    prev = prev_scr[...]
    row8 = lax.broadcasted_iota(jnp.int32, (SUBLANES, 1), 0)
    acc = cb_ref[...] + cw_ref[CONV_W - 1:CONV_W, :] * x
    xc_scr[...] = acc
    top = cb_ref[...] + cw_ref[CONV_W - 1:CONV_W, :] * x[0:SUBLANES, :]
    for j in range(1, CONV_W):
        wj = cw_ref[CONV_W - 1 - j:CONV_W - j, :]
        rj = pltpu.roll(x, j, axis=0)
        xc_scr[...] += wj * rj
        pj = pltpu.roll(prev, j, axis=0)
        top += wj * jnp.where(row8 < j, pj, rj[0:SUBLANES, :])
    xc_scr[0:SUBLANES, :] = top
    prev_scr[...] = x[ts - SUBLANES:ts, :]

    row = lax.broadcasted_iota(jnp.int32, (ts, 1), 0)
    is_first = jnp.logical_and(row == 0, s == 0)
    for n in range(LRU_BLOCKS):
        cols = slice(n * LANES, (n + 1) * LANES)
        xc = xc_scr[:, cols]
        xb = xc.astype(BF16)
        r = jax.nn.sigmoid(jnp.dot(xb, wa_ref[n], preferred_element_type=F32) + ba_ref[:, cols])
        i = jax.nn.sigmoid(jnp.dot(xb, wx_ref[n], preferred_element_type=F32) + bx_ref[:, cols])
        lam = lam_ref[:, cols]
        softplus_neg = jnp.maximum(-lam, 0.0) + jnp.log1p(jnp.exp(-jnp.abs(lam)))
        log_a = (-LRU_C * softplus_neg) * r
        a = jnp.exp(log_a)
        mult = jnp.sqrt(1.0 - a * a)
        mult = jnp.where(is_first, 1.0, mult)
        u = mult * (i * xc)
        d = 1
        while d < ts:
            keep = row >= d
            a_sh = jnp.where(keep, pltpu.roll(a, d, axis=0), 1.0)
            u_sh = jnp.where(keep, pltpu.roll(u, d, axis=0), 0.0)
            u = u + a * u_sh
            a = a * a_sh
            d *= 2
        h = u + a * h_scr[:, cols]
        h_scr[:, cols] = h[ts - 1:ts, :]
        gate = jax.nn.gelu(gr_ref[:, cols].astype(F32))
        o_ref[:, cols] = (h * gate).astype(o_ref.dtype)


def _lru(proj, conv_w, conv_b, wa_bf, ba, wx_bf, bx, lam, B, S):
    C = conv_w.shape[1]
    ts = _tile(S, 256)
    ns = S // ts
    vec = lambda: pl.BlockSpec((1, C), lambda b, s: (0, 0))
    blk = lambda: pl.BlockSpec((LRU_BLOCKS, LANES, LANES), lambda b, s: (0, 0, 0))
    return pl.pallas_call(
        functools.partial(_lru_kernel, ts=ts),
        grid=(B, ns),
        in_specs=[pl.BlockSpec((ts, C), lambda b, s: (b * ns + s, 0)),
                  pl.BlockSpec((ts, C), lambda b, s: (b * ns + s, 1)),
                  pl.BlockSpec((CONV_W, C), lambda b, s: (0, 0)),
                  vec(), blk(), vec(), blk(), vec(), vec()],
        out_specs=pl.BlockSpec((ts, C), lambda b, s: (b * ns + s, 0)),
        out_shape=jax.ShapeDtypeStruct((B * S, C), BF16),
        scratch_shapes=[pltpu.VMEM((ts, C), F32), pltpu.VMEM((SUBLANES, C), F32),
                        pltpu.VMEM((1, C), F32)],
        compiler_params=_cparams(("parallel", "arbitrary")),
        name="lru",
    )(proj, proj, conv_w, conv_b.reshape(1, C), wa_bf, ba.reshape(1, C), wx_bf, bx.reshape(1, C),
      lam.reshape(1, C))


def _attn_kernel(slope_ref, lamv_ref, gsub_ref, q_ref, k_ref, v_ref, o_ref,
                 vt_scr, s_scr, p_scr, b_scr, a_scr, m_scr, l_scr, acc_scr, *, tq, lam_init):
    hd = pl.program_id(1)
    qi = pl.program_id(2)
    tk = tq
    nk = vt_scr.shape[0]

    @pl.when(qi == 0)
    def _():
        for j in range(nk):
            vt_scr[j] = v_ref[j * tk:(j + 1) * tk, :].astype(F32).T.astype(BF16)

    slope2 = slope_ref[hd] * LOG2E
    lane = lax.broadcasted_iota(jnp.int32, (1, V_DIM), 1)
    q = q_ref[...]
    zero = jnp.zeros_like(q)
    qm = (jnp.where(lane < HEAD_DIM, q, zero), jnp.where(lane >= HEAD_DIM, q, zero))
    m_scr[...] = jnp.full_like(m_scr, -jnp.inf)
    l_scr[...] = jnp.zeros_like(l_scr)
    acc_scr[...] = jnp.zeros_like(acc_scr)
    key_iota = lax.broadcasted_iota(jnp.int32, (tk, LANES), 0)
    row8 = lax.broadcasted_iota(jnp.int32, (SUBLANES, LANES), 0)
    row16 = lax.broadcasted_iota(jnp.int32, (2 * SUBLANES, LANES), 0)
    col = lax.broadcasted_iota(jnp.int32, (1, LANES), 1)
    neg_inf = jnp.float32(-jnp.inf)

    def tree(op, parts):
        parts = [p for p in parts if p is not None]
        while len(parts) > 1:
            parts = [op(parts[i], parts[i + 1]) if i + 1 < len(parts) else parts[i]
                     for i in range(0, len(parts), 2)]
        return parts[0]

    def block(ki, masked):
        start = pl.multiple_of(ki * tk, tk)
        k = k_ref[pl.ds(start, tk), :]
        b_scr[...] = slope2 * (start + key_iota).astype(F32)
        for mp in range(2):
            s_scr[mp] = lax.dot_general(k, qm[mp], (((1,), (1,)), ((), ())),
                                        preferred_element_type=F32)
        for mp in range(2):
            for c in range(tq // LANES):
                cols = slice(c * LANES, (c + 1) * LANES)
                lo_col, hi_col = c * LANES, (c + 1) * LANES - 1

                def scores(r0, n, riota):
                    t = s_scr[mp, r0:r0 + n, cols] + b_scr[r0:r0 + n, :]
                    if masked and r0 + n - 1 > lo_col:
                        t = jnp.where(riota + r0 <= col + lo_col, t, neg_inf)
                    return t

                accs = [None] * 4
                for i in range(tk // SUBLANES):
                    r0 = i * SUBLANES
                    if masked and r0 > hi_col:
                        continue
                    t = scores(r0, SUBLANES, row8)
                    accs[i % 4] = t if accs[i % 4] is None else jnp.maximum(accs[i % 4], t)
                mx = jnp.max(tree(jnp.maximum, accs), axis=0, keepdims=True)
                m_old = m_scr[mp, :, cols]
                m_new = jnp.maximum(m_old, mx)
                alpha = jnp.exp2(m_old - m_new)
                sums = [None] * 4
                for i in range(tk // (2 * SUBLANES)):
                    r0 = i * 2 * SUBLANES
                    if masked and r0 > hi_col:
                        p_scr[mp, r0:r0 + 2 * SUBLANES, cols] = jnp.zeros((2 * SUBLANES, LANES), BF16)
                        continue
                    p = jnp.exp2(scores(r0, 2 * SUBLANES, row16) - m_new)
                    sums[i % 4] = p if sums[i % 4] is None else sums[i % 4] + p
                    p_scr[mp, r0:r0 + 2 * SUBLANES, cols] = p.astype(BF16)
                psum = jnp.sum(tree(lambda a, b: a + b, sums), axis=0, keepdims=True)
                l_scr[mp, :, cols] = alpha * l_scr[mp, :, cols] + psum
                m_scr[mp, :, cols] = m_new
                a_scr[mp, :, cols] = alpha
        for mp in range(2):
            pv = jnp.dot(vt_scr[ki], p_scr[mp], preferred_element_type=F32)
            acc_scr[mp] = a_scr[mp] * acc_scr[mp] + pv

    def full_block(ki, carry):
        block(ki, False)
        return carry

    lax.fori_loop(0, qi, full_block, 0)
    block(qi, True)

    lv = lamv_ref[...]
    lam = (jnp.exp(jnp.sum(lv[0:1, :] * lv[1:2, :], axis=-1, keepdims=True))
           - jnp.exp(jnp.sum(lv[2:3, :] * lv[3:4, :], axis=-1, keepdims=True)) + lam_init)
    o_t = acc_scr[0] * (1.0 / l_scr[0]) - lam * (acc_scr[1] * (1.0 / l_scr[1]))
    o_t = o_t * lax.rsqrt(jnp.mean(o_t * o_t, axis=0, keepdims=True) + EPS) * gsub_ref[...]
    o_ref[...] = (o_t * (1.0 - lam_init)).T.astype(o_ref.dtype)


def _attn(proj, slopes, lamv, g_subln, B, S, lam_init):
    tq = _tile(S, 512)
    nq = S // tq
    qc, kc, vc = 2 * 8, 3 * 8, 4 * 8
    grid_spec = pltpu.PrefetchScalarGridSpec(
        num_scalar_prefetch=1,
        grid=(B, N_HEADS, nq),
        in_specs=[pl.BlockSpec((4, HEAD_DIM), lambda b, h, qi, sl: (0, 0)),
                  pl.BlockSpec((V_DIM, 1), lambda b, h, qi, sl: (0, 0)),
                  pl.BlockSpec((tq, V_DIM), lambda b, h, qi, sl: (b * nq + qi, qc + h)),
                  pl.BlockSpec((S, V_DIM), lambda b, h, qi, sl: (b, kc + h)),
                  pl.BlockSpec((S, V_DIM), lambda b, h, qi, sl: (b, vc + h))],
        out_specs=pl.BlockSpec((tq, V_DIM), lambda b, h, qi, sl: (b * nq + qi, h)),
        scratch_shapes=[pltpu.VMEM((nq, V_DIM, tq), BF16),
                        pltpu.VMEM((2, tq, tq), F32), pltpu.VMEM((2, tq, tq), BF16),
                        pltpu.VMEM((tq, LANES), F32),
                        pltpu.VMEM((2, 1, tq), F32), pltpu.VMEM((2, 1, tq), F32),
                        pltpu.VMEM((2, 1, tq), F32), pltpu.VMEM((2, V_DIM, tq), F32)],
    )
    return pl.pallas_call(
        functools.partial(_attn_kernel, tq=tq, lam_init=lam_init),
        grid_spec=grid_spec,
        out_shape=jax.ShapeDtypeStruct((B * S, N_HEADS * V_DIM), BF16),
        compiler_params=_cparams(("parallel", "parallel", "arbitrary")),
        name="attn",
    )(slopes, lamv, g_subln.reshape(V_DIM, 1), proj, proj, proj)


def _pack_rows(x):
    half = x.shape[1] // 2
    lo = lax.bitcast_convert_type(x[:, :half].astype(BF16).astype(F32), jnp.uint32)
    hi = lax.bitcast_convert_type(x[:, half:].astype(BF16).astype(F32), jnp.uint32)
    return (hi & jnp.uint32(0xFFFF0000)) | (lo >> 16)


def _unpack_rows(u):
    lo = lax.bitcast_convert_type(u << 16, F32)
    hi = lax.bitcast_convert_type(u & jnp.uint32(0xFFFF0000), F32)
    return lo, hi


def _mixout_kernel(x_ref, yr_ref, ao_ref, ga_ref, gb_ref, wr_ref, wa_ref, wo_ref,
                   gpost_ref, gt1_ref, gpre_ref, sh2_ref, sc2_ref, x1_ref, h2_ref, h2p_ref):
    ya = jnp.dot(yr_ref[...], wr_ref[...], preferred_element_type=F32)
    yb = jnp.dot(ao_ref[...], wa_ref[...], preferred_element_type=F32)
    merged = (jax.nn.sigmoid(ga_ref[...].astype(F32)) * ya
              + jax.nn.sigmoid(gb_ref[...].astype(F32)) * yb)
    y = jnp.dot(merged.astype(BF16), wo_ref[...], preferred_element_type=F32)
    x1 = x_ref[...] + gt1_ref[...] * _rms(y, gpost_ref[...])
    x1_ref[...] = x1
    h2 = _rms(x1, gpre_ref[...]) * (1.0 + sc2_ref[...]) + sh2_ref[...]
    h2_ref[...] = h2
    h2p_ref[...] = _pack_rows(h2)


def _mixout(x2, yr, ao, proj, wr_bf, wa_bf, wo_bf, g_post, g_pre, mod4, S):
    T, D = x2.shape
    tm = _tile(S, 512)
    per_b = S // tm
    gac, gbc = 5, 6
    row = lambda: pl.BlockSpec((tm, D), lambda i: (i, 0))
    wsp = lambda: pl.BlockSpec((D, D), lambda i: (0, 0))
    vec = lambda: pl.BlockSpec((1, D), lambda i: (0, 0))
    modv = lambda j: pl.BlockSpec((None, None, 1, D), lambda i: (i // per_b, j, 0, 0))
    return pl.pallas_call(
        _mixout_kernel,
        grid=(T // tm,),
        in_specs=[row(), row(), row(),
                  pl.BlockSpec((tm, D), lambda i: (i, gac)),
                  pl.BlockSpec((tm, D), lambda i: (i, gbc)),
                  wsp(), wsp(), wsp(), vec(), modv(2), vec(), modv(3), modv(4)],
        out_specs=[row(), row(), pl.BlockSpec((tm, D // 2), lambda i: (i, 0))],
        out_shape=[jax.ShapeDtypeStruct((T, D), F32), jax.ShapeDtypeStruct((T, D), F32),
                   jax.ShapeDtypeStruct((T, D // 2), jnp.uint32)],
        compiler_params=_cparams(("parallel",)),
        name="mixout",
    )(x2, yr, ao, proj, proj, wr_bf, wa_bf, wo_bf, g_post.reshape(1, D), mod4,
      g_pre.reshape(1, D), mod4, mod4)


def _first_argmax(vals, ids, sentinel):
    m = jnp.max(vals, axis=0, keepdims=True)
    idx = jnp.min(jnp.where(vals == m, ids, sentinel), axis=0, keepdims=True)
    return m, idx


def _router_kernel(h_ref, wr_ref, eb_ref, e_ref, w_ref, r_ref, cnt_ref, cnt_scr, *, tm):
    step = pl.program_id(0)

    @pl.when(step == 0)
    def _():
        cnt_scr[...] = jnp.zeros_like(cnt_scr)

    logits = lax.dot_general(wr_ref[...], h_ref[...], (((1,), (1,)), ((), ())),
                             preferred_element_type=F32, precision=lax.Precision.HIGHEST)
    scores = jax.nn.sigmoid(logits)
    choice = scores + eb_ref[...]
    i8 = lax.broadcasted_iota(jnp.int32, (GROUP_SIZE, tm), 0)
    neg_inf = jnp.float32(-jnp.inf)

    slabs = [choice[g * GROUP_SIZE:(g + 1) * GROUP_SIZE, :] for g in range(N_GROUPS)]
    sc_slabs = [scores[g * GROUP_SIZE:(g + 1) * GROUP_SIZE, :] for g in range(N_GROUPS)]

    gs = jnp.zeros((N_GROUPS, tm), F32)
    for g in range(N_GROUPS):
        m1, idx1 = _first_argmax(slabs[g], i8, GROUP_SIZE)
        m2 = jnp.max(jnp.where(i8 == idx1, neg_inf, slabs[g]), axis=0, keepdims=True)
        gs = jnp.where(i8 == g, m1 + m2, gs)

    sel = jnp.zeros((N_GROUPS, tm), jnp.int32)
    cur = gs
    for _ in range(TOPK_GROUPS):
        _, idx = _first_argmax(cur, i8, N_GROUPS)
        hit = i8 == idx
        sel = jnp.where(hit, 1, sel)
        cur = jnp.where(hit, neg_inf, cur)

    masked = [jnp.where(sel[g:g + 1, :] > 0, slabs[g], neg_inf) for g in range(N_GROUPS)]
    ids = [i8 + g * GROUP_SIZE for g in range(N_GROUPS)]
    onehot = [jnp.zeros((GROUP_SIZE, tm), F32) for _ in range(N_GROUPS)]
    picks = []
    wts = []
    for _ in range(TOP_K):
        m = functools.reduce(jnp.maximum,
                             [jnp.max(c, axis=0, keepdims=True) for c in masked])
        idx = functools.reduce(
            jnp.minimum,
            [jnp.min(jnp.where(c == m, i, N_EXPERTS), axis=0, keepdims=True)
             for c, i in zip(masked, ids)])
        w = jnp.zeros((1, tm), F32)
        for g in range(N_GROUPS):
            hit = ids[g] == idx
            w = w + jnp.sum(jnp.where(hit, sc_slabs[g], 0.0), axis=0, keepdims=True)
            masked[g] = jnp.where(hit, neg_inf, masked[g])
            onehot[g] = jnp.where(hit, 1.0, onehot[g])
        picks.append(idx)
        wts.append(w)

    wsum = functools.reduce(lambda a, b: a + b, wts)
    norm = ROUTED_SCALE / (wsum + 1e-20)

    t_row = lax.broadcasted_iota(jnp.int32, (tm, tm), 0)
    t_col = lax.broadcasted_iota(jnp.int32, (tm, tm), 1)
    before = jnp.where(t_row < t_col, 1.0, 0.0).astype(BF16)
    cum = [jnp.dot(onehot[g].astype(BF16), before, preferred_element_type=F32)
           + cnt_scr[g * GROUP_SIZE:(g + 1) * GROUP_SIZE, :] for g in range(N_GROUPS)]

    for kk in range(TOP_K):
        rank = jnp.zeros((1, tm), F32)
        for g in range(N_GROUPS):
            rank = rank + jnp.sum(jnp.where(ids[g] == picks[kk], cum[g], 0.0),
                                  axis=0, keepdims=True)
        e_ref[kk:kk + 1, :] = picks[kk]
        w_ref[kk:kk + 1, :] = wts[kk] * norm
        r_ref[kk:kk + 1, :] = rank.astype(jnp.int32)

    for g in range(N_GROUPS):
        rows = slice(g * GROUP_SIZE, (g + 1) * GROUP_SIZE)
        cnt_scr[rows, :] = cnt_scr[rows, :] + jnp.sum(onehot[g], axis=1, keepdims=True)
    cnt_ref[...] = jnp.broadcast_to(cnt_scr[...], cnt_ref.shape)


def _router(h2, w_router_t, e_bias):
    T, D = h2.shape
    tm = _tile(T, 512)
    return pl.pallas_call(
        functools.partial(_router_kernel, tm=tm),
        grid=(T // tm,),
        in_specs=[pl.BlockSpec((tm, D), lambda i: (i, 0)),
                  pl.BlockSpec((N_EXPERTS, D), lambda i: (0, 0)),
                  pl.BlockSpec((N_EXPERTS, 1), lambda i: (0, 0))],
        out_specs=[pl.BlockSpec((TOP_K, tm), lambda i: (0, i)),
                   pl.BlockSpec((TOP_K, tm), lambda i: (0, i)),
                   pl.BlockSpec((TOP_K, tm), lambda i: (0, i)),
                   pl.BlockSpec((N_EXPERTS, LANES), lambda i: (0, 0))],
        out_shape=[jax.ShapeDtypeStruct((TOP_K, T), jnp.int32),
                   jax.ShapeDtypeStruct((TOP_K, T), F32),
                   jax.ShapeDtypeStruct((TOP_K, T), jnp.int32),
                   jax.ShapeDtypeStruct((N_EXPERTS, LANES), F32)],
        scratch_shapes=[pltpu.VMEM((N_EXPERTS, 1), F32)],
        compiler_params=_cparams(("arbitrary",)),
        name="router",
    )(h2, w_router_t, e_bias.reshape(N_EXPERTS, 1))


def _slots_kernel(offs_ref, e_ref, r_ref, d_ref):
    e = e_ref[...]
    d = r_ref[...]
    for ex in range(N_EXPERTS):
        d = d + jnp.where(e == ex, offs_ref[ex], 0)
    d_ref[...] = d


def _slots(offs, top_e, rank):
    K, T = top_e.shape
    tm = _tile(T, 4096)
    grid_spec = pltpu.PrefetchScalarGridSpec(
        num_scalar_prefetch=1,
        grid=(T // tm,),
        in_specs=[pl.BlockSpec((K, tm), lambda i, o: (0, i)),
                  pl.BlockSpec((K, tm), lambda i, o: (0, i))],
        out_specs=pl.BlockSpec((K, tm), lambda i, o: (0, i)),
    )
    return pl.pallas_call(
        _slots_kernel,
        grid_spec=grid_spec,
        out_shape=jax.ShapeDtypeStruct((K, T), jnp.int32),
        compiler_params=_cparams(("parallel",)),
        name="slots",
    )(offs, top_e, rank)


def _row_copy(src, src_row, dst, dst_row, sem):
    return pltpu.make_async_copy(src.at[pl.ds(src_row, 1), :], dst.at[pl.ds(dst_row, 1), :], sem)


def _dispatch_kernel(dest_ref, h_ref, xs_ref, sem, *, td):
    def issue(j, carry):
        for kk in range(TOP_K):
            _row_copy(h_ref, j, xs_ref, dest_ref[0, kk * td + j], sem).start()
        return carry

    lax.fori_loop(0, td, issue, 0)
    for _ in range(TOP_K):
        pltpu.make_async_copy(h_ref, xs_ref.at[pl.ds(0, td), :], sem).wait()


def _dispatch(h2, dest_tiles, td):
    T, D = h2.shape
    nt = T // td
    return pl.pallas_call(
        functools.partial(_dispatch_kernel, td=td),
        grid=(nt,),
        in_specs=[pl.BlockSpec((None, 1, TOP_K * td), lambda i: (i, 0, 0),
                               memory_space=pltpu.SMEM),
                  pl.BlockSpec((td, D), lambda i: (i, 0))],
        out_specs=pl.BlockSpec(memory_space=pl.ANY),
        out_shape=jax.ShapeDtypeStruct((T * TOP_K, D), h2.dtype),
        scratch_shapes=[pltpu.SemaphoreType.DMA],
        compiler_params=_cparams(("arbitrary",)),
        name="dispatch",
    )(dest_tiles, h2)


def _gmm_kernel(blk_ref, exp_ref, lo_ref, hi_ref, first_ref, x_ref, w1_ref, w3_ref, w2_ref,
                o_ref, *, bm):
    it = pl.program_id(0)
    rows = lax.broadcasted_iota(jnp.int32, (bm, 1), 0)
    valid = jnp.logical_and(rows >= lo_ref[it], rows < hi_ref[it])
    x_lo, x_hi = _unpack_rows(jnp.where(valid, x_ref[...], jnp.uint32(0)))
    x_lo = x_lo.astype(BF16)
    x_hi = x_hi.astype(BF16)
    half = x_lo.shape[1]
    h1 = (jnp.dot(x_lo, w1_ref[:half, :], preferred_element_type=F32)
          + jnp.dot(x_hi, w1_ref[half:, :], preferred_element_type=F32))
    h3 = (jnp.dot(x_lo, w3_ref[:half, :], preferred_element_type=F32)
          + jnp.dot(x_hi, w3_ref[half:, :], preferred_element_type=F32))
    hb = (h1 * jax.nn.sigmoid(h1) * h3).astype(BF16)
    y = jnp.dot(hb, w2_ref[...], preferred_element_type=F32)

    @pl.when(first_ref[it] == 1)
    def _():
        o_ref[...] = _pack_rows(y)

    @pl.when(first_ref[it] == 0)
    def _():
        o_lo, o_hi = _unpack_rows(o_ref[...])
        o_ref[...] = _pack_rows(y + jnp.concatenate([o_lo, o_hi], axis=1))


def _gmm(xs, items, w1_bf, w3_bf, w2_bf, bm):
    A = xs.shape[0]
    D = w1_bf.shape[1]
    F = w1_bf.shape[2]
    n_items = items[0].shape[0]
    grid_spec = pltpu.PrefetchScalarGridSpec(
        num_scalar_prefetch=5,
        grid=(n_items,),
        in_specs=[pl.BlockSpec((bm, D // 2), lambda i, blk, ex, lo, hi, fi: (blk[i], 0)),
                  pl.BlockSpec((None, D, F), lambda i, blk, ex, lo, hi, fi: (ex[i], 0, 0)),
                  pl.BlockSpec((None, D, F), lambda i, blk, ex, lo, hi, fi: (ex[i], 0, 0)),
                  pl.BlockSpec((None, F, D), lambda i, blk, ex, lo, hi, fi: (ex[i], 0, 0))],
        out_specs=pl.BlockSpec((bm, D // 2), lambda i, blk, ex, lo, hi, fi: (blk[i], 0)),
    )
    return pl.pallas_call(
        functools.partial(_gmm_kernel, bm=bm),
        grid_spec=grid_spec,
        out_shape=jax.ShapeDtypeStruct((A, D // 2), jnp.uint32),
        compiler_params=_cparams(("arbitrary",)),
        name="gmm",
    )(*items, xs, w1_bf, w3_bf, w2_bf)


def _work_items(counts, bm, n_blocks):
    n_items = n_blocks + N_EXPERTS - 1
    ends = jnp.cumsum(counts)
    starts = ends - counts
    nb = jnp.where(counts > 0, (ends - 1) // bm - starts // bm + 1, 0)
    item_end = jnp.cumsum(nb)
    item_start = item_end - nb
    n_real = item_end[-1]
    i = jnp.arange(n_items, dtype=jnp.int32)
    e = jnp.minimum(jnp.sum(item_end[None, :] <= i[:, None], axis=1), N_EXPERTS - 1).astype(jnp.int32)
    blk = starts[e] // bm + (i - item_start[e])
    lo = jnp.clip(starts[e] - blk * bm, 0, bm)
    hi = jnp.clip(ends[e] - blk * bm, 0, bm)
    real = i < n_real
    blk = jnp.where(real, blk, n_blocks - 1).astype(jnp.int32)
    lo = jnp.where(real, lo, 0).astype(jnp.int32)
    hi = jnp.where(real, hi, 0).astype(jnp.int32)
    first = jnp.concatenate([jnp.ones((1,), jnp.int32),
                             (blk[1:] != blk[:-1]).astype(jnp.int32)])
    return blk, e, lo, hi, first


def _combine_kernel(dest_ref, ys_ref, w_ref, x1_ref, h_ref, w1_ref, w3_ref, w2_ref,
                    gt2_ref, g_ref, o_ref, buf, sem, *, tc):
    def issue(j, carry):
        for kk in range(TOP_K):
            _row_copy(ys_ref, dest_ref[0, kk * tc + j], buf.at[kk], j, sem).start()
        return carry

    lax.fori_loop(0, tc, issue, 0)

    h_lo, h_hi = _unpack_rows(h_ref[...])
    h_lo = h_lo.astype(BF16)
    h_hi = h_hi.astype(BF16)
    half = h_lo.shape[1]
    h1 = (jnp.dot(h_lo, w1_ref[:half, :], preferred_element_type=F32)
          + jnp.dot(h_hi, w1_ref[half:, :], preferred_element_type=F32))
    h3 = (jnp.dot(h_lo, w3_ref[:half, :], preferred_element_type=F32)
          + jnp.dot(h_hi, w3_ref[half:, :], preferred_element_type=F32))
    y = jnp.dot((h1 * jax.nn.sigmoid(h1) * h3).astype(BF16), w2_ref[...],
                preferred_element_type=F32)

    for kk in range(TOP_K):
        pltpu.make_async_copy(ys_ref.at[pl.ds(0, tc), :], buf.at[kk], sem).wait()
    moe_lo = None
    for kk in range(TOP_K):
        e_lo, e_hi = _unpack_rows(buf[kk])
        wk = w_ref[:, kk:kk + 1]
        moe_lo = wk * e_lo if kk == 0 else moe_lo + wk * e_lo
        moe_hi = wk * e_hi if kk == 0 else moe_hi + wk * e_hi
    moe = jnp.concatenate([moe_lo, moe_hi], axis=1)
    o_ref[...] = x1_ref[...] + gt2_ref[...] * _rms(moe + y, g_ref[...])


def _combine(ys, dest_tiles, w_tok, x1, h2, w1s, w3s, w2s, mod4, g_post, S, tc):
    T, D = x1.shape
    F = w1s.shape[1]
    per_b = S // tc
    row = lambda: pl.BlockSpec((tc, D), lambda i: (i, 0))
    return pl.pallas_call(
        functools.partial(_combine_kernel, tc=tc),
        grid=(T // tc,),
        in_specs=[pl.BlockSpec((None, 1, TOP_K * tc), lambda i: (i, 0, 0),
                               memory_space=pltpu.SMEM),
                  pl.BlockSpec(memory_space=pl.ANY),
                  pl.BlockSpec((tc, TOP_K), lambda i: (i, 0)),
                  row(), pl.BlockSpec((tc, D // 2), lambda i: (i, 0)),
                  pl.BlockSpec((D, F), lambda i: (0, 0)),
                  pl.BlockSpec((D, F), lambda i: (0, 0)),
                  pl.BlockSpec((F, D), lambda i: (0, 0)),
                  pl.BlockSpec((None, None, 1, D), lambda i: (i // per_b, 5, 0, 0)),
                  pl.BlockSpec((1, D), lambda i: (0, 0))],
        out_specs=row(),
        out_shape=jax.ShapeDtypeStruct((T, D), F32),
        scratch_shapes=[pltpu.VMEM((TOP_K, tc, D // 2), jnp.uint32), pltpu.SemaphoreType.DMA],
        compiler_params=_cparams(("arbitrary",)),
        name="combine",
    )(dest_tiles, ys, w_tok, x1, h2, w1s, w3s, w2s, mod4, g_post.reshape(1, D))


def _lambda_init(layer):
    return 0.8 - 0.6 * math.exp(-0.3 * layer)


def kernel(x, c, w_ada, b_ada, g_pre_mix, w_in, conv_w, conv_b, lru_wa, lru_ba, lru_wx, lru_bx,
           lru_lambda, lam_q1, lam_k1, lam_q2, lam_k2, g_subln, w_proj_rnn, w_proj_att, w_out,
           g_post_mix, g_pre_ffn, w_router, e_bias, w1_e, w3_e, w2_e, w1_s, w3_s, w2_s,
           g_post_ffn):
    B, S, D = x.shape
    T = B * S
    depth = w_ada.shape[0]
    slopes = jnp.exp2(-8.0 * jnp.arange(1, N_HEADS + 1, dtype=F32) / N_HEADS)
    tt = _tile(S, 256)
    bm = _tile(T * TOP_K, 512)
    n_blocks = T * TOP_K // bm

    x2 = x.reshape(T, D)
    for l in range(depth):
        lam_init = _lambda_init(l)
        mod4 = _ada(c, w_ada[l], b_ada[l]).reshape(B, 6, 1, D)

        proj = _inproj(x2, g_pre_mix[l], mod4, w_in[l].astype(BF16), S)
        yr = _lru(proj, conv_w[l], conv_b[l], lru_wa[l].astype(BF16), lru_ba[l],
                  lru_wx[l].astype(BF16), lru_bx[l], lru_lambda[l], B, S)
        lamv = jnp.stack([lam_q1[l], lam_k1[l], lam_q2[l], lam_k2[l]])
        ao = _attn(proj, slopes, lamv, g_subln[l], B, S, lam_init)
        x1, h2, h2p = _mixout(x2, yr, ao, proj, w_proj_rnn[l].astype(BF16),
                         w_proj_att[l].astype(BF16), w_out[l].astype(BF16),
                         g_post_mix[l], g_pre_ffn[l], mod4, S)

        top_e, top_w, rank, cnt = _router(h2, w_router[l].T, e_bias[l])
        counts = cnt[:, 0].astype(jnp.int32)
        offs = jnp.cumsum(counts) - counts
        dest = _slots(offs, top_e, rank)
        dest_tiles = dest.reshape(TOP_K, T // tt, tt).transpose(1, 0, 2).reshape(T // tt, 1,
                                                                                  TOP_K * tt)
        xs = _dispatch(h2p, dest_tiles, tt)
        items = _work_items(counts, bm, n_blocks)
        ys = _gmm(xs, items, w1_e[l].astype(BF16), w3_e[l].astype(BF16), w2_e[l].astype(BF16), bm)
        x2 = _combine(ys, dest_tiles, top_w.T, x1, h2p, w1_s[l].astype(BF16),
                      w3_s[l].astype(BF16), w2_s[l].astype(BF16), mod4, g_post_ffn[l], S, tt)
    return x2.reshape(B, S, D)
```

```python
import functools
import math

import jax
import jax.numpy as jnp
import numpy as np
from jax import lax
from jax.experimental import pallas as pl
from jax.experimental.pallas import tpu as pltpu

F32 = jnp.float32
BF16 = jnp.bfloat16

EPS = 1e-6
N_HEADS = 8
HEAD_DIM = 64
V_DIM = 2 * HEAD_DIM
LRU_BLOCKS = 8
CONV_W = 4
LRU_C = 8.0
N_EXPERTS = 64
TOP_K = 8
N_GROUPS = 8
GROUP_SIZE = N_EXPERTS // N_GROUPS
TOPK_GROUPS = 4
ROUTED_SCALE = 2.5

LANES = 128
SUBLANES = 8
VMEM_LIMIT = 48 * 1024 * 1024


def _cparams(sem):
    return pltpu.CompilerParams(dimension_semantics=sem, vmem_limit_bytes=VMEM_LIMIT)


def _tile(n, pref):
    t = min(n, pref)
    while n % t:
        t //= 2
    return t


def _rms(x, g):
    return x * lax.rsqrt(jnp.mean(x * x, axis=-1, keepdims=True) + EPS) * g


def _ada_kernel(c_ref, w_ref, b_ref, o_ref):
    c = c_ref[...]
    cond = c * jax.nn.sigmoid(c)
    o_ref[...] = jnp.dot(cond, w_ref[...], preferred_element_type=F32) + b_ref[...]


def _ada(c, w, b):
    B, D = c.shape
    N = w.shape[1]
    tn = _tile(N, 1024)
    return pl.pallas_call(
        _ada_kernel,
        grid=(N // tn,),
        in_specs=[pl.BlockSpec((B, D), lambda j: (0, 0)),
                  pl.BlockSpec((D, tn), lambda j: (0, j)),
                  pl.BlockSpec((1, tn), lambda j: (0, j))],
        out_specs=pl.BlockSpec((B, tn), lambda j: (0, j)),
        out_shape=jax.ShapeDtypeStruct((B, N), F32),
        compiler_params=_cparams(("parallel",)),
        name="ada",
    )(c, w, b.reshape(1, N))


LOG2E = 1.4426950408889634
Q_COL_BLOCK = 2
Q_PRESCALE = HEAD_DIM ** -0.5 * LOG2E


def _inproj_kernel(x_ref, g_ref, sh_ref, sc_ref, w_ref, o_ref, h_scr):
    @pl.when(pl.program_id(1) == 0)
    def _():
        h = _rms(x_ref[...], g_ref[...]) * (1.0 + sc_ref[...]) + sh_ref[...]
        h_scr[...] = h.astype(BF16)

    r = jnp.dot(h_scr[...], w_ref[...], preferred_element_type=F32)
    r = r * jnp.where(pl.program_id(1) == Q_COL_BLOCK, Q_PRESCALE, 1.0)
    o_ref[...] = r.astype(o_ref.dtype)


def _inproj(x2, g, mod4, w_bf, S):
    T, D = x2.shape
    N = w_bf.shape[1]
    tm = _tile(S, 1024)
    tn = 1024
    per_b = S // tm
    return pl.pallas_call(
        _inproj_kernel,
        grid=(T // tm, N // tn),
        in_specs=[pl.BlockSpec((tm, D), lambda i, j: (i, 0)),
                  pl.BlockSpec((1, D), lambda i, j: (0, 0)),
                  pl.BlockSpec((None, None, 1, D), lambda i, j: (i // per_b, 0, 0, 0)),
                  pl.BlockSpec((None, None, 1, D), lambda i, j: (i // per_b, 1, 0, 0)),
                  pl.BlockSpec((D, tn), lambda i, j: (0, j))],
        out_specs=pl.BlockSpec((tm, tn), lambda i, j: (i, j)),
        out_shape=jax.ShapeDtypeStruct((T, N), BF16),
        scratch_shapes=[pltpu.VMEM((tm, D), BF16)],
        compiler_params=_cparams(("parallel", "arbitrary")),
        name="inproj",
    )(x2, g.reshape(1, D), mod4, mod4, w_bf)


def _lru_kernel(xr_ref, gr_ref, cw_ref, cb_ref, wa_ref, ba_ref, wx_ref, bx_ref, lam_ref,
                o_ref, xc_scr, prev_scr, h_scr, *, ts):
    s = pl.program_id(1)

    @pl.when(s == 0)
    def _():
        prev_scr[...] = jnp.zeros_like(prev_scr)
        h_scr[...] = jnp.zeros_like(h_scr)

    x = xr_ref[...].astype(F32)
    prev = prev_scr[...]
    row8 = lax.broadcasted_iota(jnp.int32, (SUBLANES, 1), 0)
    acc = cb_ref[...] + cw_ref[CONV_W - 1:CONV_W, :] * x
    xc_scr[...] = acc
    top = cb_ref[...] + cw_ref[CONV_W - 1:CONV_W, :] * x[0:SUBLANES, :]
    for j in range(1, CONV_W):
        wj = cw_ref[CONV_W - 1 - j:CONV_W - j, :]
        rj = pltpu.roll(x, j, axis=0)
        xc_scr[...] += wj * rj
        pj = pltpu.roll(prev, j, axis=0)
        top += wj * jnp.where(row8 < j, pj, rj[0:SUBLANES, :])
    xc_scr[0:SUBLANES, :] = top
    prev_scr[...] = x[ts - SUBLANES:ts, :]

    row = lax.broadcasted_iota(jnp.int32, (ts, 1), 0)
    is_first = jnp.logical_and(row == 0, s == 0)
    sub = lax.broadcasted_iota(jnp.int32, (1, SUBLANES, 1), 1)
    for n in range(LRU_BLOCKS):
        cols = slice(n * LANES, (n + 1) * LANES)
        xc = xc_scr[:, cols]
        xb = xc.astype(BF16)
        r = jax.nn.sigmoid(jnp.dot(xb, wa_ref[n], preferred_element_type=F32) + ba_ref[:, cols])
        i = jax.nn.sigmoid(jnp.dot(xb, wx_ref[n], preferred_element_type=F32) + bx_ref[:, cols])
        lam = lam_ref[:, cols]
        softplus_neg = jnp.maximum(-lam, 0.0) + jnp.log1p(jnp.exp(-jnp.abs(lam)))
        log_a = (-LRU_C * softplus_neg) * r
        a = jnp.exp(log_a)
        m2 = 1.0 - a * a
        mult = jnp.where(m2 > 0.0, m2 * lax.rsqrt(m2), 0.0)
        mult = jnp.where(is_first, 1.0, mult)
        u = mult * (i * xc)
        a = a.reshape(ts // SUBLANES, SUBLANES, LANES)
        u = u.reshape(ts // SUBLANES, SUBLANES, LANES)
        for d in (1, 2, 4):
            keep = sub >= d
            a_sh = jnp.where(keep, pltpu.roll(a, d, axis=1), 1.0)
            u_sh = jnp.where(keep, pltpu.roll(u, d, axis=1), 0.0)
            u = u + a * u_sh
            a = a * a_sh
        a = a.reshape(ts, LANES)
        u = u.reshape(ts, LANES)
        gate = jax.nn.gelu(gr_ref[:, cols].astype(F32))
        carry = h_scr[:, cols]
        step = 2 * SUBLANES
        for g in range(ts // step):
            r0 = g * step
            h0 = u[r0:r0 + SUBLANES, :] + a[r0:r0 + SUBLANES, :] * carry
            carry = h0[SUBLANES - 1:SUBLANES, :]
            h1 = u[r0 + SUBLANES:r0 + step, :] + a[r0 + SUBLANES:r0 + step, :] * carry
            carry = h1[SUBLANES - 1:SUBLANES, :]
            hg = jnp.concatenate([h0, h1], axis=0) * gate[r0:r0 + step, :]
            o_ref[r0:r0 + step, cols] = hg.astype(o_ref.dtype)
        h_scr[:, cols] = carry


def _lru(proj, conv_w, conv_b, wa_bf, ba, wx_bf, bx, lam, B, S):
    C = conv_w.shape[1]
    ts = _tile(S, 256)
    ns = S // ts
    vec = lambda: pl.BlockSpec((1, C), lambda b, s: (0, 0))
    blk = lambda: pl.BlockSpec((LRU_BLOCKS, LANES, LANES), lambda b, s: (0, 0, 0))
    return pl.pallas_call(
        functools.partial(_lru_kernel, ts=ts),
        grid=(B, ns),
        in_specs=[pl.BlockSpec((ts, C), lambda b, s: (b * ns + s, 0)),
                  pl.BlockSpec((ts, C), lambda b, s: (b * ns + s, 1)),
                  pl.BlockSpec((CONV_W, C), lambda b, s: (0, 0)),
                  vec(), blk(), vec(), blk(), vec(), vec()],
        out_specs=pl.BlockSpec((ts, C), lambda b, s: (b * ns + s, 0)),
        out_shape=jax.ShapeDtypeStruct((B * S, C), BF16),
        scratch_shapes=[pltpu.VMEM((ts, C), F32), pltpu.VMEM((SUBLANES, C), F32),
                        pltpu.VMEM((1, C), F32)],
        compiler_params=_cparams(("parallel", "arbitrary")),
        name="lru",
    )(proj, proj, conv_w, conv_b.reshape(1, C), wa_bf, ba.reshape(1, C), wx_bf, bx.reshape(1, C),
      lam.reshape(1, C))


BIAS_LANES = 3
ACC_ROWS = V_DIM + 2 * SUBLANES


def _alibi_tables(slopes, S):
    def top16(x):
        return (x.view(np.uint32) & np.uint32(0xFFFF0000)).view(np.float32)

    pos = np.arange(S, dtype=np.float32)
    b = (slopes.astype(np.float32) * np.float32(LOG2E))[:, None] * pos[None, :]
    hi = top16(b)
    mid = top16(b - hi)
    lo = top16(b - hi - mid)
    half = np.zeros(b.shape + (HEAD_DIM,), np.float32)
    half[..., 0], half[..., 1], half[..., 2] = hi, mid, lo
    zero = np.zeros_like(half)
    table = np.stack([np.concatenate([zero, half], axis=-1),
                      np.concatenate([half, zero], axis=-1)], axis=1)
    return jnp.asarray(table, dtype=BF16)


def _attn_kernel(lamv_ref, gsub_ref, q_ref, k_ref, v_ref, kb_ref, o_ref,
                 vt_scr, kx_scr, qx_scr, s_scr, p_scr, a_scr, m_scr, acc_scr, *, tq, lam_init):
    qi = pl.program_id(2)
    tk = tq
    nk = vt_scr.shape[0]

    lane = lax.broadcasted_iota(jnp.int32, (1, V_DIM), 1)
    own = (jnp.where(lane < HEAD_DIM, 1.0, 0.0), jnp.where(lane >= HEAD_DIM, 1.0, 0.0))
    ones_row = (jnp.where(jnp.logical_and(lane >= HEAD_DIM, lane < HEAD_DIM + BIAS_LANES), 1.0, 0.0),
                jnp.where(lane < BIAS_LANES, 1.0, 0.0))

    @pl.when(qi == 0)
    def _():
        for j in range(nk):
            rows = slice(j * tk, (j + 1) * tk)
            vt_scr[j, 0:V_DIM, :] = v_ref[rows, :].astype(F32).T.astype(BF16)
            vt_scr[j, V_DIM:ACC_ROWS, :] = jnp.ones((ACC_ROWS - V_DIM, tk), BF16)
            kf = k_ref[rows, :].astype(F32)
            for mp in range(2):
                kx_scr[mp, rows, :] = (kf * own[mp] + kb_ref[mp, rows, :].astype(F32)).astype(BF16)

    q = q_ref[...].astype(F32)
    for mp in range(2):
        qx_scr[mp] = (q * own[mp] + ones_row[mp]).astype(BF16)
    m_scr[...] = jnp.full_like(m_scr, -jnp.inf)
    acc_scr[...] = jnp.zeros_like(acc_scr)
    row8 = lax.broadcasted_iota(jnp.int32, (SUBLANES, LANES), 0)
    col = lax.broadcasted_iota(jnp.int32, (1, LANES), 1)
    neg_inf = jnp.float32(-jnp.inf)

    def tree(op, parts):
        parts = [p for p in parts if p is not None]
        while len(parts) > 1:
            parts = [op(parts[i], parts[i + 1]) if i + 1 < len(parts) else parts[i]
                     for i in range(0, len(parts), 2)]
        return parts[0]

    def block(ki, masked):
        start = pl.multiple_of(ki * tk, tk)
        for mp in range(2):
            s_scr[mp] = lax.dot_general(kx_scr[mp, pl.ds(start, tk), :], qx_scr[mp],
                                        (((1,), (1,)), ((), ())),
                                        preferred_element_type=F32)
        for mp in range(2):
            for c in range(tq // LANES):
                cols = slice(c * LANES, (c + 1) * LANES)
                lo_col, hi_col = c * LANES, (c + 1) * LANES - 1
                accs = [None] * 4
                for i in range(tk // SUBLANES):
                    r0 = i * SUBLANES
                    if masked and r0 > hi_col:
                        continue
                    t = s_scr[mp, r0:r0 + SUBLANES, cols]
                    if masked and r0 + SUBLANES - 1 > lo_col:
                        t = jnp.where(row8 + r0 <= col + lo_col, t, neg_inf)
                        s_scr[mp, r0:r0 + SUBLANES, cols] = t
                    accs[i % 4] = t if accs[i % 4] is None else jnp.maximum(accs[i % 4], t)
                mx = jnp.max(tree(jnp.maximum, accs), axis=0, keepdims=True)
                m_old = m_scr[mp, :, cols]
                m_new = jnp.maximum(m_old, mx)
                a_scr[mp, :, cols] = jnp.exp2(m_old - m_new)
                m_scr[mp, :, cols] = m_new
                for i in range(tk // (2 * SUBLANES)):
                    r0 = i * 2 * SUBLANES
                    if masked and r0 > hi_col:
                        p_scr[mp, r0:r0 + 2 * SUBLANES, cols] = jnp.zeros((2 * SUBLANES, LANES), BF16)
                        continue
                    p = jnp.exp2(s_scr[mp, r0:r0 + 2 * SUBLANES, cols] - m_new)
                    p_scr[mp, r0:r0 + 2 * SUBLANES, cols] = p.astype(BF16)
        for mp in range(2):
            pv = jnp.dot(vt_scr[ki], p_scr[mp], preferred_element_type=F32)
            acc_scr[mp] = a_scr[mp] * acc_scr[mp] + pv

    def full_block(ki, carry):
        block(ki, False)
        return carry

    lax.fori_loop(0, qi, full_block, 0)
    block(qi, True)

    lv = lamv_ref[...]
    lam = (jnp.exp(jnp.sum(lv[0:1, :] * lv[1:2, :], axis=-1, keepdims=True))
           - jnp.exp(jnp.sum(lv[2:3, :] * lv[3:4, :], axis=-1, keepdims=True)) + lam_init)
    o0 = acc_scr[0, 0:V_DIM, :] * (1.0 / acc_scr[0, V_DIM:V_DIM + 1, :])
    o1 = acc_scr[1, 0:V_DIM, :] * (1.0 / acc_scr[1, V_DIM:V_DIM + 1, :])
    o_t = o0 - lam * o1
    o_t = o_t * lax.rsqrt(jnp.mean(o_t * o_t, axis=0, keepdims=True) + EPS) * gsub_ref[...]
    o_ref[...] = (o_t * (1.0 - lam_init)).T.astype(o_ref.dtype)


def _attn(proj, kbias, lamv, g_subln, B, S, lam_init):
    tq = _tile(S, 512)
    nq = S // tq
    qc, kc, vc = 2 * 8, 3 * 8, 4 * 8
    return pl.pallas_call(
        functools.partial(_attn_kernel, tq=tq, lam_init=lam_init),
        grid=(B, N_HEADS, nq),
        in_specs=[pl.BlockSpec((4, HEAD_DIM), lambda b, h, qi: (0, 0)),
                  pl.BlockSpec((V_DIM, 1), lambda b, h, qi: (0, 0)),
                  pl.BlockSpec((tq, V_DIM), lambda b, h, qi: (b * nq + qi, qc + h)),
                  pl.BlockSpec((S, V_DIM), lambda b, h, qi: (b, kc + h)),
                  pl.BlockSpec((S, V_DIM), lambda b, h, qi: (b, vc + h)),
                  pl.BlockSpec((None, 2, S, V_DIM), lambda b, h, qi: (h, 0, 0, 0))],
        out_specs=pl.BlockSpec((tq, V_DIM), lambda b, h, qi: (b * nq + qi, h)),
        out_shape=jax.ShapeDtypeStruct((B * S, N_HEADS * V_DIM), BF16),
        scratch_shapes=[pltpu.VMEM((nq, ACC_ROWS, tq), BF16), pltpu.VMEM((2, S, V_DIM), BF16),
                        pltpu.VMEM((2, tq, V_DIM), BF16),
                        pltpu.VMEM((2, tq, tq), F32), pltpu.VMEM((2, tq, tq), BF16),
                        pltpu.VMEM((2, 1, tq), F32), pltpu.VMEM((2, 1, tq), F32),
                        pltpu.VMEM((2, ACC_ROWS, tq), F32)],
        compiler_params=_cparams(("parallel", "parallel", "arbitrary")),
        name="attn",
    )(lamv, g_subln.reshape(V_DIM, 1), proj, proj, proj, kbias)


def _pack_rows(x):
    half = x.shape[1] // 2
    lo = lax.bitcast_convert_type(x[:, :half].astype(BF16).astype(F32), jnp.uint32)
    hi = lax.bitcast_convert_type(x[:, half:].astype(BF16).astype(F32), jnp.uint32)
    return (hi & jnp.uint32(0xFFFF0000)) | (lo >> 16)


def _unpack_rows(u):
    lo = lax.bitcast_convert_type(u << 16, F32)
    hi = lax.bitcast_convert_type(u & jnp.uint32(0xFFFF0000), F32)
    return lo, hi


def _mixout_kernel(x_ref, yr_ref, ao_ref, ga_ref, gb_ref, wr_ref, wa_ref, wo_ref,
                   gpost_ref, gt1_ref, gpre_ref, sh2_ref, sc2_ref, x1_ref, h2_ref, h2p_ref):
    ya = jnp.dot(yr_ref[...], wr_ref[...], preferred_element_type=F32)
    yb = jnp.dot(ao_ref[...], wa_ref[...], preferred_element_type=F32)
    merged = (jax.nn.sigmoid(ga_ref[...].astype(F32)) * ya
              + jax.nn.sigmoid(gb_ref[...].astype(F32)) * yb)
    y = jnp.dot(merged.astype(BF16), wo_ref[...], preferred_element_type=F32)
    x1 = x_ref[...] + gt1_ref[...] * _rms(y, gpost_ref[...])
    x1_ref[...] = x1
    h2 = _rms(x1, gpre_ref[...]) * (1.0 + sc2_ref[...]) + sh2_ref[...]
    h2_ref[...] = h2
    h2p_ref[...] = _pack_rows(h2)


def _mixout(x2, yr, ao, proj, wr_bf, wa_bf, wo_bf, g_post, g_pre, mod4, S):
    T, D = x2.shape
    tm = _tile(S, 512)
    per_b = S // tm
    gac, gbc = 5, 6
    row = lambda: pl.BlockSpec((tm, D), lambda i: (i, 0))
    wsp = lambda: pl.BlockSpec((D, D), lambda i: (0, 0))
    vec = lambda: pl.BlockSpec((1, D), lambda i: (0, 0))
    modv = lambda j: pl.BlockSpec((None, None, 1, D), lambda i: (i // per_b, j, 0, 0))
    return pl.pallas_call(
        _mixout_kernel,
        grid=(T // tm,),
        in_specs=[row(), row(), row(),
                  pl.BlockSpec((tm, D), lambda i: (i, gac)),
                  pl.BlockSpec((tm, D), lambda i: (i, gbc)),
                  wsp(), wsp(), wsp(), vec(), modv(2), vec(), modv(3), modv(4)],
        out_specs=[row(), row(), pl.BlockSpec((tm, D // 2), lambda i: (i, 0))],
        out_shape=[jax.ShapeDtypeStruct((T, D), F32), jax.ShapeDtypeStruct((T, D), F32),
                   jax.ShapeDtypeStruct((T, D // 2), jnp.uint32)],
        compiler_params=_cparams(("parallel",)),
        name="mixout",
    )(x2, yr, ao, proj, proj, wr_bf, wa_bf, wo_bf, g_post.reshape(1, D), mod4,
      g_pre.reshape(1, D), mod4, mod4)


def _first_argmax(vals, ids, sentinel):
    m = jnp.max(vals, axis=0, keepdims=True)
    idx = jnp.min(jnp.where(vals == m, ids, sentinel), axis=0, keepdims=True)
    return m, idx


def _router_kernel(h_ref, wr_ref, eb_ref, e_ref, w_ref, r_ref, cnt_ref, cnt_scr, *, tm):
    step = pl.program_id(0)

    @pl.when(step == 0)
    def _():
        cnt_scr[...] = jnp.zeros_like(cnt_scr)

    logits = lax.dot_general(wr_ref[...], h_ref[...], (((1,), (1,)), ((), ())),
                             preferred_element_type=F32, precision=lax.Precision.HIGHEST)
    scores = jax.nn.sigmoid(logits)
    choice = scores + eb_ref[...]
    i8 = lax.broadcasted_iota(jnp.int32, (GROUP_SIZE, tm), 0)
    neg_inf = jnp.float32(-jnp.inf)

    slabs = [choice[g * GROUP_SIZE:(g + 1) * GROUP_SIZE, :] for g in range(N_GROUPS)]
    sc_slabs = [scores[g * GROUP_SIZE:(g + 1) * GROUP_SIZE, :] for g in range(N_GROUPS)]

    gs = jnp.zeros((N_GROUPS, tm), F32)
    for g in range(N_GROUPS):
        m1, idx1 = _first_argmax(slabs[g], i8, GROUP_SIZE)
        m2 = jnp.max(jnp.where(i8 == idx1, neg_inf, slabs[g]), axis=0, keepdims=True)
        gs = jnp.where(i8 == g, m1 + m2, gs)

    sel = jnp.zeros((N_GROUPS, tm), jnp.int32)
    cur = gs
    for _ in range(TOPK_GROUPS):
        _, idx = _first_argmax(cur, i8, N_GROUPS)
        hit = i8 == idx
        sel = jnp.where(hit, 1, sel)
        cur = jnp.where(hit, neg_inf, cur)

    masked = [jnp.where(sel[g:g + 1, :] > 0, slabs[g], neg_inf) for g in range(N_GROUPS)]
    ids = [i8 + g * GROUP_SIZE for g in range(N_GROUPS)]
    onehot = [jnp.zeros((GROUP_SIZE, tm), F32) for _ in range(N_GROUPS)]
    picks = []
    wts = []
    for _ in range(TOP_K):
        m = functools.reduce(jnp.maximum,
                             [jnp.max(c, axis=0, keepdims=True) for c in masked])
        idx = functools.reduce(
            jnp.minimum,
            [jnp.min(jnp.where(c == m, i, N_EXPERTS), axis=0, keepdims=True)
             for c, i in zip(masked, ids)])
        w = jnp.zeros((1, tm), F32)
        for g in range(N_GROUPS):
            hit = ids[g] == idx
            w = w + jnp.sum(jnp.where(hit, sc_slabs[g], 0.0), axis=0, keepdims=True)
            masked[g] = jnp.where(hit, neg_inf, masked[g])
            onehot[g] = jnp.where(hit, 1.0, onehot[g])
        picks.append(idx)
        wts.append(w)

    wsum = functools.reduce(lambda a, b: a + b, wts)
    norm = ROUTED_SCALE / (wsum + 1e-20)

    t_row = lax.broadcasted_iota(jnp.int32, (tm, tm), 0)
    t_col = lax.broadcasted_iota(jnp.int32, (tm, tm), 1)
    before = jnp.where(t_row < t_col, 1.0, 0.0).astype(BF16)
    cum = [jnp.dot(onehot[g].astype(BF16), before, preferred_element_type=F32)
           + cnt_scr[g * GROUP_SIZE:(g + 1) * GROUP_SIZE, :] for g in range(N_GROUPS)]

    for kk in range(TOP_K):
        rank = jnp.zeros((1, tm), F32)
        for g in range(N_GROUPS):
            rank = rank + jnp.sum(jnp.where(ids[g] == picks[kk], cum[g], 0.0),
                                  axis=0, keepdims=True)
        e_ref[kk:kk + 1, :] = picks[kk]
        w_ref[kk:kk + 1, :] = wts[kk] * norm
        r_ref[kk:kk + 1, :] = rank.astype(jnp.int32)

    for g in range(N_GROUPS):
        rows = slice(g * GROUP_SIZE, (g + 1) * GROUP_SIZE)
        cnt_scr[rows, :] = cnt_scr[rows, :] + jnp.sum(onehot[g], axis=1, keepdims=True)
    cnt_ref[...] = jnp.broadcast_to(cnt_scr[...], cnt_ref.shape)


def _router(h2, w_router_t, e_bias):
    T, D = h2.shape
    tm = _tile(T, 512)
    return pl.pallas_call(
        functools.partial(_router_kernel, tm=tm),
        grid=(T // tm,),
        in_specs=[pl.BlockSpec((tm, D), lambda i: (i, 0)),
                  pl.BlockSpec((N_EXPERTS, D), lambda i: (0, 0)),
                  pl.BlockSpec((N_EXPERTS, 1), lambda i: (0, 0))],
        out_specs=[pl.BlockSpec((TOP_K, tm), lambda i: (0, i)),
                   pl.BlockSpec((TOP_K, tm), lambda i: (0, i)),
                   pl.BlockSpec((TOP_K, tm), lambda i: (0, i)),
                   pl.BlockSpec((N_EXPERTS, LANES), lambda i: (0, 0))],
        out_shape=[jax.ShapeDtypeStruct((TOP_K, T), jnp.int32),
                   jax.ShapeDtypeStruct((TOP_K, T), F32),
                   jax.ShapeDtypeStruct((TOP_K, T), jnp.int32),
                   jax.ShapeDtypeStruct((N_EXPERTS, LANES), F32)],
        scratch_shapes=[pltpu.VMEM((N_EXPERTS, 1), F32)],
        compiler_params=_cparams(("arbitrary",)),
        name="router",
    )(h2, w_router_t, e_bias.reshape(N_EXPERTS, 1))


def _slots_kernel(offs_ref, e_ref, r_ref, d_ref):
    e = e_ref[...]
    d = r_ref[...]
    for ex in range(N_EXPERTS):
        d = d + jnp.where(e == ex, offs_ref[ex], 0)
    d_ref[...] = d


def _slots(offs, top_e, rank):
    K, T = top_e.shape
    tm = _tile(T, 4096)
    grid_spec = pltpu.PrefetchScalarGridSpec(
        num_scalar_prefetch=1,
        grid=(T // tm,),
        in_specs=[pl.BlockSpec((K, tm), lambda i, o: (0, i)),
                  pl.BlockSpec((K, tm), lambda i, o: (0, i))],
        out_specs=pl.BlockSpec((K, tm), lambda i, o: (0, i)),
    )
    return pl.pallas_call(
        _slots_kernel,
        grid_spec=grid_spec,
        out_shape=jax.ShapeDtypeStruct((K, T), jnp.int32),
        compiler_params=_cparams(("parallel",)),
        name="slots",
    )(offs, top_e, rank)


def _row_copy(src, src_row, dst, dst_row, sem):
    return pltpu.make_async_copy(src.at[pl.ds(src_row, 1), :], dst.at[pl.ds(dst_row, 1), :], sem)


def _dispatch_kernel(dest_ref, h_ref, xs_ref, sem, *, td):
    def issue(j, carry):
        for kk in range(TOP_K):
            _row_copy(h_ref, j, xs_ref, dest_ref[0, kk * td + j], sem).start()
        return carry

    lax.fori_loop(0, td, issue, 0)
    for _ in range(TOP_K):
        pltpu.make_async_copy(h_ref, xs_ref.at[pl.ds(0, td), :], sem).wait()


def _dispatch(h2, dest_tiles, td):
    T, D = h2.shape
    nt = T // td
    return pl.pallas_call(
        functools.partial(_dispatch_kernel, td=td),
        grid=(nt,),
        in_specs=[pl.BlockSpec((None, 1, TOP_K * td), lambda i: (i, 0, 0),
                               memory_space=pltpu.SMEM),
                  pl.BlockSpec((td, D), lambda i: (i, 0))],
        out_specs=pl.BlockSpec(memory_space=pl.ANY),
        out_shape=jax.ShapeDtypeStruct((T * TOP_K, D), h2.dtype),
        scratch_shapes=[pltpu.SemaphoreType.DMA],
        compiler_params=_cparams(("arbitrary",)),
        name="dispatch",
    )(dest_tiles, h2)


def _gmm_kernel(blk_ref, exp_ref, lo_ref, hi_ref, first_ref, newe_ref, x_ref, w1_ref, w3_ref,
                w2_ref, o_ref, w1b, w3b, w2b, *, bm):
    it = pl.program_id(0)

    @pl.when(newe_ref[it] == 1)
    def _():
        w1b[...] = w1_ref[...].astype(BF16)
        w3b[...] = w3_ref[...].astype(BF16)
        w2b[...] = w2_ref[...].astype(BF16)

    rows = lax.broadcasted_iota(jnp.int32, (bm, 1), 0)
    valid = jnp.logical_and(rows >= lo_ref[it], rows < hi_ref[it])
    x_lo, x_hi = _unpack_rows(jnp.where(valid, x_ref[...], jnp.uint32(0)))
    x_lo = x_lo.astype(BF16)
    x_hi = x_hi.astype(BF16)
    half = x_lo.shape[1]
    h1 = (jnp.dot(x_lo, w1b[:half, :], preferred_element_type=F32)
          + jnp.dot(x_hi, w1b[half:, :], preferred_element_type=F32))
    h3 = (jnp.dot(x_lo, w3b[:half, :], preferred_element_type=F32)
          + jnp.dot(x_hi, w3b[half:, :], preferred_element_type=F32))
    hb = (h1 * jax.nn.sigmoid(h1) * h3).astype(BF16)
    y = jnp.dot(hb, w2b[...], preferred_element_type=F32)

    @pl.when(first_ref[it] == 1)
    def _():
        o_ref[...] = _pack_rows(y)

    @pl.when(first_ref[it] == 0)
    def _():
        o_lo, o_hi = _unpack_rows(o_ref[...])
        o_ref[...] = _pack_rows(y + jnp.concatenate([o_lo, o_hi], axis=1))


def _gmm(xs, items, w1, w3, w2, bm):
    A = xs.shape[0]
    D = w1.shape[1]
    F = w1.shape[2]
    n_items = items[0].shape[0]
    grid_spec = pltpu.PrefetchScalarGridSpec(
        num_scalar_prefetch=6,
        grid=(n_items,),
        in_specs=[pl.BlockSpec((bm, D // 2), lambda i, blk, ex, *_: (blk[i], 0)),
                  pl.BlockSpec((None, D, F), lambda i, blk, ex, *_: (ex[i], 0, 0)),
                  pl.BlockSpec((None, D, F), lambda i, blk, ex, *_: (ex[i], 0, 0)),
                  pl.BlockSpec((None, F, D), lambda i, blk, ex, *_: (ex[i], 0, 0))],
        out_specs=pl.BlockSpec((bm, D // 2), lambda i, blk, ex, *_: (blk[i], 0)),
        scratch_shapes=[pltpu.VMEM((D, F), BF16), pltpu.VMEM((D, F), BF16),
                        pltpu.VMEM((F, D), BF16)],
    )
    return pl.pallas_call(
        functools.partial(_gmm_kernel, bm=bm),
        grid_spec=grid_spec,
        out_shape=jax.ShapeDtypeStruct((A, D // 2), jnp.uint32),
        compiler_params=_cparams(("arbitrary",)),
        name="gmm",
    )(*items, xs, w1, w3, w2)


def _work_items(counts, bm, n_blocks):
    n_items = n_blocks + N_EXPERTS - 1
    ends = jnp.cumsum(counts)
    starts = ends - counts
    nb = jnp.where(counts > 0, (ends - 1) // bm - starts // bm + 1, 0)
    item_end = jnp.cumsum(nb)
    item_start = item_end - nb
    n_real = item_end[-1]
    i = jnp.arange(n_items, dtype=jnp.int32)
    e = jnp.minimum(jnp.sum(item_end[None, :] <= i[:, None], axis=1), N_EXPERTS - 1).astype(jnp.int32)
    onehot = e[:, None] == jnp.arange(N_EXPERTS, dtype=jnp.int32)[None, :]
    pick = lambda v: jnp.sum(jnp.where(onehot, v[None, :], 0), axis=1)
    start_e, end_e = pick(starts), pick(ends)
    blk = start_e // bm + (i - pick(item_start))
    lo = jnp.clip(start_e - blk * bm, 0, bm)
    hi = jnp.clip(end_e - blk * bm, 0, bm)
    real = i < n_real
    blk = jnp.where(real, blk, n_blocks - 1).astype(jnp.int32)
    lo = jnp.where(real, lo, 0).astype(jnp.int32)
    hi = jnp.where(real, hi, 0).astype(jnp.int32)
    one = jnp.ones((1,), jnp.int32)
    first = jnp.concatenate([one, (blk[1:] != blk[:-1]).astype(jnp.int32)])
    new_e = jnp.concatenate([one, (e[1:] != e[:-1]).astype(jnp.int32)])
    return blk, e, lo, hi, first, new_e


def _combine_kernel(dest_ref, ys_ref, w_ref, x1_ref, h_ref, w1_ref, w3_ref, w2_ref,
                    gt2_ref, g_ref, o_ref, buf, sem, *, tc):
    def issue(j, carry):
        for kk in range(TOP_K):
            _row_copy(ys_ref, dest_ref[0, kk * tc + j], buf.at[kk], j, sem).start()
        return carry

    lax.fori_loop(0, tc, issue, 0)

    h_lo, h_hi = _unpack_rows(h_ref[...])
    h_lo = h_lo.astype(BF16)
    h_hi = h_hi.astype(BF16)
    half = h_lo.shape[1]
    h1 = (jnp.dot(h_lo, w1_ref[:half, :], preferred_element_type=F32)
          + jnp.dot(h_hi, w1_ref[half:, :], preferred_element_type=F32))
    h3 = (jnp.dot(h_lo, w3_ref[:half, :], preferred_element_type=F32)
          + jnp.dot(h_hi, w3_ref[half:, :], preferred_element_type=F32))
    y = jnp.dot((h1 * jax.nn.sigmoid(h1) * h3).astype(BF16), w2_ref[...],
                preferred_element_type=F32)

    for kk in range(TOP_K):
        pltpu.make_async_copy(ys_ref.at[pl.ds(0, tc), :], buf.at[kk], sem).wait()
    moe_lo = None
    for kk in range(TOP_K):
        e_lo, e_hi = _unpack_rows(buf[kk])
        wk = w_ref[:, kk:kk + 1]
        moe_lo = wk * e_lo if kk == 0 else moe_lo + wk * e_lo
        moe_hi = wk * e_hi if kk == 0 else moe_hi + wk * e_hi
    moe = jnp.concatenate([moe_lo, moe_hi], axis=1)
    o_ref[...] = x1_ref[...] + gt2_ref[...] * _rms(moe + y, g_ref[...])


def _combine(ys, dest_tiles, w_tok, x1, h2, w1s, w3s, w2s, mod4, g_post, S, tc):
    T, D = x1.shape
    F = w1s.shape[1]
    per_b = S // tc
    row = lambda: pl.BlockSpec((tc, D), lambda i: (i, 0))
    return pl.pallas_call(
        functools.partial(_combine_kernel, tc=tc),
        grid=(T // tc,),
        in_specs=[pl.BlockSpec((None, 1, TOP_K * tc), lambda i: (i, 0, 0),
                               memory_space=pltpu.SMEM),
                  pl.BlockSpec(memory_space=pl.ANY),
                  pl.BlockSpec((tc, TOP_K), lambda i: (i, 0)),
                  row(), pl.BlockSpec((tc, D // 2), lambda i: (i, 0)),
                  pl.BlockSpec((D, F), lambda i: (0, 0)),
                  pl.BlockSpec((D, F), lambda i: (0, 0)),
                  pl.BlockSpec((F, D), lambda i: (0, 0)),
                  pl.BlockSpec((None, None, 1, D), lambda i: (i // per_b, 5, 0, 0)),
                  pl.BlockSpec((1, D), lambda i: (0, 0))],
        out_specs=row(),
        out_shape=jax.ShapeDtypeStruct((T, D), F32),
        scratch_shapes=[pltpu.VMEM((TOP_K, tc, D // 2), jnp.uint32), pltpu.SemaphoreType.DMA],
        compiler_params=_cparams(("arbitrary",)),
        name="combine",
    )(dest_tiles, ys, w_tok, x1, h2, w1s, w3s, w2s, mod4, g_post.reshape(1, D))


def _lambda_init(layer):
    return 0.8 - 0.6 * math.exp(-0.3 * layer)


def kernel(x, c, w_ada, b_ada, g_pre_mix, w_in, conv_w, conv_b, lru_wa, lru_ba, lru_wx, lru_bx,
           lru_lambda, lam_q1, lam_k1, lam_q2, lam_k2, g_subln, w_proj_rnn, w_proj_att, w_out,
           g_post_mix, g_pre_ffn, w_router, e_bias, w1_e, w3_e, w2_e, w1_s, w3_s, w2_s,
           g_post_ffn):
    B, S, D = x.shape
    T = B * S
    depth = w_ada.shape[0]
    slopes = np.exp2(-8.0 * np.arange(1, N_HEADS + 1, dtype=np.float32) / N_HEADS)
    kbias = _alibi_tables(slopes, S)
    tt = _tile(S, 256)
    bm = _tile(T * TOP_K, 512)
    n_blocks = T * TOP_K // bm

    x2 = x.reshape(T, D)
    for l in range(depth):
        lam_init = _lambda_init(l)
        mod4 = _ada(c, w_ada[l], b_ada[l]).reshape(B, 6, 1, D)

        proj = _inproj(x2, g_pre_mix[l], mod4, w_in[l].astype(BF16), S)
        yr = _lru(proj, conv_w[l], conv_b[l], lru_wa[l].astype(BF16), lru_ba[l],
                  lru_wx[l].astype(BF16), lru_bx[l], lru_lambda[l], B, S)
        lamv = jnp.stack([lam_q1[l], lam_k1[l], lam_q2[l], lam_k2[l]])
        ao = _attn(proj, kbias, lamv, g_subln[l], B, S, lam_init)
        x1, h2, h2p = _mixout(x2, yr, ao, proj, w_proj_rnn[l].astype(BF16),
                         w_proj_att[l].astype(BF16), w_out[l].astype(BF16),
                         g_post_mix[l], g_pre_ffn[l], mod4, S)

        top_e, top_w, rank, cnt = _router(h2, w_router[l].T, e_bias[l])
        counts = cnt[:, 0].astype(jnp.int32)
        offs = jnp.cumsum(counts) - counts
        dest = _slots(offs, top_e, rank)
        dest_tiles = dest.reshape(TOP_K, T // tt, tt).transpose(1, 0, 2).reshape(T // tt, 1,
                                                                                  TOP_K * tt)
        xs = _dispatch(h2p, dest_tiles, tt)
        items = _work_items(counts, bm, n_blocks)
        ys = _gmm(xs, items, w1_e[l], w3_e[l], w2_e[l], bm)
        x2 = _combine(ys, dest_tiles, top_w.T, x1, h2p, w1_s[l].astype(BF16),
                      w3_s[l].astype(BF16), w2_s[l].astype(BF16), mod4, g_post_ffn[l], S, tt)
    return x2.reshape(B, S, D)
```

```python
import functools
import math

import jax
import jax.numpy as jnp
import numpy as np
from jax import lax
from jax.experimental import pallas as pl
from jax.experimental.pallas import tpu as pltpu

F32 = jnp.float32
BF16 = jnp.bfloat16

EPS = 1e-6
N_HEADS = 8
HEAD_DIM = 64
V_DIM = 2 * HEAD_DIM
LRU_BLOCKS = 8
CONV_W = 4
LRU_C = 8.0
N_EXPERTS = 64
TOP_K = 8
N_GROUPS = 8
GROUP_SIZE = N_EXPERTS // N_GROUPS
TOPK_GROUPS = 4
ROUTED_SCALE = 2.5

LANES = 128
SUBLANES = 8
VMEM_LIMIT = 48 * 1024 * 1024


def _cparams(sem):
    return pltpu.CompilerParams(dimension_semantics=sem, vmem_limit_bytes=VMEM_LIMIT)


def _tile(n, pref):
    t = min(n, pref)
    while n % t:
        t //= 2
    return t


def _rms(x, g):
    return x * lax.rsqrt(jnp.mean(x * x, axis=-1, keepdims=True) + EPS) * g


def _ada_kernel(c_ref, w_ref, b_ref, o_ref):
    c = c_ref[...]
    cond = c * jax.nn.sigmoid(c)
    o_ref[...] = jnp.dot(cond, w_ref[...], preferred_element_type=F32) + b_ref[...]


def _ada(c, w, b):
    B, D = c.shape
    N = w.shape[1]
    tn = _tile(N, 1024)
    return pl.pallas_call(
        _ada_kernel,
        grid=(N // tn,),
        in_specs=[pl.BlockSpec((B, D), lambda j: (0, 0)),
                  pl.BlockSpec((D, tn), lambda j: (0, j)),
                  pl.BlockSpec((1, tn), lambda j: (0, j))],
        out_specs=pl.BlockSpec((B, tn), lambda j: (0, j)),
        out_shape=jax.ShapeDtypeStruct((B, N), F32),
        compiler_params=_cparams(("parallel",)),
        name="ada",
    )(c, w, b.reshape(1, N))


LOG2E = 1.4426950408889634
Q_COL_BLOCK = 2
Q_PRESCALE = HEAD_DIM ** -0.5 * LOG2E


def _inproj_kernel(x_ref, g_ref, sh_ref, sc_ref, w_ref, o_ref, h_scr):
    @pl.when(pl.program_id(1) == 0)
    def _():
        h = _rms(x_ref[...], g_ref[...]) * (1.0 + sc_ref[...]) + sh_ref[...]
        h_scr[...] = h.astype(BF16)

    r = jnp.dot(h_scr[...], w_ref[...], preferred_element_type=F32)
    r = r * jnp.where(pl.program_id(1) == Q_COL_BLOCK, Q_PRESCALE, 1.0)
    o_ref[...] = r.astype(o_ref.dtype)


def _inproj(x2, g, mod4, w_bf, S):
    T, D = x2.shape
    N = w_bf.shape[1]
    tm = _tile(S, 1024)
    tn = 1024
    per_b = S // tm
    return pl.pallas_call(
        _inproj_kernel,
        grid=(T // tm, N // tn),
        in_specs=[pl.BlockSpec((tm, D), lambda i, j: (i, 0)),
                  pl.BlockSpec((1, D), lambda i, j: (0, 0)),
                  pl.BlockSpec((None, None, 1, D), lambda i, j: (i // per_b, 0, 0, 0)),
                  pl.BlockSpec((None, None, 1, D), lambda i, j: (i // per_b, 1, 0, 0)),
                  pl.BlockSpec((D, tn), lambda i, j: (0, j))],
        out_specs=pl.BlockSpec((tm, tn), lambda i, j: (i, j)),
        out_shape=jax.ShapeDtypeStruct((T, N), BF16),
        scratch_shapes=[pltpu.VMEM((tm, D), BF16)],
        compiler_params=_cparams(("parallel", "arbitrary")),
        name="inproj",
    )(x2, g.reshape(1, D), mod4, mod4, w_bf)


def _lru_kernel(xr_ref, gr_ref, cw_ref, cb_ref, wa_ref, ba_ref, wx_ref, bx_ref, lam_ref,
                o_ref, xc_scr, prev_scr, h_scr, *, ts):
    s = pl.program_id(1)

    @pl.when(s == 0)
    def _():
        prev_scr[...] = jnp.zeros_like(prev_scr)
        h_scr[...] = jnp.zeros_like(h_scr)

    x = xr_ref[...].astype(F32)
    prev = prev_scr[...]
    row8 = lax.broadcasted_iota(jnp.int32, (SUBLANES, 1), 0)
    acc = cb_ref[...] + cw_ref[CONV_W - 1:CONV_W, :] * x
    xc_scr[...] = acc
    top = cb_ref[...] + cw_ref[CONV_W - 1:CONV_W, :] * x[0:SUBLANES, :]
    for j in range(1, CONV_W):
        wj = cw_ref[CONV_W - 1 - j:CONV_W - j, :]
        rj = pltpu.roll(x, j, axis=0)
        xc_scr[...] += wj * rj
        pj = pltpu.roll(prev, j, axis=0)
        top += wj * jnp.where(row8 < j, pj, rj[0:SUBLANES, :])
    xc_scr[0:SUBLANES, :] = top
    prev_scr[...] = x[ts - SUBLANES:ts, :]

    row = lax.broadcasted_iota(jnp.int32, (ts, 1), 0)
    is_first = jnp.logical_and(row == 0, s == 0)
    sub = lax.broadcasted_iota(jnp.int32, (1, SUBLANES, 1), 1)
    for n in range(LRU_BLOCKS):
        cols = slice(n * LANES, (n + 1) * LANES)
        xc = xc_scr[:, cols]
        xb = xc.astype(BF16)
        r = jax.nn.sigmoid(jnp.dot(xb, wa_ref[n], preferred_element_type=F32) + ba_ref[:, cols])
        i = jax.nn.sigmoid(jnp.dot(xb, wx_ref[n], preferred_element_type=F32) + bx_ref[:, cols])
        lam = lam_ref[:, cols]
        softplus_neg = jnp.maximum(-lam, 0.0) + jnp.log1p(jnp.exp(-jnp.abs(lam)))
        log_a = (-LRU_C * softplus_neg) * r
        a = jnp.exp(log_a)
        m2 = 1.0 - a * a
        mult = jnp.where(m2 > 0.0, m2 * lax.rsqrt(m2), 0.0)
        mult = jnp.where(is_first, 1.0, mult)
        u = mult * (i * xc)
        a = a.reshape(ts // SUBLANES, SUBLANES, LANES)
        u = u.reshape(ts // SUBLANES, SUBLANES, LANES)
        for d in (1, 2, 4):
            keep = sub >= d
            a_sh = jnp.where(keep, pltpu.roll(a, d, axis=1), 1.0)
            u_sh = jnp.where(keep, pltpu.roll(u, d, axis=1), 0.0)
            u = u + a * u_sh
            a = a * a_sh
        a = a.reshape(ts, LANES)
        u = u.reshape(ts, LANES)
        gate = jax.nn.gelu(gr_ref[:, cols].astype(F32))
        carry = h_scr[:, cols]
        step = 2 * SUBLANES
        for g in range(ts // step):
            r0 = g * step
            h0 = u[r0:r0 + SUBLANES, :] + a[r0:r0 + SUBLANES, :] * carry
            carry = h0[SUBLANES - 1:SUBLANES, :]
            h1 = u[r0 + SUBLANES:r0 + step, :] + a[r0 + SUBLANES:r0 + step, :] * carry
            carry = h1[SUBLANES - 1:SUBLANES, :]
            hg = jnp.concatenate([h0, h1], axis=0) * gate[r0:r0 + step, :]
            o_ref[r0:r0 + step, cols] = hg.astype(o_ref.dtype)
        h_scr[:, cols] = carry


def _lru(proj, conv_w, conv_b, wa_bf, ba, wx_bf, bx, lam, B, S):
    C = conv_w.shape[1]
    ts = _tile(S, 256)
    ns = S // ts
    vec = lambda: pl.BlockSpec((1, C), lambda b, s: (0, 0))
    blk = lambda: pl.BlockSpec((LRU_BLOCKS, LANES, LANES), lambda b, s: (0, 0, 0))
    return pl.pallas_call(
        functools.partial(_lru_kernel, ts=ts),
        grid=(B, ns),
        in_specs=[pl.BlockSpec((ts, C), lambda b, s: (b * ns + s, 0)),
                  pl.BlockSpec((ts, C), lambda b, s: (b * ns + s, 1)),
                  pl.BlockSpec((CONV_W, C), lambda b, s: (0, 0)),
                  vec(), blk(), vec(), blk(), vec(), vec()],
        out_specs=pl.BlockSpec((ts, C), lambda b, s: (b * ns + s, 0)),
        out_shape=jax.ShapeDtypeStruct((B * S, C), BF16),
        scratch_shapes=[pltpu.VMEM((ts, C), F32), pltpu.VMEM((SUBLANES, C), F32),
                        pltpu.VMEM((1, C), F32)],
        compiler_params=_cparams(("parallel", "arbitrary")),
        name="lru",
    )(proj, proj, conv_w, conv_b.reshape(1, C), wa_bf, ba.reshape(1, C), wx_bf, bx.reshape(1, C),
      lam.reshape(1, C))


BIAS_LANES = 3
ACC_ROWS = V_DIM + 2 * SUBLANES


def _alibi_tables(slopes, S):
    def top16(x):
        return (x.view(np.uint32) & np.uint32(0xFFFF0000)).view(np.float32)

    pos = np.arange(S, dtype=np.float32)
    b = (slopes.astype(np.float32) * np.float32(LOG2E))[:, None] * pos[None, :]
    hi = top16(b)
    mid = top16(b - hi)
    lo = top16(b - hi - mid)
    half = np.zeros(b.shape + (HEAD_DIM,), np.float32)
    half[..., 0], half[..., 1], half[..., 2] = hi, mid, lo
    zero = np.zeros_like(half)
    table = np.stack([np.concatenate([zero, half], axis=-1),
                      np.concatenate([half, zero], axis=-1)], axis=1)
    return jnp.asarray(table, dtype=BF16)


def _attn_kernel(lamv_ref, gsub_ref, q_ref, k_ref, v_ref, kb_ref, o_ref,
                 vt_scr, kx_scr, qx_scr, s_scr, p_scr, a_scr, m_scr, acc_scr, *, tq, lam_init):
    qi = pl.program_id(2)
    tk = tq
    nk = vt_scr.shape[0]

    lane = lax.broadcasted_iota(jnp.int32, (1, V_DIM), 1)
    own = (jnp.where(lane < HEAD_DIM, 1.0, 0.0), jnp.where(lane >= HEAD_DIM, 1.0, 0.0))
    ones_row = (jnp.where(jnp.logical_and(lane >= HEAD_DIM, lane < HEAD_DIM + BIAS_LANES), 1.0, 0.0),
                jnp.where(lane < BIAS_LANES, 1.0, 0.0))

    @pl.when(qi == 0)
    def _():
        for j in range(nk):
            rows = slice(j * tk, (j + 1) * tk)
            vt_scr[j, 0:V_DIM, :] = v_ref[rows, :].astype(F32).T.astype(BF16)
            vt_scr[j, V_DIM:ACC_ROWS, :] = jnp.ones((ACC_ROWS - V_DIM, tk), BF16)
            kf = k_ref[rows, :].astype(F32)
            for mp in range(2):
                kx_scr[mp, rows, :] = (kf * own[mp] + kb_ref[mp, rows, :].astype(F32)).astype(BF16)

    q = q_ref[...].astype(F32)
    for mp in range(2):
        qx_scr[mp] = (q * own[mp] + ones_row[mp]).astype(BF16)
    m_scr[...] = jnp.full_like(m_scr, -jnp.inf)
    acc_scr[...] = jnp.zeros_like(acc_scr)
    row8 = lax.broadcasted_iota(jnp.int32, (SUBLANES, LANES), 0)
    col = lax.broadcasted_iota(jnp.int32, (1, LANES), 1)
    neg_inf = jnp.float32(-jnp.inf)

    def tree(op, parts):
        parts = [p for p in parts if p is not None]
        while len(parts) > 1:
            parts = [op(parts[i], parts[i + 1]) if i + 1 < len(parts) else parts[i]
                     for i in range(0, len(parts), 2)]
        return parts[0]

    def scores(j, par):
        start = pl.multiple_of(j * tk, tk)
        for mp in range(2):
            s_scr[par, mp] = lax.dot_general(kx_scr[mp, pl.ds(start, tk), :], qx_scr[mp],
                                             (((1,), (1,)), ((), ())),
                                             preferred_element_type=F32)

    def softmax(par, masked):
        for mp in range(2):
            for c in range(tq // LANES):
                cols = slice(c * LANES, (c + 1) * LANES)
                lo_col, hi_col = c * LANES, (c + 1) * LANES - 1
                accs = [None] * 4
                for i in range(tk // SUBLANES):
                    r0 = i * SUBLANES
                    if masked and r0 > hi_col:
                        continue
                    t = s_scr[par, mp, r0:r0 + SUBLANES, cols]
                    if masked and r0 + SUBLANES - 1 > lo_col:
                        t = jnp.where(row8 + r0 <= col + lo_col, t, neg_inf)
                        s_scr[par, mp, r0:r0 + SUBLANES, cols] = t
                    accs[i % 4] = t if accs[i % 4] is None else jnp.maximum(accs[i % 4], t)
                mx = jnp.max(tree(jnp.maximum, accs), axis=0, keepdims=True)
                m_old = m_scr[mp, :, cols]
                m_new = jnp.maximum(m_old, mx)
                a_scr[par, mp, :, cols] = jnp.exp2(m_old - m_new)
                m_scr[mp, :, cols] = m_new
                for i in range(tk // (2 * SUBLANES)):
                    r0 = i * 2 * SUBLANES
                    if masked and r0 > hi_col:
                        p_scr[par, mp, r0:r0 + 2 * SUBLANES, cols] = jnp.zeros(
                            (2 * SUBLANES, LANES), BF16)
                        continue
                    p = jnp.exp2(s_scr[par, mp, r0:r0 + 2 * SUBLANES, cols] - m_new)
                    p_scr[par, mp, r0:r0 + 2 * SUBLANES, cols] = p.astype(BF16)

    def values(j, par):
        for mp in range(2):
            pv = jnp.dot(vt_scr[j], p_scr[par, mp], preferred_element_type=F32)
            acc_scr[mp] = a_scr[par, mp] * acc_scr[mp] + pv

    p_scr[1] = jnp.zeros(p_scr.shape[1:], BF16)
    a_scr[1] = jnp.ones(a_scr.shape[1:], F32)
    scores(0, 0)

    def stage(j, par):
        scores(j + 1, 1 - par)
        softmax(par, False)
        values(jnp.maximum(j - 1, 0), 1 - par)

    def stage_pair(i, carry):
        stage(2 * i, 0)
        stage(2 * i + 1, 1)
        return carry

    def tail(par):
        softmax(par, True)
        values(jnp.maximum(qi - 1, 0), 1 - par)
        values(qi, par)

    lax.fori_loop(0, qi // 2, stage_pair, 0)

    @pl.when(qi % 2 == 1)
    def _():
        stage(qi - 1, 0)
        tail(1)

    @pl.when(qi % 2 == 0)
    def _():
        tail(0)

    lv = lamv_ref[...]
    lam = (jnp.exp(jnp.sum(lv[0:1, :] * lv[1:2, :], axis=-1, keepdims=True))
           - jnp.exp(jnp.sum(lv[2:3, :] * lv[3:4, :], axis=-1, keepdims=True)) + lam_init)
    o0 = acc_scr[0, 0:V_DIM, :] * (1.0 / acc_scr[0, V_DIM:V_DIM + 1, :])
    o1 = acc_scr[1, 0:V_DIM, :] * (1.0 / acc_scr[1, V_DIM:V_DIM + 1, :])
    o_t = o0 - lam * o1
    o_t = o_t * lax.rsqrt(jnp.mean(o_t * o_t, axis=0, keepdims=True) + EPS) * gsub_ref[...]
    o_ref[...] = (o_t * (1.0 - lam_init)).T.astype(o_ref.dtype)


def _attn(proj, kbias, lamv, g_subln, B, S, lam_init):
    tq = _tile(S, 512)
    nq = S // tq
    qc, kc, vc = 2 * 8, 3 * 8, 4 * 8
    return pl.pallas_call(
        functools.partial(_attn_kernel, tq=tq, lam_init=lam_init),
        grid=(B, N_HEADS, nq),
        in_specs=[pl.BlockSpec((4, HEAD_DIM), lambda b, h, qi: (0, 0)),
                  pl.BlockSpec((V_DIM, 1), lambda b, h, qi: (0, 0)),
                  pl.BlockSpec((tq, V_DIM), lambda b, h, qi: (b * nq + qi, qc + h)),
                  pl.BlockSpec((S, V_DIM), lambda b, h, qi: (b, kc + h)),
                  pl.BlockSpec((S, V_DIM), lambda b, h, qi: (b, vc + h)),
                  pl.BlockSpec((None, 2, S, V_DIM), lambda b, h, qi: (h, 0, 0, 0))],
        out_specs=pl.BlockSpec((tq, V_DIM), lambda b, h, qi: (b * nq + qi, h)),
        out_shape=jax.ShapeDtypeStruct((B * S, N_HEADS * V_DIM), BF16),
        scratch_shapes=[pltpu.VMEM((nq, ACC_ROWS, tq), BF16), pltpu.VMEM((2, S, V_DIM), BF16),
                        pltpu.VMEM((2, tq, V_DIM), BF16),
                        pltpu.VMEM((2, 2, tq, tq), F32), pltpu.VMEM((2, 2, tq, tq), BF16),
                        pltpu.VMEM((2, 2, 1, tq), F32), pltpu.VMEM((2, 1, tq), F32),
                        pltpu.VMEM((2, ACC_ROWS, tq), F32)],
        compiler_params=_cparams(("parallel", "parallel", "arbitrary")),
        name="attn",
    )(lamv, g_subln.reshape(V_DIM, 1), proj, proj, proj, kbias)


def _pack_rows(x):
    half = x.shape[1] // 2
    lo = lax.bitcast_convert_type(x[:, :half].astype(BF16).astype(F32), jnp.uint32)
    hi = lax.bitcast_convert_type(x[:, half:].astype(BF16).astype(F32), jnp.uint32)
    return (hi & jnp.uint32(0xFFFF0000)) | (lo >> 16)


def _unpack_rows(u):
    lo = lax.bitcast_convert_type(u << 16, F32)
    hi = lax.bitcast_convert_type(u & jnp.uint32(0xFFFF0000), F32)
    return lo, hi


def _mixout_kernel(x_ref, yr_ref, ao_ref, ga_ref, gb_ref, wr_ref, wa_ref, wo_ref,
                   gpost_ref, gt1_ref, gpre_ref, sh2_ref, sc2_ref, x1_ref, h2_ref, h2p_ref):
    ya = jnp.dot(yr_ref[...], wr_ref[...], preferred_element_type=F32)
    yb = jnp.dot(ao_ref[...], wa_ref[...], preferred_element_type=F32)
    merged = (jax.nn.sigmoid(ga_ref[...].astype(F32)) * ya
              + jax.nn.sigmoid(gb_ref[...].astype(F32)) * yb)
    y = jnp.dot(merged.astype(BF16), wo_ref[...], preferred_element_type=F32)
    x1 = x_ref[...] + gt1_ref[...] * _rms(y, gpost_ref[...])
    x1_ref[...] = x1
    h2 = _rms(x1, gpre_ref[...]) * (1.0 + sc2_ref[...]) + sh2_ref[...]
    h2_ref[...] = h2
    h2p_ref[...] = _pack_rows(h2)


def _mixout(x2, yr, ao, proj, wr_bf, wa_bf, wo_bf, g_post, g_pre, mod4, S):
    T, D = x2.shape
    tm = _tile(S, 512)
    per_b = S // tm
    gac, gbc = 5, 6
    row = lambda: pl.BlockSpec((tm, D), lambda i: (i, 0))
    wsp = lambda: pl.BlockSpec((D, D), lambda i: (0, 0))
    vec = lambda: pl.BlockSpec((1, D), lambda i: (0, 0))
    modv = lambda j: pl.BlockSpec((None, None, 1, D), lambda i: (i // per_b, j, 0, 0))
    return pl.pallas_call(
        _mixout_kernel,
        grid=(T // tm,),
        in_specs=[row(), row(), row(),
                  pl.BlockSpec((tm, D), lambda i: (i, gac)),
                  pl.BlockSpec((tm, D), lambda i: (i, gbc)),
                  wsp(), wsp(), wsp(), vec(), modv(2), vec(), modv(3), modv(4)],
        out_specs=[row(), row(), pl.BlockSpec((tm, D // 2), lambda i: (i, 0))],
        out_shape=[jax.ShapeDtypeStruct((T, D), F32), jax.ShapeDtypeStruct((T, D), F32),
                   jax.ShapeDtypeStruct((T, D // 2), jnp.uint32)],
        compiler_params=_cparams(("parallel",)),
        name="mixout",
    )(x2, yr, ao, proj, proj, wr_bf, wa_bf, wo_bf, g_post.reshape(1, D), mod4,
      g_pre.reshape(1, D), mod4, mod4)


def _first_argmax(vals, ids, sentinel):
    m = jnp.max(vals, axis=0, keepdims=True)
    idx = jnp.min(jnp.where(vals == m, ids, sentinel), axis=0, keepdims=True)
    return m, idx


def _router_kernel(h_ref, wr_ref, eb_ref, e_ref, w_ref, r_ref, cnt_ref, cnt_scr, *, tm):
    step = pl.program_id(0)

    @pl.when(step == 0)
    def _():
        cnt_scr[...] = jnp.zeros_like(cnt_scr)

    logits = lax.dot_general(wr_ref[...], h_ref[...], (((1,), (1,)), ((), ())),
                             preferred_element_type=F32, precision=lax.Precision.HIGHEST)
    scores = jax.nn.sigmoid(logits)
    choice = scores + eb_ref[...]
    i8 = lax.broadcasted_iota(jnp.int32, (GROUP_SIZE, tm), 0)
    neg_inf = jnp.float32(-jnp.inf)

    slabs = [choice[g * GROUP_SIZE:(g + 1) * GROUP_SIZE, :] for g in range(N_GROUPS)]
    sc_slabs = [scores[g * GROUP_SIZE:(g + 1) * GROUP_SIZE, :] for g in range(N_GROUPS)]

    gs = jnp.zeros((N_GROUPS, tm), F32)
    for g in range(N_GROUPS):
        m1, idx1 = _first_argmax(slabs[g], i8, GROUP_SIZE)
        m2 = jnp.max(jnp.where(i8 == idx1, neg_inf, slabs[g]), axis=0, keepdims=True)
        gs = jnp.where(i8 == g, m1 + m2, gs)

    sel = jnp.zeros((N_GROUPS, tm), jnp.int32)
    cur = gs
    for _ in range(TOPK_GROUPS):
        _, idx = _first_argmax(cur, i8, N_GROUPS)
        hit = i8 == idx
        sel = jnp.where(hit, 1, sel)
        cur = jnp.where(hit, neg_inf, cur)

    masked = [jnp.where(sel[g:g + 1, :] > 0, slabs[g], neg_inf) for g in range(N_GROUPS)]
    ids = [i8 + g * GROUP_SIZE for g in range(N_GROUPS)]
    onehot = [jnp.zeros((GROUP_SIZE, tm), F32) for _ in range(N_GROUPS)]
    picks = []
    wts = []
    for _ in range(TOP_K):
        m = functools.reduce(jnp.maximum,
                             [jnp.max(c, axis=0, keepdims=True) for c in masked])
        idx = functools.reduce(
            jnp.minimum,
            [jnp.min(jnp.where(c == m, i, N_EXPERTS), axis=0, keepdims=True)
             for c, i in zip(masked, ids)])
        w = jnp.zeros((1, tm), F32)
        for g in range(N_GROUPS):
            hit = ids[g] == idx
            w = w + jnp.sum(jnp.where(hit, sc_slabs[g], 0.0), axis=0, keepdims=True)
            masked[g] = jnp.where(hit, neg_inf, masked[g])
            onehot[g] = jnp.where(hit, 1.0, onehot[g])
        picks.append(idx)
        wts.append(w)

    wsum = functools.reduce(lambda a, b: a + b, wts)
    norm = ROUTED_SCALE / (wsum + 1e-20)

    t_row = lax.broadcasted_iota(jnp.int32, (tm, tm), 0)
    t_col = lax.broadcasted_iota(jnp.int32, (tm, tm), 1)
    before = jnp.where(t_row < t_col, 1.0, 0.0).astype(BF16)
    cum = [jnp.dot(onehot[g].astype(BF16), before, preferred_element_type=F32)
           + cnt_scr[g * GROUP_SIZE:(g + 1) * GROUP_SIZE, :] for g in range(N_GROUPS)]

    for kk in range(TOP_K):
        rank = jnp.zeros((1, tm), F32)
        for g in range(N_GROUPS):
            rank = rank + jnp.sum(jnp.where(ids[g] == picks[kk], cum[g], 0.0),
                                  axis=0, keepdims=True)
        e_ref[kk:kk + 1, :] = picks[kk]
        w_ref[kk:kk + 1, :] = wts[kk] * norm
        r_ref[kk:kk + 1, :] = rank.astype(jnp.int32)

    for g in range(N_GROUPS):
        rows = slice(g * GROUP_SIZE, (g + 1) * GROUP_SIZE)
        cnt_scr[rows, :] = cnt_scr[rows, :] + jnp.sum(onehot[g], axis=1, keepdims=True)
    cnt_ref[...] = jnp.broadcast_to(cnt_scr[...], cnt_ref.shape)


def _router(h2, w_router_t, e_bias):
    T, D = h2.shape
    tm = _tile(T, 512)
    return pl.pallas_call(
        functools.partial(_router_kernel, tm=tm),
        grid=(T // tm,),
        in_specs=[pl.BlockSpec((tm, D), lambda i: (i, 0)),
                  pl.BlockSpec((N_EXPERTS, D), lambda i: (0, 0)),
                  pl.BlockSpec((N_EXPERTS, 1), lambda i: (0, 0))],
        out_specs=[pl.BlockSpec((TOP_K, tm), lambda i: (0, i)),
                   pl.BlockSpec((TOP_K, tm), lambda i: (0, i)),
                   pl.BlockSpec((TOP_K, tm), lambda i: (0, i)),
                   pl.BlockSpec((N_EXPERTS, LANES), lambda i: (0, 0))],
        out_shape=[jax.ShapeDtypeStruct((TOP_K, T), jnp.int32),
                   jax.ShapeDtypeStruct((TOP_K, T), F32),
                   jax.ShapeDtypeStruct((TOP_K, T), jnp.int32),
                   jax.ShapeDtypeStruct((N_EXPERTS, LANES), F32)],
        scratch_shapes=[pltpu.VMEM((N_EXPERTS, 1), F32)],
        compiler_params=_cparams(("arbitrary",)),
        name="router",
    )(h2, w_router_t, e_bias.reshape(N_EXPERTS, 1))


def _slots_kernel(offs_ref, e_ref, r_ref, d_ref):
    e = e_ref[...]
    d = r_ref[...]
    for ex in range(N_EXPERTS):
        d = d + jnp.where(e == ex, offs_ref[ex], 0)
    d_ref[...] = d


def _slots(offs, top_e, rank):
    K, T = top_e.shape
    tm = _tile(T, 4096)
    grid_spec = pltpu.PrefetchScalarGridSpec(
        num_scalar_prefetch=1,
        grid=(T // tm,),
        in_specs=[pl.BlockSpec((K, tm), lambda i, o: (0, i)),
                  pl.BlockSpec((K, tm), lambda i, o: (0, i))],
        out_specs=pl.BlockSpec((K, tm), lambda i, o: (0, i)),
    )
    return pl.pallas_call(
        _slots_kernel,
        grid_spec=grid_spec,
        out_shape=jax.ShapeDtypeStruct((K, T), jnp.int32),
        compiler_params=_cparams(("parallel",)),
        name="slots",
    )(offs, top_e, rank)


def _row_copy(src, src_row, dst, dst_row, sem):
    return pltpu.make_async_copy(src.at[pl.ds(src_row, 1), :], dst.at[pl.ds(dst_row, 1), :], sem)


def _dispatch_kernel(dest_ref, h_ref, xs_ref, sem, *, td):
    def issue(j, carry):
        for kk in range(TOP_K):
            _row_copy(h_ref, j, xs_ref, dest_ref[0, kk * td + j], sem).start()
        return carry

    lax.fori_loop(0, td, issue, 0)
    for _ in range(TOP_K):
        pltpu.make_async_copy(h_ref, xs_ref.at[pl.ds(0, td), :], sem).wait()


def _dispatch(h2, dest_tiles, td):
    T, D = h2.shape
    nt = T // td
    return pl.pallas_call(
        functools.partial(_dispatch_kernel, td=td),
        grid=(nt,),
        in_specs=[pl.BlockSpec((None, 1, TOP_K * td), lambda i: (i, 0, 0),
                               memory_space=pltpu.SMEM),
                  pl.BlockSpec((td, D), lambda i: (i, 0))],
        out_specs=pl.BlockSpec(memory_space=pl.ANY),
        out_shape=jax.ShapeDtypeStruct((T * TOP_K, D), h2.dtype),
        scratch_shapes=[pltpu.SemaphoreType.DMA],
        compiler_params=_cparams(("arbitrary",)),
        name="dispatch",
    )(dest_tiles, h2)


def _gmm_kernel(blk_ref, exp_ref, lo_ref, hi_ref, first_ref, newe_ref, x_ref, w1_ref, w3_ref,
                w2_ref, o_ref, w1b, w3b, w2b, *, bm):
    it = pl.program_id(0)

    @pl.when(newe_ref[it] == 1)
    def _():
        w1b[...] = w1_ref[...].astype(BF16)
        w3b[...] = w3_ref[...].astype(BF16)
        w2b[...] = w2_ref[...].astype(BF16)

    rows = lax.broadcasted_iota(jnp.int32, (bm, 1), 0)
    valid = jnp.logical_and(rows >= lo_ref[it], rows < hi_ref[it])
    x_lo, x_hi = _unpack_rows(jnp.where(valid, x_ref[...], jnp.uint32(0)))
    x_lo = x_lo.astype(BF16)
    x_hi = x_hi.astype(BF16)
    half = x_lo.shape[1]
    h1 = (jnp.dot(x_lo, w1b[:half, :], preferred_element_type=F32)
          + jnp.dot(x_hi, w1b[half:, :], preferred_element_type=F32))
    h3 = (jnp.dot(x_lo, w3b[:half, :], preferred_element_type=F32)
          + jnp.dot(x_hi, w3b[half:, :], preferred_element_type=F32))
    hb = (h1 * jax.nn.sigmoid(h1) * h3).astype(BF16)
    y = jnp.dot(hb, w2b[...], preferred_element_type=F32)

    @pl.when(first_ref[it] == 1)
    def _():
        o_ref[...] = _pack_rows(y)

    @pl.when(first_ref[it] == 0)
    def _():
        o_lo, o_hi = _unpack_rows(o_ref[...])
        o_ref[...] = _pack_rows(y + jnp.concatenate([o_lo, o_hi], axis=1))


def _gmm(xs, items, w1, w3, w2, bm):
    A = xs.shape[0]
    D = w1.shape[1]
    F = w1.shape[2]
    n_items = items[0].shape[0]
    grid_spec = pltpu.PrefetchScalarGridSpec(
        num_scalar_prefetch=6,
        grid=(n_items,),
        in_specs=[pl.BlockSpec((bm, D // 2), lambda i, blk, ex, *_: (blk[i], 0)),
                  pl.BlockSpec((None, D, F), lambda i, blk, ex, *_: (ex[i], 0, 0)),
                  pl.BlockSpec((None, D, F), lambda i, blk, ex, *_: (ex[i], 0, 0)),
                  pl.BlockSpec((None, F, D), lambda i, blk, ex, *_: (ex[i], 0, 0))],
        out_specs=pl.BlockSpec((bm, D // 2), lambda i, blk, ex, *_: (blk[i], 0)),
        scratch_shapes=[pltpu.VMEM((D, F), BF16), pltpu.VMEM((D, F), BF16),
                        pltpu.VMEM((F, D), BF16)],
    )
    return pl.pallas_call(
        functools.partial(_gmm_kernel, bm=bm),
        grid_spec=grid_spec,
        out_shape=jax.ShapeDtypeStruct((A, D // 2), jnp.uint32),
        compiler_params=_cparams(("arbitrary",)),
        name="gmm",
    )(*items, xs, w1, w3, w2)


def _work_items(counts, bm, n_blocks):
    n_items = n_blocks + N_EXPERTS - 1
    ends = jnp.cumsum(counts)
    starts = ends - counts
    nb = jnp.where(counts > 0, (ends - 1) // bm - starts // bm + 1, 0)
    item_end = jnp.cumsum(nb)
    item_start = item_end - nb
    n_real = item_end[-1]
    i = jnp.arange(n_items, dtype=jnp.int32)
    e = jnp.minimum(jnp.sum(item_end[None, :] <= i[:, None], axis=1), N_EXPERTS - 1).astype(jnp.int32)
    onehot = e[:, None] == jnp.arange(N_EXPERTS, dtype=jnp.int32)[None, :]
    pick = lambda v: jnp.sum(jnp.where(onehot, v[None, :], 0), axis=1)
    start_e, end_e = pick(starts), pick(ends)
    blk = start_e // bm + (i - pick(item_start))
    lo = jnp.clip(start_e - blk * bm, 0, bm)
    hi = jnp.clip(end_e - blk * bm, 0, bm)
    real = i < n_real
    blk = jnp.where(real, blk, n_blocks - 1).astype(jnp.int32)
    lo = jnp.where(real, lo, 0).astype(jnp.int32)
    hi = jnp.where(real, hi, 0).astype(jnp.int32)
    one = jnp.ones((1,), jnp.int32)
    first = jnp.concatenate([one, (blk[1:] != blk[:-1]).astype(jnp.int32)])
    new_e = jnp.concatenate([one, (e[1:] != e[:-1]).astype(jnp.int32)])
    return blk, e, lo, hi, first, new_e


def _combine_kernel(dest_ref, ys_ref, w_ref, x1_ref, h_ref, w1_ref, w3_ref, w2_ref,
                    gt2_ref, g_ref, o_ref, buf, sem, *, tc):
    def issue(j, carry):
        for kk in range(TOP_K):
            _row_copy(ys_ref, dest_ref[0, kk * tc + j], buf.at[kk], j, sem).start()
        return carry

    lax.fori_loop(0, tc, issue, 0)

    h_lo, h_hi = _unpack_rows(h_ref[...])
    h_lo = h_lo.astype(BF16)
    h_hi = h_hi.astype(BF16)
    half = h_lo.shape[1]
    h1 = (jnp.dot(h_lo, w1_ref[:half, :], preferred_element_type=F32)
          + jnp.dot(h_hi, w1_ref[half:, :], preferred_element_type=F32))
    h3 = (jnp.dot(h_lo, w3_ref[:half, :], preferred_element_type=F32)
          + jnp.dot(h_hi, w3_ref[half:, :], preferred_element_type=F32))
    y = jnp.dot((h1 * jax.nn.sigmoid(h1) * h3).astype(BF16), w2_ref[...],
                preferred_element_type=F32)

    for kk in range(TOP_K):
        pltpu.make_async_copy(ys_ref.at[pl.ds(0, tc), :], buf.at[kk], sem).wait()
    moe_lo = None
    for kk in range(TOP_K):
        e_lo, e_hi = _unpack_rows(buf[kk])
        wk = w_ref[:, kk:kk + 1]
        moe_lo = wk * e_lo if kk == 0 else moe_lo + wk * e_lo
        moe_hi = wk * e_hi if kk == 0 else moe_hi + wk * e_hi
    moe = jnp.concatenate([moe_lo, moe_hi], axis=1)
    o_ref[...] = x1_ref[...] + gt2_ref[...] * _rms(moe + y, g_ref[...])


def _combine(ys, dest_tiles, w_tok, x1, h2, w1s, w3s, w2s, mod4, g_post, S, tc):
    T, D = x1.shape
    F = w1s.shape[1]
    per_b = S // tc
    row = lambda: pl.BlockSpec((tc, D), lambda i: (i, 0))
    return pl.pallas_call(
        functools.partial(_combine_kernel, tc=tc),
        grid=(T // tc,),
        in_specs=[pl.BlockSpec((None, 1, TOP_K * tc), lambda i: (i, 0, 0),
                               memory_space=pltpu.SMEM),
                  pl.BlockSpec(memory_space=pl.ANY),
                  pl.BlockSpec((tc, TOP_K), lambda i: (i, 0)),
                  row(), pl.BlockSpec((tc, D // 2), lambda i: (i, 0)),
                  pl.BlockSpec((D, F), lambda i: (0, 0)),
                  pl.BlockSpec((D, F), lambda i: (0, 0)),
                  pl.BlockSpec((F, D), lambda i: (0, 0)),
                  pl.BlockSpec((None, None, 1, D), lambda i: (i // per_b, 5, 0, 0)),
                  pl.BlockSpec((1, D), lambda i: (0, 0))],
        out_specs=row(),
        out_shape=jax.ShapeDtypeStruct((T, D), F32),
        scratch_shapes=[pltpu.VMEM((TOP_K, tc, D // 2), jnp.uint32), pltpu.SemaphoreType.DMA],
        compiler_params=_cparams(("arbitrary",)),
        name="combine",
    )(dest_tiles, ys, w_tok, x1, h2, w1s, w3s, w2s, mod4, g_post.reshape(1, D))


def _lambda_init(layer):
    return 0.8 - 0.6 * math.exp(-0.3 * layer)


def kernel(x, c, w_ada, b_ada, g_pre_mix, w_in, conv_w, conv_b, lru_wa, lru_ba, lru_wx, lru_bx,
           lru_lambda, lam_q1, lam_k1, lam_q2, lam_k2, g_subln, w_proj_rnn, w_proj_att, w_out,
           g_post_mix, g_pre_ffn, w_router, e_bias, w1_e, w3_e, w2_e, w1_s, w3_s, w2_s,
           g_post_ffn):
    B, S, D = x.shape
    T = B * S
    depth = w_ada.shape[0]
    slopes = np.exp2(-8.0 * np.arange(1, N_HEADS + 1, dtype=np.float32) / N_HEADS)
    kbias = _alibi_tables(slopes, S)
    tt = _tile(S, 256)
    bm = _tile(T * TOP_K, 512)
    n_blocks = T * TOP_K // bm

    x2 = x.reshape(T, D)
    for l in range(depth):
        lam_init = _lambda_init(l)
        mod4 = _ada(c, w_ada[l], b_ada[l]).reshape(B, 6, 1, D)

        proj = _inproj(x2, g_pre_mix[l], mod4, w_in[l].astype(BF16), S)
        yr = _lru(proj, conv_w[l], conv_b[l], lru_wa[l].astype(BF16), lru_ba[l],
                  lru_wx[l].astype(BF16), lru_bx[l], lru_lambda[l], B, S)
        lamv = jnp.stack([lam_q1[l], lam_k1[l], lam_q2[l], lam_k2[l]])
        ao = _attn(proj, kbias, lamv, g_subln[l], B, S, lam_init)
        x1, h2, h2p = _mixout(x2, yr, ao, proj, w_proj_rnn[l].astype(BF16),
                         w_proj_att[l].astype(BF16), w_out[l].astype(BF16),
                         g_post_mix[l], g_pre_ffn[l], mod4, S)

        top_e, top_w, rank, cnt = _router(h2, w_router[l].T, e_bias[l])
        counts = cnt[:, 0].astype(jnp.int32)
        offs = jnp.cumsum(counts) - counts
        dest = _slots(offs, top_e, rank)
        dest_tiles = dest.reshape(TOP_K, T // tt, tt).transpose(1, 0, 2).reshape(T // tt, 1,
                                                                                  TOP_K * tt)
        xs = _dispatch(h2p, dest_tiles, tt)
        items = _work_items(counts, bm, n_blocks)
        ys = _gmm(xs, items, w1_e[l], w3_e[l], w2_e[l], bm)
        x2 = _combine(ys, dest_tiles, top_w.T, x1, h2p, w1_s[l].astype(BF16),
                      w3_s[l].astype(BF16), w2_s[l].astype(BF16), mod4, g_post_ffn[l], S, tt)
    return x2.reshape(B, S, D)
```

```python
import functools
import math

import jax
import jax.numpy as jnp
import numpy as np
from jax import lax
from jax.experimental import pallas as pl
from jax.experimental.pallas import tpu as pltpu

F32 = jnp.float32
BF16 = jnp.bfloat16

EPS = 1e-6
N_HEADS = 8
HEAD_DIM = 64
V_DIM = 2 * HEAD_DIM
LRU_BLOCKS = 8
CONV_W = 4
LRU_C = 8.0
N_EXPERTS = 64
TOP_K = 8
N_GROUPS = 8
GROUP_SIZE = N_EXPERTS // N_GROUPS
TOPK_GROUPS = 4
ROUTED_SCALE = 2.5

LANES = 128
SUBLANES = 8
VMEM_LIMIT = 48 * 1024 * 1024


def _cparams(sem):
    return pltpu.CompilerParams(dimension_semantics=sem, vmem_limit_bytes=VMEM_LIMIT)


def _tile(n, pref):
    t = min(n, pref)
    while n % t:
        t //= 2
    return t


def _rms(x, g):
    return x * lax.rsqrt(jnp.mean(x * x, axis=-1, keepdims=True) + EPS) * g


def _ada_kernel(c_ref, w_ref, b_ref, o_ref):
    c = c_ref[...]
    cond = c * jax.nn.sigmoid(c)
    o_ref[...] = jnp.dot(cond, w_ref[...], preferred_element_type=F32) + b_ref[...]


def _ada(c, w, b):
    B, D = c.shape
    N = w.shape[1]
    tn = _tile(N, 1024)
    return pl.pallas_call(
        _ada_kernel,
        grid=(N // tn,),
        in_specs=[pl.BlockSpec((B, D), lambda j: (0, 0)),
                  pl.BlockSpec((D, tn), lambda j: (0, j)),
                  pl.BlockSpec((1, tn), lambda j: (0, j))],
        out_specs=pl.BlockSpec((B, tn), lambda j: (0, j)),
        out_shape=jax.ShapeDtypeStruct((B, N), F32),
        compiler_params=_cparams(("parallel",)),
        name="ada",
    )(c, w, b.reshape(1, N))


LOG2E = 1.4426950408889634
Q_COL_BLOCK = 2
Q_PRESCALE = HEAD_DIM ** -0.5 * LOG2E


def _inproj_kernel(x_ref, g_ref, sh_ref, sc_ref, w_ref, o_ref, h_scr):
    @pl.when(pl.program_id(1) == 0)
    def _():
        h = _rms(x_ref[...], g_ref[...]) * (1.0 + sc_ref[...]) + sh_ref[...]
        h_scr[...] = h.astype(BF16)

    r = jnp.dot(h_scr[...], w_ref[...], preferred_element_type=F32)
    r = r * jnp.where(pl.program_id(1) == Q_COL_BLOCK, Q_PRESCALE, 1.0)
    o_ref[...] = r.astype(o_ref.dtype)


def _inproj(x2, g, mod4, w_bf, S):
    T, D = x2.shape
    N = w_bf.shape[1]
    tm = _tile(S, 1024)
    tn = 1024
    per_b = S // tm
    return pl.pallas_call(
        _inproj_kernel,
        grid=(T // tm, N // tn),
        in_specs=[pl.BlockSpec((tm, D), lambda i, j: (i, 0)),
                  pl.BlockSpec((1, D), lambda i, j: (0, 0)),
                  pl.BlockSpec((None, None, 1, D), lambda i, j: (i // per_b, 0, 0, 0)),
                  pl.BlockSpec((None, None, 1, D), lambda i, j: (i // per_b, 1, 0, 0)),
                  pl.BlockSpec((D, tn), lambda i, j: (0, j))],
        out_specs=pl.BlockSpec((tm, tn), lambda i, j: (i, j)),
        out_shape=jax.ShapeDtypeStruct((T, N), BF16),
        scratch_shapes=[pltpu.VMEM((tm, D), BF16)],
        compiler_params=_cparams(("parallel", "arbitrary")),
        name="inproj",
    )(x2, g.reshape(1, D), mod4, mod4, w_bf)


def _lru_kernel(xr_ref, gr_ref, cw_ref, cb_ref, wa_ref, ba_ref, wx_ref, bx_ref, lam_ref,
                o_ref, xc_scr, prev_scr, h_scr, *, ts):
    s = pl.program_id(1)

    @pl.when(s == 0)
    def _():
        prev_scr[...] = jnp.zeros_like(prev_scr)
        h_scr[...] = jnp.zeros_like(h_scr)

    x = xr_ref[...].astype(F32)
    prev = prev_scr[...]
    row8 = lax.broadcasted_iota(jnp.int32, (SUBLANES, 1), 0)
    acc = cb_ref[...] + cw_ref[CONV_W - 1:CONV_W, :] * x
    xc_scr[...] = acc
    top = cb_ref[...] + cw_ref[CONV_W - 1:CONV_W, :] * x[0:SUBLANES, :]
    for j in range(1, CONV_W):
        wj = cw_ref[CONV_W - 1 - j:CONV_W - j, :]
        rj = pltpu.roll(x, j, axis=0)
        xc_scr[...] += wj * rj
        pj = pltpu.roll(prev, j, axis=0)
        top += wj * jnp.where(row8 < j, pj, rj[0:SUBLANES, :])
    xc_scr[0:SUBLANES, :] = top
    prev_scr[...] = x[ts - SUBLANES:ts, :]

    row = lax.broadcasted_iota(jnp.int32, (ts, 1), 0)
    is_first = jnp.logical_and(row == 0, s == 0)
    sub = lax.broadcasted_iota(jnp.int32, (1, SUBLANES, 1), 1)
    for n in range(LRU_BLOCKS):
        cols = slice(n * LANES, (n + 1) * LANES)
        xc = xc_scr[:, cols]
        xb = xc.astype(BF16)
        r = jax.nn.sigmoid(jnp.dot(xb, wa_ref[n], preferred_element_type=F32) + ba_ref[:, cols])
        i = jax.nn.sigmoid(jnp.dot(xb, wx_ref[n], preferred_element_type=F32) + bx_ref[:, cols])
        lam = lam_ref[:, cols]
        softplus_neg = jnp.maximum(-lam, 0.0) + jnp.log1p(jnp.exp(-jnp.abs(lam)))
        log_a = (-LRU_C * softplus_neg) * r
        a = jnp.exp(log_a)
        m2 = 1.0 - a * a
        mult = jnp.where(m2 > 0.0, m2 * lax.rsqrt(m2), 0.0)
        mult = jnp.where(is_first, 1.0, mult)
        u = mult * (i * xc)
        a = a.reshape(ts // SUBLANES, SUBLANES, LANES)
        u = u.reshape(ts // SUBLANES, SUBLANES, LANES)
        for d in (1, 2, 4):
            keep = sub >= d
            a_sh = jnp.where(keep, pltpu.roll(a, d, axis=1), 1.0)
            u_sh = jnp.where(keep, pltpu.roll(u, d, axis=1), 0.0)
            u = u + a * u_sh
            a = a * a_sh
        a = a.reshape(ts, LANES)
        u = u.reshape(ts, LANES)
        gate = jax.nn.gelu(gr_ref[:, cols].astype(F32))
        carry = h_scr[:, cols]
        step = 2 * SUBLANES
        for g in range(ts // step):
            r0 = g * step
            h0 = u[r0:r0 + SUBLANES, :] + a[r0:r0 + SUBLANES, :] * carry
            carry = h0[SUBLANES - 1:SUBLANES, :]
            h1 = u[r0 + SUBLANES:r0 + step, :] + a[r0 + SUBLANES:r0 + step, :] * carry
            carry = h1[SUBLANES - 1:SUBLANES, :]
            hg = jnp.concatenate([h0, h1], axis=0) * gate[r0:r0 + step, :]
            o_ref[r0:r0 + step, cols] = hg.astype(o_ref.dtype)
        h_scr[:, cols] = carry


def _lru(proj, conv_w, conv_b, wa_bf, ba, wx_bf, bx, lam, B, S):
    C = conv_w.shape[1]
    ts = _tile(S, 256)
    ns = S // ts
    vec = lambda: pl.BlockSpec((1, C), lambda b, s: (0, 0))
    blk = lambda: pl.BlockSpec((LRU_BLOCKS, LANES, LANES), lambda b, s: (0, 0, 0))
    return pl.pallas_call(
        functools.partial(_lru_kernel, ts=ts),
        grid=(B, ns),
        in_specs=[pl.BlockSpec((ts, C), lambda b, s: (b * ns + s, 0)),
                  pl.BlockSpec((ts, C), lambda b, s: (b * ns + s, 1)),
                  pl.BlockSpec((CONV_W, C), lambda b, s: (0, 0)),
                  vec(), blk(), vec(), blk(), vec(), vec()],
        out_specs=pl.BlockSpec((ts, C), lambda b, s: (b * ns + s, 0)),
        out_shape=jax.ShapeDtypeStruct((B * S, C), BF16),
        scratch_shapes=[pltpu.VMEM((ts, C), F32), pltpu.VMEM((SUBLANES, C), F32),
                        pltpu.VMEM((1, C), F32)],
        compiler_params=_cparams(("parallel", "arbitrary")),
        name="lru",
    )(proj, proj, conv_w, conv_b.reshape(1, C), wa_bf, ba.reshape(1, C), wx_bf, bx.reshape(1, C),
      lam.reshape(1, C))


BIAS_LANES = 3
ACC_ROWS = V_DIM + 2 * SUBLANES


def _alibi_tables(slopes, S):
    def top16(x):
        return (x.view(np.uint32) & np.uint32(0xFFFF0000)).view(np.float32)

    pos = np.arange(S, dtype=np.float32)
    b = (slopes.astype(np.float32) * np.float32(LOG2E))[:, None] * pos[None, :]
    hi = top16(b)
    mid = top16(b - hi)
    lo = top16(b - hi - mid)
    half = np.zeros(b.shape + (HEAD_DIM,), np.float32)
    half[..., 0], half[..., 1], half[..., 2] = hi, mid, lo
    zero = np.zeros_like(half)
    table = np.stack([np.concatenate([zero, half], axis=-1),
                      np.concatenate([half, zero], axis=-1)], axis=1)
    return jnp.asarray(table, dtype=BF16)


def _attn_kernel(lamv_ref, gsub_ref, q_ref, k_ref, v_ref, kb_ref, o_ref,
                 vt_scr, kx_scr, qx_scr, s_scr, p_scr, a_scr, m_scr, acc_scr, *, tq, lam_init):
    qi = pl.program_id(2)
    tk = tq
    nk = vt_scr.shape[0]

    lane = lax.broadcasted_iota(jnp.int32, (1, V_DIM), 1)
    own = (jnp.where(lane < HEAD_DIM, 1.0, 0.0), jnp.where(lane >= HEAD_DIM, 1.0, 0.0))
    ones_row = (jnp.where(jnp.logical_and(lane >= HEAD_DIM, lane < HEAD_DIM + BIAS_LANES), 1.0, 0.0),
                jnp.where(lane < BIAS_LANES, 1.0, 0.0))

    @pl.when(qi == 0)
    def _():
        for j in range(nk):
            rows = slice(j * tk, (j + 1) * tk)
            vt_scr[j, 0:V_DIM, :] = v_ref[rows, :].astype(F32).T.astype(BF16)
            vt_scr[j, V_DIM:ACC_ROWS, :] = jnp.ones((ACC_ROWS - V_DIM, tk), BF16)
            kf = k_ref[rows, :].astype(F32)
            for mp in range(2):
                kx_scr[mp, rows, :] = (kf * own[mp] + kb_ref[mp, rows, :].astype(F32)).astype(BF16)

    q = q_ref[...].astype(F32)
    for mp in range(2):
        qx_scr[mp] = (q * own[mp] + ones_row[mp]).astype(BF16)
    m_scr[...] = jnp.full_like(m_scr, -jnp.inf)
    acc_scr[...] = jnp.zeros_like(acc_scr)
    row8 = lax.broadcasted_iota(jnp.int32, (SUBLANES, LANES), 0)
    col = lax.broadcasted_iota(jnp.int32, (1, LANES), 1)
    neg_inf = jnp.float32(-jnp.inf)

    def tree(op, parts):
        parts = [p for p in parts if p is not None]
        while len(parts) > 1:
            parts = [op(parts[i], parts[i + 1]) if i + 1 < len(parts) else parts[i]
                     for i in range(0, len(parts), 2)]
        return parts[0]

    def scores(j, par):
        start = pl.multiple_of(j * tk, tk)
        for mp in range(2):
            s = lax.dot_general(kx_scr[mp, pl.ds(start, tk), :], qx_scr[mp],
                                (((1,), (1,)), ((), ())), preferred_element_type=F32)
            for c in range(tq // LANES):
                s_scr[par, mp, c] = s[:, c * LANES:(c + 1) * LANES]

    def softmax(par, masked):
        for mp in range(2):
            for c in range(tq // LANES):
                cols = slice(c * LANES, (c + 1) * LANES)
                lo_col, hi_col = c * LANES, (c + 1) * LANES - 1
                s_c = s_scr.at[par, mp, c]
                p_c = p_scr.at[par, mp, c]
                accs = [None] * 4
                for i in range(tk // SUBLANES):
                    r0 = i * SUBLANES
                    if masked and r0 > hi_col:
                        continue
                    t = s_c[r0:r0 + SUBLANES, :]
                    if masked and r0 + SUBLANES - 1 > lo_col:
                        t = jnp.where(row8 + r0 <= col + lo_col, t, neg_inf)
                        s_c[r0:r0 + SUBLANES, :] = t
                    accs[i % 4] = t if accs[i % 4] is None else jnp.maximum(accs[i % 4], t)
                mx = jnp.max(tree(jnp.maximum, accs), axis=0, keepdims=True)
                m_old = m_scr[mp, :, cols]
                m_new = jnp.maximum(m_old, mx)
                a_scr[par, mp, :, cols] = jnp.exp2(m_old - m_new)
                m_scr[mp, :, cols] = m_new
                for i in range(tk // (2 * SUBLANES)):
                    r0 = i * 2 * SUBLANES
                    if masked and r0 > hi_col:
                        p_c[r0:r0 + 2 * SUBLANES, :] = jnp.zeros((2 * SUBLANES, LANES), BF16)
                        continue
                    p = jnp.exp2(s_c[r0:r0 + 2 * SUBLANES, :] - m_new)
                    p_c[r0:r0 + 2 * SUBLANES, :] = p.astype(BF16)

    def values(j, par):
        for mp in range(2):
            p = jnp.concatenate([p_scr[par, mp, c] for c in range(tq // LANES)], axis=1)
            pv = jnp.dot(vt_scr[j], p, preferred_element_type=F32)
            acc_scr[mp] = a_scr[par, mp] * acc_scr[mp] + pv

    p_scr[1] = jnp.zeros(p_scr.shape[1:], BF16)
    a_scr[1] = jnp.ones(a_scr.shape[1:], F32)
    scores(0, 0)

    def stage(j, par):
        scores(j + 1, 1 - par)
        softmax(par, False)
        values(jnp.maximum(j - 1, 0), 1 - par)

    def stage_pair(i, carry):
        stage(2 * i, 0)
        stage(2 * i + 1, 1)
        return carry

    def tail(par):
        softmax(par, True)
        values(jnp.maximum(qi - 1, 0), 1 - par)
        values(qi, par)

    lax.fori_loop(0, qi // 2, stage_pair, 0)

    @pl.when(qi % 2 == 1)
    def _():
        stage(qi - 1, 0)
        tail(1)

    @pl.when(qi % 2 == 0)
    def _():
        tail(0)

    lv = lamv_ref[...]
    lam = (jnp.exp(jnp.sum(lv[0:1, :] * lv[1:2, :], axis=-1, keepdims=True))
           - jnp.exp(jnp.sum(lv[2:3, :] * lv[3:4, :], axis=-1, keepdims=True)) + lam_init)
    o0 = acc_scr[0, 0:V_DIM, :] * (1.0 / acc_scr[0, V_DIM:V_DIM + 1, :])
    o1 = acc_scr[1, 0:V_DIM, :] * (1.0 / acc_scr[1, V_DIM:V_DIM + 1, :])
    o_t = o0 - lam * o1
    o_t = o_t * lax.rsqrt(jnp.mean(o_t * o_t, axis=0, keepdims=True) + EPS) * gsub_ref[...]
    o_ref[...] = (o_t * (1.0 - lam_init)).T.astype(o_ref.dtype)


def _attn(proj, kbias, lamv, g_subln, B, S, lam_init):
    tq = _tile(S, 512)
    nq = S // tq
    qc, kc, vc = 2 * 8, 3 * 8, 4 * 8
    return pl.pallas_call(
        functools.partial(_attn_kernel, tq=tq, lam_init=lam_init),
        grid=(B, N_HEADS, nq),
        in_specs=[pl.BlockSpec((4, HEAD_DIM), lambda b, h, qi: (0, 0)),
                  pl.BlockSpec((V_DIM, 1), lambda b, h, qi: (0, 0)),
                  pl.BlockSpec((tq, V_DIM), lambda b, h, qi: (b * nq + qi, qc + h)),
                  pl.BlockSpec((S, V_DIM), lambda b, h, qi: (b, kc + h)),
                  pl.BlockSpec((S, V_DIM), lambda b, h, qi: (b, vc + h)),
                  pl.BlockSpec((None, 2, S, V_DIM), lambda b, h, qi: (h, 0, 0, 0))],
        out_specs=pl.BlockSpec((tq, V_DIM), lambda b, h, qi: (b * nq + qi, h)),
        out_shape=jax.ShapeDtypeStruct((B * S, N_HEADS * V_DIM), BF16),
        scratch_shapes=[pltpu.VMEM((nq, ACC_ROWS, tq), BF16), pltpu.VMEM((2, S, V_DIM), BF16),
                        pltpu.VMEM((2, tq, V_DIM), BF16),
                        pltpu.VMEM((2, 2, tq // LANES, tq, LANES), F32),
                        pltpu.VMEM((2, 2, tq // LANES, tq, LANES), BF16),
                        pltpu.VMEM((2, 2, 1, tq), F32), pltpu.VMEM((2, 1, tq), F32),
                        pltpu.VMEM((2, ACC_ROWS, tq), F32)],
        compiler_params=_cparams(("parallel", "parallel", "arbitrary")),
        name="attn",
    )(lamv, g_subln.reshape(V_DIM, 1), proj, proj, proj, kbias)


def _pack_rows(x):
    half = x.shape[1] // 2
    lo = lax.bitcast_convert_type(x[:, :half].astype(BF16).astype(F32), jnp.uint32)
    hi = lax.bitcast_convert_type(x[:, half:].astype(BF16).astype(F32), jnp.uint32)
    return (hi & jnp.uint32(0xFFFF0000)) | (lo >> 16)


def _unpack_rows(u):
    lo = lax.bitcast_convert_type(u << 16, F32)
    hi = lax.bitcast_convert_type(u & jnp.uint32(0xFFFF0000), F32)
    return lo, hi


def _mixout_kernel(x_ref, yr_ref, ao_ref, ga_ref, gb_ref, wr_ref, wa_ref, wo_ref,
                   gpost_ref, gt1_ref, gpre_ref, sh2_ref, sc2_ref, x1_ref, h2_ref, h2p_ref):
    ya = jnp.dot(yr_ref[...], wr_ref[...], preferred_element_type=F32)
    yb = jnp.dot(ao_ref[...], wa_ref[...], preferred_element_type=F32)
    merged = (jax.nn.sigmoid(ga_ref[...].astype(F32)) * ya
              + jax.nn.sigmoid(gb_ref[...].astype(F32)) * yb)
    y = jnp.dot(merged.astype(BF16), wo_ref[...], preferred_element_type=F32)
    x1 = x_ref[...] + gt1_ref[...] * _rms(y, gpost_ref[...])
    x1_ref[...] = x1
    h2 = _rms(x1, gpre_ref[...]) * (1.0 + sc2_ref[...]) + sh2_ref[...]
    h2_ref[...] = h2
    h2p_ref[...] = _pack_rows(h2)


def _mixout(x2, yr, ao, proj, wr_bf, wa_bf, wo_bf, g_post, g_pre, mod4, S):
    T, D = x2.shape
    tm = _tile(S, 512)
    per_b = S // tm
    gac, gbc = 5, 6
    row = lambda: pl.BlockSpec((tm, D), lambda i: (i, 0))
    wsp = lambda: pl.BlockSpec((D, D), lambda i: (0, 0))
    vec = lambda: pl.BlockSpec((1, D), lambda i: (0, 0))
    modv = lambda j: pl.BlockSpec((None, None, 1, D), lambda i: (i // per_b, j, 0, 0))
    return pl.pallas_call(
        _mixout_kernel,
        grid=(T // tm,),
        in_specs=[row(), row(), row(),
                  pl.BlockSpec((tm, D), lambda i: (i, gac)),
                  pl.BlockSpec((tm, D), lambda i: (i, gbc)),
                  wsp(), wsp(), wsp(), vec(), modv(2), vec(), modv(3), modv(4)],
        out_specs=[row(), row(), pl.BlockSpec((tm, D // 2), lambda i: (i, 0))],
        out_shape=[jax.ShapeDtypeStruct((T, D), F32), jax.ShapeDtypeStruct((T, D), F32),
                   jax.ShapeDtypeStruct((T, D // 2), jnp.uint32)],
        compiler_params=_cparams(("parallel",)),
        name="mixout",
    )(x2, yr, ao, proj, proj, wr_bf, wa_bf, wo_bf, g_post.reshape(1, D), mod4,
      g_pre.reshape(1, D), mod4, mod4)


def _first_argmax(vals, ids, sentinel):
    m = jnp.max(vals, axis=0, keepdims=True)
    idx = jnp.min(jnp.where(vals == m, ids, sentinel), axis=0, keepdims=True)
    return m, idx


def _router_kernel(h_ref, wr_ref, eb_ref, e_ref, w_ref, r_ref, cnt_ref, cnt_scr, *, tm):
    step = pl.program_id(0)

    @pl.when(step == 0)
    def _():
        cnt_scr[...] = jnp.zeros_like(cnt_scr)

    logits = lax.dot_general(wr_ref[...], h_ref[...], (((1,), (1,)), ((), ())),
                             preferred_element_type=F32, precision=lax.Precision.HIGHEST)
    scores = jax.nn.sigmoid(logits)
    choice = scores + eb_ref[...]
    i8 = lax.broadcasted_iota(jnp.int32, (GROUP_SIZE, tm), 0)
    neg_inf = jnp.float32(-jnp.inf)

    slabs = [choice[g * GROUP_SIZE:(g + 1) * GROUP_SIZE, :] for g in range(N_GROUPS)]
    sc_slabs = [scores[g * GROUP_SIZE:(g + 1) * GROUP_SIZE, :] for g in range(N_GROUPS)]

    gs = jnp.zeros((N_GROUPS, tm), F32)
    for g in range(N_GROUPS):
        m1, idx1 = _first_argmax(slabs[g], i8, GROUP_SIZE)
        m2 = jnp.max(jnp.where(i8 == idx1, neg_inf, slabs[g]), axis=0, keepdims=True)
        gs = jnp.where(i8 == g, m1 + m2, gs)

    sel = jnp.zeros((N_GROUPS, tm), jnp.int32)
    cur = gs
    for _ in range(TOPK_GROUPS):
        _, idx = _first_argmax(cur, i8, N_GROUPS)
        hit = i8 == idx
        sel = jnp.where(hit, 1, sel)
        cur = jnp.where(hit, neg_inf, cur)

    masked = [jnp.where(sel[g:g + 1, :] > 0, slabs[g], neg_inf) for g in range(N_GROUPS)]
    ids = [i8 + g * GROUP_SIZE for g in range(N_GROUPS)]
    onehot = [jnp.zeros((GROUP_SIZE, tm), F32) for _ in range(N_GROUPS)]
    picks = []
    wts = []
    for _ in range(TOP_K):
        m = functools.reduce(jnp.maximum,
                             [jnp.max(c, axis=0, keepdims=True) for c in masked])
        idx = functools.reduce(
            jnp.minimum,
            [jnp.min(jnp.where(c == m, i, N_EXPERTS), axis=0, keepdims=True)
             for c, i in zip(masked, ids)])
        w = jnp.zeros((1, tm), F32)
        for g in range(N_GROUPS):
            hit = ids[g] == idx
            w = w + jnp.sum(jnp.where(hit, sc_slabs[g], 0.0), axis=0, keepdims=True)
            masked[g] = jnp.where(hit, neg_inf, masked[g])
            onehot[g] = jnp.where(hit, 1.0, onehot[g])
        picks.append(idx)
        wts.append(w)

    wsum = functools.reduce(lambda a, b: a + b, wts)
    norm = ROUTED_SCALE / (wsum + 1e-20)

    t_row = lax.broadcasted_iota(jnp.int32, (tm, tm), 0)
    t_col = lax.broadcasted_iota(jnp.int32, (tm, tm), 1)
    before = jnp.where(t_row < t_col, 1.0, 0.0).astype(BF16)
    cum = [jnp.dot(onehot[g].astype(BF16), before, preferred_element_type=F32)
           + cnt_scr[g * GROUP_SIZE:(g + 1) * GROUP_SIZE, :] for g in range(N_GROUPS)]

    for kk in range(TOP_K):
        rank = jnp.zeros((1, tm), F32)
        for g in range(N_GROUPS):
            rank = rank + jnp.sum(jnp.where(ids[g] == picks[kk], cum[g], 0.0),
                                  axis=0, keepdims=True)
        e_ref[kk:kk + 1, :] = picks[kk]
        w_ref[kk:kk + 1, :] = wts[kk] * norm
        r_ref[kk:kk + 1, :] = rank.astype(jnp.int32)

    for g in range(N_GROUPS):
        rows = slice(g * GROUP_SIZE, (g + 1) * GROUP_SIZE)
        cnt_scr[rows, :] = cnt_scr[rows, :] + jnp.sum(onehot[g], axis=1, keepdims=True)
    cnt_ref[...] = jnp.broadcast_to(cnt_scr[...], cnt_ref.shape)


def _router(h2, w_router_t, e_bias):
    T, D = h2.shape
    tm = _tile(T, 512)
    return pl.pallas_call(
        functools.partial(_router_kernel, tm=tm),
        grid=(T // tm,),
        in_specs=[pl.BlockSpec((tm, D), lambda i: (i, 0)),
                  pl.BlockSpec((N_EXPERTS, D), lambda i: (0, 0)),
                  pl.BlockSpec((N_EXPERTS, 1), lambda i: (0, 0))],
        out_specs=[pl.BlockSpec((TOP_K, tm), lambda i: (0, i)),
                   pl.BlockSpec((TOP_K, tm), lambda i: (0, i)),
                   pl.BlockSpec((TOP_K, tm), lambda i: (0, i)),
                   pl.BlockSpec((N_EXPERTS, LANES), lambda i: (0, 0))],
        out_shape=[jax.ShapeDtypeStruct((TOP_K, T), jnp.int32),
                   jax.ShapeDtypeStruct((TOP_K, T), F32),
                   jax.ShapeDtypeStruct((TOP_K, T), jnp.int32),
                   jax.ShapeDtypeStruct((N_EXPERTS, LANES), F32)],
        scratch_shapes=[pltpu.VMEM((N_EXPERTS, 1), F32)],
        compiler_params=_cparams(("arbitrary",)),
        name="router",
    )(h2, w_router_t, e_bias.reshape(N_EXPERTS, 1))


def _slots_kernel(offs_ref, e_ref, r_ref, d_ref):
    e = e_ref[...]
    d = r_ref[...]
    for ex in range(N_EXPERTS):
        d = d + jnp.where(e == ex, offs_ref[ex], 0)
    d_ref[...] = d


def _slots(offs, top_e, rank):
    K, T = top_e.shape
    tm = _tile(T, 4096)
    grid_spec = pltpu.PrefetchScalarGridSpec(
        num_scalar_prefetch=1,
        grid=(T // tm,),
        in_specs=[pl.BlockSpec((K, tm), lambda i, o: (0, i)),
                  pl.BlockSpec((K, tm), lambda i, o: (0, i))],
        out_specs=pl.BlockSpec((K, tm), lambda i, o: (0, i)),
    )
    return pl.pallas_call(
        _slots_kernel,
        grid_spec=grid_spec,
        out_shape=jax.ShapeDtypeStruct((K, T), jnp.int32),
        compiler_params=_cparams(("parallel",)),
        name="slots",
    )(offs, top_e, rank)


def _row_copy(src, src_row, dst, dst_row, sem):
    return pltpu.make_async_copy(src.at[pl.ds(src_row, 1), :], dst.at[pl.ds(dst_row, 1), :], sem)


def _dispatch_kernel(dest_ref, h_ref, xs_ref, sem, *, td):
    def issue(j, carry):
        for kk in range(TOP_K):
            _row_copy(h_ref, j, xs_ref, dest_ref[0, kk * td + j], sem).start()
        return carry

    lax.fori_loop(0, td, issue, 0)
    for _ in range(TOP_K):
        pltpu.make_async_copy(h_ref, xs_ref.at[pl.ds(0, td), :], sem).wait()


def _dispatch(h2, dest_tiles, td):
    T, D = h2.shape
    nt = T // td
    return pl.pallas_call(
        functools.partial(_dispatch_kernel, td=td),
        grid=(nt,),
        in_specs=[pl.BlockSpec((None, 1, TOP_K * td), lambda i: (i, 0, 0),
                               memory_space=pltpu.SMEM),
                  pl.BlockSpec((td, D), lambda i: (i, 0))],
        out_specs=pl.BlockSpec(memory_space=pl.ANY),
        out_shape=jax.ShapeDtypeStruct((T * TOP_K, D), h2.dtype),
        scratch_shapes=[pltpu.SemaphoreType.DMA],
        compiler_params=_cparams(("arbitrary",)),
        name="dispatch",
    )(dest_tiles, h2)


def _gmm_kernel(blk_ref, exp_ref, lo_ref, hi_ref, first_ref, newe_ref, x_ref, w1_ref, w3_ref,
                w2_ref, o_ref, w1b, w3b, w2b, *, bm):
    it = pl.program_id(0)

    @pl.when(newe_ref[it] == 1)
    def _():
        w1b[...] = w1_ref[...].astype(BF16)
        w3b[...] = w3_ref[...].astype(BF16)
        w2b[...] = w2_ref[...].astype(BF16)

    rows = lax.broadcasted_iota(jnp.int32, (bm, 1), 0)
    valid = jnp.logical_and(rows >= lo_ref[it], rows < hi_ref[it])
    x_lo, x_hi = _unpack_rows(jnp.where(valid, x_ref[...], jnp.uint32(0)))
    x_lo = x_lo.astype(BF16)
    x_hi = x_hi.astype(BF16)
    half = x_lo.shape[1]
    h1 = (jnp.dot(x_lo, w1b[:half, :], preferred_element_type=F32)
          + jnp.dot(x_hi, w1b[half:, :], preferred_element_type=F32))
    h3 = (jnp.dot(x_lo, w3b[:half, :], preferred_element_type=F32)
          + jnp.dot(x_hi, w3b[half:, :], preferred_element_type=F32))
    hb = (h1 * jax.nn.sigmoid(h1) * h3).astype(BF16)
    y = jnp.dot(hb, w2b[...], preferred_element_type=F32)

    @pl.when(first_ref[it] == 1)
    def _():
        o_ref[...] = _pack_rows(y)

    @pl.when(first_ref[it] == 0)
    def _():
        o_lo, o_hi = _unpack_rows(o_ref[...])
        o_ref[...] = _pack_rows(y + jnp.concatenate([o_lo, o_hi], axis=1))


def _gmm(xs, items, w1, w3, w2, bm):
    A = xs.shape[0]
    D = w1.shape[1]
    F = w1.shape[2]
    n_items = items[0].shape[0]
    grid_spec = pltpu.PrefetchScalarGridSpec(
        num_scalar_prefetch=6,
        grid=(n_items,),
        in_specs=[pl.BlockSpec((bm, D // 2), lambda i, blk, ex, *_: (blk[i], 0)),
                  pl.BlockSpec((None, D, F), lambda i, blk, ex, *_: (ex[i], 0, 0)),
                  pl.BlockSpec((None, D, F), lambda i, blk, ex, *_: (ex[i], 0, 0)),
                  pl.BlockSpec((None, F, D), lambda i, blk, ex, *_: (ex[i], 0, 0))],
        out_specs=pl.BlockSpec((bm, D // 2), lambda i, blk, ex, *_: (blk[i], 0)),
        scratch_shapes=[pltpu.VMEM((D, F), BF16), pltpu.VMEM((D, F), BF16),
                        pltpu.VMEM((F, D), BF16)],
    )
    return pl.pallas_call(
        functools.partial(_gmm_kernel, bm=bm),
        grid_spec=grid_spec,
        out_shape=jax.ShapeDtypeStruct((A, D // 2), jnp.uint32),
        compiler_params=_cparams(("arbitrary",)),
        name="gmm",
    )(*items, xs, w1, w3, w2)


def _work_items(counts, bm, n_blocks):
    n_items = n_blocks + N_EXPERTS - 1
    ends = jnp.cumsum(counts)
    starts = ends - counts
    nb = jnp.where(counts > 0, (ends - 1) // bm - starts // bm + 1, 0)
    item_end = jnp.cumsum(nb)
    item_start = item_end - nb
    n_real = item_end[-1]
    i = jnp.arange(n_items, dtype=jnp.int32)
    e = jnp.minimum(jnp.sum(item_end[None, :] <= i[:, None], axis=1), N_EXPERTS - 1).astype(jnp.int32)
    onehot = e[:, None] == jnp.arange(N_EXPERTS, dtype=jnp.int32)[None, :]
    pick = lambda v: jnp.sum(jnp.where(onehot, v[None, :], 0), axis=1)
    start_e, end_e = pick(starts), pick(ends)
    blk = start_e // bm + (i - pick(item_start))
    lo = jnp.clip(start_e - blk * bm, 0, bm)
    hi = jnp.clip(end_e - blk * bm, 0, bm)
    real = i < n_real
    blk = jnp.where(real, blk, n_blocks - 1).astype(jnp.int32)
    lo = jnp.where(real, lo, 0).astype(jnp.int32)
    hi = jnp.where(real, hi, 0).astype(jnp.int32)
    one = jnp.ones((1,), jnp.int32)
    first = jnp.concatenate([one, (blk[1:] != blk[:-1]).astype(jnp.int32)])
    new_e = jnp.concatenate([one, (e[1:] != e[:-1]).astype(jnp.int32)])
    return blk, e, lo, hi, first, new_e


def _combine_kernel(dest_ref, ys_ref, w_ref, x1_ref, h_ref, w1_ref, w3_ref, w2_ref,
                    gt2_ref, g_ref, o_ref, buf, sem, *, tc):
    def issue(j, carry):
        for kk in range(TOP_K):
            _row_copy(ys_ref, dest_ref[0, kk * tc + j], buf.at[kk], j, sem).start()
        return carry

    lax.fori_loop(0, tc, issue, 0)

    h_lo, h_hi = _unpack_rows(h_ref[...])
    h_lo = h_lo.astype(BF16)
    h_hi = h_hi.astype(BF16)
    half = h_lo.shape[1]
    h1 = (jnp.dot(h_lo, w1_ref[:half, :], preferred_element_type=F32)
          + jnp.dot(h_hi, w1_ref[half:, :], preferred_element_type=F32))
    h3 = (jnp.dot(h_lo, w3_ref[:half, :], preferred_element_type=F32)
          + jnp.dot(h_hi, w3_ref[half:, :], preferred_element_type=F32))
    y = jnp.dot((h1 * jax.nn.sigmoid(h1) * h3).astype(BF16), w2_ref[...],
                preferred_element_type=F32)

    for kk in range(TOP_K):
        pltpu.make_async_copy(ys_ref.at[pl.ds(0, tc), :], buf.at[kk], sem).wait()
    moe_lo = None
    for kk in range(TOP_K):
        e_lo, e_hi = _unpack_rows(buf[kk])
        wk = w_ref[:, kk:kk + 1]
        moe_lo = wk * e_lo if kk == 0 else moe_lo + wk * e_lo
        moe_hi = wk * e_hi if kk == 0 else moe_hi + wk * e_hi
    moe = jnp.concatenate([moe_lo, moe_hi], axis=1)
    o_ref[...] = x1_ref[...] + gt2_ref[...] * _rms(moe + y, g_ref[...])


def _combine(ys, dest_tiles, w_tok, x1, h2, w1s, w3s, w2s, mod4, g_post, S, tc):
    T, D = x1.shape
    F = w1s.shape[1]
    per_b = S // tc
    row = lambda: pl.BlockSpec((tc, D), lambda i: (i, 0))
    return pl.pallas_call(
        functools.partial(_combine_kernel, tc=tc),
        grid=(T // tc,),
        in_specs=[pl.BlockSpec((None, 1, TOP_K * tc), lambda i: (i, 0, 0),
                               memory_space=pltpu.SMEM),
                  pl.BlockSpec(memory_space=pl.ANY),
                  pl.BlockSpec((tc, TOP_K), lambda i: (i, 0)),
                  row(), pl.BlockSpec((tc, D // 2), lambda i: (i, 0)),
                  pl.BlockSpec((D, F), lambda i: (0, 0)),
                  pl.BlockSpec((D, F), lambda i: (0, 0)),
                  pl.BlockSpec((F, D), lambda i: (0, 0)),
                  pl.BlockSpec((None, None, 1, D), lambda i: (i // per_b, 5, 0, 0)),
                  pl.BlockSpec((1, D), lambda i: (0, 0))],
        out_specs=row(),
        out_shape=jax.ShapeDtypeStruct((T, D), F32),
        scratch_shapes=[pltpu.VMEM((TOP_K, tc, D // 2), jnp.uint32), pltpu.SemaphoreType.DMA],
        compiler_params=_cparams(("arbitrary",)),
        name="combine",
    )(dest_tiles, ys, w_tok, x1, h2, w1s, w3s, w2s, mod4, g_post.reshape(1, D))


def _lambda_init(layer):
    return 0.8 - 0.6 * math.exp(-0.3 * layer)


def kernel(x, c, w_ada, b_ada, g_pre_mix, w_in, conv_w, conv_b, lru_wa, lru_ba, lru_wx, lru_bx,
           lru_lambda, lam_q1, lam_k1, lam_q2, lam_k2, g_subln, w_proj_rnn, w_proj_att, w_out,
           g_post_mix, g_pre_ffn, w_router, e_bias, w1_e, w3_e, w2_e, w1_s, w3_s, w2_s,
           g_post_ffn):
    B, S, D = x.shape
    T = B * S
    depth = w_ada.shape[0]
    slopes = np.exp2(-8.0 * np.arange(1, N_HEADS + 1, dtype=np.float32) / N_HEADS)
    kbias = _alibi_tables(slopes, S)
    tt = _tile(S, 256)
    bm = _tile(T * TOP_K, 512)
    n_blocks = T * TOP_K // bm

    x2 = x.reshape(T, D)
    for l in range(depth):
        lam_init = _lambda_init(l)
        mod4 = _ada(c, w_ada[l], b_ada[l]).reshape(B, 6, 1, D)

        proj = _inproj(x2, g_pre_mix[l], mod4, w_in[l].astype(BF16), S)
        yr = _lru(proj, conv_w[l], conv_b[l], lru_wa[l].astype(BF16), lru_ba[l],
                  lru_wx[l].astype(BF16), lru_bx[l], lru_lambda[l], B, S)
        lamv = jnp.stack([lam_q1[l], lam_k1[l], lam_q2[l], lam_k2[l]])
        ao = _attn(proj, kbias, lamv, g_subln[l], B, S, lam_init)
        x1, h2, h2p = _mixout(x2, yr, ao, proj, w_proj_rnn[l].astype(BF16),
                         w_proj_att[l].astype(BF16), w_out[l].astype(BF16),
                         g_post_mix[l], g_pre_ffn[l], mod4, S)

        top_e, top_w, rank, cnt = _router(h2, w_router[l].T, e_bias[l])
        counts = cnt[:, 0].astype(jnp.int32)
        offs = jnp.cumsum(counts) - counts
        dest = _slots(offs, top_e, rank)
        dest_tiles = dest.reshape(TOP_K, T // tt, tt).transpose(1, 0, 2).reshape(T // tt, 1,
                                                                                  TOP_K * tt)
        xs = _dispatch(h2p, dest_tiles, tt)
        items = _work_items(counts, bm, n_blocks)
        ys = _gmm(xs, items, w1_e[l], w3_e[l], w2_e[l], bm)
        x2 = _combine(ys, dest_tiles, top_w.T, x1, h2p, w1_s[l].astype(BF16),
                      w3_s[l].astype(BF16), w2_s[l].astype(BF16), mod4, g_post_ffn[l], S, tt)
    return x2.reshape(B, S, D)
```

```python
import functools
import math

import jax
import jax.numpy as jnp
import numpy as np
from jax import lax
from jax.experimental import pallas as pl
from jax.experimental.pallas import tpu as pltpu

F32 = jnp.float32
BF16 = jnp.bfloat16

EPS = 1e-6
N_HEADS = 8
HEAD_DIM = 64
V_DIM = 2 * HEAD_DIM
LRU_BLOCKS = 8
CONV_W = 4
LRU_C = 8.0
N_EXPERTS = 64
TOP_K = 8
N_GROUPS = 8
GROUP_SIZE = N_EXPERTS // N_GROUPS
TOPK_GROUPS = 4
ROUTED_SCALE = 2.5

LANES = 128
SUBLANES = 8
VMEM_LIMIT = 48 * 1024 * 1024


def _cparams(sem):
    return pltpu.CompilerParams(dimension_semantics=sem, vmem_limit_bytes=VMEM_LIMIT)


def _tile(n, pref):
    t = min(n, pref)
    while n % t:
        t //= 2
    return t


def _rms(x, g):
    return x * lax.rsqrt(jnp.mean(x * x, axis=-1, keepdims=True) + EPS) * g


def _ada_kernel(c_ref, w_ref, b_ref, o_ref):
    c = c_ref[...]
    cond = c * jax.nn.sigmoid(c)
    o_ref[...] = jnp.dot(cond, w_ref[...], preferred_element_type=F32) + b_ref[...]


def _ada(c, w, b):
    B, D = c.shape
    N = w.shape[1]
    tn = _tile(N, 1024)
    return pl.pallas_call(
        _ada_kernel,
        grid=(N // tn,),
        in_specs=[pl.BlockSpec((B, D), lambda j: (0, 0)),
                  pl.BlockSpec((D, tn), lambda j: (0, j)),
                  pl.BlockSpec((1, tn), lambda j: (0, j))],
        out_specs=pl.BlockSpec((B, tn), lambda j: (0, j)),
        out_shape=jax.ShapeDtypeStruct((B, N), F32),
        compiler_params=_cparams(("parallel",)),
        name="ada",
    )(c, w, b.reshape(1, N))


LOG2E = 1.4426950408889634
Q_COL_BLOCK = 2
Q_PRESCALE = HEAD_DIM ** -0.5 * LOG2E


def _inproj_kernel(x_ref, g_ref, sh_ref, sc_ref, w_ref, o_ref, h_scr):
    @pl.when(pl.program_id(1) == 0)
    def _():
        h = _rms(x_ref[...], g_ref[...]) * (1.0 + sc_ref[...]) + sh_ref[...]
        h_scr[...] = h.astype(BF16)

    r = jnp.dot(h_scr[...], w_ref[...], preferred_element_type=F32)
    r = r * jnp.where(pl.program_id(1) == Q_COL_BLOCK, Q_PRESCALE, 1.0)
    o_ref[...] = r.astype(o_ref.dtype)


def _inproj(x2, g, mod4, w_bf, S):
    T, D = x2.shape
    N = w_bf.shape[1]
    tm = _tile(S, 1024)
    tn = 1024
    per_b = S // tm
    return pl.pallas_call(
        _inproj_kernel,
        grid=(T // tm, N // tn),
        in_specs=[pl.BlockSpec((tm, D), lambda i, j: (i, 0)),
                  pl.BlockSpec((1, D), lambda i, j: (0, 0)),
                  pl.BlockSpec((None, None, 1, D), lambda i, j: (i // per_b, 0, 0, 0)),
                  pl.BlockSpec((None, None, 1, D), lambda i, j: (i // per_b, 1, 0, 0)),
                  pl.BlockSpec((D, tn), lambda i, j: (0, j))],
        out_specs=pl.BlockSpec((tm, tn), lambda i, j: (i, j)),
        out_shape=jax.ShapeDtypeStruct((T, N), BF16),
        scratch_shapes=[pltpu.VMEM((tm, D), BF16)],
        compiler_params=_cparams(("parallel", "arbitrary")),
        name="inproj",
    )(x2, g.reshape(1, D), mod4, mod4, w_bf)


def _lru_kernel(xr_ref, gr_ref, cw_ref, cb_ref, wa_ref, ba_ref, wx_ref, bx_ref, lam_ref,
                o_ref, xc_scr, prev_scr, h_scr, *, ts):
    s = pl.program_id(1)

    @pl.when(s == 0)
    def _():
        prev_scr[...] = jnp.zeros_like(prev_scr)
        h_scr[...] = jnp.zeros_like(h_scr)

    x = xr_ref[...].astype(F32)
    prev = prev_scr[...]
    row8 = lax.broadcasted_iota(jnp.int32, (SUBLANES, 1), 0)
    acc = cb_ref[...] + cw_ref[CONV_W - 1:CONV_W, :] * x
    xc_scr[...] = acc
    top = cb_ref[...] + cw_ref[CONV_W - 1:CONV_W, :] * x[0:SUBLANES, :]
    for j in range(1, CONV_W):
        wj = cw_ref[CONV_W - 1 - j:CONV_W - j, :]
        rj = pltpu.roll(x, j, axis=0)
        xc_scr[...] += wj * rj
        pj = pltpu.roll(prev, j, axis=0)
        top += wj * jnp.where(row8 < j, pj, rj[0:SUBLANES, :])
    xc_scr[0:SUBLANES, :] = top
    prev_scr[...] = x[ts - SUBLANES:ts, :]

    row = lax.broadcasted_iota(jnp.int32, (ts, 1), 0)
    is_first = jnp.logical_and(row == 0, s == 0)
    sub = lax.broadcasted_iota(jnp.int32, (1, SUBLANES, 1), 1)
    for n in range(LRU_BLOCKS):
        cols = slice(n * LANES, (n + 1) * LANES)
        xc = xc_scr[:, cols]
        xb = xc.astype(BF16)
        r = jax.nn.sigmoid(jnp.dot(xb, wa_ref[n], preferred_element_type=F32) + ba_ref[:, cols])
        i = jax.nn.sigmoid(jnp.dot(xb, wx_ref[n], preferred_element_type=F32) + bx_ref[:, cols])
        lam = lam_ref[:, cols]
        softplus_neg = jnp.maximum(-lam, 0.0) + jnp.log1p(jnp.exp(-jnp.abs(lam)))
        log_a = (-LRU_C * softplus_neg) * r
        a = jnp.exp(log_a)
        m2 = 1.0 - a * a
        mult = jnp.where(m2 > 0.0, m2 * lax.rsqrt(m2), 0.0)
        mult = jnp.where(is_first, 1.0, mult)
        u = mult * (i * xc)
        a = a.reshape(ts // SUBLANES, SUBLANES, LANES)
        u = u.reshape(ts // SUBLANES, SUBLANES, LANES)
        for d in (1, 2, 4):
            keep = sub >= d
            a_sh = jnp.where(keep, pltpu.roll(a, d, axis=1), 1.0)
            u_sh = jnp.where(keep, pltpu.roll(u, d, axis=1), 0.0)
            u = u + a * u_sh
            a = a * a_sh
        a = a.reshape(ts, LANES)
        u = u.reshape(ts, LANES)
        gate = jax.nn.gelu(gr_ref[:, cols].astype(F32))
        carry = h_scr[:, cols]
        step = 2 * SUBLANES
        for g in range(ts // step):
            r0 = g * step
            h0 = u[r0:r0 + SUBLANES, :] + a[r0:r0 + SUBLANES, :] * carry
            carry = h0[SUBLANES - 1:SUBLANES, :]
            h1 = u[r0 + SUBLANES:r0 + step, :] + a[r0 + SUBLANES:r0 + step, :] * carry
            carry = h1[SUBLANES - 1:SUBLANES, :]
            hg = jnp.concatenate([h0, h1], axis=0) * gate[r0:r0 + step, :]
            o_ref[r0:r0 + step, cols] = hg.astype(o_ref.dtype)
        h_scr[:, cols] = carry


def _lru(proj, conv_w, conv_b, wa_bf, ba, wx_bf, bx, lam, B, S):
    C = conv_w.shape[1]
    ts = _tile(S, 256)
    ns = S // ts
    vec = lambda: pl.BlockSpec((1, C), lambda b, s: (0, 0))
    blk = lambda: pl.BlockSpec((LRU_BLOCKS, LANES, LANES), lambda b, s: (0, 0, 0))
    return pl.pallas_call(
        functools.partial(_lru_kernel, ts=ts),
        grid=(B, ns),
        in_specs=[pl.BlockSpec((ts, C), lambda b, s: (b * ns + s, 0)),
                  pl.BlockSpec((ts, C), lambda b, s: (b * ns + s, 1)),
                  pl.BlockSpec((CONV_W, C), lambda b, s: (0, 0)),
                  vec(), blk(), vec(), blk(), vec(), vec()],
        out_specs=pl.BlockSpec((ts, C), lambda b, s: (b * ns + s, 0)),
        out_shape=jax.ShapeDtypeStruct((B * S, C), BF16),
        scratch_shapes=[pltpu.VMEM((ts, C), F32), pltpu.VMEM((SUBLANES, C), F32),
                        pltpu.VMEM((1, C), F32)],
        compiler_params=_cparams(("parallel", "arbitrary")),
        name="lru",
    )(proj, proj, conv_w, conv_b.reshape(1, C), wa_bf, ba.reshape(1, C), wx_bf, bx.reshape(1, C),
      lam.reshape(1, C))


BIAS_LANES = 3
ACC_ROWS = V_DIM + 2 * SUBLANES


def _alibi_tables(slopes, S):
    def top16(x):
        return (x.view(np.uint32) & np.uint32(0xFFFF0000)).view(np.float32)

    pos = np.arange(S, dtype=np.float32)
    b = (slopes.astype(np.float32) * np.float32(LOG2E))[:, None] * pos[None, :]
    hi = top16(b)
    mid = top16(b - hi)
    lo = top16(b - hi - mid)
    half = np.zeros(b.shape + (HEAD_DIM,), np.float32)
    half[..., 0], half[..., 1], half[..., 2] = hi, mid, lo
    zero = np.zeros_like(half)
    table = np.stack([np.concatenate([zero, half], axis=-1),
                      np.concatenate([half, zero], axis=-1)], axis=1)
    return jnp.asarray(table, dtype=BF16)


def _attn_kernel(lamv_ref, gsub_ref, q_ref, k_ref, v_ref, kb_ref, o_ref,
                 vt_scr, kx_scr, qx_scr, s_scr, p_scr, a_scr, m_scr, acc_scr, *, tq, lam_init):
    qi = pl.program_id(2)
    tk = tq
    nk = vt_scr.shape[0]

    lane = lax.broadcasted_iota(jnp.int32, (1, V_DIM), 1)
    own = (jnp.where(lane < HEAD_DIM, 1.0, 0.0), jnp.where(lane >= HEAD_DIM, 1.0, 0.0))
    ones_row = (jnp.where(jnp.logical_and(lane >= HEAD_DIM, lane < HEAD_DIM + BIAS_LANES), 1.0, 0.0),
                jnp.where(lane < BIAS_LANES, 1.0, 0.0))

    @pl.when(qi == 0)
    def _():
        for j in range(nk):
            rows = slice(j * tk, (j + 1) * tk)
            vt_scr[j, 0:V_DIM, :] = v_ref[rows, :].astype(F32).T.astype(BF16)
            vt_scr[j, V_DIM:ACC_ROWS, :] = jnp.ones((ACC_ROWS - V_DIM, tk), BF16)
            kf = k_ref[rows, :].astype(F32)
            for mp in range(2):
                kx_scr[mp, rows, :] = (kf * own[mp] + kb_ref[mp, rows, :].astype(F32)).astype(BF16)

    q = q_ref[...].astype(F32)
    for mp in range(2):
        qx_scr[mp] = (q * own[mp] + ones_row[mp]).astype(BF16)
    m_scr[...] = jnp.full_like(m_scr, -jnp.inf)
    acc_scr[...] = jnp.zeros_like(acc_scr)
    row8 = lax.broadcasted_iota(jnp.int32, (SUBLANES, LANES), 0)
    col = lax.broadcasted_iota(jnp.int32, (1, LANES), 1)
    neg_inf = jnp.float32(-jnp.inf)

    def tree(op, parts):
        parts = [p for p in parts if p is not None]
        while len(parts) > 1:
            parts = [op(parts[i], parts[i + 1]) if i + 1 < len(parts) else parts[i]
                     for i in range(0, len(parts), 2)]
        return parts[0]

    def scores(j, par):
        start = pl.multiple_of(j * tk, tk)
        for mp in range(2):
            s = lax.dot_general(kx_scr[mp, pl.ds(start, tk), :], qx_scr[mp],
                                (((1,), (1,)), ((), ())), preferred_element_type=F32)
            for c in range(tq // LANES):
                s_scr[par, mp, c] = s[:, c * LANES:(c + 1) * LANES]

    def softmax(par, masked):
        for mp in range(2):
            for c in range(tq // LANES):
                cols = slice(c * LANES, (c + 1) * LANES)
                lo_col, hi_col = c * LANES, (c + 1) * LANES - 1
                s_c = s_scr.at[par, mp, c]
                p_c = p_scr.at[par, mp, c]
                accs = [None] * 4
                for i in range(tk // SUBLANES):
                    r0 = i * SUBLANES
                    if masked and r0 > hi_col:
                        continue
                    t = s_c[r0:r0 + SUBLANES, :]
                    if masked and r0 + SUBLANES - 1 > lo_col:
                        t = jnp.where(row8 + r0 <= col + lo_col, t, neg_inf)
                        s_c[r0:r0 + SUBLANES, :] = t
                    accs[i % 4] = t if accs[i % 4] is None else jnp.maximum(accs[i % 4], t)
                mx = jnp.max(tree(jnp.maximum, accs), axis=0, keepdims=True)
                m_old = m_scr[mp, :, cols]
                m_new = jnp.maximum(m_old, mx)
                a_scr[par, mp, :, cols] = jnp.exp2(m_old - m_new)
                m_scr[mp, :, cols] = m_new
                for i in range(tk // (2 * SUBLANES)):
                    r0 = i * 2 * SUBLANES
                    if masked and r0 > hi_col:
                        p_c[r0:r0 + 2 * SUBLANES, :] = jnp.zeros((2 * SUBLANES, LANES), BF16)
                        continue
                    p = jnp.exp2(s_c[r0:r0 + 2 * SUBLANES, :] - m_new)
                    p_c[r0:r0 + 2 * SUBLANES, :] = p.astype(BF16)

    def values(j, par):
        for mp in range(2):
            p = jnp.concatenate([p_scr[par, mp, c] for c in range(tq // LANES)], axis=1)
            pv = jnp.dot(vt_scr[j], p, preferred_element_type=F32)
            acc_scr[mp] = a_scr[par, mp] * acc_scr[mp] + pv

    p_scr[1] = jnp.zeros(p_scr.shape[1:], BF16)
    a_scr[1] = jnp.ones(a_scr.shape[1:], F32)
    scores(0, 0)

    def stage(j, par):
        scores(j + 1, 1 - par)
        softmax(par, False)
        values(jnp.maximum(j - 1, 0), 1 - par)

    def stage_pair(i, carry):
        stage(2 * i, 0)
        stage(2 * i + 1, 1)
        return carry

    def tail(par):
        softmax(par, True)
        values(jnp.maximum(qi - 1, 0), 1 - par)
        values(qi, par)

    lax.fori_loop(0, qi // 2, stage_pair, 0)

    @pl.when(qi % 2 == 1)
    def _():
        stage(qi - 1, 0)
        tail(1)

    @pl.when(qi % 2 == 0)
    def _():
        tail(0)

    lv = lamv_ref[...]
    lam = (jnp.exp(jnp.sum(lv[0:1, :] * lv[1:2, :], axis=-1, keepdims=True))
           - jnp.exp(jnp.sum(lv[2:3, :] * lv[3:4, :], axis=-1, keepdims=True)) + lam_init)
    o0 = acc_scr[0, 0:V_DIM, :] * (1.0 / acc_scr[0, V_DIM:V_DIM + 1, :])
    o1 = acc_scr[1, 0:V_DIM, :] * (1.0 / acc_scr[1, V_DIM:V_DIM + 1, :])
    o_t = o0 - lam * o1
    o_t = o_t * lax.rsqrt(jnp.mean(o_t * o_t, axis=0, keepdims=True) + EPS) * gsub_ref[...]
    o_ref[...] = (o_t * (1.0 - lam_init)).T.astype(o_ref.dtype)


def _attn(proj, kbias, lamv, g_subln, B, S, lam_init):
    tq = _tile(S, 512)
    nq = S // tq
    qc, kc, vc = 2 * 8, 3 * 8, 4 * 8
    return pl.pallas_call(
        functools.partial(_attn_kernel, tq=tq, lam_init=lam_init),
        grid=(B, N_HEADS, nq),
        in_specs=[pl.BlockSpec((4, HEAD_DIM), lambda b, h, qi: (0, 0)),
                  pl.BlockSpec((V_DIM, 1), lambda b, h, qi: (0, 0)),
                  pl.BlockSpec((tq, V_DIM), lambda b, h, qi: (b * nq + qi, qc + h)),
                  pl.BlockSpec((S, V_DIM), lambda b, h, qi: (b, kc + h)),
                  pl.BlockSpec((S, V_DIM), lambda b, h, qi: (b, vc + h)),
                  pl.BlockSpec((None, 2, S, V_DIM), lambda b, h, qi: (h, 0, 0, 0))],
        out_specs=pl.BlockSpec((tq, V_DIM), lambda b, h, qi: (b * nq + qi, h)),
        out_shape=jax.ShapeDtypeStruct((B * S, N_HEADS * V_DIM), BF16),
        scratch_shapes=[pltpu.VMEM((nq, ACC_ROWS, tq), BF16), pltpu.VMEM((2, S, V_DIM), BF16),
                        pltpu.VMEM((2, tq, V_DIM), BF16),
                        pltpu.VMEM((2, 2, tq // LANES, tq, LANES), F32),
                        pltpu.VMEM((2, 2, tq // LANES, tq, LANES), BF16),
                        pltpu.VMEM((2, 2, 1, tq), F32), pltpu.VMEM((2, 1, tq), F32),
                        pltpu.VMEM((2, ACC_ROWS, tq), F32)],
        compiler_params=_cparams(("parallel", "parallel", "arbitrary")),
        name="attn",
    )(lamv, g_subln.reshape(V_DIM, 1), proj, proj, proj, kbias)


def _pack_rows(x):
    half = x.shape[1] // 2
    lo = lax.bitcast_convert_type(x[:, :half].astype(BF16).astype(F32), jnp.uint32)
    hi = lax.bitcast_convert_type(x[:, half:].astype(BF16).astype(F32), jnp.uint32)
    return (hi & jnp.uint32(0xFFFF0000)) | (lo >> 16)


def _unpack_rows(u):
    lo = lax.bitcast_convert_type(u << 16, F32)
    hi = lax.bitcast_convert_type(u & jnp.uint32(0xFFFF0000), F32)
    return lo, hi


def _mixout_kernel(x_ref, yr_ref, ao_ref, ga_ref, gb_ref, wr_ref, wa_ref, wo_ref,
                   gpost_ref, gt1_ref, gpre_ref, sh2_ref, sc2_ref, x1_ref, h2_ref, h2p_ref):
    ya = jnp.dot(yr_ref[...], wr_ref[...], preferred_element_type=F32)
    yb = jnp.dot(ao_ref[...], wa_ref[...], preferred_element_type=F32)
    merged = (jax.nn.sigmoid(ga_ref[...].astype(F32)) * ya
              + jax.nn.sigmoid(gb_ref[...].astype(F32)) * yb)
    y = jnp.dot(merged.astype(BF16), wo_ref[...], preferred_element_type=F32)
    x1 = x_ref[...] + gt1_ref[...] * _rms(y, gpost_ref[...])
    x1_ref[...] = x1
    h2 = _rms(x1, gpre_ref[...]) * (1.0 + sc2_ref[...]) + sh2_ref[...]
    h2_ref[...] = h2
    h2p_ref[...] = _pack_rows(h2)


def _mixout(x2, yr, ao, proj, wr_bf, wa_bf, wo_bf, g_post, g_pre, mod4, S):
    T, D = x2.shape
    tm = _tile(S, 512)
    per_b = S // tm
    gac, gbc = 5, 6
    row = lambda: pl.BlockSpec((tm, D), lambda i: (i, 0))
    wsp = lambda: pl.BlockSpec((D, D), lambda i: (0, 0))
    vec = lambda: pl.BlockSpec((1, D), lambda i: (0, 0))
    modv = lambda j: pl.BlockSpec((None, None, 1, D), lambda i: (i // per_b, j, 0, 0))
    return pl.pallas_call(
        _mixout_kernel,
        grid=(T // tm,),
        in_specs=[row(), row(), row(),
                  pl.BlockSpec((tm, D), lambda i: (i, gac)),
                  pl.BlockSpec((tm, D), lambda i: (i, gbc)),
                  wsp(), wsp(), wsp(), vec(), modv(2), vec(), modv(3), modv(4)],
        out_specs=[row(), row(), pl.BlockSpec((tm, D // 2), lambda i: (i, 0))],
        out_shape=[jax.ShapeDtypeStruct((T, D), F32), jax.ShapeDtypeStruct((T, D), F32),
                   jax.ShapeDtypeStruct((T, D // 2), jnp.uint32)],
        compiler_params=_cparams(("parallel",)),
        name="mixout",
    )(x2, yr, ao, proj, proj, wr_bf, wa_bf, wo_bf, g_post.reshape(1, D), mod4,
      g_pre.reshape(1, D), mod4, mod4)


def _first_argmax(vals, ids, sentinel):
    m = jnp.max(vals, axis=0, keepdims=True)
    idx = jnp.min(jnp.where(vals == m, ids, sentinel), axis=0, keepdims=True)
    return m, idx


def _router_kernel(h_ref, wr_ref, eb_ref, e_ref, w_ref, r_ref, cnt_ref, cnt_scr, *, tm):
    step = pl.program_id(0)

    @pl.when(step == 0)
    def _():
        cnt_scr[...] = jnp.zeros_like(cnt_scr)

    logits = lax.dot_general(wr_ref[...], h_ref[...], (((1,), (1,)), ((), ())),
                             preferred_element_type=F32, precision=lax.Precision.HIGHEST)
    scores = jax.nn.sigmoid(logits)
    choice = scores + eb_ref[...]
    i8 = lax.broadcasted_iota(jnp.int32, (GROUP_SIZE, tm), 0)
    neg_inf = jnp.float32(-jnp.inf)

    slabs = [choice[g * GROUP_SIZE:(g + 1) * GROUP_SIZE, :] for g in range(N_GROUPS)]
    sc_slabs = [scores[g * GROUP_SIZE:(g + 1) * GROUP_SIZE, :] for g in range(N_GROUPS)]

    gs = jnp.zeros((N_GROUPS, tm), F32)
    for g in range(N_GROUPS):
        m1, idx1 = _first_argmax(slabs[g], i8, GROUP_SIZE)
        m2 = jnp.max(jnp.where(i8 == idx1, neg_inf, slabs[g]), axis=0, keepdims=True)
        gs = jnp.where(i8 == g, m1 + m2, gs)

    sel = jnp.zeros((N_GROUPS, tm), jnp.int32)
    cur = gs
    for _ in range(TOPK_GROUPS):
        _, idx = _first_argmax(cur, i8, N_GROUPS)
        hit = i8 == idx
        sel = jnp.where(hit, 1, sel)
        cur = jnp.where(hit, neg_inf, cur)

    masked = [jnp.where(sel[g:g + 1, :] > 0, slabs[g], neg_inf) for g in range(N_GROUPS)]
    ids = [i8 + g * GROUP_SIZE for g in range(N_GROUPS)]
    onehot = [jnp.zeros((GROUP_SIZE, tm), F32) for _ in range(N_GROUPS)]
    picks = []
    wts = []
    for _ in range(TOP_K):
        m = functools.reduce(jnp.maximum,
                             [jnp.max(c, axis=0, keepdims=True) for c in masked])
        idx = functools.reduce(
            jnp.minimum,
            [jnp.min(jnp.where(c == m, i, N_EXPERTS), axis=0, keepdims=True)
             for c, i in zip(masked, ids)])
        w = jnp.zeros((1, tm), F32)
        for g in range(N_GROUPS):
            hit = ids[g] == idx
            w = w + jnp.sum(jnp.where(hit, sc_slabs[g], 0.0), axis=0, keepdims=True)
            masked[g] = jnp.where(hit, neg_inf, masked[g])
            onehot[g] = jnp.where(hit, 1.0, onehot[g])
        picks.append(idx)
        wts.append(w)

    wsum = functools.reduce(lambda a, b: a + b, wts)
    norm = ROUTED_SCALE / (wsum + 1e-20)

    t_row = lax.broadcasted_iota(jnp.int32, (tm, tm), 0)
    t_col = lax.broadcasted_iota(jnp.int32, (tm, tm), 1)
    before = jnp.where(t_row < t_col, 1.0, 0.0).astype(BF16)
    cum = [jnp.dot(onehot[g].astype(BF16), before, preferred_element_type=F32)
           + cnt_scr[g * GROUP_SIZE:(g + 1) * GROUP_SIZE, :] for g in range(N_GROUPS)]

    for kk in range(TOP_K):
        rank = jnp.zeros((1, tm), F32)
        for g in range(N_GROUPS):
            rank = rank + jnp.sum(jnp.where(ids[g] == picks[kk], cum[g], 0.0),
                                  axis=0, keepdims=True)
        e_ref[kk:kk + 1, :] = picks[kk]
        w_ref[kk:kk + 1, :] = wts[kk] * norm
        r_ref[kk:kk + 1, :] = rank.astype(jnp.int32)

    for g in range(N_GROUPS):
        rows = slice(g * GROUP_SIZE, (g + 1) * GROUP_SIZE)
        cnt_scr[rows, :] = cnt_scr[rows, :] + jnp.sum(onehot[g], axis=1, keepdims=True)
    cnt_ref[...] = jnp.broadcast_to(cnt_scr[...], cnt_ref.shape)


def _router(h2, w_router_t, e_bias):
    T, D = h2.shape
    tm = _tile(T, 512)
    return pl.pallas_call(
        functools.partial(_router_kernel, tm=tm),
        grid=(T // tm,),
        in_specs=[pl.BlockSpec((tm, D), lambda i: (i, 0)),
                  pl.BlockSpec((N_EXPERTS, D), lambda i: (0, 0)),
                  pl.BlockSpec((N_EXPERTS, 1), lambda i: (0, 0))],
        out_specs=[pl.BlockSpec((TOP_K, tm), lambda i: (0, i)),
                   pl.BlockSpec((TOP_K, tm), lambda i: (0, i)),
                   pl.BlockSpec((TOP_K, tm), lambda i: (0, i)),
                   pl.BlockSpec((N_EXPERTS, LANES), lambda i: (0, 0))],
        out_shape=[jax.ShapeDtypeStruct((TOP_K, T), jnp.int32),
                   jax.ShapeDtypeStruct((TOP_K, T), F32),
                   jax.ShapeDtypeStruct((TOP_K, T), jnp.int32),
                   jax.ShapeDtypeStruct((N_EXPERTS, LANES), F32)],
        scratch_shapes=[pltpu.VMEM((N_EXPERTS, 1), F32)],
        compiler_params=_cparams(("arbitrary",)),
        name="router",
    )(h2, w_router_t, e_bias.reshape(N_EXPERTS, 1))


def _slots_kernel(offs_ref, e_ref, r_ref, d_ref):
    e = e_ref[...]
    d = r_ref[...]
    for ex in range(N_EXPERTS):
        d = d + jnp.where(e == ex, offs_ref[ex], 0)
    d_ref[...] = d


def _slots(offs, top_e, rank):
    K, T = top_e.shape
    tm = _tile(T, 4096)
    grid_spec = pltpu.PrefetchScalarGridSpec(
        num_scalar_prefetch=1,
        grid=(T // tm,),
        in_specs=[pl.BlockSpec((K, tm), lambda i, o: (0, i)),
                  pl.BlockSpec((K, tm), lambda i, o: (0, i))],
        out_specs=pl.BlockSpec((K, tm), lambda i, o: (0, i)),
    )
    return pl.pallas_call(
        _slots_kernel,
        grid_spec=grid_spec,
        out_shape=jax.ShapeDtypeStruct((K, T), jnp.int32),
        compiler_params=_cparams(("parallel",)),
        name="slots",
    )(offs, top_e, rank)


def _row_copy(src, src_row, dst, dst_row, sem):
    return pltpu.make_async_copy(src.at[pl.ds(src_row, 1), :], dst.at[pl.ds(dst_row, 1), :], sem)


def _dispatch_kernel(dest_ref, h_ref, xs_ref, sem, *, td):
    def issue(j, carry):
        for kk in range(TOP_K):
            _row_copy(h_ref, j, xs_ref, dest_ref[0, kk * td + j], sem).start(priority=kk % 2)
        return carry

    lax.fori_loop(0, td, issue, 0)
    for _ in range(TOP_K):
        pltpu.make_async_copy(h_ref, xs_ref.at[pl.ds(0, td), :], sem).wait()


def _dispatch(h2, dest_tiles, td):
    T, D = h2.shape
    nt = T // td
    return pl.pallas_call(
        functools.partial(_dispatch_kernel, td=td),
        grid=(nt,),
        in_specs=[pl.BlockSpec((None, 1, TOP_K * td), lambda i: (i, 0, 0),
                               memory_space=pltpu.SMEM),
                  pl.BlockSpec((td, D), lambda i: (i, 0))],
        out_specs=pl.BlockSpec(memory_space=pl.ANY),
        out_shape=jax.ShapeDtypeStruct((T * TOP_K, D), h2.dtype),
        scratch_shapes=[pltpu.SemaphoreType.DMA],
        compiler_params=_cparams(("arbitrary",)),
        name="dispatch",
    )(dest_tiles, h2)


def _gmm_kernel(blk_ref, exp_ref, lo_ref, hi_ref, first_ref, newe_ref, x_ref, w1_ref, w3_ref,
                w2_ref, o_ref, w1b, w3b, w2b, *, bm):
    it = pl.program_id(0)

    @pl.when(newe_ref[it] == 1)
    def _():
        w1b[...] = w1_ref[...].astype(BF16)
        w3b[...] = w3_ref[...].astype(BF16)
        w2b[...] = w2_ref[...].astype(BF16)

    rows = lax.broadcasted_iota(jnp.int32, (bm, 1), 0)
    valid = jnp.logical_and(rows >= lo_ref[it], rows < hi_ref[it])
    x_lo, x_hi = _unpack_rows(jnp.where(valid, x_ref[...], jnp.uint32(0)))
    x_lo = x_lo.astype(BF16)
    x_hi = x_hi.astype(BF16)
    half = x_lo.shape[1]
    h1 = (jnp.dot(x_lo, w1b[:half, :], preferred_element_type=F32)
          + jnp.dot(x_hi, w1b[half:, :], preferred_element_type=F32))
    h3 = (jnp.dot(x_lo, w3b[:half, :], preferred_element_type=F32)
          + jnp.dot(x_hi, w3b[half:, :], preferred_element_type=F32))
    hb = (h1 * jax.nn.sigmoid(h1) * h3).astype(BF16)
    y = jnp.dot(hb, w2b[...], preferred_element_type=F32)

    @pl.when(first_ref[it] == 1)
    def _():
        o_ref[...] = _pack_rows(y)

    @pl.when(first_ref[it] == 0)
    def _():
        o_lo, o_hi = _unpack_rows(o_ref[...])
        o_ref[...] = _pack_rows(y + jnp.concatenate([o_lo, o_hi], axis=1))


def _gmm(xs, items, w1, w3, w2, bm):
    A = xs.shape[0]
    D = w1.shape[1]
    F = w1.shape[2]
    n_items = items[0].shape[0]
    grid_spec = pltpu.PrefetchScalarGridSpec(
        num_scalar_prefetch=6,
        grid=(n_items,),
        in_specs=[pl.BlockSpec((bm, D // 2), lambda i, blk, ex, *_: (blk[i], 0)),
                  pl.BlockSpec((None, D, F), lambda i, blk, ex, *_: (ex[i], 0, 0)),
                  pl.BlockSpec((None, D, F), lambda i, blk, ex, *_: (ex[i], 0, 0)),
                  pl.BlockSpec((None, F, D), lambda i, blk, ex, *_: (ex[i], 0, 0))],
        out_specs=pl.BlockSpec((bm, D // 2), lambda i, blk, ex, *_: (blk[i], 0)),
        scratch_shapes=[pltpu.VMEM((D, F), BF16), pltpu.VMEM((D, F), BF16),
                        pltpu.VMEM((F, D), BF16)],
    )
    return pl.pallas_call(
        functools.partial(_gmm_kernel, bm=bm),
        grid_spec=grid_spec,
        out_shape=jax.ShapeDtypeStruct((A, D // 2), jnp.uint32),
        compiler_params=_cparams(("arbitrary",)),
        name="gmm",
    )(*items, xs, w1, w3, w2)


def _work_items(counts, bm, n_blocks):
    n_items = n_blocks + N_EXPERTS - 1
    ends = jnp.cumsum(counts)
    starts = ends - counts
    nb = jnp.where(counts > 0, (ends - 1) // bm - starts // bm + 1, 0)
    item_end = jnp.cumsum(nb)
    item_start = item_end - nb
    n_real = item_end[-1]
    i = jnp.arange(n_items, dtype=jnp.int32)
    e = jnp.minimum(jnp.sum(item_end[None, :] <= i[:, None], axis=1), N_EXPERTS - 1).astype(jnp.int32)
    onehot = e[:, None] == jnp.arange(N_EXPERTS, dtype=jnp.int32)[None, :]
    pick = lambda v: jnp.sum(jnp.where(onehot, v[None, :], 0), axis=1)
    start_e, end_e = pick(starts), pick(ends)
    blk = start_e // bm + (i - pick(item_start))
    lo = jnp.clip(start_e - blk * bm, 0, bm)
    hi = jnp.clip(end_e - blk * bm, 0, bm)
    real = i < n_real
    blk = jnp.where(real, blk, n_blocks - 1).astype(jnp.int32)
    lo = jnp.where(real, lo, 0).astype(jnp.int32)
    hi = jnp.where(real, hi, 0).astype(jnp.int32)
    one = jnp.ones((1,), jnp.int32)
    first = jnp.concatenate([one, (blk[1:] != blk[:-1]).astype(jnp.int32)])
    new_e = jnp.concatenate([one, (e[1:] != e[:-1]).astype(jnp.int32)])
    return blk, e, lo, hi, first, new_e


def _combine_kernel(dest_ref, ys_ref, w_ref, x1_ref, h_ref, w1_ref, w3_ref, w2_ref,
                    gt2_ref, g_ref, o_ref, buf, sem, *, tc):
    def issue(j, carry):
        for kk in range(TOP_K):
            _row_copy(ys_ref, dest_ref[0, kk * tc + j], buf.at[kk], j, sem).start(priority=kk % 2)
        return carry

    lax.fori_loop(0, tc, issue, 0)

    h_lo, h_hi = _unpack_rows(h_ref[...])
    h_lo = h_lo.astype(BF16)
    h_hi = h_hi.astype(BF16)
    half = h_lo.shape[1]
    h1 = (jnp.dot(h_lo, w1_ref[:half, :], preferred_element_type=F32)
          + jnp.dot(h_hi, w1_ref[half:, :], preferred_element_type=F32))
    h3 = (jnp.dot(h_lo, w3_ref[:half, :], preferred_element_type=F32)
          + jnp.dot(h_hi, w3_ref[half:, :], preferred_element_type=F32))
    y = jnp.dot((h1 * jax.nn.sigmoid(h1) * h3).astype(BF16), w2_ref[...],
                preferred_element_type=F32)

    for kk in range(TOP_K):
        pltpu.make_async_copy(ys_ref.at[pl.ds(0, tc), :], buf.at[kk], sem).wait()
    moe_lo = None
    for kk in range(TOP_K):
        e_lo, e_hi = _unpack_rows(buf[kk])
        wk = w_ref[:, kk:kk + 1]
        moe_lo = wk * e_lo if kk == 0 else moe_lo + wk * e_lo
        moe_hi = wk * e_hi if kk == 0 else moe_hi + wk * e_hi
    moe = jnp.concatenate([moe_lo, moe_hi], axis=1)
    o_ref[...] = x1_ref[...] + gt2_ref[...] * _rms(moe + y, g_ref[...])


def _combine(ys, dest_tiles, w_tok, x1, h2, w1s, w3s, w2s, mod4, g_post, S, tc):
    T, D = x1.shape
    F = w1s.shape[1]
    per_b = S // tc
    row = lambda: pl.BlockSpec((tc, D), lambda i: (i, 0))
    return pl.pallas_call(
        functools.partial(_combine_kernel, tc=tc),
        grid=(T // tc,),
        in_specs=[pl.BlockSpec((None, 1, TOP_K * tc), lambda i: (i, 0, 0),
                               memory_space=pltpu.SMEM),
                  pl.BlockSpec(memory_space=pl.ANY),
                  pl.BlockSpec((tc, TOP_K), lambda i: (i, 0)),
                  row(), pl.BlockSpec((tc, D // 2), lambda i: (i, 0)),
                  pl.BlockSpec((D, F), lambda i: (0, 0)),
                  pl.BlockSpec((D, F), lambda i: (0, 0)),
                  pl.BlockSpec((F, D), lambda i: (0, 0)),
                  pl.BlockSpec((None, None, 1, D), lambda i: (i // per_b, 5, 0, 0)),
                  pl.BlockSpec((1, D), lambda i: (0, 0))],
        out_specs=row(),
        out_shape=jax.ShapeDtypeStruct((T, D), F32),
        scratch_shapes=[pltpu.VMEM((TOP_K, tc, D // 2), jnp.uint32), pltpu.SemaphoreType.DMA],
        compiler_params=_cparams(("arbitrary",)),
        name="combine",
    )(dest_tiles, ys, w_tok, x1, h2, w1s, w3s, w2s, mod4, g_post.reshape(1, D))


def _lambda_init(layer):
    return 0.8 - 0.6 * math.exp(-0.3 * layer)


def kernel(x, c, w_ada, b_ada, g_pre_mix, w_in, conv_w, conv_b, lru_wa, lru_ba, lru_wx, lru_bx,
           lru_lambda, lam_q1, lam_k1, lam_q2, lam_k2, g_subln, w_proj_rnn, w_proj_att, w_out,
           g_post_mix, g_pre_ffn, w_router, e_bias, w1_e, w3_e, w2_e, w1_s, w3_s, w2_s,
           g_post_ffn):
    B, S, D = x.shape
    T = B * S
    depth = w_ada.shape[0]
    slopes = np.exp2(-8.0 * np.arange(1, N_HEADS + 1, dtype=np.float32) / N_HEADS)
    kbias = _alibi_tables(slopes, S)
    tt = _tile(S, 256)
    bm = _tile(T * TOP_K, 512)
    n_blocks = T * TOP_K // bm

    x2 = x.reshape(T, D)
    for l in range(depth):
        lam_init = _lambda_init(l)
        mod4 = _ada(c, w_ada[l], b_ada[l]).reshape(B, 6, 1, D)

        proj = _inproj(x2, g_pre_mix[l], mod4, w_in[l].astype(BF16), S)
        yr = _lru(proj, conv_w[l], conv_b[l], lru_wa[l].astype(BF16), lru_ba[l],
                  lru_wx[l].astype(BF16), lru_bx[l], lru_lambda[l], B, S)
        lamv = jnp.stack([lam_q1[l], lam_k1[l], lam_q2[l], lam_k2[l]])
        ao = _attn(proj, kbias, lamv, g_subln[l], B, S, lam_init)
        x1, h2, h2p = _mixout(x2, yr, ao, proj, w_proj_rnn[l].astype(BF16),
                         w_proj_att[l].astype(BF16), w_out[l].astype(BF16),
                         g_post_mix[l], g_pre_ffn[l], mod4, S)

        top_e, top_w, rank, cnt = _router(h2, w_router[l].T, e_bias[l])
        counts = cnt[:, 0].astype(jnp.int32)
        offs = jnp.cumsum(counts) - counts
        dest = _slots(offs, top_e, rank)
        dest_tiles = dest.reshape(TOP_K, T // tt, tt).transpose(1, 0, 2).reshape(T // tt, 1,
                                                                                  TOP_K * tt)
        xs = _dispatch(h2p, dest_tiles, tt)
        items = _work_items(counts, bm, n_blocks)
        ys = _gmm(xs, items, w1_e[l], w3_e[l], w2_e[l], bm)
        x2 = _combine(ys, dest_tiles, top_w.T, x1, h2p, w1_s[l].astype(BF16),
                      w3_s[l].astype(BF16), w2_s[l].astype(BF16), mod4, g_post_ffn[l], S, tt)
    return x2.reshape(B, S, D)
```

```python
import functools
import math

import jax
import jax.numpy as jnp
import numpy as np
from jax import lax
from jax.experimental import pallas as pl
from jax.experimental.pallas import tpu as pltpu

F32 = jnp.float32
BF16 = jnp.bfloat16

EPS = 1e-6
N_HEADS = 8
HEAD_DIM = 64
V_DIM = 2 * HEAD_DIM
LRU_BLOCKS = 8
CONV_W = 4
LRU_C = 8.0
N_EXPERTS = 64
TOP_K = 8
N_GROUPS = 8
GROUP_SIZE = N_EXPERTS // N_GROUPS
TOPK_GROUPS = 4
ROUTED_SCALE = 2.5

LANES = 128
SUBLANES = 8
VMEM_LIMIT = 48 * 1024 * 1024


def _cparams(sem):
    return pltpu.CompilerParams(dimension_semantics=sem, vmem_limit_bytes=VMEM_LIMIT)


def _tile(n, pref):
    t = min(n, pref)
    while n % t:
        t //= 2
    return t


def _rms(x, g):
    return x * lax.rsqrt(jnp.mean(x * x, axis=-1, keepdims=True) + EPS) * g


def _ada_kernel(c_ref, w_ref, b_ref, o_ref):
    c = c_ref[...]
    cond = c * jax.nn.sigmoid(c)
    o_ref[...] = jnp.dot(cond, w_ref[...], preferred_element_type=F32) + b_ref[...]


def _ada(c, w, b):
    B, D = c.shape
    N = w.shape[1]
    tn = _tile(N, 1024)
    return pl.pallas_call(
        _ada_kernel,
        grid=(N // tn,),
        in_specs=[pl.BlockSpec((B, D), lambda j: (0, 0)),
                  pl.BlockSpec((D, tn), lambda j: (0, j)),
                  pl.BlockSpec((1, tn), lambda j: (0, j))],
        out_specs=pl.BlockSpec((B, tn), lambda j: (0, j)),
        out_shape=jax.ShapeDtypeStruct((B, N), F32),
        compiler_params=_cparams(("parallel",)),
        name="ada",
    )(c, w, b.reshape(1, N))


LOG2E = 1.4426950408889634
Q_COL_BLOCK = 2
Q_PRESCALE = HEAD_DIM ** -0.5 * LOG2E


def _inproj_kernel(x_ref, g_ref, sh_ref, sc_ref, w_ref, o_ref, h_scr):
    @pl.when(pl.program_id(1) == 0)
    def _():
        h = _rms(x_ref[...], g_ref[...]) * (1.0 + sc_ref[...]) + sh_ref[...]
        h_scr[...] = h.astype(BF16)

    r = jnp.dot(h_scr[...], w_ref[...], preferred_element_type=F32)
    r = r * jnp.where(pl.program_id(1) == Q_COL_BLOCK, Q_PRESCALE, 1.0)
    o_ref[...] = r.astype(o_ref.dtype)


def _inproj(x2, g, mod4, w_bf, S):
    T, D = x2.shape
    N = w_bf.shape[1]
    tm = _tile(S, 1024)
    tn = 1024
    per_b = S // tm
    return pl.pallas_call(
        _inproj_kernel,
        grid=(T // tm, N // tn),
        in_specs=[pl.BlockSpec((tm, D), lambda i, j: (i, 0)),
                  pl.BlockSpec((1, D), lambda i, j: (0, 0)),
                  pl.BlockSpec((None, None, 1, D), lambda i, j: (i // per_b, 0, 0, 0)),
                  pl.BlockSpec((None, None, 1, D), lambda i, j: (i // per_b, 1, 0, 0)),
                  pl.BlockSpec((D, tn), lambda i, j: (0, j))],
        out_specs=pl.BlockSpec((tm, tn), lambda i, j: (i, j)),
        out_shape=jax.ShapeDtypeStruct((T, N), BF16),
        scratch_shapes=[pltpu.VMEM((tm, D), BF16)],
        compiler_params=_cparams(("parallel", "arbitrary")),
        name="inproj",
    )(x2, g.reshape(1, D), mod4, mod4, w_bf)


def _lru_kernel(xr_ref, gr_ref, cw_ref, cb_ref, wa_ref, ba_ref, wx_ref, bx_ref, lam_ref,
                o_ref, xc_scr, prev_scr, h_scr, *, ts):
    s = pl.program_id(1)

    @pl.when(s == 0)
    def _():
        prev_scr[...] = jnp.zeros_like(prev_scr)
        h_scr[...] = jnp.zeros_like(h_scr)

    x = xr_ref[...].astype(F32)
    prev = prev_scr[...]
    row8 = lax.broadcasted_iota(jnp.int32, (SUBLANES, 1), 0)
    acc = cb_ref[...] + cw_ref[CONV_W - 1:CONV_W, :] * x
    xc_scr[...] = acc
    top = cb_ref[...] + cw_ref[CONV_W - 1:CONV_W, :] * x[0:SUBLANES, :]
    for j in range(1, CONV_W):
        wj = cw_ref[CONV_W - 1 - j:CONV_W - j, :]
        rj = pltpu.roll(x, j, axis=0)
        xc_scr[...] += wj * rj
        pj = pltpu.roll(prev, j, axis=0)
        top += wj * jnp.where(row8 < j, pj, rj[0:SUBLANES, :])
    xc_scr[0:SUBLANES, :] = top
    prev_scr[...] = x[ts - SUBLANES:ts, :]

    row = lax.broadcasted_iota(jnp.int32, (ts, 1), 0)
    is_first = jnp.logical_and(row == 0, s == 0)
    sub = lax.broadcasted_iota(jnp.int32, (1, SUBLANES, 1), 1)
    for n in range(LRU_BLOCKS):
        cols = slice(n * LANES, (n + 1) * LANES)
        xc = xc_scr[:, cols]
        xb = xc.astype(BF16)
        r = jax.nn.sigmoid(jnp.dot(xb, wa_ref[n], preferred_element_type=F32) + ba_ref[:, cols])
        i = jax.nn.sigmoid(jnp.dot(xb, wx_ref[n], preferred_element_type=F32) + bx_ref[:, cols])
        lam = lam_ref[:, cols]
        softplus_neg = jnp.maximum(-lam, 0.0) + jnp.log1p(jnp.exp(-jnp.abs(lam)))
        log_a = (-LRU_C * softplus_neg) * r
        a = jnp.exp(log_a)
        m2 = 1.0 - a * a
        mult = jnp.where(m2 > 0.0, m2 * lax.rsqrt(m2), 0.0)
        mult = jnp.where(is_first, 1.0, mult)
        u = mult * (i * xc)
        a = a.reshape(ts // SUBLANES, SUBLANES, LANES)
        u = u.reshape(ts // SUBLANES, SUBLANES, LANES)
        for d in (1, 2, 4):
            keep = sub >= d
            a_sh = jnp.where(keep, pltpu.roll(a, d, axis=1), 1.0)
            u_sh = jnp.where(keep, pltpu.roll(u, d, axis=1), 0.0)
            u = u + a * u_sh
            a = a * a_sh
        a = a.reshape(ts, LANES)
        u = u.reshape(ts, LANES)
        gate = jax.nn.gelu(gr_ref[:, cols].astype(F32))
        carry = h_scr[:, cols]
        step = 2 * SUBLANES
        for g in range(ts // step):
            r0 = g * step
            h0 = u[r0:r0 + SUBLANES, :] + a[r0:r0 + SUBLANES, :] * carry
            carry = h0[SUBLANES - 1:SUBLANES, :]
            h1 = u[r0 + SUBLANES:r0 + step, :] + a[r0 + SUBLANES:r0 + step, :] * carry
            carry = h1[SUBLANES - 1:SUBLANES, :]
            hg = jnp.concatenate([h0, h1], axis=0) * gate[r0:r0 + step, :]
            o_ref[r0:r0 + step, cols] = hg.astype(o_ref.dtype)
        h_scr[:, cols] = carry


def _lru(proj, conv_w, conv_b, wa_bf, ba, wx_bf, bx, lam, B, S):
    C = conv_w.shape[1]
    ts = _tile(S, 256)
    ns = S // ts
    vec = lambda: pl.BlockSpec((1, C), lambda b, s: (0, 0))
    blk = lambda: pl.BlockSpec((LRU_BLOCKS, LANES, LANES), lambda b, s: (0, 0, 0))
    return pl.pallas_call(
        functools.partial(_lru_kernel, ts=ts),
        grid=(B, ns),
        in_specs=[pl.BlockSpec((ts, C), lambda b, s: (b * ns + s, 0)),
                  pl.BlockSpec((ts, C), lambda b, s: (b * ns + s, 1)),
                  pl.BlockSpec((CONV_W, C), lambda b, s: (0, 0)),
                  vec(), blk(), vec(), blk(), vec(), vec()],
        out_specs=pl.BlockSpec((ts, C), lambda b, s: (b * ns + s, 0)),
        out_shape=jax.ShapeDtypeStruct((B * S, C), BF16),
        scratch_shapes=[pltpu.VMEM((ts, C), F32), pltpu.VMEM((SUBLANES, C), F32),
                        pltpu.VMEM((1, C), F32)],
        compiler_params=_cparams(("parallel", "arbitrary")),
        name="lru",
    )(proj, proj, conv_w, conv_b.reshape(1, C), wa_bf, ba.reshape(1, C), wx_bf, bx.reshape(1, C),
      lam.reshape(1, C))


BIAS_LANES = 3
ACC_ROWS = V_DIM + 2 * SUBLANES


def _alibi_tables(slopes, S):
    def top16(x):
        return (x.view(np.uint32) & np.uint32(0xFFFF0000)).view(np.float32)

    pos = np.arange(S, dtype=np.float32)
    b = (slopes.astype(np.float32) * np.float32(LOG2E))[:, None] * pos[None, :]
    hi = top16(b)
    mid = top16(b - hi)
    lo = top16(b - hi - mid)
    half = np.zeros(b.shape + (HEAD_DIM,), np.float32)
    half[..., 0], half[..., 1], half[..., 2] = hi, mid, lo
    zero = np.zeros_like(half)
    table = np.stack([np.concatenate([zero, half], axis=-1),
                      np.concatenate([half, zero], axis=-1)], axis=1)
    return jnp.asarray(table, dtype=BF16)


def _attn_kernel(lamv_ref, gsub_ref, q_ref, k_ref, v_ref, kb_ref, o_ref,
                 vt_scr, kx_scr, qx_scr, s_scr, p_scr, a_scr, m_scr, acc_scr, *, tq, lam_init):
    qi = pl.program_id(2)
    tk = tq
    nk = vt_scr.shape[0]

    lane = lax.broadcasted_iota(jnp.int32, (1, V_DIM), 1)
    own = (jnp.where(lane < HEAD_DIM, 1.0, 0.0), jnp.where(lane >= HEAD_DIM, 1.0, 0.0))
    ones_row = (jnp.where(jnp.logical_and(lane >= HEAD_DIM, lane < HEAD_DIM + BIAS_LANES), 1.0, 0.0),
                jnp.where(lane < BIAS_LANES, 1.0, 0.0))

    @pl.when(qi == 0)
    def _():
        for j in range(nk):
            rows = slice(j * tk, (j + 1) * tk)
            vt_scr[j, 0:V_DIM, :] = v_ref[rows, :].astype(F32).T.astype(BF16)
            vt_scr[j, V_DIM:ACC_ROWS, :] = jnp.ones((ACC_ROWS - V_DIM, tk), BF16)
            kf = k_ref[rows, :].astype(F32)
            for mp in range(2):
                kx_scr[mp, rows, :] = (kf * own[mp] + kb_ref[mp, rows, :].astype(F32)).astype(BF16)

    q = q_ref[...].astype(F32)
    for mp in range(2):
        qx_scr[mp] = (q * own[mp] + ones_row[mp]).astype(BF16)
    m_scr[...] = jnp.full_like(m_scr, -jnp.inf)
    acc_scr[...] = jnp.zeros_like(acc_scr)
    row8 = lax.broadcasted_iota(jnp.int32, (SUBLANES, LANES), 0)
    col = lax.broadcasted_iota(jnp.int32, (1, LANES), 1)
    neg_inf = jnp.float32(-jnp.inf)

    def tree(op, parts):
        parts = [p for p in parts if p is not None]
        while len(parts) > 1:
            parts = [op(parts[i], parts[i + 1]) if i + 1 < len(parts) else parts[i]
                     for i in range(0, len(parts), 2)]
        return parts[0]

    def scores(j, par):
        start = pl.multiple_of(j * tk, tk)
        for mp in range(2):
            s = lax.dot_general(kx_scr[mp, pl.ds(start, tk), :], qx_scr[mp],
                                (((1,), (1,)), ((), ())), preferred_element_type=F32)
            for c in range(tq // LANES):
                s_scr[par, mp, c] = s[:, c * LANES:(c + 1) * LANES]

    def softmax(par, masked):
        for mp in range(2):
            for c in range(tq // LANES):
                cols = slice(c * LANES, (c + 1) * LANES)
                lo_col, hi_col = c * LANES, (c + 1) * LANES - 1
                s_c = s_scr.at[par, mp, c]
                p_c = p_scr.at[par, mp, c]
                accs = [None] * 4
                for i in range(tk // SUBLANES):
                    r0 = i * SUBLANES
                    if masked and r0 > hi_col:
                        continue
                    t = s_c[r0:r0 + SUBLANES, :]
                    if masked and r0 + SUBLANES - 1 > lo_col:
                        t = jnp.where(row8 + r0 <= col + lo_col, t, neg_inf)
                        s_c[r0:r0 + SUBLANES, :] = t
                    accs[i % 4] = t if accs[i % 4] is None else jnp.maximum(accs[i % 4], t)
                mx = jnp.max(tree(jnp.maximum, accs), axis=0, keepdims=True)
                m_old = m_scr[mp, :, cols]
                m_new = jnp.maximum(m_old, mx)
                a_scr[par, mp, :, cols] = jnp.exp2(m_old - m_new)
                m_scr[mp, :, cols] = m_new
                for i in range(tk // (2 * SUBLANES)):
                    r0 = i * 2 * SUBLANES
                    if masked and r0 > hi_col:
                        p_c[r0:r0 + 2 * SUBLANES, :] = jnp.zeros((2 * SUBLANES, LANES), BF16)
                        continue
                    p = jnp.exp2(s_c[r0:r0 + 2 * SUBLANES, :] - m_new)
                    p_c[r0:r0 + 2 * SUBLANES, :] = p.astype(BF16)

    def values(j, par):
        for mp in range(2):
            p = jnp.concatenate([p_scr[par, mp, c] for c in range(tq // LANES)], axis=1)
            pv = jnp.dot(vt_scr[j], p, preferred_element_type=F32)
            acc_scr[mp] = a_scr[par, mp] * acc_scr[mp] + pv

    p_scr[1] = jnp.zeros(p_scr.shape[1:], BF16)
    a_scr[1] = jnp.ones(a_scr.shape[1:], F32)
    scores(0, 0)

    def stage(j, par):
        scores(j + 1, 1 - par)
        softmax(par, False)
        values(jnp.maximum(j - 1, 0), 1 - par)

    def stage_pair(i, carry):
        stage(2 * i, 0)
        stage(2 * i + 1, 1)
        return carry

    def tail(par):
        softmax(par, True)
        values(jnp.maximum(qi - 1, 0), 1 - par)
        values(qi, par)

    lax.fori_loop(0, qi // 2, stage_pair, 0)

    @pl.when(qi % 2 == 1)
    def _():
        stage(qi - 1, 0)
        tail(1)

    @pl.when(qi % 2 == 0)
    def _():
        tail(0)

    lv = lamv_ref[...]
    lam = (jnp.exp(jnp.sum(lv[0:1, :] * lv[1:2, :], axis=-1, keepdims=True))
           - jnp.exp(jnp.sum(lv[2:3, :] * lv[3:4, :], axis=-1, keepdims=True)) + lam_init)
    o0 = acc_scr[0, 0:V_DIM, :] * (1.0 / acc_scr[0, V_DIM:V_DIM + 1, :])
    o1 = acc_scr[1, 0:V_DIM, :] * (1.0 / acc_scr[1, V_DIM:V_DIM + 1, :])
    o_t = o0 - lam * o1
    o_t = o_t * lax.rsqrt(jnp.mean(o_t * o_t, axis=0, keepdims=True) + EPS) * gsub_ref[...]
    o_ref[...] = (o_t * (1.0 - lam_init)).T.astype(o_ref.dtype)


def _attn(proj, kbias, lamv, g_subln, B, S, lam_init):
    tq = _tile(S, 512)
    nq = S // tq
    qc, kc, vc = 2 * 8, 3 * 8, 4 * 8
    return pl.pallas_call(
        functools.partial(_attn_kernel, tq=tq, lam_init=lam_init),
        grid=(B, N_HEADS, nq),
        in_specs=[pl.BlockSpec((4, HEAD_DIM), lambda b, h, qi: (0, 0)),
                  pl.BlockSpec((V_DIM, 1), lambda b, h, qi: (0, 0)),
                  pl.BlockSpec((tq, V_DIM), lambda b, h, qi: (b * nq + qi, qc + h)),
                  pl.BlockSpec((S, V_DIM), lambda b, h, qi: (b, kc + h)),
                  pl.BlockSpec((S, V_DIM), lambda b, h, qi: (b, vc + h)),
                  pl.BlockSpec((None, 2, S, V_DIM), lambda b, h, qi: (h, 0, 0, 0))],
        out_specs=pl.BlockSpec((tq, V_DIM), lambda b, h, qi: (b * nq + qi, h)),
        out_shape=jax.ShapeDtypeStruct((B * S, N_HEADS * V_DIM), BF16),
        scratch_shapes=[pltpu.VMEM((nq, ACC_ROWS, tq), BF16), pltpu.VMEM((2, S, V_DIM), BF16),
                        pltpu.VMEM((2, tq, V_DIM), BF16),
                        pltpu.VMEM((2, 2, tq // LANES, tq, LANES), F32),
                        pltpu.VMEM((2, 2, tq // LANES, tq, LANES), BF16),
                        pltpu.VMEM((2, 2, 1, tq), F32), pltpu.VMEM((2, 1, tq), F32),
                        pltpu.VMEM((2, ACC_ROWS, tq), F32)],
        compiler_params=_cparams(("parallel", "parallel", "arbitrary")),
        name="attn",
    )(lamv, g_subln.reshape(V_DIM, 1), proj, proj, proj, kbias)


def _pack_rows(x):
    half = x.shape[1] // 2
    lo = lax.bitcast_convert_type(x[:, :half].astype(BF16).astype(F32), jnp.uint32)
    hi = lax.bitcast_convert_type(x[:, half:].astype(BF16).astype(F32), jnp.uint32)
    return (hi & jnp.uint32(0xFFFF0000)) | (lo >> 16)


def _unpack_rows(u):
    lo = lax.bitcast_convert_type(u << 16, F32)
    hi = lax.bitcast_convert_type(u & jnp.uint32(0xFFFF0000), F32)
    return lo, hi


SLAB = 4


def _load_slabs(ref, n):
    return jnp.concatenate([ref[pl.ds(c, n, stride=SLAB), :] for c in range(SLAB)], axis=1)


def _store_slabs(ref, u):
    n = u.shape[0]
    for c in range(SLAB):
        ref[pl.ds(c, n, stride=SLAB), :] = u[:, c * LANES:(c + 1) * LANES]


def _mixout_kernel(x_ref, yr_ref, ao_ref, ga_ref, gb_ref, wr_ref, wa_ref, wo_ref,
                   gpost_ref, gt1_ref, gpre_ref, sh2_ref, sc2_ref, x1_ref, h2_ref, h2p_ref):
    ya = jnp.dot(yr_ref[...], wr_ref[...], preferred_element_type=F32)
    yb = jnp.dot(ao_ref[...], wa_ref[...], preferred_element_type=F32)
    merged = (jax.nn.sigmoid(ga_ref[...].astype(F32)) * ya
              + jax.nn.sigmoid(gb_ref[...].astype(F32)) * yb)
    y = jnp.dot(merged.astype(BF16), wo_ref[...], preferred_element_type=F32)
    x1 = x_ref[...] + gt1_ref[...] * _rms(y, gpost_ref[...])
    x1_ref[...] = x1
    h2 = _rms(x1, gpre_ref[...]) * (1.0 + sc2_ref[...]) + sh2_ref[...]
    h2_ref[...] = h2
    _store_slabs(h2p_ref, _pack_rows(h2))


def _mixout(x2, yr, ao, proj, wr_bf, wa_bf, wo_bf, g_post, g_pre, mod4, S):
    T, D = x2.shape
    tm = _tile(S, 512)
    per_b = S // tm
    gac, gbc = 5, 6
    row = lambda: pl.BlockSpec((tm, D), lambda i: (i, 0))
    wsp = lambda: pl.BlockSpec((D, D), lambda i: (0, 0))
    vec = lambda: pl.BlockSpec((1, D), lambda i: (0, 0))
    modv = lambda j: pl.BlockSpec((None, None, 1, D), lambda i: (i // per_b, j, 0, 0))
    return pl.pallas_call(
        _mixout_kernel,
        grid=(T // tm,),
        in_specs=[row(), row(), row(),
                  pl.BlockSpec((tm, D), lambda i: (i, gac)),
                  pl.BlockSpec((tm, D), lambda i: (i, gbc)),
                  wsp(), wsp(), wsp(), vec(), modv(2), vec(), modv(3), modv(4)],
        out_specs=[row(), row(), pl.BlockSpec((tm * SLAB, LANES), lambda i: (i, 0))],
        out_shape=[jax.ShapeDtypeStruct((T, D), F32), jax.ShapeDtypeStruct((T, D), F32),
                   jax.ShapeDtypeStruct((T * SLAB, LANES), jnp.uint32)],
        compiler_params=_cparams(("parallel",)),
        name="mixout",
    )(x2, yr, ao, proj, proj, wr_bf, wa_bf, wo_bf, g_post.reshape(1, D), mod4,
      g_pre.reshape(1, D), mod4, mod4)


def _first_argmax(vals, ids, sentinel):
    m = jnp.max(vals, axis=0, keepdims=True)
    idx = jnp.min(jnp.where(vals == m, ids, sentinel), axis=0, keepdims=True)
    return m, idx


def _router_kernel(h_ref, wr_ref, eb_ref, e_ref, w_ref, r_ref, cnt_ref, cnt_scr, *, tm):
    step = pl.program_id(0)

    @pl.when(step == 0)
    def _():
        cnt_scr[...] = jnp.zeros_like(cnt_scr)

    logits = lax.dot_general(wr_ref[...], h_ref[...], (((1,), (1,)), ((), ())),
                             preferred_element_type=F32, precision=lax.Precision.HIGHEST)
    scores = jax.nn.sigmoid(logits)
    choice = scores + eb_ref[...]
    i8 = lax.broadcasted_iota(jnp.int32, (GROUP_SIZE, tm), 0)
    neg_inf = jnp.float32(-jnp.inf)

    slabs = [choice[g * GROUP_SIZE:(g + 1) * GROUP_SIZE, :] for g in range(N_GROUPS)]
    sc_slabs = [scores[g * GROUP_SIZE:(g + 1) * GROUP_SIZE, :] for g in range(N_GROUPS)]

    gs = jnp.zeros((N_GROUPS, tm), F32)
    for g in range(N_GROUPS):
        m1, idx1 = _first_argmax(slabs[g], i8, GROUP_SIZE)
        m2 = jnp.max(jnp.where(i8 == idx1, neg_inf, slabs[g]), axis=0, keepdims=True)
        gs = jnp.where(i8 == g, m1 + m2, gs)

    sel = jnp.zeros((N_GROUPS, tm), jnp.int32)
    cur = gs
    for _ in range(TOPK_GROUPS):
        _, idx = _first_argmax(cur, i8, N_GROUPS)
        hit = i8 == idx
        sel = jnp.where(hit, 1, sel)
        cur = jnp.where(hit, neg_inf, cur)

    masked = [jnp.where(sel[g:g + 1, :] > 0, slabs[g], neg_inf) for g in range(N_GROUPS)]
    ids = [i8 + g * GROUP_SIZE for g in range(N_GROUPS)]
    onehot = [jnp.zeros((GROUP_SIZE, tm), F32) for _ in range(N_GROUPS)]
    picks = []
    wts = []
    for _ in range(TOP_K):
        m = functools.reduce(jnp.maximum,
                             [jnp.max(c, axis=0, keepdims=True) for c in masked])
        idx = functools.reduce(
            jnp.minimum,
            [jnp.min(jnp.where(c == m, i, N_EXPERTS), axis=0, keepdims=True)
             for c, i in zip(masked, ids)])
        w = jnp.zeros((1, tm), F32)
        for g in range(N_GROUPS):
            hit = ids[g] == idx
            w = w + jnp.sum(jnp.where(hit, sc_slabs[g], 0.0), axis=0, keepdims=True)
            masked[g] = jnp.where(hit, neg_inf, masked[g])
            onehot[g] = jnp.where(hit, 1.0, onehot[g])
        picks.append(idx)
        wts.append(w)

    wsum = functools.reduce(lambda a, b: a + b, wts)
    norm = ROUTED_SCALE / (wsum + 1e-20)

    t_row = lax.broadcasted_iota(jnp.int32, (tm, tm), 0)
    t_col = lax.broadcasted_iota(jnp.int32, (tm, tm), 1)
    before = jnp.where(t_row < t_col, 1.0, 0.0).astype(BF16)
    cum = [jnp.dot(onehot[g].astype(BF16), before, preferred_element_type=F32)
           + cnt_scr[g * GROUP_SIZE:(g + 1) * GROUP_SIZE, :] for g in range(N_GROUPS)]

    for kk in range(TOP_K):
        rank = jnp.zeros((1, tm), F32)
        for g in range(N_GROUPS):
            rank = rank + jnp.sum(jnp.where(ids[g] == picks[kk], cum[g], 0.0),
                                  axis=0, keepdims=True)
        e_ref[kk:kk + 1, :] = picks[kk]
        w_ref[kk:kk + 1, :] = wts[kk] * norm
        r_ref[kk:kk + 1, :] = rank.astype(jnp.int32)

    for g in range(N_GROUPS):
        rows = slice(g * GROUP_SIZE, (g + 1) * GROUP_SIZE)
        cnt_scr[rows, :] = cnt_scr[rows, :] + jnp.sum(onehot[g], axis=1, keepdims=True)
    cnt_ref[...] = jnp.broadcast_to(cnt_scr[...], cnt_ref.shape)


def _router(h2, w_router_t, e_bias):
    T, D = h2.shape
    tm = _tile(T, 512)
    return pl.pallas_call(
        functools.partial(_router_kernel, tm=tm),
        grid=(T // tm,),
        in_specs=[pl.BlockSpec((tm, D), lambda i: (i, 0)),
                  pl.BlockSpec((N_EXPERTS, D), lambda i: (0, 0)),
                  pl.BlockSpec((N_EXPERTS, 1), lambda i: (0, 0))],
        out_specs=[pl.BlockSpec((TOP_K, tm), lambda i: (0, i)),
                   pl.BlockSpec((TOP_K, tm), lambda i: (0, i)),
                   pl.BlockSpec((TOP_K, tm), lambda i: (0, i)),
                   pl.BlockSpec((N_EXPERTS, LANES), lambda i: (0, 0))],
        out_shape=[jax.ShapeDtypeStruct((TOP_K, T), jnp.int32),
                   jax.ShapeDtypeStruct((TOP_K, T), F32),
                   jax.ShapeDtypeStruct((TOP_K, T), jnp.int32),
                   jax.ShapeDtypeStruct((N_EXPERTS, LANES), F32)],
        scratch_shapes=[pltpu.VMEM((N_EXPERTS, 1), F32)],
        compiler_params=_cparams(("arbitrary",)),
        name="router",
    )(h2, w_router_t, e_bias.reshape(N_EXPERTS, 1))


def _slots_kernel(offs_ref, e_ref, r_ref, d_ref):
    e = e_ref[...]
    d = r_ref[...]
    for ex in range(N_EXPERTS):
        d = d + jnp.where(e == ex, offs_ref[ex], 0)
    d_ref[...] = d


def _slots(offs, top_e, rank):
    K, T = top_e.shape
    tm = _tile(T, 4096)
    grid_spec = pltpu.PrefetchScalarGridSpec(
        num_scalar_prefetch=1,
        grid=(T // tm,),
        in_specs=[pl.BlockSpec((K, tm), lambda i, o: (0, i)),
                  pl.BlockSpec((K, tm), lambda i, o: (0, i))],
        out_specs=pl.BlockSpec((K, tm), lambda i, o: (0, i)),
    )
    return pl.pallas_call(
        _slots_kernel,
        grid_spec=grid_spec,
        out_shape=jax.ShapeDtypeStruct((K, T), jnp.int32),
        compiler_params=_cparams(("parallel",)),
        name="slots",
    )(offs, top_e, rank)


def _row_copy(src, src_row, dst, dst_row, sem):
    s0 = pl.multiple_of(src_row * SLAB, SLAB)
    d0 = pl.multiple_of(dst_row * SLAB, SLAB)
    return pltpu.make_async_copy(src.at[pl.ds(s0, SLAB), :], dst.at[pl.ds(d0, SLAB), :], sem)


def _dispatch_kernel(dest_ref, h_ref, xs_ref, sem, *, td):
    def issue(j, carry):
        for kk in range(TOP_K):
            _row_copy(h_ref, j, xs_ref, dest_ref[0, kk * td + j], sem).start(priority=kk % 2)
        return carry

    lax.fori_loop(0, td, issue, 0)
    for _ in range(TOP_K):
        pltpu.make_async_copy(h_ref, xs_ref.at[pl.ds(0, td * SLAB), :], sem).wait()


def _dispatch(h2, dest_tiles, td):
    T = h2.shape[0] // SLAB
    nt = T // td
    return pl.pallas_call(
        functools.partial(_dispatch_kernel, td=td),
        grid=(nt,),
        in_specs=[pl.BlockSpec((None, 1, TOP_K * td), lambda i: (i, 0, 0),
                               memory_space=pltpu.SMEM),
                  pl.BlockSpec((td * SLAB, LANES), lambda i: (i, 0))],
        out_specs=pl.BlockSpec(memory_space=pl.ANY),
        out_shape=jax.ShapeDtypeStruct((T * TOP_K * SLAB, LANES), h2.dtype),
        scratch_shapes=[pltpu.SemaphoreType.DMA],
        compiler_params=_cparams(("arbitrary",)),
        name="dispatch",
    )(dest_tiles, h2)


def _gmm_kernel(blk_ref, exp_ref, lo_ref, hi_ref, first_ref, newe_ref, x_ref, w1_ref, w3_ref,
                w2_ref, o_ref, w1b, w3b, w2b, *, bm):
    it = pl.program_id(0)

    @pl.when(newe_ref[it] == 1)
    def _():
        w1b[...] = w1_ref[...].astype(BF16)
        w3b[...] = w3_ref[...].astype(BF16)
        w2b[...] = w2_ref[...].astype(BF16)

    rows = lax.broadcasted_iota(jnp.int32, (bm, 1), 0)
    valid = jnp.logical_and(rows >= lo_ref[it], rows < hi_ref[it])
    x_lo, x_hi = _unpack_rows(jnp.where(valid, _load_slabs(x_ref, bm), jnp.uint32(0)))
    x_lo = x_lo.astype(BF16)
    x_hi = x_hi.astype(BF16)
    half = x_lo.shape[1]
    h1 = (jnp.dot(x_lo, w1b[:half, :], preferred_element_type=F32)
          + jnp.dot(x_hi, w1b[half:, :], preferred_element_type=F32))
    h3 = (jnp.dot(x_lo, w3b[:half, :], preferred_element_type=F32)
          + jnp.dot(x_hi, w3b[half:, :], preferred_element_type=F32))
    hb = (h1 * jax.nn.sigmoid(h1) * h3).astype(BF16)
    y = jnp.dot(hb, w2b[...], preferred_element_type=F32)

    @pl.when(first_ref[it] == 1)
    def _():
        _store_slabs(o_ref, _pack_rows(y))

    @pl.when(first_ref[it] == 0)
    def _():
        o_lo, o_hi = _unpack_rows(_load_slabs(o_ref, bm))
        _store_slabs(o_ref, _pack_rows(y + jnp.concatenate([o_lo, o_hi], axis=1)))


def _gmm(xs, items, w1, w3, w2, bm):
    A = xs.shape[0] // SLAB
    D = w1.shape[1]
    F = w1.shape[2]
    n_items = items[0].shape[0]
    grid_spec = pltpu.PrefetchScalarGridSpec(
        num_scalar_prefetch=6,
        grid=(n_items,),
        in_specs=[pl.BlockSpec((bm * SLAB, LANES), lambda i, blk, ex, *_: (blk[i], 0)),
                  pl.BlockSpec((None, D, F), lambda i, blk, ex, *_: (ex[i], 0, 0)),
                  pl.BlockSpec((None, D, F), lambda i, blk, ex, *_: (ex[i], 0, 0)),
                  pl.BlockSpec((None, F, D), lambda i, blk, ex, *_: (ex[i], 0, 0))],
        out_specs=pl.BlockSpec((bm * SLAB, LANES), lambda i, blk, ex, *_: (blk[i], 0)),
        scratch_shapes=[pltpu.VMEM((D, F), BF16), pltpu.VMEM((D, F), BF16),
                        pltpu.VMEM((F, D), BF16)],
    )
    return pl.pallas_call(
        functools.partial(_gmm_kernel, bm=bm),
        grid_spec=grid_spec,
        out_shape=jax.ShapeDtypeStruct((A * SLAB, LANES), jnp.uint32),
        compiler_params=_cparams(("arbitrary",)),
        name="gmm",
    )(*items, xs, w1, w3, w2)


def _work_items(counts, bm, n_blocks):
    n_items = n_blocks + N_EXPERTS - 1
    ends = jnp.cumsum(counts)
    starts = ends - counts
    nb = jnp.where(counts > 0, (ends - 1) // bm - starts // bm + 1, 0)
    item_end = jnp.cumsum(nb)
    item_start = item_end - nb
    n_real = item_end[-1]
    i = jnp.arange(n_items, dtype=jnp.int32)
    e = jnp.minimum(jnp.sum(item_end[None, :] <= i[:, None], axis=1), N_EXPERTS - 1).astype(jnp.int32)
    onehot = e[:, None] == jnp.arange(N_EXPERTS, dtype=jnp.int32)[None, :]
    pick = lambda v: jnp.sum(jnp.where(onehot, v[None, :], 0), axis=1)
    start_e, end_e = pick(starts), pick(ends)
    blk = start_e // bm + (i - pick(item_start))
    lo = jnp.clip(start_e - blk * bm, 0, bm)
    hi = jnp.clip(end_e - blk * bm, 0, bm)
    real = i < n_real
    blk = jnp.where(real, blk, n_blocks - 1).astype(jnp.int32)
    lo = jnp.where(real, lo, 0).astype(jnp.int32)
    hi = jnp.where(real, hi, 0).astype(jnp.int32)
    one = jnp.ones((1,), jnp.int32)
    first = jnp.concatenate([one, (blk[1:] != blk[:-1]).astype(jnp.int32)])
    new_e = jnp.concatenate([one, (e[1:] != e[:-1]).astype(jnp.int32)])
    return blk, e, lo, hi, first, new_e


def _combine_kernel(dest_ref, ys_ref, w_ref, x1_ref, h_ref, w1_ref, w3_ref, w2_ref,
                    gt2_ref, g_ref, o_ref, buf, sem, *, tc):
    def issue(j, carry):
        for kk in range(TOP_K):
            _row_copy(ys_ref, dest_ref[0, kk * tc + j], buf.at[kk], j, sem).start(priority=kk % 2)
        return carry

    lax.fori_loop(0, tc, issue, 0)

    h_lo, h_hi = _unpack_rows(_load_slabs(h_ref, tc))
    h_lo = h_lo.astype(BF16)
    h_hi = h_hi.astype(BF16)
    half = h_lo.shape[1]
    h1 = (jnp.dot(h_lo, w1_ref[:half, :], preferred_element_type=F32)
          + jnp.dot(h_hi, w1_ref[half:, :], preferred_element_type=F32))
    h3 = (jnp.dot(h_lo, w3_ref[:half, :], preferred_element_type=F32)
          + jnp.dot(h_hi, w3_ref[half:, :], preferred_element_type=F32))
    y = jnp.dot((h1 * jax.nn.sigmoid(h1) * h3).astype(BF16), w2_ref[...],
                preferred_element_type=F32)

    for kk in range(TOP_K):
        pltpu.make_async_copy(ys_ref.at[pl.ds(0, tc * SLAB), :], buf.at[kk], sem).wait()
    moe_lo = None
    for kk in range(TOP_K):
        e_lo, e_hi = _unpack_rows(_load_slabs(buf.at[kk], tc))
        wk = w_ref[:, kk:kk + 1]
        moe_lo = wk * e_lo if kk == 0 else moe_lo + wk * e_lo
        moe_hi = wk * e_hi if kk == 0 else moe_hi + wk * e_hi
    moe = jnp.concatenate([moe_lo, moe_hi], axis=1)
    o_ref[...] = x1_ref[...] + gt2_ref[...] * _rms(moe + y, g_ref[...])


def _combine(ys, dest_tiles, w_tok, x1, h2, w1s, w3s, w2s, mod4, g_post, S, tc):
    T, D = x1.shape
    F = w1s.shape[1]
    per_b = S // tc
    row = lambda: pl.BlockSpec((tc, D), lambda i: (i, 0))
    return pl.pallas_call(
        functools.partial(_combine_kernel, tc=tc),
        grid=(T // tc,),
        in_specs=[pl.BlockSpec((None, 1, TOP_K * tc), lambda i: (i, 0, 0),
                               memory_space=pltpu.SMEM),
                  pl.BlockSpec(memory_space=pl.ANY),
                  pl.BlockSpec((tc, TOP_K), lambda i: (i, 0)),
                  row(), pl.BlockSpec((tc * SLAB, LANES), lambda i: (i, 0)),
                  pl.BlockSpec((D, F), lambda i: (0, 0)),
                  pl.BlockSpec((D, F), lambda i: (0, 0)),
                  pl.BlockSpec((F, D), lambda i: (0, 0)),
                  pl.BlockSpec((None, None, 1, D), lambda i: (i // per_b, 5, 0, 0)),
                  pl.BlockSpec((1, D), lambda i: (0, 0))],
        out_specs=row(),
        out_shape=jax.ShapeDtypeStruct((T, D), F32),
        scratch_shapes=[pltpu.VMEM((TOP_K, tc * SLAB, LANES), jnp.uint32), pltpu.SemaphoreType.DMA],
        compiler_params=_cparams(("arbitrary",)),
        name="combine",
    )(dest_tiles, ys, w_tok, x1, h2, w1s, w3s, w2s, mod4, g_post.reshape(1, D))


def _lambda_init(layer):
    return 0.8 - 0.6 * math.exp(-0.3 * layer)


def kernel(x, c, w_ada, b_ada, g_pre_mix, w_in, conv_w, conv_b, lru_wa, lru_ba, lru_wx, lru_bx,
           lru_lambda, lam_q1, lam_k1, lam_q2, lam_k2, g_subln, w_proj_rnn, w_proj_att, w_out,
           g_post_mix, g_pre_ffn, w_router, e_bias, w1_e, w3_e, w2_e, w1_s, w3_s, w2_s,
           g_post_ffn):
    B, S, D = x.shape
    T = B * S
    depth = w_ada.shape[0]
    slopes = np.exp2(-8.0 * np.arange(1, N_HEADS + 1, dtype=np.float32) / N_HEADS)
    kbias = _alibi_tables(slopes, S)
    tt = _tile(S, 256)
    bm = _tile(T * TOP_K, 512)
    n_blocks = T * TOP_K // bm

    x2 = x.reshape(T, D)
    for l in range(depth):
        lam_init = _lambda_init(l)
        mod4 = _ada(c, w_ada[l], b_ada[l]).reshape(B, 6, 1, D)

        proj = _inproj(x2, g_pre_mix[l], mod4, w_in[l].astype(BF16), S)
        yr = _lru(proj, conv_w[l], conv_b[l], lru_wa[l].astype(BF16), lru_ba[l],
                  lru_wx[l].astype(BF16), lru_bx[l], lru_lambda[l], B, S)
        lamv = jnp.stack([lam_q1[l], lam_k1[l], lam_q2[l], lam_k2[l]])
        ao = _attn(proj, kbias, lamv, g_subln[l], B, S, lam_init)
        x1, h2, h2p = _mixout(x2, yr, ao, proj, w_proj_rnn[l].astype(BF16),
                         w_proj_att[l].astype(BF16), w_out[l].astype(BF16),
                         g_post_mix[l], g_pre_ffn[l], mod4, S)

        top_e, top_w, rank, cnt = _router(h2, w_router[l].T, e_bias[l])
        counts = cnt[:, 0].astype(jnp.int32)
        offs = jnp.cumsum(counts) - counts
        dest = _slots(offs, top_e, rank)
        dest_tiles = dest.reshape(TOP_K, T // tt, tt).transpose(1, 0, 2).reshape(T // tt, 1,
                                                                                  TOP_K * tt)
        xs = _dispatch(h2p, dest_tiles, tt)
        items = _work_items(counts, bm, n_blocks)
        ys = _gmm(xs, items, w1_e[l], w3_e[l], w2_e[l], bm)
        x2 = _combine(ys, dest_tiles, top_w.T, x1, h2p, w1_s[l].astype(BF16),
                      w3_s[l].astype(BF16), w2_s[l].astype(BF16), mod4, g_post_ffn[l], S, tt)
    return x2.reshape(B, S, D)
```

```python
import functools
import math

import jax
import jax.numpy as jnp
import numpy as np
from jax import lax
from jax.experimental import pallas as pl
from jax.experimental.pallas import tpu as pltpu

F32 = jnp.float32
BF16 = jnp.bfloat16

EPS = 1e-6
N_HEADS = 8
HEAD_DIM = 64
V_DIM = 2 * HEAD_DIM
LRU_BLOCKS = 8
CONV_W = 4
LRU_C = 8.0
N_EXPERTS = 64
TOP_K = 8
N_GROUPS = 8
GROUP_SIZE = N_EXPERTS // N_GROUPS
TOPK_GROUPS = 4
ROUTED_SCALE = 2.5

LANES = 128
SUBLANES = 8
VMEM_LIMIT = 48 * 1024 * 1024


def _cparams(sem):
    return pltpu.CompilerParams(dimension_semantics=sem, vmem_limit_bytes=VMEM_LIMIT)


def _tile(n, pref):
    t = min(n, pref)
    while n % t:
        t //= 2
    return t


def _rms(x, g):
    return x * lax.rsqrt(jnp.mean(x * x, axis=-1, keepdims=True) + EPS) * g


def _ada_kernel(c_ref, w_ref, b_ref, o_ref):
    c = c_ref[...]
    cond = c * jax.nn.sigmoid(c)
    o_ref[...] = jnp.dot(cond, w_ref[...], preferred_element_type=F32) + b_ref[...]


def _ada(c, w, b):
    B, D = c.shape
    N = w.shape[1]
    tn = _tile(N, 1024)
    return pl.pallas_call(
        _ada_kernel,
        grid=(N // tn,),
        in_specs=[pl.BlockSpec((B, D), lambda j: (0, 0)),
                  pl.BlockSpec((D, tn), lambda j: (0, j)),
                  pl.BlockSpec((1, tn), lambda j: (0, j))],
        out_specs=pl.BlockSpec((B, tn), lambda j: (0, j)),
        out_shape=jax.ShapeDtypeStruct((B, N), F32),
        compiler_params=_cparams(("parallel",)),
        name="ada",
    )(c, w, b.reshape(1, N))


LOG2E = 1.4426950408889634
Q_COL_BLOCK = 2
Q_PRESCALE = HEAD_DIM ** -0.5 * LOG2E


def _inproj_kernel(x_ref, g_ref, sh_ref, sc_ref, w_ref, o_ref, h_scr):
    @pl.when(pl.program_id(1) == 0)
    def _():
        h = _rms(x_ref[...], g_ref[...]) * (1.0 + sc_ref[...]) + sh_ref[...]
        h_scr[...] = h.astype(BF16)

    r = jnp.dot(h_scr[...], w_ref[...], preferred_element_type=F32)
    r = r * jnp.where(pl.program_id(1) == Q_COL_BLOCK, Q_PRESCALE, 1.0)
    o_ref[...] = r.astype(o_ref.dtype)


def _inproj(x2, g, mod4, w_bf, S):
    T, D = x2.shape
    N = w_bf.shape[1]
    tm = _tile(S, 1024)
    tn = 1024
    per_b = S // tm
    return pl.pallas_call(
        _inproj_kernel,
        grid=(T // tm, N // tn),
        in_specs=[pl.BlockSpec((tm, D), lambda i, j: (i, 0)),
                  pl.BlockSpec((1, D), lambda i, j: (0, 0)),
                  pl.BlockSpec((None, None, 1, D), lambda i, j: (i // per_b, 0, 0, 0)),
                  pl.BlockSpec((None, None, 1, D), lambda i, j: (i // per_b, 1, 0, 0)),
                  pl.BlockSpec((D, tn), lambda i, j: (0, j))],
        out_specs=pl.BlockSpec((tm, tn), lambda i, j: (i, j)),
        out_shape=jax.ShapeDtypeStruct((T, N), BF16),
        scratch_shapes=[pltpu.VMEM((tm, D), BF16)],
        compiler_params=_cparams(("parallel", "arbitrary")),
        name="inproj",
    )(x2, g.reshape(1, D), mod4, mod4, w_bf)


def _lru_kernel(xr_ref, gr_ref, cw_ref, cb_ref, wa_ref, ba_ref, wx_ref, bx_ref, lam_ref,
                o_ref, xc_scr, prev_scr, h_scr, *, ts):
    s = pl.program_id(1)

    @pl.when(s == 0)
    def _():
        prev_scr[...] = jnp.zeros_like(prev_scr)
        h_scr[...] = jnp.zeros_like(h_scr)

    x = xr_ref[...].astype(F32)
    prev = prev_scr[...]
    row8 = lax.broadcasted_iota(jnp.int32, (SUBLANES, 1), 0)
    acc = cb_ref[...] + cw_ref[CONV_W - 1:CONV_W, :] * x
    xc_scr[...] = acc
    top = cb_ref[...] + cw_ref[CONV_W - 1:CONV_W, :] * x[0:SUBLANES, :]
    for j in range(1, CONV_W):
        wj = cw_ref[CONV_W - 1 - j:CONV_W - j, :]
        rj = pltpu.roll(x, j, axis=0)
        xc_scr[...] += wj * rj
        pj = pltpu.roll(prev, j, axis=0)
        top += wj * jnp.where(row8 < j, pj, rj[0:SUBLANES, :])
    xc_scr[0:SUBLANES, :] = top
    prev_scr[...] = x[ts - SUBLANES:ts, :]

    row = lax.broadcasted_iota(jnp.int32, (ts, 1), 0)
    is_first = jnp.logical_and(row == 0, s == 0)
    sub = lax.broadcasted_iota(jnp.int32, (1, SUBLANES, 1), 1)
    for n in range(LRU_BLOCKS):
        cols = slice(n * LANES, (n + 1) * LANES)
        xc = xc_scr[:, cols]
        xb = xc.astype(BF16)
        r = jax.nn.sigmoid(jnp.dot(xb, wa_ref[n], preferred_element_type=F32) + ba_ref[:, cols])
        i = jax.nn.sigmoid(jnp.dot(xb, wx_ref[n], preferred_element_type=F32) + bx_ref[:, cols])
        lam = lam_ref[:, cols]
        softplus_neg = jnp.maximum(-lam, 0.0) + jnp.log1p(jnp.exp(-jnp.abs(lam)))
        log_a = (-LRU_C * softplus_neg) * r
        a = jnp.exp(log_a)
        m2 = 1.0 - a * a
        mult = jnp.where(m2 > 0.0, m2 * lax.rsqrt(m2), 0.0)
        mult = jnp.where(is_first, 1.0, mult)
        u = mult * (i * xc)
        a = a.reshape(ts // SUBLANES, SUBLANES, LANES)
        u = u.reshape(ts // SUBLANES, SUBLANES, LANES)
        for d in (1, 2, 4):
            keep = sub >= d
            a_sh = jnp.where(keep, pltpu.roll(a, d, axis=1), 1.0)
            u_sh = jnp.where(keep, pltpu.roll(u, d, axis=1), 0.0)
            u = u + a * u_sh
            a = a * a_sh
        a = a.reshape(ts, LANES)
        u = u.reshape(ts, LANES)
        gate = jax.nn.gelu(gr_ref[:, cols].astype(F32))
        carry = h_scr[:, cols]
        step = 2 * SUBLANES
        for g in range(ts // step):
            r0 = g * step
            h0 = u[r0:r0 + SUBLANES, :] + a[r0:r0 + SUBLANES, :] * carry
            carry = h0[SUBLANES - 1:SUBLANES, :]
            h1 = u[r0 + SUBLANES:r0 + step, :] + a[r0 + SUBLANES:r0 + step, :] * carry
            carry = h1[SUBLANES - 1:SUBLANES, :]
            hg = jnp.concatenate([h0, h1], axis=0) * gate[r0:r0 + step, :]
            o_ref[r0:r0 + step, cols] = hg.astype(o_ref.dtype)
        h_scr[:, cols] = carry


def _lru(proj, conv_w, conv_b, wa_bf, ba, wx_bf, bx, lam, B, S):
    C = conv_w.shape[1]
    ts = _tile(S, 256)
    ns = S // ts
    vec = lambda: pl.BlockSpec((1, C), lambda b, s: (0, 0))
    blk = lambda: pl.BlockSpec((LRU_BLOCKS, LANES, LANES), lambda b, s: (0, 0, 0))
    return pl.pallas_call(
        functools.partial(_lru_kernel, ts=ts),
        grid=(B, ns),
        in_specs=[pl.BlockSpec((ts, C), lambda b, s: (b * ns + s, 0)),
                  pl.BlockSpec((ts, C), lambda b, s: (b * ns + s, 1)),
                  pl.BlockSpec((CONV_W, C), lambda b, s: (0, 0)),
                  vec(), blk(), vec(), blk(), vec(), vec()],
        out_specs=pl.BlockSpec((ts, C), lambda b, s: (b * ns + s, 0)),
        out_shape=jax.ShapeDtypeStruct((B * S, C), BF16),
        scratch_shapes=[pltpu.VMEM((ts, C), F32), pltpu.VMEM((SUBLANES, C), F32),
                        pltpu.VMEM((1, C), F32)],
        compiler_params=_cparams(("parallel", "arbitrary")),
        name="lru",
    )(proj, proj, conv_w, conv_b.reshape(1, C), wa_bf, ba.reshape(1, C), wx_bf, bx.reshape(1, C),
      lam.reshape(1, C))


BIAS_LANES = 3
ACC_ROWS = V_DIM + 2 * SUBLANES


def _alibi_tables(slopes, S):
    def top16(x):
        return (x.view(np.uint32) & np.uint32(0xFFFF0000)).view(np.float32)

    pos = np.arange(S, dtype=np.float32)
    b = (slopes.astype(np.float32) * np.float32(LOG2E))[:, None] * pos[None, :]
    hi = top16(b)
    mid = top16(b - hi)
    lo = top16(b - hi - mid)
    half = np.zeros(b.shape + (HEAD_DIM,), np.float32)
    half[..., 0], half[..., 1], half[..., 2] = hi, mid, lo
    zero = np.zeros_like(half)
    table = np.stack([np.concatenate([zero, half], axis=-1),
                      np.concatenate([half, zero], axis=-1)], axis=1)
    return jnp.asarray(table, dtype=BF16)


def _attn_kernel(lamv_ref, gsub_ref, q_ref, k_ref, v_ref, kb_ref, o_ref,
                 vt_scr, kx_scr, qx_scr, s_scr, p_scr, a_scr, m_scr, acc_scr, *, tq, lam_init):
    qi = pl.program_id(2)
    tk = tq
    nk = vt_scr.shape[0]

    lane = lax.broadcasted_iota(jnp.int32, (1, V_DIM), 1)
    own = (jnp.where(lane < HEAD_DIM, 1.0, 0.0), jnp.where(lane >= HEAD_DIM, 1.0, 0.0))
    ones_row = (jnp.where(jnp.logical_and(lane >= HEAD_DIM, lane < HEAD_DIM + BIAS_LANES), 1.0, 0.0),
                jnp.where(lane < BIAS_LANES, 1.0, 0.0))

    @pl.when(qi == 0)
    def _():
        for j in range(nk):
            rows = slice(j * tk, (j + 1) * tk)
            vt_scr[j, 0:V_DIM, :] = v_ref[rows, :].astype(F32).T.astype(BF16)
            vt_scr[j, V_DIM:ACC_ROWS, :] = jnp.ones((ACC_ROWS - V_DIM, tk), BF16)
            kf = k_ref[rows, :].astype(F32)
            for mp in range(2):
                kx_scr[mp, rows, :] = (kf * own[mp] + kb_ref[mp, rows, :].astype(F32)).astype(BF16)

    q = q_ref[...].astype(F32)
    for mp in range(2):
        qx_scr[mp] = (q * own[mp] + ones_row[mp]).astype(BF16)
    m_scr[...] = jnp.full_like(m_scr, -jnp.inf)
    acc_scr[...] = jnp.zeros_like(acc_scr)
    row8 = lax.broadcasted_iota(jnp.int32, (SUBLANES, LANES), 0)
    col = lax.broadcasted_iota(jnp.int32, (1, LANES), 1)
    neg_inf = jnp.float32(-jnp.inf)

    def tree(op, parts):
        parts = [p for p in parts if p is not None]
        while len(parts) > 1:
            parts = [op(parts[i], parts[i + 1]) if i + 1 < len(parts) else parts[i]
                     for i in range(0, len(parts), 2)]
        return parts[0]

    def scores(j, par):
        start = pl.multiple_of(j * tk, tk)
        for mp in range(2):
            s = lax.dot_general(kx_scr[mp, pl.ds(start, tk), :], qx_scr[mp],
                                (((1,), (1,)), ((), ())), preferred_element_type=F32)
            for c in range(tq // LANES):
                s_scr[par, mp, c] = s[:, c * LANES:(c + 1) * LANES]

    def softmax(par, masked):
        for mp in range(2):
            for c in range(tq // LANES):
                cols = slice(c * LANES, (c + 1) * LANES)
                lo_col, hi_col = c * LANES, (c + 1) * LANES - 1
                s_c = s_scr.at[par, mp, c]
                p_c = p_scr.at[par, mp, c]
                accs = [None] * 4
                for i in range(tk // SUBLANES):
                    r0 = i * SUBLANES
                    if masked and r0 > hi_col:
                        continue
                    t = s_c[r0:r0 + SUBLANES, :]
                    if masked and r0 + SUBLANES - 1 > lo_col:
                        t = jnp.where(row8 + r0 <= col + lo_col, t, neg_inf)
                        s_c[r0:r0 + SUBLANES, :] = t
                    accs[i % 4] = t if accs[i % 4] is None else jnp.maximum(accs[i % 4], t)
                mx = jnp.max(tree(jnp.maximum, accs), axis=0, keepdims=True)
                m_old = m_scr[mp, :, cols]
                m_new = jnp.maximum(m_old, mx)
                a_scr[par, mp, :, cols] = jnp.exp2(m_old - m_new)
                m_scr[mp, :, cols] = m_new
                for i in range(tk // (2 * SUBLANES)):
                    r0 = i * 2 * SUBLANES
                    if masked and r0 > hi_col:
                        p_c[r0:r0 + 2 * SUBLANES, :] = jnp.zeros((2 * SUBLANES, LANES), BF16)
                        continue
                    p = jnp.exp2(s_c[r0:r0 + 2 * SUBLANES, :] - m_new)
                    p_c[r0:r0 + 2 * SUBLANES, :] = p.astype(BF16)

    def values(j, par):
        for mp in range(2):
            p = jnp.concatenate([p_scr[par, mp, c] for c in range(tq // LANES)], axis=1)
            pv = jnp.dot(vt_scr[j], p, preferred_element_type=F32)
            acc_scr[mp] = a_scr[par, mp] * acc_scr[mp] + pv

    p_scr[1] = jnp.zeros(p_scr.shape[1:], BF16)
    a_scr[1] = jnp.ones(a_scr.shape[1:], F32)
    scores(0, 0)

    def stage(j, par):
        scores(j + 1, 1 - par)
        softmax(par, False)
        values(jnp.maximum(j - 1, 0), 1 - par)

    def stage_pair(i, carry):
        stage(2 * i, 0)
        stage(2 * i + 1, 1)
        return carry

    def tail(par):
        softmax(par, True)
        values(jnp.maximum(qi - 1, 0), 1 - par)
        values(qi, par)

    lax.fori_loop(0, qi // 2, stage_pair, 0)

    @pl.when(qi % 2 == 1)
    def _():
        stage(qi - 1, 0)
        tail(1)

    @pl.when(qi % 2 == 0)
    def _():
        tail(0)

    lv = lamv_ref[...]
    lam = (jnp.exp(jnp.sum(lv[0:1, :] * lv[1:2, :], axis=-1, keepdims=True))
           - jnp.exp(jnp.sum(lv[2:3, :] * lv[3:4, :], axis=-1, keepdims=True)) + lam_init)
    o0 = acc_scr[0, 0:V_DIM, :] * (1.0 / acc_scr[0, V_DIM:V_DIM + 1, :])
    o1 = acc_scr[1, 0:V_DIM, :] * (1.0 / acc_scr[1, V_DIM:V_DIM + 1, :])
    o_t = o0 - lam * o1
    o_t = o_t * lax.rsqrt(jnp.mean(o_t * o_t, axis=0, keepdims=True) + EPS) * gsub_ref[...]
    o_ref[...] = (o_t * (1.0 - lam_init)).T.astype(o_ref.dtype)


def _attn(proj, kbias, lamv, g_subln, B, S, lam_init):
    tq = _tile(S, 512)
    nq = S // tq
    qc, kc, vc = 2 * 8, 3 * 8, 4 * 8
    return pl.pallas_call(
        functools.partial(_attn_kernel, tq=tq, lam_init=lam_init),
        grid=(B, N_HEADS, nq),
        in_specs=[pl.BlockSpec((4, HEAD_DIM), lambda b, h, qi: (0, 0)),
                  pl.BlockSpec((V_DIM, 1), lambda b, h, qi: (0, 0)),
                  pl.BlockSpec((tq, V_DIM), lambda b, h, qi: (b * nq + qi, qc + h)),
                  pl.BlockSpec((S, V_DIM), lambda b, h, qi: (b, kc + h)),
                  pl.BlockSpec((S, V_DIM), lambda b, h, qi: (b, vc + h)),
                  pl.BlockSpec((None, 2, S, V_DIM), lambda b, h, qi: (h, 0, 0, 0))],
        out_specs=pl.BlockSpec((tq, V_DIM), lambda b, h, qi: (b * nq + qi, h)),
        out_shape=jax.ShapeDtypeStruct((B * S, N_HEADS * V_DIM), BF16),
        scratch_shapes=[pltpu.VMEM((nq, ACC_ROWS, tq), BF16), pltpu.VMEM((2, S, V_DIM), BF16),
                        pltpu.VMEM((2, tq, V_DIM), BF16),
                        pltpu.VMEM((2, 2, tq // LANES, tq, LANES), F32),
                        pltpu.VMEM((2, 2, tq // LANES, tq, LANES), BF16),
                        pltpu.VMEM((2, 2, 1, tq), F32), pltpu.VMEM((2, 1, tq), F32),
                        pltpu.VMEM((2, ACC_ROWS, tq), F32)],
        compiler_params=_cparams(("parallel", "parallel", "arbitrary")),
        name="attn",
    )(lamv, g_subln.reshape(V_DIM, 1), proj, proj, proj, kbias)


def _pack_rows(x):
    half = x.shape[1] // 2
    lo = lax.bitcast_convert_type(x[:, :half].astype(BF16).astype(F32), jnp.uint32)
    hi = lax.bitcast_convert_type(x[:, half:].astype(BF16).astype(F32), jnp.uint32)
    return (hi & jnp.uint32(0xFFFF0000)) | (lo >> 16)


def _unpack_rows(u):
    lo = lax.bitcast_convert_type(u << 16, F32)
    hi = lax.bitcast_convert_type(u & jnp.uint32(0xFFFF0000), F32)
    return lo, hi


SLAB = 4


def _load_slabs(ref, n):
    return jnp.concatenate([ref[pl.ds(c, n, stride=SLAB), :] for c in range(SLAB)], axis=1)


def _store_slabs(ref, u):
    n = u.shape[0]
    for c in range(SLAB):
        ref[pl.ds(c, n, stride=SLAB), :] = u[:, c * LANES:(c + 1) * LANES]


def _mixout_kernel(x_ref, yr_ref, ao_ref, ga_ref, gb_ref, wr_ref, wa_ref, wo_ref,
                   gpost_ref, gt1_ref, gpre_ref, sh2_ref, sc2_ref, x1_ref, h2_ref, h2p_ref):
    ya = jnp.dot(yr_ref[...], wr_ref[...], preferred_element_type=F32)
    yb = jnp.dot(ao_ref[...], wa_ref[...], preferred_element_type=F32)
    merged = (jax.nn.sigmoid(ga_ref[...].astype(F32)) * ya
              + jax.nn.sigmoid(gb_ref[...].astype(F32)) * yb)
    y = jnp.dot(merged.astype(BF16), wo_ref[...], preferred_element_type=F32)
    x1 = x_ref[...] + gt1_ref[...] * _rms(y, gpost_ref[...])
    x1_ref[...] = x1
    h2 = _rms(x1, gpre_ref[...]) * (1.0 + sc2_ref[...]) + sh2_ref[...]
    h2_ref[...] = h2
    _store_slabs(h2p_ref, _pack_rows(h2))


def _mixout(x2, yr, ao, proj, wr_bf, wa_bf, wo_bf, g_post, g_pre, mod4, S):
    T, D = x2.shape
    tm = _tile(S, 512)
    per_b = S // tm
    gac, gbc = 5, 6
    row = lambda: pl.BlockSpec((tm, D), lambda i: (i, 0))
    wsp = lambda: pl.BlockSpec((D, D), lambda i: (0, 0))
    vec = lambda: pl.BlockSpec((1, D), lambda i: (0, 0))
    modv = lambda j: pl.BlockSpec((None, None, 1, D), lambda i: (i // per_b, j, 0, 0))
    return pl.pallas_call(
        _mixout_kernel,
        grid=(T // tm,),
        in_specs=[row(), row(), row(),
                  pl.BlockSpec((tm, D), lambda i: (i, gac)),
                  pl.BlockSpec((tm, D), lambda i: (i, gbc)),
                  wsp(), wsp(), wsp(), vec(), modv(2), vec(), modv(3), modv(4)],
        out_specs=[row(), row(), pl.BlockSpec((tm * SLAB, LANES), lambda i: (i, 0))],
        out_shape=[jax.ShapeDtypeStruct((T, D), F32), jax.ShapeDtypeStruct((T, D), F32),
                   jax.ShapeDtypeStruct((T * SLAB, LANES), jnp.uint32)],
        compiler_params=_cparams(("parallel",)),
        name="mixout",
    )(x2, yr, ao, proj, proj, wr_bf, wa_bf, wo_bf, g_post.reshape(1, D), mod4,
      g_pre.reshape(1, D), mod4, mod4)


def _first_argmax(vals, ids, sentinel):
    m = jnp.max(vals, axis=0, keepdims=True)
    idx = jnp.min(jnp.where(vals == m, ids, sentinel), axis=0, keepdims=True)
    return m, idx


def _router_kernel(h_ref, wr_ref, eb_ref, e_ref, w_ref, r_ref, cnt_ref, cnt_scr, *, tm):
    step = pl.program_id(0)

    @pl.when(step == 0)
    def _():
        cnt_scr[...] = jnp.zeros_like(cnt_scr)

    logits = lax.dot_general(wr_ref[...], h_ref[...], (((1,), (1,)), ((), ())),
                             preferred_element_type=F32, precision=lax.Precision.HIGHEST)
    scores = jax.nn.sigmoid(logits)
    choice = scores + eb_ref[...]
    i8 = lax.broadcasted_iota(jnp.int32, (GROUP_SIZE, tm), 0)
    neg_inf = jnp.float32(-jnp.inf)

    slabs = [choice[g * GROUP_SIZE:(g + 1) * GROUP_SIZE, :] for g in range(N_GROUPS)]
    sc_slabs = [scores[g * GROUP_SIZE:(g + 1) * GROUP_SIZE, :] for g in range(N_GROUPS)]

    gs = jnp.zeros((N_GROUPS, tm), F32)
    for g in range(N_GROUPS):
        m1, idx1 = _first_argmax(slabs[g], i8, GROUP_SIZE)
        m2 = jnp.max(jnp.where(i8 == idx1, neg_inf, slabs[g]), axis=0, keepdims=True)
        gs = jnp.where(i8 == g, m1 + m2, gs)

    sel = jnp.zeros((N_GROUPS, tm), jnp.int32)
    cur = gs
    for _ in range(TOPK_GROUPS):
        _, idx = _first_argmax(cur, i8, N_GROUPS)
        hit = i8 == idx
        sel = jnp.where(hit, 1, sel)
        cur = jnp.where(hit, neg_inf, cur)

    masked = [jnp.where(sel[g:g + 1, :] > 0, slabs[g], neg_inf) for g in range(N_GROUPS)]
    ids = [i8 + g * GROUP_SIZE for g in range(N_GROUPS)]
    onehot = [jnp.zeros((GROUP_SIZE, tm), F32) for _ in range(N_GROUPS)]
    picks = []
    wts = []
    for _ in range(TOP_K):
        m = functools.reduce(jnp.maximum,
                             [jnp.max(c, axis=0, keepdims=True) for c in masked])
        idx = functools.reduce(
            jnp.minimum,
            [jnp.min(jnp.where(c == m, i, N_EXPERTS), axis=0, keepdims=True)
             for c, i in zip(masked, ids)])
        w = jnp.zeros((1, tm), F32)
        for g in range(N_GROUPS):
            hit = ids[g] == idx
            w = w + jnp.sum(jnp.where(hit, sc_slabs[g], 0.0), axis=0, keepdims=True)
            masked[g] = jnp.where(hit, neg_inf, masked[g])
            onehot[g] = jnp.where(hit, 1.0, onehot[g])
        picks.append(idx)
        wts.append(w)

    wsum = functools.reduce(lambda a, b: a + b, wts)
    norm = ROUTED_SCALE / (wsum + 1e-20)

    t_row = lax.broadcasted_iota(jnp.int32, (tm, tm), 0)
    t_col = lax.broadcasted_iota(jnp.int32, (tm, tm), 1)
    before = jnp.where(t_row < t_col, 1.0, 0.0).astype(BF16)
    cum = [jnp.dot(onehot[g].astype(BF16), before, preferred_element_type=F32)
           + cnt_scr[g * GROUP_SIZE:(g + 1) * GROUP_SIZE, :] for g in range(N_GROUPS)]

    for kk in range(TOP_K):
        rank = jnp.zeros((1, tm), F32)
        for g in range(N_GROUPS):
            rank = rank + jnp.sum(jnp.where(ids[g] == picks[kk], cum[g], 0.0),
                                  axis=0, keepdims=True)
        e_ref[kk:kk + 1, :] = picks[kk]
        w_ref[kk:kk + 1, :] = wts[kk] * norm
        r_ref[kk:kk + 1, :] = rank.astype(jnp.int32)

    for g in range(N_GROUPS):
        rows = slice(g * GROUP_SIZE, (g + 1) * GROUP_SIZE)
        cnt_scr[rows, :] = cnt_scr[rows, :] + jnp.sum(onehot[g], axis=1, keepdims=True)
    cnt_ref[...] = jnp.broadcast_to(cnt_scr[...], cnt_ref.shape)


def _router(h2, w_router_t, e_bias):
    T, D = h2.shape
    tm = _tile(T, 512)
    return pl.pallas_call(
        functools.partial(_router_kernel, tm=tm),
        grid=(T // tm,),
        in_specs=[pl.BlockSpec((tm, D), lambda i: (i, 0)),
                  pl.BlockSpec((N_EXPERTS, D), lambda i: (0, 0)),
                  pl.BlockSpec((N_EXPERTS, 1), lambda i: (0, 0))],
        out_specs=[pl.BlockSpec((TOP_K, tm), lambda i: (0, i)),
                   pl.BlockSpec((TOP_K, tm), lambda i: (0, i)),
                   pl.BlockSpec((TOP_K, tm), lambda i: (0, i)),
                   pl.BlockSpec((N_EXPERTS, LANES), lambda i: (0, 0))],
        out_shape=[jax.ShapeDtypeStruct((TOP_K, T), jnp.int32),
                   jax.ShapeDtypeStruct((TOP_K, T), F32),
                   jax.ShapeDtypeStruct((TOP_K, T), jnp.int32),
                   jax.ShapeDtypeStruct((N_EXPERTS, LANES), F32)],
        scratch_shapes=[pltpu.VMEM((N_EXPERTS, 1), F32)],
        compiler_params=_cparams(("arbitrary",)),
        name="router",
    )(h2, w_router_t, e_bias.reshape(N_EXPERTS, 1))


def _slots_kernel(offs_ref, e_ref, r_ref, d_ref):
    e = e_ref[...]
    d = r_ref[...]
    for ex in range(N_EXPERTS):
        d = d + jnp.where(e == ex, offs_ref[ex], 0)
    d_ref[...] = d


def _slots(offs, top_e, rank):
    K, T = top_e.shape
    tm = _tile(T, 4096)
    grid_spec = pltpu.PrefetchScalarGridSpec(
        num_scalar_prefetch=1,
        grid=(T // tm,),
        in_specs=[pl.BlockSpec((K, tm), lambda i, o: (0, i)),
                  pl.BlockSpec((K, tm), lambda i, o: (0, i))],
        out_specs=pl.BlockSpec((K, tm), lambda i, o: (0, i)),
    )
    return pl.pallas_call(
        _slots_kernel,
        grid_spec=grid_spec,
        out_shape=jax.ShapeDtypeStruct((K, T), jnp.int32),
        compiler_params=_cparams(("parallel",)),
        name="slots",
    )(offs, top_e, rank)


def _row_copy(src, src_row, dst, dst_row, sem):
    s0 = pl.multiple_of(src_row * SLAB, SLAB)
    d0 = pl.multiple_of(dst_row * SLAB, SLAB)
    return pltpu.make_async_copy(src.at[pl.ds(s0, SLAB), :], dst.at[pl.ds(d0, SLAB), :], sem)


def _dispatch_kernel(dest_ref, h_ref, xs_ref, sem, *, td):
    def issue(j, carry):
        for kk in range(TOP_K):
            _row_copy(h_ref, j, xs_ref, dest_ref[0, kk * td + j], sem).start(priority=kk % 2)
        return carry

    lax.fori_loop(0, td, issue, 0)
    for _ in range(TOP_K):
        pltpu.make_async_copy(h_ref, xs_ref.at[pl.ds(0, td * SLAB), :], sem).wait()


def _dispatch(h2, dest_tiles, td):
    T = h2.shape[0] // SLAB
    nt = T // td
    return pl.pallas_call(
        functools.partial(_dispatch_kernel, td=td),
        grid=(nt,),
        in_specs=[pl.BlockSpec((None, 1, TOP_K * td), lambda i: (i, 0, 0),
                               memory_space=pltpu.SMEM),
                  pl.BlockSpec((td * SLAB, LANES), lambda i: (i, 0))],
        out_specs=pl.BlockSpec(memory_space=pl.ANY),
        out_shape=jax.ShapeDtypeStruct((T * TOP_K * SLAB, LANES), h2.dtype),
        scratch_shapes=[pltpu.SemaphoreType.DMA],
        compiler_params=_cparams(("arbitrary",)),
        name="dispatch",
    )(dest_tiles, h2)


def _gmm_kernel(blk_ref, exp_ref, lo_ref, hi_ref, first_ref, newe_ref, x_ref, w1_ref, w3_ref,
                w2_ref, o_ref, w1b, w3b, w2b, *, bm):
    it = pl.program_id(0)

    @pl.when(newe_ref[it] == 1)
    def _():
        w1b[...] = w1_ref[...].astype(BF16)
        w3b[...] = w3_ref[...].astype(BF16)
        w2b[...] = w2_ref[...].astype(BF16)

    rows = lax.broadcasted_iota(jnp.int32, (bm, 1), 0)
    valid = jnp.logical_and(rows >= lo_ref[it], rows < hi_ref[it])
    x_lo, x_hi = _unpack_rows(jnp.where(valid, _load_slabs(x_ref, bm), jnp.uint32(0)))
    x_lo = x_lo.astype(BF16)
    x_hi = x_hi.astype(BF16)
    half = x_lo.shape[1]
    h1 = (jnp.dot(x_lo, w1b[:half, :], preferred_element_type=F32)
          + jnp.dot(x_hi, w1b[half:, :], preferred_element_type=F32))
    h3 = (jnp.dot(x_lo, w3b[:half, :], preferred_element_type=F32)
          + jnp.dot(x_hi, w3b[half:, :], preferred_element_type=F32))
    hb = (h1 * jax.nn.sigmoid(h1) * h3).astype(BF16)
    y = jnp.dot(hb, w2b[...], preferred_element_type=F32)

    @pl.when(first_ref[it] == 1)
    def _():
        _store_slabs(o_ref, _pack_rows(y))

    @pl.when(first_ref[it] == 0)
    def _():
        o_lo, o_hi = _unpack_rows(_load_slabs(o_ref, bm))
        _store_slabs(o_ref, _pack_rows(y + jnp.concatenate([o_lo, o_hi], axis=1)))


def _gmm(xs, items, w1, w3, w2, bm):
    A = xs.shape[0] // SLAB
    D = w1.shape[1]
    F = w1.shape[2]
    n_items = items[0].shape[0]
    grid_spec = pltpu.PrefetchScalarGridSpec(
        num_scalar_prefetch=6,
        grid=(n_items,),
        in_specs=[pl.BlockSpec((bm * SLAB, LANES), lambda i, blk, ex, *_: (blk[i], 0)),
                  pl.BlockSpec((None, D, F), lambda i, blk, ex, *_: (ex[i], 0, 0)),
                  pl.BlockSpec((None, D, F), lambda i, blk, ex, *_: (ex[i], 0, 0)),
                  pl.BlockSpec((None, F, D), lambda i, blk, ex, *_: (ex[i], 0, 0))],
        out_specs=pl.BlockSpec((bm * SLAB, LANES), lambda i, blk, ex, *_: (blk[i], 0)),
        scratch_shapes=[pltpu.VMEM((D, F), BF16), pltpu.VMEM((D, F), BF16),
                        pltpu.VMEM((F, D), BF16)],
    )
    return pl.pallas_call(
        functools.partial(_gmm_kernel, bm=bm),
        grid_spec=grid_spec,
        out_shape=jax.ShapeDtypeStruct((A * SLAB, LANES), jnp.uint32),
        compiler_params=_cparams(("arbitrary",)),
        name="gmm",
    )(*items, xs, w1, w3, w2)


def _work_items(counts, bm, n_blocks):
    n_items = n_blocks + N_EXPERTS - 1
    ends = jnp.cumsum(counts)
    starts = ends - counts
    nb = jnp.where(counts > 0, (ends - 1) // bm - starts // bm + 1, 0)
    item_end = jnp.cumsum(nb)
    item_start = item_end - nb
    n_real = item_end[-1]
    i = jnp.arange(n_items, dtype=jnp.int32)
    e = jnp.minimum(jnp.sum(item_end[None, :] <= i[:, None], axis=1), N_EXPERTS - 1).astype(jnp.int32)
    onehot = e[:, None] == jnp.arange(N_EXPERTS, dtype=jnp.int32)[None, :]
    pick = lambda v: jnp.sum(jnp.where(onehot, v[None, :], 0), axis=1)
    start_e, end_e = pick(starts), pick(ends)
    blk = start_e // bm + (i - pick(item_start))
    lo = jnp.clip(start_e - blk * bm, 0, bm)
    hi = jnp.clip(end_e - blk * bm, 0, bm)
    real = i < n_real
    blk = jnp.where(real, blk, n_blocks - 1).astype(jnp.int32)
    lo = jnp.where(real, lo, 0).astype(jnp.int32)
    hi = jnp.where(real, hi, 0).astype(jnp.int32)
    one = jnp.ones((1,), jnp.int32)
    first = jnp.concatenate([one, (blk[1:] != blk[:-1]).astype(jnp.int32)])
    new_e = jnp.concatenate([one, (e[1:] != e[:-1]).astype(jnp.int32)])
    return blk, e, lo, hi, first, new_e


def _combine_kernel(dest_ref, nxt_ref, ys_ref, w_ref, x1_ref, h_ref, w1_ref, w3_ref, w2_ref,
                    gt2_ref, g_ref, o_ref, buf, y_scr, sems, *, tc):
    i = pl.program_id(0)
    n = pl.num_programs(0)
    slot = i % 2

    def gather(idx_ref, s):
        def issue(j, carry):
            for kk in range(TOP_K):
                _row_copy(ys_ref, idx_ref[0, kk * tc + j], buf.at[s, kk], j,
                          sems.at[s]).start(priority=kk % 2)
            return carry

        lax.fori_loop(0, tc, issue, 0, unroll=4)

    @pl.when(i == 0)
    def _():
        gather(dest_ref, 0)

    @pl.when(i + 1 < n)
    def _():
        gather(nxt_ref, 1 - slot)

    h_lo, h_hi = _unpack_rows(_load_slabs(h_ref, tc))
    h_lo = h_lo.astype(BF16)
    h_hi = h_hi.astype(BF16)
    half = h_lo.shape[1]
    h1 = (jnp.dot(h_lo, w1_ref[:half, :], preferred_element_type=F32)
          + jnp.dot(h_hi, w1_ref[half:, :], preferred_element_type=F32))
    h3 = (jnp.dot(h_lo, w3_ref[:half, :], preferred_element_type=F32)
          + jnp.dot(h_hi, w3_ref[half:, :], preferred_element_type=F32))
    y_scr[...] = jnp.dot((h1 * jax.nn.sigmoid(h1) * h3).astype(BF16), w2_ref[...],
                         preferred_element_type=F32)

    for kk in range(TOP_K):
        pltpu.make_async_copy(ys_ref.at[pl.ds(0, tc * SLAB), :], buf.at[slot, kk],
                              sems.at[slot]).wait()
    rows = 2 * SUBLANES
    for r0 in range(0, tc, rows):
        for kk in range(TOP_K):
            u = jnp.concatenate([buf[slot, kk, pl.ds(r0 * SLAB + c, rows, stride=SLAB), :]
                                 for c in range(SLAB)], axis=1)
            e_lo, e_hi = _unpack_rows(u)
            wk = w_ref[r0:r0 + rows, kk:kk + 1]
            moe_lo = wk * e_lo if kk == 0 else moe_lo + wk * e_lo
            moe_hi = wk * e_hi if kk == 0 else moe_hi + wk * e_hi
        y_scr[r0:r0 + rows, :] += jnp.concatenate([moe_lo, moe_hi], axis=1)
    o_ref[...] = x1_ref[...] + gt2_ref[...] * _rms(y_scr[...], g_ref[...])


def _combine(ys, dest_tiles, w_tok, x1, h2, w1s, w3s, w2s, mod4, g_post, S, tc):
    T, D = x1.shape
    F = w1s.shape[1]
    per_b = S // tc
    nt = T // tc
    row = lambda: pl.BlockSpec((tc, D), lambda i: (i, 0))
    return pl.pallas_call(
        functools.partial(_combine_kernel, tc=tc),
        grid=(nt,),
        in_specs=[pl.BlockSpec((None, 1, TOP_K * tc), lambda i: (i, 0, 0),
                               memory_space=pltpu.SMEM),
                  pl.BlockSpec((None, 1, TOP_K * tc), lambda i: (jnp.minimum(i + 1, nt - 1), 0, 0),
                               memory_space=pltpu.SMEM),
                  pl.BlockSpec(memory_space=pl.ANY),
                  pl.BlockSpec((tc, TOP_K), lambda i: (i, 0)),
                  row(), pl.BlockSpec((tc * SLAB, LANES), lambda i: (i, 0)),
                  pl.BlockSpec((D, F), lambda i: (0, 0)),
                  pl.BlockSpec((D, F), lambda i: (0, 0)),
                  pl.BlockSpec((F, D), lambda i: (0, 0)),
                  pl.BlockSpec((None, None, 1, D), lambda i: (i // per_b, 5, 0, 0)),
                  pl.BlockSpec((1, D), lambda i: (0, 0))],
        out_specs=row(),
        out_shape=jax.ShapeDtypeStruct((T, D), F32),
        scratch_shapes=[pltpu.VMEM((2, TOP_K, tc * SLAB, LANES), jnp.uint32),
                        pltpu.VMEM((tc, D), F32), pltpu.SemaphoreType.DMA((2,))],
        compiler_params=_cparams(("arbitrary",)),
        name="combine",
    )(dest_tiles, dest_tiles, ys, w_tok, x1, h2, w1s, w3s, w2s, mod4, g_post.reshape(1, D))


def _lambda_init(layer):
    return 0.8 - 0.6 * math.exp(-0.3 * layer)


def kernel(x, c, w_ada, b_ada, g_pre_mix, w_in, conv_w, conv_b, lru_wa, lru_ba, lru_wx, lru_bx,
           lru_lambda, lam_q1, lam_k1, lam_q2, lam_k2, g_subln, w_proj_rnn, w_proj_att, w_out,
           g_post_mix, g_pre_ffn, w_router, e_bias, w1_e, w3_e, w2_e, w1_s, w3_s, w2_s,
           g_post_ffn):
    B, S, D = x.shape
    T = B * S
    depth = w_ada.shape[0]
    slopes = np.exp2(-8.0 * np.arange(1, N_HEADS + 1, dtype=np.float32) / N_HEADS)
    kbias = _alibi_tables(slopes, S)
    tt = _tile(S, 256)
    bm = _tile(T * TOP_K, 512)
    n_blocks = T * TOP_K // bm

    x2 = x.reshape(T, D)
    for l in range(depth):
        lam_init = _lambda_init(l)
        mod4 = _ada(c, w_ada[l], b_ada[l]).reshape(B, 6, 1, D)

        proj = _inproj(x2, g_pre_mix[l], mod4, w_in[l].astype(BF16), S)
        yr = _lru(proj, conv_w[l], conv_b[l], lru_wa[l].astype(BF16), lru_ba[l],
                  lru_wx[l].astype(BF16), lru_bx[l], lru_lambda[l], B, S)
        lamv = jnp.stack([lam_q1[l], lam_k1[l], lam_q2[l], lam_k2[l]])
        ao = _attn(proj, kbias, lamv, g_subln[l], B, S, lam_init)
        x1, h2, h2p = _mixout(x2, yr, ao, proj, w_proj_rnn[l].astype(BF16),
                         w_proj_att[l].astype(BF16), w_out[l].astype(BF16),
                         g_post_mix[l], g_pre_ffn[l], mod4, S)

        top_e, top_w, rank, cnt = _router(h2, w_router[l].T, e_bias[l])
        counts = cnt[:, 0].astype(jnp.int32)
        offs = jnp.cumsum(counts) - counts
        dest = _slots(offs, top_e, rank)
        dest_tiles = dest.reshape(TOP_K, T // tt, tt).transpose(1, 0, 2).reshape(T // tt, 1,
                                                                                  TOP_K * tt)
        xs = _dispatch(h2p, dest_tiles, tt)
        items = _work_items(counts, bm, n_blocks)
        ys = _gmm(xs, items, w1_e[l], w3_e[l], w2_e[l], bm)
        x2 = _combine(ys, dest_tiles, top_w.T, x1, h2p, w1_s[l].astype(BF16),
                      w3_s[l].astype(BF16), w2_s[l].astype(BF16), mod4, g_post_ffn[l], S, tt)
    return x2.reshape(B, S, D)
```

```python
import functools
import math

import jax
import jax.numpy as jnp
import numpy as np
from jax import lax
from jax.experimental import pallas as pl
from jax.experimental.pallas import tpu as pltpu

F32 = jnp.float32
BF16 = jnp.bfloat16

EPS = 1e-6
N_HEADS = 8
HEAD_DIM = 64
V_DIM = 2 * HEAD_DIM
LRU_BLOCKS = 8
CONV_W = 4
LRU_C = 8.0
N_EXPERTS = 64
TOP_K = 8
N_GROUPS = 8
GROUP_SIZE = N_EXPERTS // N_GROUPS
TOPK_GROUPS = 4
ROUTED_SCALE = 2.5

LANES = 128
SUBLANES = 8
VMEM_LIMIT = 48 * 1024 * 1024


def _cparams(sem):
    return pltpu.CompilerParams(dimension_semantics=sem, vmem_limit_bytes=VMEM_LIMIT)


def _tile(n, pref):
    t = min(n, pref)
    while n % t:
        t //= 2
    return t


def _rms(x, g):
    return x * lax.rsqrt(jnp.mean(x * x, axis=-1, keepdims=True) + EPS) * g


def _ada_kernel(c_ref, w_ref, b_ref, o_ref):
    c = c_ref[...]
    cond = c * jax.nn.sigmoid(c)
    o_ref[...] = jnp.dot(cond, w_ref[...], preferred_element_type=F32) + b_ref[...]


def _ada(c, w, b):
    B, D = c.shape
    N = w.shape[1]
    tn = _tile(N, 1024)
    return pl.pallas_call(
        _ada_kernel,
        grid=(N // tn,),
        in_specs=[pl.BlockSpec((B, D), lambda j: (0, 0)),
                  pl.BlockSpec((D, tn), lambda j: (0, j)),
                  pl.BlockSpec((1, tn), lambda j: (0, j))],
        out_specs=pl.BlockSpec((B, tn), lambda j: (0, j)),
        out_shape=jax.ShapeDtypeStruct((B, N), F32),
        compiler_params=_cparams(("parallel",)),
        name="ada",
    )(c, w, b.reshape(1, N))


LOG2E = 1.4426950408889634
Q_COL_BLOCK = 2
Q_PRESCALE = HEAD_DIM ** -0.5 * LOG2E


def _inproj_kernel(x_ref, g_ref, sh_ref, sc_ref, w_ref, o_ref, h_scr):
    @pl.when(pl.program_id(1) == 0)
    def _():
        h = _rms(x_ref[...], g_ref[...]) * (1.0 + sc_ref[...]) + sh_ref[...]
        h_scr[...] = h.astype(BF16)

    r = jnp.dot(h_scr[...], w_ref[...], preferred_element_type=F32)
    r = r * jnp.where(pl.program_id(1) == Q_COL_BLOCK, Q_PRESCALE, 1.0)
    o_ref[...] = r.astype(o_ref.dtype)


def _inproj(x2, g, mod4, w_bf, S):
    T, D = x2.shape
    N = w_bf.shape[1]
    tm = _tile(S, 2048)
    tn = 1024
    per_b = S // tm
    return pl.pallas_call(
        _inproj_kernel,
        grid=(T // tm, N // tn),
        in_specs=[pl.BlockSpec((tm, D), lambda i, j: (i, 0)),
                  pl.BlockSpec((1, D), lambda i, j: (0, 0)),
                  pl.BlockSpec((None, None, 1, D), lambda i, j: (i // per_b, 0, 0, 0)),
                  pl.BlockSpec((None, None, 1, D), lambda i, j: (i // per_b, 1, 0, 0)),
                  pl.BlockSpec((D, tn), lambda i, j: (0, j))],
        out_specs=pl.BlockSpec((tm, tn), lambda i, j: (i, j)),
        out_shape=jax.ShapeDtypeStruct((T, N), BF16),
        scratch_shapes=[pltpu.VMEM((tm, D), BF16)],
        compiler_params=_cparams(("parallel", "arbitrary")),
        name="inproj",
    )(x2, g.reshape(1, D), mod4, mod4, w_bf)


def _lru_kernel(xr_ref, gr_ref, cw_ref, cb_ref, wa_ref, ba_ref, wx_ref, bx_ref, lam_ref,
                o_ref, xc_scr, prev_scr, h_scr, *, ts):
    s = pl.program_id(1)

    @pl.when(s == 0)
    def _():
        prev_scr[...] = jnp.zeros_like(prev_scr)
        h_scr[...] = jnp.zeros_like(h_scr)

    x = xr_ref[...].astype(F32)
    prev = prev_scr[...]
    row8 = lax.broadcasted_iota(jnp.int32, (SUBLANES, 1), 0)
    acc = cb_ref[...] + cw_ref[CONV_W - 1:CONV_W, :] * x
    xc_scr[...] = acc
    top = cb_ref[...] + cw_ref[CONV_W - 1:CONV_W, :] * x[0:SUBLANES, :]
    for j in range(1, CONV_W):
        wj = cw_ref[CONV_W - 1 - j:CONV_W - j, :]
        rj = pltpu.roll(x, j, axis=0)
        xc_scr[...] += wj * rj
        pj = pltpu.roll(prev, j, axis=0)
        top += wj * jnp.where(row8 < j, pj, rj[0:SUBLANES, :])
    xc_scr[0:SUBLANES, :] = top
    prev_scr[...] = x[ts - SUBLANES:ts, :]

    row = lax.broadcasted_iota(jnp.int32, (ts, 1), 0)
    is_first = jnp.logical_and(row == 0, s == 0)
    sub = lax.broadcasted_iota(jnp.int32, (1, SUBLANES, 1), 1)
    for n in range(LRU_BLOCKS):
        cols = slice(n * LANES, (n + 1) * LANES)
        xc = xc_scr[:, cols]
        xb = xc.astype(BF16)
        r = jax.nn.sigmoid(jnp.dot(xb, wa_ref[n], preferred_element_type=F32) + ba_ref[:, cols])
        i = jax.nn.sigmoid(jnp.dot(xb, wx_ref[n], preferred_element_type=F32) + bx_ref[:, cols])
        lam = lam_ref[:, cols]
        softplus_neg = jnp.maximum(-lam, 0.0) + jnp.log1p(jnp.exp(-jnp.abs(lam)))
        log_a = (-LRU_C * softplus_neg) * r
        a = jnp.exp(log_a)
        m2 = 1.0 - a * a
        mult = jnp.where(m2 > 0.0, m2 * lax.rsqrt(m2), 0.0)
        mult = jnp.where(is_first, 1.0, mult)
        u = mult * (i * xc)
        a = a.reshape(ts // SUBLANES, SUBLANES, LANES)
        u = u.reshape(ts // SUBLANES, SUBLANES, LANES)
        for d in (1, 2, 4):
            keep = sub >= d
            a_sh = jnp.where(keep, pltpu.roll(a, d, axis=1), 1.0)
            u_sh = jnp.where(keep, pltpu.roll(u, d, axis=1), 0.0)
            u = u + a * u_sh
            a = a * a_sh
        a = a.reshape(ts, LANES)
        u = u.reshape(ts, LANES)
        gate = jax.nn.gelu(gr_ref[:, cols].astype(F32))
        carry = h_scr[:, cols]
        step = 2 * SUBLANES
        for g in range(ts // step):
            r0 = g * step
            h0 = u[r0:r0 + SUBLANES, :] + a[r0:r0 + SUBLANES, :] * carry
            carry = h0[SUBLANES - 1:SUBLANES, :]
            h1 = u[r0 + SUBLANES:r0 + step, :] + a[r0 + SUBLANES:r0 + step, :] * carry
            carry = h1[SUBLANES - 1:SUBLANES, :]
            hg = jnp.concatenate([h0, h1], axis=0) * gate[r0:r0 + step, :]
            o_ref[r0:r0 + step, cols] = hg.astype(o_ref.dtype)
        h_scr[:, cols] = carry


def _lru(proj, conv_w, conv_b, wa_bf, ba, wx_bf, bx, lam, B, S):
    C = conv_w.shape[1]
    ts = _tile(S, 256)
    ns = S // ts
    vec = lambda: pl.BlockSpec((1, C), lambda b, s: (0, 0))
    blk = lambda: pl.BlockSpec((LRU_BLOCKS, LANES, LANES), lambda b, s: (0, 0, 0))
    return pl.pallas_call(
        functools.partial(_lru_kernel, ts=ts),
        grid=(B, ns),
        in_specs=[pl.BlockSpec((ts, C), lambda b, s: (b * ns + s, 0)),
                  pl.BlockSpec((ts, C), lambda b, s: (b * ns + s, 1)),
                  pl.BlockSpec((CONV_W, C), lambda b, s: (0, 0)),
                  vec(), blk(), vec(), blk(), vec(), vec()],
        out_specs=pl.BlockSpec((ts, C), lambda b, s: (b * ns + s, 0)),
        out_shape=jax.ShapeDtypeStruct((B * S, C), BF16),
        scratch_shapes=[pltpu.VMEM((ts, C), F32), pltpu.VMEM((SUBLANES, C), F32),
                        pltpu.VMEM((1, C), F32)],
        compiler_params=_cparams(("parallel", "arbitrary")),
        name="lru",
    )(proj, proj, conv_w, conv_b.reshape(1, C), wa_bf, ba.reshape(1, C), wx_bf, bx.reshape(1, C),
      lam.reshape(1, C))


BIAS_LANES = 3
ACC_ROWS = V_DIM + 2 * SUBLANES


def _alibi_tables(slopes, S):
    def top16(x):
        return (x.view(np.uint32) & np.uint32(0xFFFF0000)).view(np.float32)

    pos = np.arange(S, dtype=np.float32)
    b = (slopes.astype(np.float32) * np.float32(LOG2E))[:, None] * pos[None, :]
    hi = top16(b)
    mid = top16(b - hi)
    lo = top16(b - hi - mid)
    half = np.zeros(b.shape + (HEAD_DIM,), np.float32)
    half[..., 0], half[..., 1], half[..., 2] = hi, mid, lo
    zero = np.zeros_like(half)
    table = np.stack([np.concatenate([zero, half], axis=-1),
                      np.concatenate([half, zero], axis=-1)], axis=1)
    return jnp.asarray(table, dtype=BF16)


def _attn_kernel(lamv_ref, gsub_ref, q_ref, k_ref, v_ref, kb_ref, o_ref,
                 vt_scr, kx_scr, qx_scr, s_scr, p_scr, a_scr, m_scr, acc_scr, *, tq, lam_init):
    qi = pl.program_id(2)
    tk = tq
    nk = vt_scr.shape[0]

    lane = lax.broadcasted_iota(jnp.int32, (1, V_DIM), 1)
    own = (jnp.where(lane < HEAD_DIM, 1.0, 0.0), jnp.where(lane >= HEAD_DIM, 1.0, 0.0))
    ones_row = (jnp.where(jnp.logical_and(lane >= HEAD_DIM, lane < HEAD_DIM + BIAS_LANES), 1.0, 0.0),
                jnp.where(lane < BIAS_LANES, 1.0, 0.0))

    @pl.when(qi == 0)
    def _():
        for j in range(nk):
            rows = slice(j * tk, (j + 1) * tk)
            vt_scr[j, 0:V_DIM, :] = v_ref[rows, :].astype(F32).T.astype(BF16)
            vt_scr[j, V_DIM:ACC_ROWS, :] = jnp.ones((ACC_ROWS - V_DIM, tk), BF16)
            kf = k_ref[rows, :].astype(F32)
            for mp in range(2):
                kx_scr[mp, rows, :] = (kf * own[mp] + kb_ref[mp, rows, :].astype(F32)).astype(BF16)

    q = q_ref[...].astype(F32)
    for mp in range(2):
        qx_scr[mp] = (q * own[mp] + ones_row[mp]).astype(BF16)
    m_scr[...] = jnp.full_like(m_scr, -jnp.inf)
    acc_scr[...] = jnp.zeros_like(acc_scr)
    row8 = lax.broadcasted_iota(jnp.int32, (SUBLANES, LANES), 0)
    col = lax.broadcasted_iota(jnp.int32, (1, LANES), 1)
    neg_inf = jnp.float32(-jnp.inf)

    def tree(op, parts):
        parts = [p for p in parts if p is not None]
        while len(parts) > 1:
            parts = [op(parts[i], parts[i + 1]) if i + 1 < len(parts) else parts[i]
                     for i in range(0, len(parts), 2)]
        return parts[0]

    def scores(j, par):
        start = pl.multiple_of(j * tk, tk)
        for mp in range(2):
            s = lax.dot_general(kx_scr[mp, pl.ds(start, tk), :], qx_scr[mp],
                                (((1,), (1,)), ((), ())), preferred_element_type=F32)
            for c in range(tq // LANES):
                s_scr[par, mp, c] = s[:, c * LANES:(c + 1) * LANES]

    def softmax(par, masked):
        for mp in range(2):
            for c in range(tq // LANES):
                cols = slice(c * LANES, (c + 1) * LANES)
                lo_col, hi_col = c * LANES, (c + 1) * LANES - 1
                s_c = s_scr.at[par, mp, c]
                p_c = p_scr.at[par, mp, c]
                accs = [None] * 4
                for i in range(tk // SUBLANES):
                    r0 = i * SUBLANES
                    if masked and r0 > hi_col:
                        continue
                    t = s_c[r0:r0 + SUBLANES, :]
                    if masked and r0 + SUBLANES - 1 > lo_col:
                        t = jnp.where(row8 + r0 <= col + lo_col, t, neg_inf)
                        s_c[r0:r0 + SUBLANES, :] = t
                    accs[i % 4] = t if accs[i % 4] is None else jnp.maximum(accs[i % 4], t)
                mx = jnp.max(tree(jnp.maximum, accs), axis=0, keepdims=True)
                m_old = m_scr[mp, :, cols]
                m_new = jnp.maximum(m_old, mx)
                a_scr[par, mp, :, cols] = jnp.exp2(m_old - m_new)
                m_scr[mp, :, cols] = m_new
                for i in range(tk // (2 * SUBLANES)):
                    r0 = i * 2 * SUBLANES
                    if masked and r0 > hi_col:
                        p_c[r0:r0 + 2 * SUBLANES, :] = jnp.zeros((2 * SUBLANES, LANES), BF16)
                        continue
                    p = jnp.exp2(s_c[r0:r0 + 2 * SUBLANES, :] - m_new)
                    p_c[r0:r0 + 2 * SUBLANES, :] = p.astype(BF16)

    def values(j, par):
        for mp in range(2):
            p = jnp.concatenate([p_scr[par, mp, c] for c in range(tq // LANES)], axis=1)
            pv = jnp.dot(vt_scr[j], p, preferred_element_type=F32)
            acc_scr[mp] = a_scr[par, mp] * acc_scr[mp] + pv

    p_scr[1] = jnp.zeros(p_scr.shape[1:], BF16)
    a_scr[1] = jnp.ones(a_scr.shape[1:], F32)
    scores(0, 0)

    def stage(j, par):
        scores(j + 1, 1 - par)
        softmax(par, False)
        values(jnp.maximum(j - 1, 0), 1 - par)

    def stage_pair(i, carry):
        stage(2 * i, 0)
        stage(2 * i + 1, 1)
        return carry

    def tail(par):
        softmax(par, True)
        values(jnp.maximum(qi - 1, 0), 1 - par)
        values(qi, par)

    lax.fori_loop(0, qi // 2, stage_pair, 0)

    @pl.when(qi % 2 == 1)
    def _():
        stage(qi - 1, 0)
        tail(1)

    @pl.when(qi % 2 == 0)
    def _():
        tail(0)

    lv = lamv_ref[...]
    lam = (jnp.exp(jnp.sum(lv[0:1, :] * lv[1:2, :], axis=-1, keepdims=True))
           - jnp.exp(jnp.sum(lv[2:3, :] * lv[3:4, :], axis=-1, keepdims=True)) + lam_init)
    o0 = acc_scr[0, 0:V_DIM, :] * (1.0 / acc_scr[0, V_DIM:V_DIM + 1, :])
    o1 = acc_scr[1, 0:V_DIM, :] * (1.0 / acc_scr[1, V_DIM:V_DIM + 1, :])
    o_t = o0 - lam * o1
    o_t = o_t * lax.rsqrt(jnp.mean(o_t * o_t, axis=0, keepdims=True) + EPS) * gsub_ref[...]
    o_ref[...] = (o_t * (1.0 - lam_init)).T.astype(o_ref.dtype)


def _attn(proj, kbias, lamv, g_subln, B, S, lam_init):
    tq = _tile(S, 512)
    nq = S // tq
    qc, kc, vc = 2 * 8, 3 * 8, 4 * 8
    return pl.pallas_call(
        functools.partial(_attn_kernel, tq=tq, lam_init=lam_init),
        grid=(B, N_HEADS, nq),
        in_specs=[pl.BlockSpec((4, HEAD_DIM), lambda b, h, qi: (0, 0)),
                  pl.BlockSpec((V_DIM, 1), lambda b, h, qi: (0, 0)),
                  pl.BlockSpec((tq, V_DIM), lambda b, h, qi: (b * nq + qi, qc + h)),
                  pl.BlockSpec((S, V_DIM), lambda b, h, qi: (b, kc + h)),
                  pl.BlockSpec((S, V_DIM), lambda b, h, qi: (b, vc + h)),
                  pl.BlockSpec((None, 2, S, V_DIM), lambda b, h, qi: (h, 0, 0, 0))],
        out_specs=pl.BlockSpec((tq, V_DIM), lambda b, h, qi: (b * nq + qi, h)),
        out_shape=jax.ShapeDtypeStruct((B * S, N_HEADS * V_DIM), BF16),
        scratch_shapes=[pltpu.VMEM((nq, ACC_ROWS, tq), BF16), pltpu.VMEM((2, S, V_DIM), BF16),
                        pltpu.VMEM((2, tq, V_DIM), BF16),
                        pltpu.VMEM((2, 2, tq // LANES, tq, LANES), F32),
                        pltpu.VMEM((2, 2, tq // LANES, tq, LANES), BF16),
                        pltpu.VMEM((2, 2, 1, tq), F32), pltpu.VMEM((2, 1, tq), F32),
                        pltpu.VMEM((2, ACC_ROWS, tq), F32)],
        compiler_params=_cparams(("parallel", "parallel", "arbitrary")),
        name="attn",
    )(lamv, g_subln.reshape(V_DIM, 1), proj, proj, proj, kbias)


def _pack_rows(x):
    half = x.shape[1] // 2
    lo = lax.bitcast_convert_type(x[:, :half].astype(BF16).astype(F32), jnp.uint32)
    hi = lax.bitcast_convert_type(x[:, half:].astype(BF16).astype(F32), jnp.uint32)
    return (hi & jnp.uint32(0xFFFF0000)) | (lo >> 16)


def _unpack_rows(u):
    lo = lax.bitcast_convert_type(u << 16, F32)
    hi = lax.bitcast_convert_type(u & jnp.uint32(0xFFFF0000), F32)
    return lo, hi


SLAB = 4


def _load_slabs(ref, n):
    return jnp.concatenate([ref[pl.ds(c, n, stride=SLAB), :] for c in range(SLAB)], axis=1)


def _store_slabs(ref, u):
    n = u.shape[0]
    for c in range(SLAB):
        ref[pl.ds(c, n, stride=SLAB), :] = u[:, c * LANES:(c + 1) * LANES]


def _mixout_kernel(x_ref, yr_ref, ao_ref, ga_ref, gb_ref, wr_ref, wa_ref, wo_ref,
                   gpost_ref, gt1_ref, gpre_ref, sh2_ref, sc2_ref, x1_ref, h2_ref, h2p_ref):
    ya = jnp.dot(yr_ref[...], wr_ref[...], preferred_element_type=F32)
    yb = jnp.dot(ao_ref[...], wa_ref[...], preferred_element_type=F32)
    merged = (jax.nn.sigmoid(ga_ref[...].astype(F32)) * ya
              + jax.nn.sigmoid(gb_ref[...].astype(F32)) * yb)
    y = jnp.dot(merged.astype(BF16), wo_ref[...], preferred_element_type=F32)
    x1 = x_ref[...] + gt1_ref[...] * _rms(y, gpost_ref[...])
    x1_ref[...] = x1
    h2 = _rms(x1, gpre_ref[...]) * (1.0 + sc2_ref[...]) + sh2_ref[...]
    h2_ref[...] = h2
    _store_slabs(h2p_ref, _pack_rows(h2))


def _mixout(x2, yr, ao, proj, wr_bf, wa_bf, wo_bf, g_post, g_pre, mod4, S):
    T, D = x2.shape
    tm = _tile(S, 512)
    per_b = S // tm
    gac, gbc = 5, 6
    row = lambda: pl.BlockSpec((tm, D), lambda i: (i, 0))
    wsp = lambda: pl.BlockSpec((D, D), lambda i: (0, 0))
    vec = lambda: pl.BlockSpec((1, D), lambda i: (0, 0))
    modv = lambda j: pl.BlockSpec((None, None, 1, D), lambda i: (i // per_b, j, 0, 0))
    return pl.pallas_call(
        _mixout_kernel,
        grid=(T // tm,),
        in_specs=[row(), row(), row(),
                  pl.BlockSpec((tm, D), lambda i: (i, gac)),
                  pl.BlockSpec((tm, D), lambda i: (i, gbc)),
                  wsp(), wsp(), wsp(), vec(), modv(2), vec(), modv(3), modv(4)],
        out_specs=[row(), row(), pl.BlockSpec((tm * SLAB, LANES), lambda i: (i, 0))],
        out_shape=[jax.ShapeDtypeStruct((T, D), F32), jax.ShapeDtypeStruct((T, D), F32),
                   jax.ShapeDtypeStruct((T * SLAB, LANES), jnp.uint32)],
        compiler_params=_cparams(("parallel",)),
        name="mixout",
    )(x2, yr, ao, proj, proj, wr_bf, wa_bf, wo_bf, g_post.reshape(1, D), mod4,
      g_pre.reshape(1, D), mod4, mod4)


def _first_argmax(vals, ids, sentinel):
    m = jnp.max(vals, axis=0, keepdims=True)
    idx = jnp.min(jnp.where(vals == m, ids, sentinel), axis=0, keepdims=True)
    return m, idx


def _router_kernel(h_ref, wr_ref, eb_ref, e_ref, w_ref, r_ref, cnt_ref, cnt_scr, *, tm):
    step = pl.program_id(0)

    @pl.when(step == 0)
    def _():
        cnt_scr[...] = jnp.zeros_like(cnt_scr)

    logits = lax.dot_general(wr_ref[...], h_ref[...], (((1,), (1,)), ((), ())),
                             preferred_element_type=F32, precision=lax.Precision.HIGHEST)
    scores = jax.nn.sigmoid(logits)
    choice = scores + eb_ref[...]
    i8 = lax.broadcasted_iota(jnp.int32, (GROUP_SIZE, tm), 0)
    neg_inf = jnp.float32(-jnp.inf)

    slabs = [choice[g * GROUP_SIZE:(g + 1) * GROUP_SIZE, :] for g in range(N_GROUPS)]
    sc_slabs = [scores[g * GROUP_SIZE:(g + 1) * GROUP_SIZE, :] for g in range(N_GROUPS)]

    gs = jnp.zeros((N_GROUPS, tm), F32)
    for g in range(N_GROUPS):
        m1, idx1 = _first_argmax(slabs[g], i8, GROUP_SIZE)
        m2 = jnp.max(jnp.where(i8 == idx1, neg_inf, slabs[g]), axis=0, keepdims=True)
        gs = jnp.where(i8 == g, m1 + m2, gs)

    sel = jnp.zeros((N_GROUPS, tm), jnp.int32)
    cur = gs
    for _ in range(TOPK_GROUPS):
        _, idx = _first_argmax(cur, i8, N_GROUPS)
        hit = i8 == idx
        sel = jnp.where(hit, 1, sel)
        cur = jnp.where(hit, neg_inf, cur)

    masked = [jnp.where(sel[g:g + 1, :] > 0, slabs[g], neg_inf) for g in range(N_GROUPS)]
    ids = [i8 + g * GROUP_SIZE for g in range(N_GROUPS)]
    onehot = [jnp.zeros((GROUP_SIZE, tm), F32) for _ in range(N_GROUPS)]
    picks = []
    wts = []
    for _ in range(TOP_K):
        m = functools.reduce(jnp.maximum,
                             [jnp.max(c, axis=0, keepdims=True) for c in masked])
        idx = functools.reduce(
            jnp.minimum,
            [jnp.min(jnp.where(c == m, i, N_EXPERTS), axis=0, keepdims=True)
             for c, i in zip(masked, ids)])
        w = jnp.zeros((1, tm), F32)
        for g in range(N_GROUPS):
            hit = ids[g] == idx
            w = w + jnp.sum(jnp.where(hit, sc_slabs[g], 0.0), axis=0, keepdims=True)
            masked[g] = jnp.where(hit, neg_inf, masked[g])
            onehot[g] = jnp.where(hit, 1.0, onehot[g])
        picks.append(idx)
        wts.append(w)

    wsum = functools.reduce(lambda a, b: a + b, wts)
    norm = ROUTED_SCALE / (wsum + 1e-20)

    t_row = lax.broadcasted_iota(jnp.int32, (tm, tm), 0)
    t_col = lax.broadcasted_iota(jnp.int32, (tm, tm), 1)
    before = jnp.where(t_row < t_col, 1.0, 0.0).astype(BF16)
    cum = [jnp.dot(onehot[g].astype(BF16), before, preferred_element_type=F32)
           + cnt_scr[g * GROUP_SIZE:(g + 1) * GROUP_SIZE, :] for g in range(N_GROUPS)]

    for kk in range(TOP_K):
        rank = jnp.zeros((1, tm), F32)
        for g in range(N_GROUPS):
            rank = rank + jnp.sum(jnp.where(ids[g] == picks[kk], cum[g], 0.0),
                                  axis=0, keepdims=True)
        e_ref[kk:kk + 1, :] = picks[kk]
        w_ref[kk:kk + 1, :] = wts[kk] * norm
        r_ref[kk:kk + 1, :] = rank.astype(jnp.int32)

    for g in range(N_GROUPS):
        rows = slice(g * GROUP_SIZE, (g + 1) * GROUP_SIZE)
        cnt_scr[rows, :] = cnt_scr[rows, :] + jnp.sum(onehot[g], axis=1, keepdims=True)
    cnt_ref[...] = jnp.broadcast_to(cnt_scr[...], cnt_ref.shape)


def _router(h2, w_router_t, e_bias):
    T, D = h2.shape
    tm = _tile(T, 512)
    return pl.pallas_call(
        functools.partial(_router_kernel, tm=tm),
        grid=(T // tm,),
        in_specs=[pl.BlockSpec((tm, D), lambda i: (i, 0)),
                  pl.BlockSpec((N_EXPERTS, D), lambda i: (0, 0)),
                  pl.BlockSpec((N_EXPERTS, 1), lambda i: (0, 0))],
        out_specs=[pl.BlockSpec((TOP_K, tm), lambda i: (0, i)),
                   pl.BlockSpec((TOP_K, tm), lambda i: (0, i)),
                   pl.BlockSpec((TOP_K, tm), lambda i: (0, i)),
                   pl.BlockSpec((N_EXPERTS, LANES), lambda i: (0, 0))],
        out_shape=[jax.ShapeDtypeStruct((TOP_K, T), jnp.int32),
                   jax.ShapeDtypeStruct((TOP_K, T), F32),
                   jax.ShapeDtypeStruct((TOP_K, T), jnp.int32),
                   jax.ShapeDtypeStruct((N_EXPERTS, LANES), F32)],
        scratch_shapes=[pltpu.VMEM((N_EXPERTS, 1), F32)],
        compiler_params=_cparams(("arbitrary",)),
        name="router",
    )(h2, w_router_t, e_bias.reshape(N_EXPERTS, 1))


def _slots_kernel(offs_ref, e_ref, r_ref, d_ref):
    e = e_ref[...]
    d = r_ref[...]
    for ex in range(N_EXPERTS):
        d = d + jnp.where(e == ex, offs_ref[ex], 0)
    d_ref[...] = d


def _slots(offs, top_e, rank):
    K, T = top_e.shape
    tm = _tile(T, 4096)
    grid_spec = pltpu.PrefetchScalarGridSpec(
        num_scalar_prefetch=1,
        grid=(T // tm,),
        in_specs=[pl.BlockSpec((K, tm), lambda i, o: (0, i)),
                  pl.BlockSpec((K, tm), lambda i, o: (0, i))],
        out_specs=pl.BlockSpec((K, tm), lambda i, o: (0, i)),
    )
    return pl.pallas_call(
        _slots_kernel,
        grid_spec=grid_spec,
        out_shape=jax.ShapeDtypeStruct((K, T), jnp.int32),
        compiler_params=_cparams(("parallel",)),
        name="slots",
    )(offs, top_e, rank)


def _row_copy(src, src_row, dst, dst_row, sem):
    s0 = pl.multiple_of(src_row * SLAB, SLAB)
    d0 = pl.multiple_of(dst_row * SLAB, SLAB)
    return pltpu.make_async_copy(src.at[pl.ds(s0, SLAB), :], dst.at[pl.ds(d0, SLAB), :], sem)


def _dispatch_kernel(dest_ref, h_ref, xs_ref, sem, *, td):
    def issue(j, carry):
        for kk in range(TOP_K):
            _row_copy(h_ref, j, xs_ref, dest_ref[0, kk * td + j], sem).start(priority=kk % 2)
        return carry

    lax.fori_loop(0, td, issue, 0)
    for _ in range(TOP_K):
        pltpu.make_async_copy(h_ref, xs_ref.at[pl.ds(0, td * SLAB), :], sem).wait()


def _dispatch(h2, dest_tiles, td):
    T = h2.shape[0] // SLAB
    nt = T // td
    return pl.pallas_call(
        functools.partial(_dispatch_kernel, td=td),
        grid=(nt,),
        in_specs=[pl.BlockSpec((None, 1, TOP_K * td), lambda i: (i, 0, 0),
                               memory_space=pltpu.SMEM),
                  pl.BlockSpec((td * SLAB, LANES), lambda i: (i, 0))],
        out_specs=pl.BlockSpec(memory_space=pl.ANY),
        out_shape=jax.ShapeDtypeStruct((T * TOP_K * SLAB, LANES), h2.dtype),
        scratch_shapes=[pltpu.SemaphoreType.DMA],
        compiler_params=_cparams(("arbitrary",)),
        name="dispatch",
    )(dest_tiles, h2)


def _gmm_kernel(blk_ref, exp_ref, lo_ref, hi_ref, first_ref, newe_ref, x_ref, w1_ref, w3_ref,
                w2_ref, o_ref, w1b, w3b, w2b, *, bm):
    it = pl.program_id(0)

    @pl.when(newe_ref[it] == 1)
    def _():
        w1b[...] = w1_ref[...].astype(BF16)
        w3b[...] = w3_ref[...].astype(BF16)
        w2b[...] = w2_ref[...].astype(BF16)

    rows = lax.broadcasted_iota(jnp.int32, (bm, 1), 0)
    valid = jnp.logical_and(rows >= lo_ref[it], rows < hi_ref[it])
    x_lo, x_hi = _unpack_rows(jnp.where(valid, _load_slabs(x_ref, bm), jnp.uint32(0)))
    x_lo = x_lo.astype(BF16)
    x_hi = x_hi.astype(BF16)
    half = x_lo.shape[1]
    h1 = (jnp.dot(x_lo, w1b[:half, :], preferred_element_type=F32)
          + jnp.dot(x_hi, w1b[half:, :], preferred_element_type=F32))
    h3 = (jnp.dot(x_lo, w3b[:half, :], preferred_element_type=F32)
          + jnp.dot(x_hi, w3b[half:, :], preferred_element_type=F32))
    hb = (h1 * jax.nn.sigmoid(h1) * h3).astype(BF16)
    y = jnp.dot(hb, w2b[...], preferred_element_type=F32)

    @pl.when(first_ref[it] == 1)
    def _():
        _store_slabs(o_ref, _pack_rows(y))

    @pl.when(first_ref[it] == 0)
    def _():
        o_lo, o_hi = _unpack_rows(_load_slabs(o_ref, bm))
        _store_slabs(o_ref, _pack_rows(y + jnp.concatenate([o_lo, o_hi], axis=1)))


def _gmm(xs, items, w1, w3, w2, bm):
    A = xs.shape[0] // SLAB
    D = w1.shape[1]
    F = w1.shape[2]
    n_items = items[0].shape[0]
    grid_spec = pltpu.PrefetchScalarGridSpec(
        num_scalar_prefetch=6,
        grid=(n_items,),
        in_specs=[pl.BlockSpec((bm * SLAB, LANES), lambda i, blk, ex, *_: (blk[i], 0)),
                  pl.BlockSpec((None, D, F), lambda i, blk, ex, *_: (ex[i], 0, 0)),
                  pl.BlockSpec((None, D, F), lambda i, blk, ex, *_: (ex[i], 0, 0)),
                  pl.BlockSpec((None, F, D), lambda i, blk, ex, *_: (ex[i], 0, 0))],
        out_specs=pl.BlockSpec((bm * SLAB, LANES), lambda i, blk, ex, *_: (blk[i], 0)),
        scratch_shapes=[pltpu.VMEM((D, F), BF16), pltpu.VMEM((D, F), BF16),
                        pltpu.VMEM((F, D), BF16)],
    )
    return pl.pallas_call(
        functools.partial(_gmm_kernel, bm=bm),
        grid_spec=grid_spec,
        out_shape=jax.ShapeDtypeStruct((A * SLAB, LANES), jnp.uint32),
        compiler_params=_cparams(("arbitrary",)),
        name="gmm",
    )(*items, xs, w1, w3, w2)


def _work_items(counts, bm, n_blocks):
    n_items = n_blocks + N_EXPERTS - 1
    ends = jnp.cumsum(counts)
    starts = ends - counts
    nb = jnp.where(counts > 0, (ends - 1) // bm - starts // bm + 1, 0)
    item_end = jnp.cumsum(nb)
    item_start = item_end - nb
    n_real = item_end[-1]
    i = jnp.arange(n_items, dtype=jnp.int32)
    e = jnp.minimum(jnp.sum(item_end[None, :] <= i[:, None], axis=1), N_EXPERTS - 1).astype(jnp.int32)
    onehot = e[:, None] == jnp.arange(N_EXPERTS, dtype=jnp.int32)[None, :]
    pick = lambda v: jnp.sum(jnp.where(onehot, v[None, :], 0), axis=1)
    start_e, end_e = pick(starts), pick(ends)
    blk = start_e // bm + (i - pick(item_start))
    lo = jnp.clip(start_e - blk * bm, 0, bm)
    hi = jnp.clip(end_e - blk * bm, 0, bm)
    real = i < n_real
    blk = jnp.where(real, blk, n_blocks - 1).astype(jnp.int32)
    lo = jnp.where(real, lo, 0).astype(jnp.int32)
    hi = jnp.where(real, hi, 0).astype(jnp.int32)
    one = jnp.ones((1,), jnp.int32)
    first = jnp.concatenate([one, (blk[1:] != blk[:-1]).astype(jnp.int32)])
    new_e = jnp.concatenate([one, (e[1:] != e[:-1]).astype(jnp.int32)])
    return blk, e, lo, hi, first, new_e


def _combine_kernel(dest_ref, nxt_ref, ys_ref, w_ref, x1_ref, h_ref, w1_ref, w3_ref, w2_ref,
                    gt2_ref, g_ref, o_ref, buf, y_scr, sems, *, tc):
    i = pl.program_id(0)
    n = pl.num_programs(0)
    slot = i % 2

    def gather(idx_ref, s):
        def issue(j, carry):
            for kk in range(TOP_K):
                _row_copy(ys_ref, idx_ref[0, kk * tc + j], buf.at[s, kk], j,
                          sems.at[s]).start(priority=kk % 2)
            return carry

        lax.fori_loop(0, tc, issue, 0, unroll=4)

    @pl.when(i == 0)
    def _():
        gather(dest_ref, 0)

    for s in range(2):
        @pl.when(jnp.logical_and(i + 1 < n, slot == 1 - s))
        def _():
            gather(nxt_ref, s)

    h_lo, h_hi = _unpack_rows(_load_slabs(h_ref, tc))
    h_lo = h_lo.astype(BF16)
    h_hi = h_hi.astype(BF16)
    half = h_lo.shape[1]
    h1 = (jnp.dot(h_lo, w1_ref[:half, :], preferred_element_type=F32)
          + jnp.dot(h_hi, w1_ref[half:, :], preferred_element_type=F32))
    h3 = (jnp.dot(h_lo, w3_ref[:half, :], preferred_element_type=F32)
          + jnp.dot(h_hi, w3_ref[half:, :], preferred_element_type=F32))
    y_scr[...] = jnp.dot((h1 * jax.nn.sigmoid(h1) * h3).astype(BF16), w2_ref[...],
                         preferred_element_type=F32)

    for kk in range(TOP_K):
        pltpu.make_async_copy(ys_ref.at[pl.ds(0, tc * SLAB), :], buf.at[slot, kk],
                              sems.at[slot]).wait()
    rows = 2 * SUBLANES
    for r0 in range(0, tc, rows):
        for kk in range(TOP_K):
            u = jnp.concatenate([buf[slot, kk, pl.ds(r0 * SLAB + c, rows, stride=SLAB), :]
                                 for c in range(SLAB)], axis=1)
            e_lo, e_hi = _unpack_rows(u)
            wk = w_ref[r0:r0 + rows, kk:kk + 1]
            moe_lo = wk * e_lo if kk == 0 else moe_lo + wk * e_lo
            moe_hi = wk * e_hi if kk == 0 else moe_hi + wk * e_hi
        y_scr[r0:r0 + rows, :] += jnp.concatenate([moe_lo, moe_hi], axis=1)
    o_ref[...] = x1_ref[...] + gt2_ref[...] * _rms(y_scr[...], g_ref[...])


def _combine(ys, dest_tiles, w_tok, x1, h2, w1s, w3s, w2s, mod4, g_post, S, tc):
    T, D = x1.shape
    F = w1s.shape[1]
    per_b = S // tc
    nt = T // tc
    row = lambda: pl.BlockSpec((tc, D), lambda i: (i, 0))
    return pl.pallas_call(
        functools.partial(_combine_kernel, tc=tc),
        grid=(nt,),
        in_specs=[pl.BlockSpec((None, 1, TOP_K * tc), lambda i: (i, 0, 0),
                               memory_space=pltpu.SMEM),
                  pl.BlockSpec((None, 1, TOP_K * tc), lambda i: (jnp.minimum(i + 1, nt - 1), 0, 0),
                               memory_space=pltpu.SMEM),
                  pl.BlockSpec(memory_space=pl.ANY),
                  pl.BlockSpec((tc, TOP_K), lambda i: (i, 0)),
                  row(), pl.BlockSpec((tc * SLAB, LANES), lambda i: (i, 0)),
                  pl.BlockSpec((D, F), lambda i: (0, 0)),
                  pl.BlockSpec((D, F), lambda i: (0, 0)),
                  pl.BlockSpec((F, D), lambda i: (0, 0)),
                  pl.BlockSpec((None, None, 1, D), lambda i: (i // per_b, 5, 0, 0)),
                  pl.BlockSpec((1, D), lambda i: (0, 0))],
        out_specs=row(),
        out_shape=jax.ShapeDtypeStruct((T, D), F32),
        scratch_shapes=[pltpu.VMEM((2, TOP_K, tc * SLAB, LANES), jnp.uint32),
                        pltpu.VMEM((tc, D), F32), pltpu.SemaphoreType.DMA((2,))],
        compiler_params=_cparams(("arbitrary",)),
        name="combine",
    )(dest_tiles, dest_tiles, ys, w_tok, x1, h2, w1s, w3s, w2s, mod4, g_post.reshape(1, D))


def _lambda_init(layer):
    return 0.8 - 0.6 * math.exp(-0.3 * layer)


def kernel(x, c, w_ada, b_ada, g_pre_mix, w_in, conv_w, conv_b, lru_wa, lru_ba, lru_wx, lru_bx,
           lru_lambda, lam_q1, lam_k1, lam_q2, lam_k2, g_subln, w_proj_rnn, w_proj_att, w_out,
           g_post_mix, g_pre_ffn, w_router, e_bias, w1_e, w3_e, w2_e, w1_s, w3_s, w2_s,
           g_post_ffn):
    B, S, D = x.shape
    T = B * S
    depth = w_ada.shape[0]
    slopes = np.exp2(-8.0 * np.arange(1, N_HEADS + 1, dtype=np.float32) / N_HEADS)
    kbias = _alibi_tables(slopes, S)
    tt = _tile(S, 256)
    bm = _tile(T * TOP_K, 512)
    n_blocks = T * TOP_K // bm

    x2 = x.reshape(T, D)
    for l in range(depth):
        lam_init = _lambda_init(l)
        mod4 = _ada(c, w_ada[l], b_ada[l]).reshape(B, 6, 1, D)

        proj = _inproj(x2, g_pre_mix[l], mod4, w_in[l].astype(BF16), S)
        yr = _lru(proj, conv_w[l], conv_b[l], lru_wa[l].astype(BF16), lru_ba[l],
                  lru_wx[l].astype(BF16), lru_bx[l], lru_lambda[l], B, S)
        lamv = jnp.stack([lam_q1[l], lam_k1[l], lam_q2[l], lam_k2[l]])
        ao = _attn(proj, kbias, lamv, g_subln[l], B, S, lam_init)
        x1, h2, h2p = _mixout(x2, yr, ao, proj, w_proj_rnn[l].astype(BF16),
                         w_proj_att[l].astype(BF16), w_out[l].astype(BF16),
                         g_post_mix[l], g_pre_ffn[l], mod4, S)

        top_e, top_w, rank, cnt = _router(h2, w_router[l].T, e_bias[l])
        counts = cnt[:, 0].astype(jnp.int32)
        offs = jnp.cumsum(counts) - counts
        dest = _slots(offs, top_e, rank)
        dest_tiles = dest.reshape(TOP_K, T // tt, tt).transpose(1, 0, 2).reshape(T // tt, 1,
                                                                                  TOP_K * tt)
        xs = _dispatch(h2p, dest_tiles, tt)
        items = _work_items(counts, bm, n_blocks)
        ys = _gmm(xs, items, w1_e[l], w3_e[l], w2_e[l], bm)
        x2 = _combine(ys, dest_tiles, top_w.T, x1, h2p, w1_s[l].astype(BF16),
                      w3_s[l].astype(BF16), w2_s[l].astype(BF16), mod4, g_post_ffn[l], S, tt)
    return x2.reshape(B, S, D)
```

```python
import functools
import math

import jax
import jax.numpy as jnp
import numpy as np
from jax import lax
from jax.experimental import pallas as pl
from jax.experimental.pallas import tpu as pltpu

F32 = jnp.float32
BF16 = jnp.bfloat16

EPS = 1e-6
N_HEADS = 8
HEAD_DIM = 64
V_DIM = 2 * HEAD_DIM
LRU_BLOCKS = 8
CONV_W = 4
LRU_C = 8.0
N_EXPERTS = 64
TOP_K = 8
N_GROUPS = 8
GROUP_SIZE = N_EXPERTS // N_GROUPS
TOPK_GROUPS = 4
ROUTED_SCALE = 2.5

LANES = 128
SUBLANES = 8
VMEM_LIMIT = 48 * 1024 * 1024


def _cparams(sem):
    return pltpu.CompilerParams(dimension_semantics=sem, vmem_limit_bytes=VMEM_LIMIT)


def _tile(n, pref):
    t = min(n, pref)
    while n % t:
        t //= 2
    return t


def _rms(x, g):
    return x * lax.rsqrt(jnp.mean(x * x, axis=-1, keepdims=True) + EPS) * g


def _ada_kernel(c_ref, w_ref, b_ref, o_ref):
    c = c_ref[...]
    cond = c * jax.nn.sigmoid(c)
    o_ref[...] = jnp.dot(cond, w_ref[...], preferred_element_type=F32) + b_ref[...]


def _ada(c, w, b):
    B, D = c.shape
    N = w.shape[1]
    tn = _tile(N, 1024)
    return pl.pallas_call(
        _ada_kernel,
        grid=(N // tn,),
        in_specs=[pl.BlockSpec((B, D), lambda j: (0, 0)),
                  pl.BlockSpec((D, tn), lambda j: (0, j)),
                  pl.BlockSpec((1, tn), lambda j: (0, j))],
        out_specs=pl.BlockSpec((B, tn), lambda j: (0, j)),
        out_shape=jax.ShapeDtypeStruct((B, N), F32),
        compiler_params=_cparams(("parallel",)),
        name="ada",
    )(c, w, b.reshape(1, N))


LOG2E = 1.4426950408889634
Q_COL_BLOCK = 2
Q_PRESCALE = HEAD_DIM ** -0.5 * LOG2E


def _inproj_kernel(x_ref, g_ref, sh_ref, sc_ref, w_ref, o_ref, h_scr):
    @pl.when(pl.program_id(1) == 0)
    def _():
        h = _rms(x_ref[...], g_ref[...]) * (1.0 + sc_ref[...]) + sh_ref[...]
        h_scr[...] = h.astype(BF16)

    r = jnp.dot(h_scr[...], w_ref[...], preferred_element_type=F32)
    r = r * jnp.where(pl.program_id(1) == Q_COL_BLOCK, Q_PRESCALE, 1.0)
    o_ref[...] = r.astype(o_ref.dtype)


def _inproj(x2, g, mod4, w_bf, S):
    T, D = x2.shape
    N = w_bf.shape[1]
    tm = _tile(S, 2048)
    tn = 1024
    per_b = S // tm
    return pl.pallas_call(
        _inproj_kernel,
        grid=(T // tm, N // tn),
        in_specs=[pl.BlockSpec((tm, D), lambda i, j: (i, 0)),
                  pl.BlockSpec((1, D), lambda i, j: (0, 0)),
                  pl.BlockSpec((None, None, 1, D), lambda i, j: (i // per_b, 0, 0, 0)),
                  pl.BlockSpec((None, None, 1, D), lambda i, j: (i // per_b, 1, 0, 0)),
                  pl.BlockSpec((D, tn), lambda i, j: (0, j))],
        out_specs=pl.BlockSpec((tm, tn), lambda i, j: (i, j)),
        out_shape=jax.ShapeDtypeStruct((T, N), BF16),
        scratch_shapes=[pltpu.VMEM((tm, D), BF16)],
        compiler_params=_cparams(("parallel", "arbitrary")),
        name="inproj",
    )(x2, g.reshape(1, D), mod4, mod4, w_bf)


def _lru_kernel(xr_ref, gr_ref, cw_ref, cb_ref, wa_ref, ba_ref, wx_ref, bx_ref, lam_ref,
                o_ref, xc_scr, prev_scr, h_scr, *, ts):
    s = pl.program_id(1)

    @pl.when(s == 0)
    def _():
        prev_scr[...] = jnp.zeros_like(prev_scr)
        h_scr[...] = jnp.zeros_like(h_scr)

    x = xr_ref[...].astype(F32)
    prev = prev_scr[...]
    row8 = lax.broadcasted_iota(jnp.int32, (SUBLANES, 1), 0)
    acc = cb_ref[...] + cw_ref[CONV_W - 1:CONV_W, :] * x
    xc_scr[...] = acc
    top = cb_ref[...] + cw_ref[CONV_W - 1:CONV_W, :] * x[0:SUBLANES, :]
    for j in range(1, CONV_W):
        wj = cw_ref[CONV_W - 1 - j:CONV_W - j, :]
        rj = pltpu.roll(x, j, axis=0)
        xc_scr[...] += wj * rj
        pj = pltpu.roll(prev, j, axis=0)
        top += wj * jnp.where(row8 < j, pj, rj[0:SUBLANES, :])
    xc_scr[0:SUBLANES, :] = top
    prev_scr[...] = x[ts - SUBLANES:ts, :]

    row = lax.broadcasted_iota(jnp.int32, (ts, 1), 0)
    is_first = jnp.logical_and(row == 0, s == 0)
    sub = lax.broadcasted_iota(jnp.int32, (1, SUBLANES, 1), 1)
    for n in range(LRU_BLOCKS):
        cols = slice(n * LANES, (n + 1) * LANES)
        xc = xc_scr[:, cols]
        xb = xc.astype(BF16)
        r = jax.nn.sigmoid(jnp.dot(xb, wa_ref[n], preferred_element_type=F32) + ba_ref[:, cols])
        i = jax.nn.sigmoid(jnp.dot(xb, wx_ref[n], preferred_element_type=F32) + bx_ref[:, cols])
        lam = lam_ref[:, cols]
        softplus_neg = jnp.maximum(-lam, 0.0) + jnp.log1p(jnp.exp(-jnp.abs(lam)))
        log_a = (-LRU_C * softplus_neg) * r
        a = jnp.exp(log_a)
        m2 = 1.0 - a * a
        mult = jnp.where(m2 > 0.0, m2 * lax.rsqrt(m2), 0.0)
        mult = jnp.where(is_first, 1.0, mult)
        u = mult * (i * xc)
        a = a.reshape(ts // SUBLANES, SUBLANES, LANES)
        u = u.reshape(ts // SUBLANES, SUBLANES, LANES)
        for d in (1, 2, 4):
            keep = sub >= d
            a_sh = jnp.where(keep, pltpu.roll(a, d, axis=1), 1.0)
            u_sh = jnp.where(keep, pltpu.roll(u, d, axis=1), 0.0)
            u = u + a * u_sh
            a = a * a_sh
        a = a.reshape(ts, LANES)
        u = u.reshape(ts, LANES)
        gate = jax.nn.gelu(gr_ref[:, cols].astype(F32))
        carry = h_scr[:, cols]
        step = 2 * SUBLANES
        for g in range(ts // step):
            r0 = g * step
            h0 = u[r0:r0 + SUBLANES, :] + a[r0:r0 + SUBLANES, :] * carry
            carry = h0[SUBLANES - 1:SUBLANES, :]
            h1 = u[r0 + SUBLANES:r0 + step, :] + a[r0 + SUBLANES:r0 + step, :] * carry
            carry = h1[SUBLANES - 1:SUBLANES, :]
            hg = jnp.concatenate([h0, h1], axis=0) * gate[r0:r0 + step, :]
            o_ref[r0:r0 + step, cols] = hg.astype(o_ref.dtype)
        h_scr[:, cols] = carry


def _lru(proj, conv_w, conv_b, wa_bf, ba, wx_bf, bx, lam, B, S):
    C = conv_w.shape[1]
    ts = _tile(S, 256)
    ns = S // ts
    vec = lambda: pl.BlockSpec((1, C), lambda b, s: (0, 0))
    blk = lambda: pl.BlockSpec((LRU_BLOCKS, LANES, LANES), lambda b, s: (0, 0, 0))
    return pl.pallas_call(
        functools.partial(_lru_kernel, ts=ts),
        grid=(B, ns),
        in_specs=[pl.BlockSpec((ts, C), lambda b, s: (b * ns + s, 0)),
                  pl.BlockSpec((ts, C), lambda b, s: (b * ns + s, 1)),
                  pl.BlockSpec((CONV_W, C), lambda b, s: (0, 0)),
                  vec(), blk(), vec(), blk(), vec(), vec()],
        out_specs=pl.BlockSpec((ts, C), lambda b, s: (b * ns + s, 0)),
        out_shape=jax.ShapeDtypeStruct((B * S, C), BF16),
        scratch_shapes=[pltpu.VMEM((ts, C), F32), pltpu.VMEM((SUBLANES, C), F32),
                        pltpu.VMEM((1, C), F32)],
        compiler_params=_cparams(("parallel", "arbitrary")),
        name="lru",
    )(proj, proj, conv_w, conv_b.reshape(1, C), wa_bf, ba.reshape(1, C), wx_bf, bx.reshape(1, C),
      lam.reshape(1, C))


BIAS_LANES = 3
ACC_ROWS = V_DIM + 2 * SUBLANES


def _alibi_tables(slopes, S):
    def top16(x):
        return (x.view(np.uint32) & np.uint32(0xFFFF0000)).view(np.float32)

    pos = np.arange(S, dtype=np.float32)
    b = (slopes.astype(np.float32) * np.float32(LOG2E))[:, None] * pos[None, :]
    hi = top16(b)
    mid = top16(b - hi)
    lo = top16(b - hi - mid)
    half = np.zeros(b.shape + (HEAD_DIM,), np.float32)
    half[..., 0], half[..., 1], half[..., 2] = hi, mid, lo
    zero = np.zeros_like(half)
    table = np.stack([np.concatenate([zero, half], axis=-1),
                      np.concatenate([half, zero], axis=-1)], axis=1)
    return jnp.asarray(table, dtype=BF16)


def _attn_kernel(lamv_ref, gsub_ref, q_ref, k_ref, v_ref, kb_ref, eye_ref, o_ref,
                 vt_scr, kx_scr, qx_scr, s_scr, p_scr, a_scr, m_scr, acc_scr, *, tq, lam_init):
    qi = pl.program_id(2)
    tk = tq
    nk = vt_scr.shape[0]

    lane = lax.broadcasted_iota(jnp.int32, (1, V_DIM), 1)
    own = (jnp.where(lane < HEAD_DIM, 1.0, 0.0), jnp.where(lane >= HEAD_DIM, 1.0, 0.0))
    ones_row = (jnp.where(jnp.logical_and(lane >= HEAD_DIM, lane < HEAD_DIM + BIAS_LANES), 1.0, 0.0),
                jnp.where(lane < BIAS_LANES, 1.0, 0.0))

    @pl.when(qi == 0)
    def _():
        for j in range(nk):
            rows = slice(j * tk, (j + 1) * tk)
            vt_scr[j, 0:V_DIM, :] = lax.dot_general(
                eye_ref[0:V_DIM, 0:V_DIM], v_ref[rows, :], (((1,), (1,)), ((), ())),
                preferred_element_type=F32).astype(BF16)
            vt_scr[j, V_DIM:ACC_ROWS, :] = jnp.ones((ACC_ROWS - V_DIM, tk), BF16)
            kf = k_ref[rows, :].astype(F32)
            for mp in range(2):
                kx_scr[mp, rows, :] = (kf * own[mp] + kb_ref[mp, rows, :].astype(F32)).astype(BF16)

    q = q_ref[...].astype(F32)
    for mp in range(2):
        qx_scr[mp] = (q * own[mp] + ones_row[mp]).astype(BF16)
    m_scr[...] = jnp.full_like(m_scr, -jnp.inf)
    acc_scr[...] = jnp.zeros_like(acc_scr)
    row8 = lax.broadcasted_iota(jnp.int32, (SUBLANES, LANES), 0)
    col = lax.broadcasted_iota(jnp.int32, (1, LANES), 1)
    neg_inf = jnp.float32(-jnp.inf)

    def tree(op, parts):
        parts = [p for p in parts if p is not None]
        while len(parts) > 1:
            parts = [op(parts[i], parts[i + 1]) if i + 1 < len(parts) else parts[i]
                     for i in range(0, len(parts), 2)]
        return parts[0]

    def scores(j, par):
        start = pl.multiple_of(j * tk, tk)
        for mp in range(2):
            s = lax.dot_general(kx_scr[mp, pl.ds(start, tk), :], qx_scr[mp],
                                (((1,), (1,)), ((), ())), preferred_element_type=F32)
            for c in range(tq // LANES):
                s_scr[par, mp, c] = s[:, c * LANES:(c + 1) * LANES]

    def softmax(par, masked):
        for mp in range(2):
            for c in range(tq // LANES):
                cols = slice(c * LANES, (c + 1) * LANES)
                lo_col, hi_col = c * LANES, (c + 1) * LANES - 1
                s_c = s_scr.at[par, mp, c]
                p_c = p_scr.at[par, mp, c]
                accs = [None] * 4
                for i in range(tk // SUBLANES):
                    r0 = i * SUBLANES
                    if masked and r0 > hi_col:
                        continue
                    t = s_c[r0:r0 + SUBLANES, :]
                    if masked and r0 + SUBLANES - 1 > lo_col:
                        t = jnp.where(row8 + r0 <= col + lo_col, t, neg_inf)
                        s_c[r0:r0 + SUBLANES, :] = t
                    accs[i % 4] = t if accs[i % 4] is None else jnp.maximum(accs[i % 4], t)
                mx = jnp.max(tree(jnp.maximum, accs), axis=0, keepdims=True)
                m_old = m_scr[mp, :, cols]
                m_new = jnp.maximum(m_old, mx)
                a_scr[par, mp, :, cols] = jnp.exp2(m_old - m_new)
                m_scr[mp, :, cols] = m_new
                for i in range(tk // (2 * SUBLANES)):
                    r0 = i * 2 * SUBLANES
                    if masked and r0 > hi_col:
                        p_c[r0:r0 + 2 * SUBLANES, :] = jnp.zeros((2 * SUBLANES, LANES), BF16)
                        continue
                    p = jnp.exp2(s_c[r0:r0 + 2 * SUBLANES, :] - m_new)
                    p_c[r0:r0 + 2 * SUBLANES, :] = p.astype(BF16)

    def values(j, par):
        for mp in range(2):
            p = jnp.concatenate([p_scr[par, mp, c] for c in range(tq // LANES)], axis=1)
            pv = jnp.dot(vt_scr[j], p, preferred_element_type=F32)
            acc_scr[mp] = a_scr[par, mp] * acc_scr[mp] + pv

    p_scr[1] = jnp.zeros(p_scr.shape[1:], BF16)
    a_scr[1] = jnp.ones(a_scr.shape[1:], F32)
    scores(0, 0)

    def stage(j, par):
        scores(j + 1, 1 - par)
        softmax(par, False)
        values(jnp.maximum(j - 1, 0), 1 - par)

    def stage_pair(i, carry):
        stage(2 * i, 0)
        stage(2 * i + 1, 1)
        return carry

    def tail(par):
        softmax(par, True)
        values(jnp.maximum(qi - 1, 0), 1 - par)
        values(qi, par)

    lax.fori_loop(0, qi // 2, stage_pair, 0)

    @pl.when(qi % 2 == 1)
    def _():
        stage(qi - 1, 0)
        tail(1)

    @pl.when(qi % 2 == 0)
    def _():
        tail(0)

    lv = lamv_ref[...]
    lam = (jnp.exp(jnp.sum(lv[0:1, :] * lv[1:2, :], axis=-1, keepdims=True))
           - jnp.exp(jnp.sum(lv[2:3, :] * lv[3:4, :], axis=-1, keepdims=True)) + lam_init)
    o0 = acc_scr[0, 0:V_DIM, :] * (1.0 / acc_scr[0, V_DIM:V_DIM + 1, :])
    o1 = acc_scr[1, 0:V_DIM, :] * (1.0 / acc_scr[1, V_DIM:V_DIM + 1, :])
    o_t = o0 - lam * o1
    o_t = o_t * lax.rsqrt(jnp.mean(o_t * o_t, axis=0, keepdims=True) + EPS) * gsub_ref[...]
    o_t = (o_t * (1.0 - lam_init)).astype(BF16)
    o_ref[...] = lax.dot_general(eye_ref[...], o_t, (((1,), (1,)), ((), ())),
                                 preferred_element_type=F32).astype(o_ref.dtype)


def _attn(proj, kbias, lamv, g_subln, B, S, lam_init):
    tq = _tile(S, 512)
    nq = S // tq
    qc, kc, vc = 2 * 8, 3 * 8, 4 * 8
    return pl.pallas_call(
        functools.partial(_attn_kernel, tq=tq, lam_init=lam_init),
        grid=(B, N_HEADS, nq),
        in_specs=[pl.BlockSpec((4, HEAD_DIM), lambda b, h, qi: (0, 0)),
                  pl.BlockSpec((V_DIM, 1), lambda b, h, qi: (0, 0)),
                  pl.BlockSpec((tq, V_DIM), lambda b, h, qi: (b * nq + qi, qc + h)),
                  pl.BlockSpec((S, V_DIM), lambda b, h, qi: (b, kc + h)),
                  pl.BlockSpec((S, V_DIM), lambda b, h, qi: (b, vc + h)),
                  pl.BlockSpec((None, 2, S, V_DIM), lambda b, h, qi: (h, 0, 0, 0)),
                  pl.BlockSpec((tq, tq), lambda b, h, qi: (0, 0))],
        out_specs=pl.BlockSpec((tq, V_DIM), lambda b, h, qi: (b * nq + qi, h)),
        out_shape=jax.ShapeDtypeStruct((B * S, N_HEADS * V_DIM), BF16),
        scratch_shapes=[pltpu.VMEM((nq, ACC_ROWS, tq), BF16), pltpu.VMEM((2, S, V_DIM), BF16),
                        pltpu.VMEM((2, tq, V_DIM), BF16),
                        pltpu.VMEM((2, 2, tq // LANES, tq, LANES), F32),
                        pltpu.VMEM((2, 2, tq // LANES, tq, LANES), BF16),
                        pltpu.VMEM((2, 2, 1, tq), F32), pltpu.VMEM((2, 1, tq), F32),
                        pltpu.VMEM((2, ACC_ROWS, tq), F32)],
        compiler_params=_cparams(("parallel", "parallel", "arbitrary")),
        name="attn",
    )(lamv, g_subln.reshape(V_DIM, 1), proj, proj, proj, kbias, jnp.eye(tq, dtype=BF16))


def _pack_rows(x):
    half = x.shape[1] // 2
    lo = lax.bitcast_convert_type(x[:, :half].astype(BF16).astype(F32), jnp.uint32)
    hi = lax.bitcast_convert_type(x[:, half:].astype(BF16).astype(F32), jnp.uint32)
    return (hi & jnp.uint32(0xFFFF0000)) | (lo >> 16)


def _unpack_rows(u):
    lo = lax.bitcast_convert_type(u << 16, F32)
    hi = lax.bitcast_convert_type(u & jnp.uint32(0xFFFF0000), F32)
    return lo, hi


SLAB = 4


def _load_slabs(ref, n):
    return jnp.concatenate([ref[pl.ds(c, n, stride=SLAB), :] for c in range(SLAB)], axis=1)


def _store_slabs(ref, u):
    n = u.shape[0]
    for c in range(SLAB):
        ref[pl.ds(c, n, stride=SLAB), :] = u[:, c * LANES:(c + 1) * LANES]


def _mixout_kernel(x_ref, yr_ref, ao_ref, ga_ref, gb_ref, wr_ref, wa_ref, wo_ref,
                   gpost_ref, gt1_ref, gpre_ref, sh2_ref, sc2_ref, x1_ref, h2_ref, h2p_ref):
    ya = jnp.dot(yr_ref[...], wr_ref[...], preferred_element_type=F32)
    yb = jnp.dot(ao_ref[...], wa_ref[...], preferred_element_type=F32)
    merged = (jax.nn.sigmoid(ga_ref[...].astype(F32)) * ya
              + jax.nn.sigmoid(gb_ref[...].astype(F32)) * yb)
    y = jnp.dot(merged.astype(BF16), wo_ref[...], preferred_element_type=F32)
    x1 = x_ref[...] + gt1_ref[...] * _rms(y, gpost_ref[...])
    x1_ref[...] = x1
    h2 = _rms(x1, gpre_ref[...]) * (1.0 + sc2_ref[...]) + sh2_ref[...]
    h2_ref[...] = h2
    _store_slabs(h2p_ref, _pack_rows(h2))


def _mixout(x2, yr, ao, proj, wr_bf, wa_bf, wo_bf, g_post, g_pre, mod4, S):
    T, D = x2.shape
    tm = _tile(S, 512)
    per_b = S // tm
    gac, gbc = 5, 6
    row = lambda: pl.BlockSpec((tm, D), lambda i: (i, 0))
    wsp = lambda: pl.BlockSpec((D, D), lambda i: (0, 0))
    vec = lambda: pl.BlockSpec((1, D), lambda i: (0, 0))
    modv = lambda j: pl.BlockSpec((None, None, 1, D), lambda i: (i // per_b, j, 0, 0))
    return pl.pallas_call(
        _mixout_kernel,
        grid=(T // tm,),
        in_specs=[row(), row(), row(),
                  pl.BlockSpec((tm, D), lambda i: (i, gac)),
                  pl.BlockSpec((tm, D), lambda i: (i, gbc)),
                  wsp(), wsp(), wsp(), vec(), modv(2), vec(), modv(3), modv(4)],
        out_specs=[row(), row(), pl.BlockSpec((tm * SLAB, LANES), lambda i: (i, 0))],
        out_shape=[jax.ShapeDtypeStruct((T, D), F32), jax.ShapeDtypeStruct((T, D), F32),
                   jax.ShapeDtypeStruct((T * SLAB, LANES), jnp.uint32)],
        compiler_params=_cparams(("parallel",)),
        name="mixout",
    )(x2, yr, ao, proj, proj, wr_bf, wa_bf, wo_bf, g_post.reshape(1, D), mod4,
      g_pre.reshape(1, D), mod4, mod4)


def _first_argmax(vals, ids, sentinel):
    m = jnp.max(vals, axis=0, keepdims=True)
    idx = jnp.min(jnp.where(vals == m, ids, sentinel), axis=0, keepdims=True)
    return m, idx


def _router_kernel(h_ref, wr_ref, eb_ref, e_ref, w_ref, r_ref, cnt_ref, cnt_scr, *, tm):
    step = pl.program_id(0)

    @pl.when(step == 0)
    def _():
        cnt_scr[...] = jnp.zeros_like(cnt_scr)

    logits = lax.dot_general(wr_ref[...], h_ref[...], (((1,), (1,)), ((), ())),
                             preferred_element_type=F32, precision=lax.Precision.HIGHEST)
    scores = jax.nn.sigmoid(logits)
    choice = scores + eb_ref[...]
    i8 = lax.broadcasted_iota(jnp.int32, (GROUP_SIZE, tm), 0)
    neg_inf = jnp.float32(-jnp.inf)

    slabs = [choice[g * GROUP_SIZE:(g + 1) * GROUP_SIZE, :] for g in range(N_GROUPS)]
    sc_slabs = [scores[g * GROUP_SIZE:(g + 1) * GROUP_SIZE, :] for g in range(N_GROUPS)]

    gs = jnp.zeros((N_GROUPS, tm), F32)
    for g in range(N_GROUPS):
        m1, idx1 = _first_argmax(slabs[g], i8, GROUP_SIZE)
        m2 = jnp.max(jnp.where(i8 == idx1, neg_inf, slabs[g]), axis=0, keepdims=True)
        gs = jnp.where(i8 == g, m1 + m2, gs)

    sel = jnp.zeros((N_GROUPS, tm), jnp.int32)
    cur = gs
    for _ in range(TOPK_GROUPS):
        _, idx = _first_argmax(cur, i8, N_GROUPS)
        hit = i8 == idx
        sel = jnp.where(hit, 1, sel)
        cur = jnp.where(hit, neg_inf, cur)

    masked = [jnp.where(sel[g:g + 1, :] > 0, slabs[g], neg_inf) for g in range(N_GROUPS)]
    ids = [i8 + g * GROUP_SIZE for g in range(N_GROUPS)]
    onehot = [jnp.zeros((GROUP_SIZE, tm), F32) for _ in range(N_GROUPS)]
    picks = []
    wts = []
    for _ in range(TOP_K):
        m = functools.reduce(jnp.maximum,
                             [jnp.max(c, axis=0, keepdims=True) for c in masked])
        idx = functools.reduce(
            jnp.minimum,
            [jnp.min(jnp.where(c == m, i, N_EXPERTS), axis=0, keepdims=True)
             for c, i in zip(masked, ids)])
        w = jnp.zeros((1, tm), F32)
        for g in range(N_GROUPS):
            hit = ids[g] == idx
            w = w + jnp.sum(jnp.where(hit, sc_slabs[g], 0.0), axis=0, keepdims=True)
            masked[g] = jnp.where(hit, neg_inf, masked[g])
            onehot[g] = jnp.where(hit, 1.0, onehot[g])
        picks.append(idx)
        wts.append(w)

    wsum = functools.reduce(lambda a, b: a + b, wts)
    norm = ROUTED_SCALE / (wsum + 1e-20)

    t_row = lax.broadcasted_iota(jnp.int32, (tm, tm), 0)
    t_col = lax.broadcasted_iota(jnp.int32, (tm, tm), 1)
    before = jnp.where(t_row < t_col, 1.0, 0.0).astype(BF16)
    cum = [jnp.dot(onehot[g].astype(BF16), before, preferred_element_type=F32)
           + cnt_scr[g * GROUP_SIZE:(g + 1) * GROUP_SIZE, :] for g in range(N_GROUPS)]

    for kk in range(TOP_K):
        rank = jnp.zeros((1, tm), F32)
        for g in range(N_GROUPS):
            rank = rank + jnp.sum(jnp.where(ids[g] == picks[kk], cum[g], 0.0),
                                  axis=0, keepdims=True)
        e_ref[kk:kk + 1, :] = picks[kk]
        w_ref[kk:kk + 1, :] = wts[kk] * norm
        r_ref[kk:kk + 1, :] = rank.astype(jnp.int32)

    for g in range(N_GROUPS):
        rows = slice(g * GROUP_SIZE, (g + 1) * GROUP_SIZE)
        cnt_scr[rows, :] = cnt_scr[rows, :] + jnp.sum(onehot[g], axis=1, keepdims=True)
    cnt_ref[...] = jnp.broadcast_to(cnt_scr[...], cnt_ref.shape)


def _router(h2, w_router_t, e_bias):
    T, D = h2.shape
    tm = _tile(T, 512)
    return pl.pallas_call(
        functools.partial(_router_kernel, tm=tm),
        grid=(T // tm,),
        in_specs=[pl.BlockSpec((tm, D), lambda i: (i, 0)),
                  pl.BlockSpec((N_EXPERTS, D), lambda i: (0, 0)),
                  pl.BlockSpec((N_EXPERTS, 1), lambda i: (0, 0))],
        out_specs=[pl.BlockSpec((TOP_K, tm), lambda i: (0, i)),
                   pl.BlockSpec((TOP_K, tm), lambda i: (0, i)),
                   pl.BlockSpec((TOP_K, tm), lambda i: (0, i)),
                   pl.BlockSpec((N_EXPERTS, LANES), lambda i: (0, 0))],
        out_shape=[jax.ShapeDtypeStruct((TOP_K, T), jnp.int32),
                   jax.ShapeDtypeStruct((TOP_K, T), F32),
                   jax.ShapeDtypeStruct((TOP_K, T), jnp.int32),
                   jax.ShapeDtypeStruct((N_EXPERTS, LANES), F32)],
        scratch_shapes=[pltpu.VMEM((N_EXPERTS, 1), F32)],
        compiler_params=_cparams(("arbitrary",)),
        name="router",
    )(h2, w_router_t, e_bias.reshape(N_EXPERTS, 1))


def _slots_kernel(offs_ref, e_ref, r_ref, d_ref):
    e = e_ref[...]
    d = r_ref[...]
    for ex in range(N_EXPERTS):
        d = d + jnp.where(e == ex, offs_ref[ex], 0)
    d_ref[...] = d


def _slots(offs, top_e, rank):
    K, T = top_e.shape
    tm = _tile(T, 4096)
    grid_spec = pltpu.PrefetchScalarGridSpec(
        num_scalar_prefetch=1,
        grid=(T // tm,),
        in_specs=[pl.BlockSpec((K, tm), lambda i, o: (0, i)),
                  pl.BlockSpec((K, tm), lambda i, o: (0, i))],
        out_specs=pl.BlockSpec((K, tm), lambda i, o: (0, i)),
    )
    return pl.pallas_call(
        _slots_kernel,
        grid_spec=grid_spec,
        out_shape=jax.ShapeDtypeStruct((K, T), jnp.int32),
        compiler_params=_cparams(("parallel",)),
        name="slots",
    )(offs, top_e, rank)


def _row_copy(src, src_row, dst, dst_row, sem):
    s0 = pl.multiple_of(src_row * SLAB, SLAB)
    d0 = pl.multiple_of(dst_row * SLAB, SLAB)
    return pltpu.make_async_copy(src.at[pl.ds(s0, SLAB), :], dst.at[pl.ds(d0, SLAB), :], sem)


def _dispatch_kernel(dest_ref, h_ref, xs_ref, sem, *, td):
    def issue(j, carry):
        for kk in range(TOP_K):
            _row_copy(h_ref, j, xs_ref, dest_ref[0, kk * td + j], sem).start(priority=kk % 2)
        return carry

    lax.fori_loop(0, td, issue, 0)
    for _ in range(TOP_K):
        pltpu.make_async_copy(h_ref, xs_ref.at[pl.ds(0, td * SLAB), :], sem).wait()


def _dispatch(h2, dest_tiles, td):
    T = h2.shape[0] // SLAB
    nt = T // td
    return pl.pallas_call(
        functools.partial(_dispatch_kernel, td=td),
        grid=(nt,),
        in_specs=[pl.BlockSpec((None, 1, TOP_K * td), lambda i: (i, 0, 0),
                               memory_space=pltpu.SMEM),
                  pl.BlockSpec((td * SLAB, LANES), lambda i: (i, 0))],
        out_specs=pl.BlockSpec(memory_space=pl.ANY),
        out_shape=jax.ShapeDtypeStruct((T * TOP_K * SLAB, LANES), h2.dtype),
        scratch_shapes=[pltpu.SemaphoreType.DMA],
        compiler_params=_cparams(("arbitrary",)),
        name="dispatch",
    )(dest_tiles, h2)


def _gmm_kernel(blk_ref, exp_ref, lo_ref, hi_ref, first_ref, newe_ref, x_ref, w1_ref, w3_ref,
                w2_ref, o_ref, w1b, w3b, w2b, *, bm):
    it = pl.program_id(0)

    @pl.when(newe_ref[it] == 1)
    def _():
        w1b[...] = w1_ref[...].astype(BF16)
        w3b[...] = w3_ref[...].astype(BF16)
        w2b[...] = w2_ref[...].astype(BF16)

    rows = lax.broadcasted_iota(jnp.int32, (bm, 1), 0)
    valid = jnp.logical_and(rows >= lo_ref[it], rows < hi_ref[it])
    x_lo, x_hi = _unpack_rows(jnp.where(valid, _load_slabs(x_ref, bm), jnp.uint32(0)))
    x_lo = x_lo.astype(BF16)
    x_hi = x_hi.astype(BF16)
    half = x_lo.shape[1]
    h1 = (jnp.dot(x_lo, w1b[:half, :], preferred_element_type=F32)
          + jnp.dot(x_hi, w1b[half:, :], preferred_element_type=F32))
    h3 = (jnp.dot(x_lo, w3b[:half, :], preferred_element_type=F32)
          + jnp.dot(x_hi, w3b[half:, :], preferred_element_type=F32))
    hb = (h1 * jax.nn.sigmoid(h1) * h3).astype(BF16)
    y = jnp.dot(hb, w2b[...], preferred_element_type=F32)

    @pl.when(first_ref[it] == 1)
    def _():
        _store_slabs(o_ref, _pack_rows(y))

    @pl.when(first_ref[it] == 0)
    def _():
        o_lo, o_hi = _unpack_rows(_load_slabs(o_ref, bm))
        _store_slabs(o_ref, _pack_rows(y + jnp.concatenate([o_lo, o_hi], axis=1)))


def _gmm(xs, items, w1, w3, w2, bm):
    A = xs.shape[0] // SLAB
    D = w1.shape[1]
    F = w1.shape[2]
    n_items = items[0].shape[0]
    grid_spec = pltpu.PrefetchScalarGridSpec(
        num_scalar_prefetch=6,
        grid=(n_items,),
        in_specs=[pl.BlockSpec((bm * SLAB, LANES), lambda i, blk, ex, *_: (blk[i], 0)),
                  pl.BlockSpec((None, D, F), lambda i, blk, ex, *_: (ex[i], 0, 0)),
                  pl.BlockSpec((None, D, F), lambda i, blk, ex, *_: (ex[i], 0, 0)),
                  pl.BlockSpec((None, F, D), lambda i, blk, ex, *_: (ex[i], 0, 0))],
        out_specs=pl.BlockSpec((bm * SLAB, LANES), lambda i, blk, ex, *_: (blk[i], 0)),
        scratch_shapes=[pltpu.VMEM((D, F), BF16), pltpu.VMEM((D, F), BF16),
                        pltpu.VMEM((F, D), BF16)],
    )
    return pl.pallas_call(
        functools.partial(_gmm_kernel, bm=bm),
        grid_spec=grid_spec,
        out_shape=jax.ShapeDtypeStruct((A * SLAB, LANES), jnp.uint32),
        compiler_params=_cparams(("arbitrary",)),
        name="gmm",
    )(*items, xs, w1, w3, w2)


def _work_items(counts, bm, n_blocks):
    n_items = n_blocks + N_EXPERTS - 1
    ends = jnp.cumsum(counts)
    starts = ends - counts
    nb = jnp.where(counts > 0, (ends - 1) // bm - starts // bm + 1, 0)
    item_end = jnp.cumsum(nb)
    item_start = item_end - nb
    n_real = item_end[-1]
    i = jnp.arange(n_items, dtype=jnp.int32)
    e = jnp.minimum(jnp.sum(item_end[None, :] <= i[:, None], axis=1), N_EXPERTS - 1).astype(jnp.int32)
    onehot = e[:, None] == jnp.arange(N_EXPERTS, dtype=jnp.int32)[None, :]
    pick = lambda v: jnp.sum(jnp.where(onehot, v[None, :], 0), axis=1)
    start_e, end_e = pick(starts), pick(ends)
    blk = start_e // bm + (i - pick(item_start))
    lo = jnp.clip(start_e - blk * bm, 0, bm)
    hi = jnp.clip(end_e - blk * bm, 0, bm)
    real = i < n_real
    blk = jnp.where(real, blk, n_blocks - 1).astype(jnp.int32)
    lo = jnp.where(real, lo, 0).astype(jnp.int32)
    hi = jnp.where(real, hi, 0).astype(jnp.int32)
    one = jnp.ones((1,), jnp.int32)
    first = jnp.concatenate([one, (blk[1:] != blk[:-1]).astype(jnp.int32)])
    new_e = jnp.concatenate([one, (e[1:] != e[:-1]).astype(jnp.int32)])
    return blk, e, lo, hi, first, new_e


def _combine_kernel(dest_ref, nxt_ref, ys_ref, w_ref, x1_ref, h_ref, w1_ref, w3_ref, w2_ref,
                    gt2_ref, g_ref, o_ref, buf, y_scr, sems, *, tc):
    i = pl.program_id(0)
    n = pl.num_programs(0)
    slot = i % 2

    def gather(idx_ref, s):
        def issue(j, carry):
            for kk in range(TOP_K):
                _row_copy(ys_ref, idx_ref[0, kk * tc + j], buf.at[s, kk], j,
                          sems.at[s]).start(priority=kk % 2)
            return carry

        lax.fori_loop(0, tc, issue, 0, unroll=4)

    @pl.when(i == 0)
    def _():
        gather(dest_ref, 0)

    for s in range(2):
        @pl.when(jnp.logical_and(i + 1 < n, slot == 1 - s))
        def _():
            gather(nxt_ref, s)

    h_lo, h_hi = _unpack_rows(_load_slabs(h_ref, tc))
    h_lo = h_lo.astype(BF16)
    h_hi = h_hi.astype(BF16)
    half = h_lo.shape[1]
    h1 = (jnp.dot(h_lo, w1_ref[:half, :], preferred_element_type=F32)
          + jnp.dot(h_hi, w1_ref[half:, :], preferred_element_type=F32))
    h3 = (jnp.dot(h_lo, w3_ref[:half, :], preferred_element_type=F32)
          + jnp.dot(h_hi, w3_ref[half:, :], preferred_element_type=F32))
    y_scr[...] = jnp.dot((h1 * jax.nn.sigmoid(h1) * h3).astype(BF16), w2_ref[...],
                         preferred_element_type=F32)

    for kk in range(TOP_K):
        pltpu.make_async_copy(ys_ref.at[pl.ds(0, tc * SLAB), :], buf.at[slot, kk],
                              sems.at[slot]).wait()
    rows = 2 * SUBLANES
    for r0 in range(0, tc, rows):
        for kk in range(TOP_K):
            u = jnp.concatenate([buf[slot, kk, pl.ds(r0 * SLAB + c, rows, stride=SLAB), :]
                                 for c in range(SLAB)], axis=1)
            e_lo, e_hi = _unpack_rows(u)
            wk = w_ref[r0:r0 + rows, kk:kk + 1]
            moe_lo = wk * e_lo if kk == 0 else moe_lo + wk * e_lo
            moe_hi = wk * e_hi if kk == 0 else moe_hi + wk * e_hi
        y_scr[r0:r0 + rows, :] += jnp.concatenate([moe_lo, moe_hi], axis=1)
    o_ref[...] = x1_ref[...] + gt2_ref[...] * _rms(y_scr[...], g_ref[...])


def _combine(ys, dest_tiles, w_tok, x1, h2, w1s, w3s, w2s, mod4, g_post, S, tc):
    T, D = x1.shape
    F = w1s.shape[1]
    per_b = S // tc
    nt = T // tc
    row = lambda: pl.BlockSpec((tc, D), lambda i: (i, 0))
    return pl.pallas_call(
        functools.partial(_combine_kernel, tc=tc),
        grid=(nt,),
        in_specs=[pl.BlockSpec((None, 1, TOP_K * tc), lambda i: (i, 0, 0),
                               memory_space=pltpu.SMEM),
                  pl.BlockSpec((None, 1, TOP_K * tc), lambda i: (jnp.minimum(i + 1, nt - 1), 0, 0),
                               memory_space=pltpu.SMEM),
                  pl.BlockSpec(memory_space=pl.ANY),
                  pl.BlockSpec((tc, TOP_K), lambda i: (i, 0)),
                  row(), pl.BlockSpec((tc * SLAB, LANES), lambda i: (i, 0)),
                  pl.BlockSpec((D, F), lambda i: (0, 0)),
                  pl.BlockSpec((D, F), lambda i: (0, 0)),
                  pl.BlockSpec((F, D), lambda i: (0, 0)),
                  pl.BlockSpec((None, None, 1, D), lambda i: (i // per_b, 5, 0, 0)),
                  pl.BlockSpec((1, D), lambda i: (0, 0))],
        out_specs=row(),
        out_shape=jax.ShapeDtypeStruct((T, D), F32),
        scratch_shapes=[pltpu.VMEM((2, TOP_K, tc * SLAB, LANES), jnp.uint32),
                        pltpu.VMEM((tc, D), F32), pltpu.SemaphoreType.DMA((2,))],
        compiler_params=_cparams(("arbitrary",)),
        name="combine",
    )(dest_tiles, dest_tiles, ys, w_tok, x1, h2, w1s, w3s, w2s, mod4, g_post.reshape(1, D))


def _lambda_init(layer):
    return 0.8 - 0.6 * math.exp(-0.3 * layer)


def kernel(x, c, w_ada, b_ada, g_pre_mix, w_in, conv_w, conv_b, lru_wa, lru_ba, lru_wx, lru_bx,
           lru_lambda, lam_q1, lam_k1, lam_q2, lam_k2, g_subln, w_proj_rnn, w_proj_att, w_out,
           g_post_mix, g_pre_ffn, w_router, e_bias, w1_e, w3_e, w2_e, w1_s, w3_s, w2_s,
           g_post_ffn):
    B, S, D = x.shape
    T = B * S
    depth = w_ada.shape[0]
    slopes = np.exp2(-8.0 * np.arange(1, N_HEADS + 1, dtype=np.float32) / N_HEADS)
    kbias = _alibi_tables(slopes, S)
    tt = _tile(S, 256)
    bm = _tile(T * TOP_K, 512)
    n_blocks = T * TOP_K // bm

    x2 = x.reshape(T, D)
    for l in range(depth):
        lam_init = _lambda_init(l)
        mod4 = _ada(c, w_ada[l], b_ada[l]).reshape(B, 6, 1, D)

        proj = _inproj(x2, g_pre_mix[l], mod4, w_in[l].astype(BF16), S)
        yr = _lru(proj, conv_w[l], conv_b[l], lru_wa[l].astype(BF16), lru_ba[l],
                  lru_wx[l].astype(BF16), lru_bx[l], lru_lambda[l], B, S)
        lamv = jnp.stack([lam_q1[l], lam_k1[l], lam_q2[l], lam_k2[l]])
        ao = _attn(proj, kbias, lamv, g_subln[l], B, S, lam_init)
        x1, h2, h2p = _mixout(x2, yr, ao, proj, w_proj_rnn[l].astype(BF16),
                         w_proj_att[l].astype(BF16), w_out[l].astype(BF16),
                         g_post_mix[l], g_pre_ffn[l], mod4, S)

        top_e, top_w, rank, cnt = _router(h2, w_router[l].T, e_bias[l])
        counts = cnt[:, 0].astype(jnp.int32)
        offs = jnp.cumsum(counts) - counts
        dest = _slots(offs, top_e, rank)
        dest_tiles = dest.reshape(TOP_K, T // tt, tt).transpose(1, 0, 2).reshape(T // tt, 1,
                                                                                  TOP_K * tt)
        xs = _dispatch(h2p, dest_tiles, tt)
        items = _work_items(counts, bm, n_blocks)
        ys = _gmm(xs, items, w1_e[l], w3_e[l], w2_e[l], bm)
        x2 = _combine(ys, dest_tiles, top_w.T, x1, h2p, w1_s[l].astype(BF16),
                      w3_s[l].astype(BF16), w2_s[l].astype(BF16), mod4, g_post_ffn[l], S, tt)
    return x2.reshape(B, S, D)
```

```python
import functools
import math

import jax
import jax.numpy as jnp
import numpy as np
from jax import lax
from jax.experimental import pallas as pl
from jax.experimental.pallas import tpu as pltpu

F32 = jnp.float32
BF16 = jnp.bfloat16

EPS = 1e-6
N_HEADS = 8
HEAD_DIM = 64
V_DIM = 2 * HEAD_DIM
LRU_BLOCKS = 8
CONV_W = 4
LRU_C = 8.0
N_EXPERTS = 64
TOP_K = 8
N_GROUPS = 8
GROUP_SIZE = N_EXPERTS // N_GROUPS
TOPK_GROUPS = 4
ROUTED_SCALE = 2.5

LANES = 128
SUBLANES = 8
VMEM_LIMIT = 48 * 1024 * 1024


def _cparams(sem):
    return pltpu.CompilerParams(dimension_semantics=sem, vmem_limit_bytes=VMEM_LIMIT)


def _tile(n, pref):
    t = min(n, pref)
    while n % t:
        t //= 2
    return t


def _rms(x, g):
    return x * lax.rsqrt(jnp.mean(x * x, axis=-1, keepdims=True) + EPS) * g


def _ada_kernel(c_ref, w_ref, b_ref, o_ref):
    c = c_ref[...]
    cond = c * jax.nn.sigmoid(c)
    o_ref[...] = jnp.dot(cond, w_ref[...], preferred_element_type=F32) + b_ref[...]


def _ada(c, w, b):
    B, D = c.shape
    N = w.shape[1]
    tn = _tile(N, 1024)
    return pl.pallas_call(
        _ada_kernel,
        grid=(N // tn,),
        in_specs=[pl.BlockSpec((B, D), lambda j: (0, 0)),
                  pl.BlockSpec((D, tn), lambda j: (0, j)),
                  pl.BlockSpec((1, tn), lambda j: (0, j))],
        out_specs=pl.BlockSpec((B, tn), lambda j: (0, j)),
        out_shape=jax.ShapeDtypeStruct((B, N), F32),
        compiler_params=_cparams(("parallel",)),
        name="ada",
    )(c, w, b.reshape(1, N))


LOG2E = 1.4426950408889634
Q_COL_BLOCK = 2
Q_PRESCALE = HEAD_DIM ** -0.5 * LOG2E


def _inproj_kernel(x_ref, g_ref, sh_ref, sc_ref, w_ref, o_ref, h_scr):
    @pl.when(pl.program_id(1) == 0)
    def _():
        h = _rms(x_ref[...], g_ref[...]) * (1.0 + sc_ref[...]) + sh_ref[...]
        h_scr[...] = h.astype(BF16)

    r = jnp.dot(h_scr[...], w_ref[...], preferred_element_type=F32)
    r = r * jnp.where(pl.program_id(1) == Q_COL_BLOCK, Q_PRESCALE, 1.0)
    o_ref[...] = r.astype(o_ref.dtype)


def _inproj(x2, g, mod4, w_bf, S):
    T, D = x2.shape
    N = w_bf.shape[1]
    tm = _tile(S, 2048)
    tn = 1024
    per_b = S // tm
    return pl.pallas_call(
        _inproj_kernel,
        grid=(T // tm, N // tn),
        in_specs=[pl.BlockSpec((tm, D), lambda i, j: (i, 0)),
                  pl.BlockSpec((1, D), lambda i, j: (0, 0)),
                  pl.BlockSpec((None, None, 1, D), lambda i, j: (i // per_b, 0, 0, 0)),
                  pl.BlockSpec((None, None, 1, D), lambda i, j: (i // per_b, 1, 0, 0)),
                  pl.BlockSpec((D, tn), lambda i, j: (0, j))],
        out_specs=pl.BlockSpec((tm, tn), lambda i, j: (i, j)),
        out_shape=jax.ShapeDtypeStruct((T, N), BF16),
        scratch_shapes=[pltpu.VMEM((tm, D), BF16)],
        compiler_params=_cparams(("parallel", "arbitrary")),
        name="inproj",
    )(x2, g.reshape(1, D), mod4, mod4, w_bf)


def _lru_kernel(xr_ref, gr_ref, cw_ref, cb_ref, wa_ref, ba_ref, wx_ref, bx_ref, lam_ref,
                o_ref, xc_scr, prev_scr, h_scr, *, ts):
    s = pl.program_id(1)

    @pl.when(s == 0)
    def _():
        prev_scr[...] = jnp.zeros_like(prev_scr)
        h_scr[...] = jnp.zeros_like(h_scr)

    x = xr_ref[...].astype(F32)
    prev = prev_scr[...]
    row8 = lax.broadcasted_iota(jnp.int32, (SUBLANES, 1), 0)
    acc = cb_ref[...] + cw_ref[CONV_W - 1:CONV_W, :] * x
    xc_scr[...] = acc
    top = cb_ref[...] + cw_ref[CONV_W - 1:CONV_W, :] * x[0:SUBLANES, :]
    for j in range(1, CONV_W):
        wj = cw_ref[CONV_W - 1 - j:CONV_W - j, :]
        rj = pltpu.roll(x, j, axis=0)
        xc_scr[...] += wj * rj
        pj = pltpu.roll(prev, j, axis=0)
        top += wj * jnp.where(row8 < j, pj, rj[0:SUBLANES, :])
    xc_scr[0:SUBLANES, :] = top
    prev_scr[...] = x[ts - SUBLANES:ts, :]

    row = lax.broadcasted_iota(jnp.int32, (ts, 1), 0)
    is_first = jnp.logical_and(row == 0, s == 0)
    sub = lax.broadcasted_iota(jnp.int32, (1, SUBLANES, 1), 1)
    for n in range(LRU_BLOCKS):
        cols = slice(n * LANES, (n + 1) * LANES)
        xc = xc_scr[:, cols]
        xb = xc.astype(BF16)
        r = jax.nn.sigmoid(jnp.dot(xb, wa_ref[n], preferred_element_type=F32) + ba_ref[:, cols])
        i = jax.nn.sigmoid(jnp.dot(xb, wx_ref[n], preferred_element_type=F32) + bx_ref[:, cols])
        lam = lam_ref[:, cols]
        softplus_neg = jnp.maximum(-lam, 0.0) + jnp.log1p(jnp.exp(-jnp.abs(lam)))
        log_a = (-LRU_C * softplus_neg) * r
        a = jnp.exp(log_a)
        m2 = 1.0 - a * a
        mult = jnp.where(m2 > 0.0, m2 * lax.rsqrt(m2), 0.0)
        mult = jnp.where(is_first, 1.0, mult)
        u = mult * (i * xc)
        a = a.reshape(ts // SUBLANES, SUBLANES, LANES)
        u = u.reshape(ts // SUBLANES, SUBLANES, LANES)
        for d in (1, 2, 4):
            keep = sub >= d
            a_sh = jnp.where(keep, pltpu.roll(a, d, axis=1), 1.0)
            u_sh = jnp.where(keep, pltpu.roll(u, d, axis=1), 0.0)
            u = u + a * u_sh
            a = a * a_sh
        a = a.reshape(ts, LANES)
        u = u.reshape(ts, LANES)
        gate = jax.nn.gelu(gr_ref[:, cols].astype(F32))
        carry = h_scr[:, cols]
        step = 2 * SUBLANES
        for g in range(ts // step):
            r0 = g * step
            h0 = u[r0:r0 + SUBLANES, :] + a[r0:r0 + SUBLANES, :] * carry
            carry = h0[SUBLANES - 1:SUBLANES, :]
            h1 = u[r0 + SUBLANES:r0 + step, :] + a[r0 + SUBLANES:r0 + step, :] * carry
            carry = h1[SUBLANES - 1:SUBLANES, :]
            hg = jnp.concatenate([h0, h1], axis=0) * gate[r0:r0 + step, :]
            o_ref[r0:r0 + step, cols] = hg.astype(o_ref.dtype)
        h_scr[:, cols] = carry


def _lru(proj, conv_w, conv_b, wa_bf, ba, wx_bf, bx, lam, B, S):
    C = conv_w.shape[1]
    ts = _tile(S, 256)
    ns = S // ts
    vec = lambda: pl.BlockSpec((1, C), lambda b, s: (0, 0))
    blk = lambda: pl.BlockSpec((LRU_BLOCKS, LANES, LANES), lambda b, s: (0, 0, 0))
    return pl.pallas_call(
        functools.partial(_lru_kernel, ts=ts),
        grid=(B, ns),
        in_specs=[pl.BlockSpec((ts, C), lambda b, s: (b * ns + s, 0)),
                  pl.BlockSpec((ts, C), lambda b, s: (b * ns + s, 1)),
                  pl.BlockSpec((CONV_W, C), lambda b, s: (0, 0)),
                  vec(), blk(), vec(), blk(), vec(), vec()],
        out_specs=pl.BlockSpec((ts, C), lambda b, s: (b * ns + s, 0)),
        out_shape=jax.ShapeDtypeStruct((B * S, C), BF16),
        scratch_shapes=[pltpu.VMEM((ts, C), F32), pltpu.VMEM((SUBLANES, C), F32),
                        pltpu.VMEM((1, C), F32)],
        compiler_params=_cparams(("parallel", "arbitrary")),
        name="lru",
    )(proj, proj, conv_w, conv_b.reshape(1, C), wa_bf, ba.reshape(1, C), wx_bf, bx.reshape(1, C),
      lam.reshape(1, C))


BIAS_LANES = 3
ACC_ROWS = V_DIM + 2 * SUBLANES


def _alibi_tables(slopes, S):
    def top16(x):
        return (x.view(np.uint32) & np.uint32(0xFFFF0000)).view(np.float32)

    pos = np.arange(S, dtype=np.float32)
    b = (slopes.astype(np.float32) * np.float32(LOG2E))[:, None] * pos[None, :]
    hi = top16(b)
    mid = top16(b - hi)
    lo = top16(b - hi - mid)
    half = np.zeros(b.shape + (HEAD_DIM,), np.float32)
    half[..., 0], half[..., 1], half[..., 2] = hi, mid, lo
    zero = np.zeros_like(half)
    table = np.stack([np.concatenate([zero, half], axis=-1),
                      np.concatenate([half, zero], axis=-1)], axis=1)
    return jnp.asarray(table, dtype=BF16)


def _lane_masks():
    lane = lax.broadcasted_iota(jnp.int32, (1, V_DIM), 1)
    own = (jnp.where(lane < HEAD_DIM, 1.0, 0.0), jnp.where(lane >= HEAD_DIM, 1.0, 0.0))
    ones_row = (jnp.where(jnp.logical_and(lane >= HEAD_DIM, lane < HEAD_DIM + BIAS_LANES), 1.0, 0.0),
                jnp.where(lane < BIAS_LANES, 1.0, 0.0))
    return own, ones_row


def _attn_kernel(lamv_ref, gsub_ref, q_ref, k_ref, v_ref, kb_ref, o_ref,
                 vt_scr, kx_scr, *scratch, tq, lam_init):
    tk = tq
    own, _ = _lane_masks()
    for j in range(vt_scr.shape[0]):
        rows = slice(j * tk, (j + 1) * tk)
        vt_scr[j, 0:V_DIM, :] = v_ref[rows, :].astype(F32).T.astype(BF16)
        vt_scr[j, V_DIM:ACC_ROWS, :] = jnp.ones((ACC_ROWS - V_DIM, tk), BF16)
        kf = k_ref[rows, :].astype(F32)
        for mp in range(2):
            kx_scr[mp, rows, :] = (kf * own[mp] + kb_ref[mp, rows, :].astype(F32)).astype(BF16)

    def q_block(qi, carry):
        _attn_q_block(qi, lamv_ref, gsub_ref, q_ref, o_ref, vt_scr, kx_scr, *scratch,
                      tq=tq, lam_init=lam_init)
        return carry

    lax.fori_loop(0, q_ref.shape[0] // tq, q_block, 0)


def _attn_q_block(qi, lamv_ref, gsub_ref, q_ref, o_ref,
                  vt_scr, kx_scr, qx_scr, s_scr, p_scr, a_scr, m_scr, acc_scr, *, tq, lam_init):
    tk = tq
    q_rows = pl.ds(pl.multiple_of(qi * tq, tq), tq)
    own, ones_row = _lane_masks()
    q = q_ref[q_rows, :].astype(F32)
    for mp in range(2):
        qx_scr[mp] = (q * own[mp] + ones_row[mp]).astype(BF16)
    m_scr[...] = jnp.full_like(m_scr, -jnp.inf)
    acc_scr[...] = jnp.zeros_like(acc_scr)
    row8 = lax.broadcasted_iota(jnp.int32, (SUBLANES, LANES), 0)
    col = lax.broadcasted_iota(jnp.int32, (1, LANES), 1)
    neg_inf = jnp.float32(-jnp.inf)

    def tree(op, parts):
        parts = [p for p in parts if p is not None]
        while len(parts) > 1:
            parts = [op(parts[i], parts[i + 1]) if i + 1 < len(parts) else parts[i]
                     for i in range(0, len(parts), 2)]
        return parts[0]

    def scores(j, par):
        start = pl.multiple_of(j * tk, tk)
        for mp in range(2):
            s = lax.dot_general(kx_scr[mp, pl.ds(start, tk), :], qx_scr[mp],
                                (((1,), (1,)), ((), ())), preferred_element_type=F32)
            for c in range(tq // LANES):
                s_scr[par, mp, c] = s[:, c * LANES:(c + 1) * LANES]

    def softmax(par, masked):
        for mp in range(2):
            for c in range(tq // LANES):
                cols = slice(c * LANES, (c + 1) * LANES)
                lo_col, hi_col = c * LANES, (c + 1) * LANES - 1
                s_c = s_scr.at[par, mp, c]
                p_c = p_scr.at[par, mp, c]
                accs = [None] * 4
                for i in range(tk // SUBLANES):
                    r0 = i * SUBLANES
                    if masked and r0 > hi_col:
                        continue
                    t = s_c[r0:r0 + SUBLANES, :]
                    if masked and r0 + SUBLANES - 1 > lo_col:
                        t = jnp.where(row8 + r0 <= col + lo_col, t, neg_inf)
                        s_c[r0:r0 + SUBLANES, :] = t
                    accs[i % 4] = t if accs[i % 4] is None else jnp.maximum(accs[i % 4], t)
                mx = jnp.max(tree(jnp.maximum, accs), axis=0, keepdims=True)
                m_old = m_scr[mp, :, cols]
                m_new = jnp.maximum(m_old, mx)
                a_scr[par, mp, :, cols] = jnp.exp2(m_old - m_new)
                m_scr[mp, :, cols] = m_new
                for i in range(tk // (2 * SUBLANES)):
                    r0 = i * 2 * SUBLANES
                    if masked and r0 > hi_col:
                        p_c[r0:r0 + 2 * SUBLANES, :] = jnp.zeros((2 * SUBLANES, LANES), BF16)
                        continue
                    p = jnp.exp2(s_c[r0:r0 + 2 * SUBLANES, :] - m_new)
                    p_c[r0:r0 + 2 * SUBLANES, :] = p.astype(BF16)

    def values(j, par):
        for mp in range(2):
            p = jnp.concatenate([p_scr[par, mp, c] for c in range(tq // LANES)], axis=1)
            pv = jnp.dot(vt_scr[j], p, preferred_element_type=F32)
            acc_scr[mp] = a_scr[par, mp] * acc_scr[mp] + pv

    p_scr[1] = jnp.zeros(p_scr.shape[1:], BF16)
    a_scr[1] = jnp.ones(a_scr.shape[1:], F32)
    scores(0, 0)

    def stage(j, par):
        scores(j + 1, 1 - par)
        softmax(par, False)
        values(jnp.maximum(j - 1, 0), 1 - par)

    def stage_pair(i, carry):
        stage(2 * i, 0)
        stage(2 * i + 1, 1)
        return carry

    def tail(par):
        softmax(par, True)
        values(jnp.maximum(qi - 1, 0), 1 - par)
        values(qi, par)

    lax.fori_loop(0, qi // 2, stage_pair, 0)

    @pl.when(qi % 2 == 1)
    def _():
        stage(qi - 1, 0)
        tail(1)

    @pl.when(qi % 2 == 0)
    def _():
        tail(0)

    lv = lamv_ref[...]
    lam = (jnp.exp(jnp.sum(lv[0:1, :] * lv[1:2, :], axis=-1, keepdims=True))
           - jnp.exp(jnp.sum(lv[2:3, :] * lv[3:4, :], axis=-1, keepdims=True)) + lam_init)
    o0 = acc_scr[0, 0:V_DIM, :] * (1.0 / acc_scr[0, V_DIM:V_DIM + 1, :])
    o1 = acc_scr[1, 0:V_DIM, :] * (1.0 / acc_scr[1, V_DIM:V_DIM + 1, :])
    o_t = o0 - lam * o1
    o_t = o_t * lax.rsqrt(jnp.mean(o_t * o_t, axis=0, keepdims=True) + EPS) * gsub_ref[...]
    o_ref[q_rows, :] = (o_t * (1.0 - lam_init)).T.astype(o_ref.dtype)


def _attn(proj, kbias, lamv, g_subln, B, S, lam_init):
    tq = _tile(S, 512)
    nq = S // tq
    qc, kc, vc = 2 * 8, 3 * 8, 4 * 8
    return pl.pallas_call(
        functools.partial(_attn_kernel, tq=tq, lam_init=lam_init),
        grid=(B, N_HEADS),
        in_specs=[pl.BlockSpec((4, HEAD_DIM), lambda b, h: (0, 0)),
                  pl.BlockSpec((V_DIM, 1), lambda b, h: (0, 0)),
                  pl.BlockSpec((S, V_DIM), lambda b, h: (b, qc + h)),
                  pl.BlockSpec((S, V_DIM), lambda b, h: (b, kc + h)),
                  pl.BlockSpec((S, V_DIM), lambda b, h: (b, vc + h)),
                  pl.BlockSpec((None, 2, S, V_DIM), lambda b, h: (h, 0, 0, 0))],
        out_specs=pl.BlockSpec((S, V_DIM), lambda b, h: (b, h)),
        out_shape=jax.ShapeDtypeStruct((B * S, N_HEADS * V_DIM), BF16),
        scratch_shapes=[pltpu.VMEM((nq, ACC_ROWS, tq), BF16), pltpu.VMEM((2, S, V_DIM), BF16),
                        pltpu.VMEM((2, tq, V_DIM), BF16),
                        pltpu.VMEM((2, 2, tq // LANES, tq, LANES), F32),
                        pltpu.VMEM((2, 2, tq // LANES, tq, LANES), BF16),
                        pltpu.VMEM((2, 2, 1, tq), F32), pltpu.VMEM((2, 1, tq), F32),
                        pltpu.VMEM((2, ACC_ROWS, tq), F32)],
        compiler_params=_cparams(("parallel", "parallel")),
        name="attn",
    )(lamv, g_subln.reshape(V_DIM, 1), proj, proj, proj, kbias)


def _pack_rows(x):
    half = x.shape[1] // 2
    lo = lax.bitcast_convert_type(x[:, :half].astype(BF16).astype(F32), jnp.uint32)
    hi = lax.bitcast_convert_type(x[:, half:].astype(BF16).astype(F32), jnp.uint32)
    return (hi & jnp.uint32(0xFFFF0000)) | (lo >> 16)


def _unpack_rows(u):
    lo = lax.bitcast_convert_type(u << 16, F32)
    hi = lax.bitcast_convert_type(u & jnp.uint32(0xFFFF0000), F32)
    return lo, hi


SLAB = 4


def _load_slabs(ref, n):
    return jnp.concatenate([ref[pl.ds(c, n, stride=SLAB), :] for c in range(SLAB)], axis=1)


def _store_slabs(ref, u):
    n = u.shape[0]
    for c in range(SLAB):
        ref[pl.ds(c, n, stride=SLAB), :] = u[:, c * LANES:(c + 1) * LANES]


def _mixout_kernel(x_ref, yr_ref, ao_ref, ga_ref, gb_ref, wr_ref, wa_ref, wo_ref,
                   gpost_ref, gt1_ref, gpre_ref, sh2_ref, sc2_ref, x1_ref, h2_ref, h2p_ref):
    ya = jnp.dot(yr_ref[...], wr_ref[...], preferred_element_type=F32)
    yb = jnp.dot(ao_ref[...], wa_ref[...], preferred_element_type=F32)
    merged = (jax.nn.sigmoid(ga_ref[...].astype(F32)) * ya
              + jax.nn.sigmoid(gb_ref[...].astype(F32)) * yb)
    y = jnp.dot(merged.astype(BF16), wo_ref[...], preferred_element_type=F32)
    x1 = x_ref[...] + gt1_ref[...] * _rms(y, gpost_ref[...])
    x1_ref[...] = x1
    h2 = _rms(x1, gpre_ref[...]) * (1.0 + sc2_ref[...]) + sh2_ref[...]
    h2_ref[...] = h2
    _store_slabs(h2p_ref, _pack_rows(h2))


def _mixout(x2, yr, ao, proj, wr_bf, wa_bf, wo_bf, g_post, g_pre, mod4, S):
    T, D = x2.shape
    tm = _tile(S, 512)
    per_b = S // tm
    gac, gbc = 5, 6
    row = lambda: pl.BlockSpec((tm, D), lambda i: (i, 0))
    wsp = lambda: pl.BlockSpec((D, D), lambda i: (0, 0))
    vec = lambda: pl.BlockSpec((1, D), lambda i: (0, 0))
    modv = lambda j: pl.BlockSpec((None, None, 1, D), lambda i: (i // per_b, j, 0, 0))
    return pl.pallas_call(
        _mixout_kernel,
        grid=(T // tm,),
        in_specs=[row(), row(), row(),
                  pl.BlockSpec((tm, D), lambda i: (i, gac)),
                  pl.BlockSpec((tm, D), lambda i: (i, gbc)),
                  wsp(), wsp(), wsp(), vec(), modv(2), vec(), modv(3), modv(4)],
        out_specs=[row(), row(), pl.BlockSpec((tm * SLAB, LANES), lambda i: (i, 0))],
        out_shape=[jax.ShapeDtypeStruct((T, D), F32), jax.ShapeDtypeStruct((T, D), F32),
                   jax.ShapeDtypeStruct((T * SLAB, LANES), jnp.uint32)],
        compiler_params=_cparams(("parallel",)),
        name="mixout",
    )(x2, yr, ao, proj, proj, wr_bf, wa_bf, wo_bf, g_post.reshape(1, D), mod4,
      g_pre.reshape(1, D), mod4, mod4)


def _first_argmax(vals, ids, sentinel):
    m = jnp.max(vals, axis=0, keepdims=True)
    idx = jnp.min(jnp.where(vals == m, ids, sentinel), axis=0, keepdims=True)
    return m, idx


def _router_kernel(h_ref, wr_ref, eb_ref, e_ref, w_ref, r_ref, cnt_ref, cnt_scr, *, tm):
    step = pl.program_id(0)

    @pl.when(step == 0)
    def _():
        cnt_scr[...] = jnp.zeros_like(cnt_scr)

    logits = lax.dot_general(wr_ref[...], h_ref[...], (((1,), (1,)), ((), ())),
                             preferred_element_type=F32, precision=lax.Precision.HIGHEST)
    scores = jax.nn.sigmoid(logits)
    choice = scores + eb_ref[...]
    i8 = lax.broadcasted_iota(jnp.int32, (GROUP_SIZE, tm), 0)
    neg_inf = jnp.float32(-jnp.inf)

    slabs = [choice[g * GROUP_SIZE:(g + 1) * GROUP_SIZE, :] for g in range(N_GROUPS)]
    sc_slabs = [scores[g * GROUP_SIZE:(g + 1) * GROUP_SIZE, :] for g in range(N_GROUPS)]

    gs = jnp.zeros((N_GROUPS, tm), F32)
    for g in range(N_GROUPS):
        m1, idx1 = _first_argmax(slabs[g], i8, GROUP_SIZE)
        m2 = jnp.max(jnp.where(i8 == idx1, neg_inf, slabs[g]), axis=0, keepdims=True)
        gs = jnp.where(i8 == g, m1 + m2, gs)

    sel = jnp.zeros((N_GROUPS, tm), jnp.int32)
    cur = gs
    for _ in range(TOPK_GROUPS):
        _, idx = _first_argmax(cur, i8, N_GROUPS)
        hit = i8 == idx
        sel = jnp.where(hit, 1, sel)
        cur = jnp.where(hit, neg_inf, cur)

    masked = [jnp.where(sel[g:g + 1, :] > 0, slabs[g], neg_inf) for g in range(N_GROUPS)]
    ids = [i8 + g * GROUP_SIZE for g in range(N_GROUPS)]
    onehot = [jnp.zeros((GROUP_SIZE, tm), F32) for _ in range(N_GROUPS)]
    picks = []
    wts = []
    for _ in range(TOP_K):
        m = functools.reduce(jnp.maximum,
                             [jnp.max(c, axis=0, keepdims=True) for c in masked])
        idx = functools.reduce(
            jnp.minimum,
            [jnp.min(jnp.where(c == m, i, N_EXPERTS), axis=0, keepdims=True)
             for c, i in zip(masked, ids)])
        w = jnp.zeros((1, tm), F32)
        for g in range(N_GROUPS):
            hit = ids[g] == idx
            w = w + jnp.sum(jnp.where(hit, sc_slabs[g], 0.0), axis=0, keepdims=True)
            masked[g] = jnp.where(hit, neg_inf, masked[g])
            onehot[g] = jnp.where(hit, 1.0, onehot[g])
        picks.append(idx)
        wts.append(w)

    wsum = functools.reduce(lambda a, b: a + b, wts)
    norm = ROUTED_SCALE / (wsum + 1e-20)

    t_row = lax.broadcasted_iota(jnp.int32, (tm, tm), 0)
    t_col = lax.broadcasted_iota(jnp.int32, (tm, tm), 1)
    before = jnp.where(t_row < t_col, 1.0, 0.0).astype(BF16)
    cum = [jnp.dot(onehot[g].astype(BF16), before, preferred_element_type=F32)
           + cnt_scr[g * GROUP_SIZE:(g + 1) * GROUP_SIZE, :] for g in range(N_GROUPS)]

    for kk in range(TOP_K):
        rank = jnp.zeros((1, tm), F32)
        for g in range(N_GROUPS):
            rank = rank + jnp.sum(jnp.where(ids[g] == picks[kk], cum[g], 0.0),
                                  axis=0, keepdims=True)
        e_ref[kk:kk + 1, :] = picks[kk]
        w_ref[kk:kk + 1, :] = wts[kk] * norm
        r_ref[kk:kk + 1, :] = rank.astype(jnp.int32)

    for g in range(N_GROUPS):
        rows = slice(g * GROUP_SIZE, (g + 1) * GROUP_SIZE)
        cnt_scr[rows, :] = cnt_scr[rows, :] + jnp.sum(onehot[g], axis=1, keepdims=True)
    cnt_ref[...] = jnp.broadcast_to(cnt_scr[...], cnt_ref.shape)


def _router(h2, w_router_t, e_bias):
    T, D = h2.shape
    tm = _tile(T, 512)
    return pl.pallas_call(
        functools.partial(_router_kernel, tm=tm),
        grid=(T // tm,),
        in_specs=[pl.BlockSpec((tm, D), lambda i: (i, 0)),
                  pl.BlockSpec((N_EXPERTS, D), lambda i: (0, 0)),
                  pl.BlockSpec((N_EXPERTS, 1), lambda i: (0, 0))],
        out_specs=[pl.BlockSpec((TOP_K, tm), lambda i: (0, i)),
                   pl.BlockSpec((TOP_K, tm), lambda i: (0, i)),
                   pl.BlockSpec((TOP_K, tm), lambda i: (0, i)),
                   pl.BlockSpec((N_EXPERTS, LANES), lambda i: (0, 0))],
        out_shape=[jax.ShapeDtypeStruct((TOP_K, T), jnp.int32),
                   jax.ShapeDtypeStruct((TOP_K, T), F32),
                   jax.ShapeDtypeStruct((TOP_K, T), jnp.int32),
                   jax.ShapeDtypeStruct((N_EXPERTS, LANES), F32)],
        scratch_shapes=[pltpu.VMEM((N_EXPERTS, 1), F32)],
        compiler_params=_cparams(("arbitrary",)),
        name="router",
    )(h2, w_router_t, e_bias.reshape(N_EXPERTS, 1))


def _slots_kernel(offs_ref, e_ref, r_ref, d_ref):
    e = e_ref[...]
    d = r_ref[...]
    for ex in range(N_EXPERTS):
        d = d + jnp.where(e == ex, offs_ref[ex], 0)
    d_ref[...] = d


def _slots(offs, top_e, rank):
    K, T = top_e.shape
    tm = _tile(T, 4096)
    grid_spec = pltpu.PrefetchScalarGridSpec(
        num_scalar_prefetch=1,
        grid=(T // tm,),
        in_specs=[pl.BlockSpec((K, tm), lambda i, o: (0, i)),
                  pl.BlockSpec((K, tm), lambda i, o: (0, i))],
        out_specs=pl.BlockSpec((K, tm), lambda i, o: (0, i)),
    )
    return pl.pallas_call(
        _slots_kernel,
        grid_spec=grid_spec,
        out_shape=jax.ShapeDtypeStruct((K, T), jnp.int32),
        compiler_params=_cparams(("parallel",)),
        name="slots",
    )(offs, top_e, rank)


def _row_copy(src, src_row, dst, dst_row, sem):
    s0 = pl.multiple_of(src_row * SLAB, SLAB)
    d0 = pl.multiple_of(dst_row * SLAB, SLAB)
    return pltpu.make_async_copy(src.at[pl.ds(s0, SLAB), :], dst.at[pl.ds(d0, SLAB), :], sem)


def _dispatch_kernel(dest_ref, h_ref, xs_ref, sem, *, td):
    def issue(j, carry):
        for kk in range(TOP_K):
            _row_copy(h_ref, j, xs_ref, dest_ref[0, kk * td + j], sem).start(priority=kk % 2)
        return carry

    lax.fori_loop(0, td, issue, 0)
    for _ in range(TOP_K):
        pltpu.make_async_copy(h_ref, xs_ref.at[pl.ds(0, td * SLAB), :], sem).wait()


def _dispatch(h2, dest_tiles, td):
    T = h2.shape[0] // SLAB
    nt = T // td
    return pl.pallas_call(
        functools.partial(_dispatch_kernel, td=td),
        grid=(nt,),
        in_specs=[pl.BlockSpec((None, 1, TOP_K * td), lambda i: (i, 0, 0),
                               memory_space=pltpu.SMEM),
                  pl.BlockSpec((td * SLAB, LANES), lambda i: (i, 0))],
        out_specs=pl.BlockSpec(memory_space=pl.ANY),
        out_shape=jax.ShapeDtypeStruct((T * TOP_K * SLAB, LANES), h2.dtype),
        scratch_shapes=[pltpu.SemaphoreType.DMA],
        compiler_params=_cparams(("arbitrary",)),
        name="dispatch",
    )(dest_tiles, h2)


def _gmm_kernel(blk_ref, exp_ref, lo_ref, hi_ref, first_ref, newe_ref, x_ref, w1_ref, w3_ref,
                w2_ref, o_ref, w1b, w3b, w2b, *, bm):
    it = pl.program_id(0)

    @pl.when(newe_ref[it] == 1)
    def _():
        w1b[...] = w1_ref[...].astype(BF16)
        w3b[...] = w3_ref[...].astype(BF16)
        w2b[...] = w2_ref[...].astype(BF16)

    rows = lax.broadcasted_iota(jnp.int32, (bm, 1), 0)
    valid = jnp.logical_and(rows >= lo_ref[it], rows < hi_ref[it])
    x_lo, x_hi = _unpack_rows(jnp.where(valid, _load_slabs(x_ref, bm), jnp.uint32(0)))
    x_lo = x_lo.astype(BF16)
    x_hi = x_hi.astype(BF16)
    half = x_lo.shape[1]
    h1 = (jnp.dot(x_lo, w1b[:half, :], preferred_element_type=F32)
          + jnp.dot(x_hi, w1b[half:, :], preferred_element_type=F32))
    h3 = (jnp.dot(x_lo, w3b[:half, :], preferred_element_type=F32)
          + jnp.dot(x_hi, w3b[half:, :], preferred_element_type=F32))
    hb = (h1 * jax.nn.sigmoid(h1) * h3).astype(BF16)
    y = jnp.dot(hb, w2b[...], preferred_element_type=F32)

    @pl.when(first_ref[it] == 1)
    def _():
        _store_slabs(o_ref, _pack_rows(y))

    @pl.when(first_ref[it] == 0)
    def _():
        o_lo, o_hi = _unpack_rows(_load_slabs(o_ref, bm))
        _store_slabs(o_ref, _pack_rows(y + jnp.concatenate([o_lo, o_hi], axis=1)))


def _gmm(xs, items, w1, w3, w2, bm):
    A = xs.shape[0] // SLAB
    D = w1.shape[1]
    F = w1.shape[2]
    n_items = items[0].shape[0]
    grid_spec = pltpu.PrefetchScalarGridSpec(
        num_scalar_prefetch=6,
        grid=(n_items,),
        in_specs=[pl.BlockSpec((bm * SLAB, LANES), lambda i, blk, ex, *_: (blk[i], 0)),
                  pl.BlockSpec((None, D, F), lambda i, blk, ex, *_: (ex[i], 0, 0)),
                  pl.BlockSpec((None, D, F), lambda i, blk, ex, *_: (ex[i], 0, 0)),
                  pl.BlockSpec((None, F, D), lambda i, blk, ex, *_: (ex[i], 0, 0))],
        out_specs=pl.BlockSpec((bm * SLAB, LANES), lambda i, blk, ex, *_: (blk[i], 0)),
        scratch_shapes=[pltpu.VMEM((D, F), BF16), pltpu.VMEM((D, F), BF16),
                        pltpu.VMEM((F, D), BF16)],
    )
    return pl.pallas_call(
        functools.partial(_gmm_kernel, bm=bm),
        grid_spec=grid_spec,
        out_shape=jax.ShapeDtypeStruct((A * SLAB, LANES), jnp.uint32),
        compiler_params=_cparams(("arbitrary",)),
        name="gmm",
    )(*items, xs, w1, w3, w2)


def _work_items(counts, bm, n_blocks):
    n_items = n_blocks + N_EXPERTS - 1
    ends = jnp.cumsum(counts)
    starts = ends - counts
    nb = jnp.where(counts > 0, (ends - 1) // bm - starts // bm + 1, 0)
    item_end = jnp.cumsum(nb)
    item_start = item_end - nb
    n_real = item_end[-1]
    i = jnp.arange(n_items, dtype=jnp.int32)
    e = jnp.minimum(jnp.sum(item_end[None, :] <= i[:, None], axis=1), N_EXPERTS - 1).astype(jnp.int32)
    onehot = e[:, None] == jnp.arange(N_EXPERTS, dtype=jnp.int32)[None, :]
    pick = lambda v: jnp.sum(jnp.where(onehot, v[None, :], 0), axis=1)
    start_e, end_e = pick(starts), pick(ends)
    blk = start_e // bm + (i - pick(item_start))
    lo = jnp.clip(start_e - blk * bm, 0, bm)
    hi = jnp.clip(end_e - blk * bm, 0, bm)
    real = i < n_real
    blk = jnp.where(real, blk, n_blocks - 1).astype(jnp.int32)
    lo = jnp.where(real, lo, 0).astype(jnp.int32)
    hi = jnp.where(real, hi, 0).astype(jnp.int32)
    one = jnp.ones((1,), jnp.int32)
    first = jnp.concatenate([one, (blk[1:] != blk[:-1]).astype(jnp.int32)])
    new_e = jnp.concatenate([one, (e[1:] != e[:-1]).astype(jnp.int32)])
    return blk, e, lo, hi, first, new_e


def _combine_kernel(dest_ref, nxt_ref, ys_ref, w_ref, x1_ref, h_ref, w1_ref, w3_ref, w2_ref,
                    gt2_ref, g_ref, o_ref, buf, y_scr, sems, *, tc):
    i = pl.program_id(0)
    n = pl.num_programs(0)
    slot = i % 2

    def gather(idx_ref, s):
        def issue(j, carry):
            for kk in range(TOP_K):
                _row_copy(ys_ref, idx_ref[0, kk * tc + j], buf.at[s, kk], j,
                          sems.at[s]).start(priority=kk % 2)
            return carry

        lax.fori_loop(0, tc, issue, 0, unroll=4)

    @pl.when(i == 0)
    def _():
        gather(dest_ref, 0)

    for s in range(2):
        @pl.when(jnp.logical_and(i + 1 < n, slot == 1 - s))
        def _():
            gather(nxt_ref, s)

    h_lo, h_hi = _unpack_rows(_load_slabs(h_ref, tc))
    h_lo = h_lo.astype(BF16)
    h_hi = h_hi.astype(BF16)
    half = h_lo.shape[1]
    h1 = (jnp.dot(h_lo, w1_ref[:half, :], preferred_element_type=F32)
          + jnp.dot(h_hi, w1_ref[half:, :], preferred_element_type=F32))
    h3 = (jnp.dot(h_lo, w3_ref[:half, :], preferred_element_type=F32)
          + jnp.dot(h_hi, w3_ref[half:, :], preferred_element_type=F32))
    y_scr[...] = jnp.dot((h1 * jax.nn.sigmoid(h1) * h3).astype(BF16), w2_ref[...],
                         preferred_element_type=F32)

    for kk in range(TOP_K):
        pltpu.make_async_copy(ys_ref.at[pl.ds(0, tc * SLAB), :], buf.at[slot, kk],
                              sems.at[slot]).wait()
    rows = 2 * SUBLANES
    for r0 in range(0, tc, rows):
        for kk in range(TOP_K):
            u = jnp.concatenate([buf[slot, kk, pl.ds(r0 * SLAB + c, rows, stride=SLAB), :]
                                 for c in range(SLAB)], axis=1)
            e_lo, e_hi = _unpack_rows(u)
            wk = w_ref[r0:r0 + rows, kk:kk + 1]
            moe_lo = wk * e_lo if kk == 0 else moe_lo + wk * e_lo
            moe_hi = wk * e_hi if kk == 0 else moe_hi + wk * e_hi
        y_scr[r0:r0 + rows, :] += jnp.concatenate([moe_lo, moe_hi], axis=1)
    o_ref[...] = x1_ref[...] + gt2_ref[...] * _rms(y_scr[...], g_ref[...])


def _combine(ys, dest_tiles, w_tok, x1, h2, w1s, w3s, w2s, mod4, g_post, S, tc):
    T, D = x1.shape
    F = w1s.shape[1]
    per_b = S // tc
    nt = T // tc
    row = lambda: pl.BlockSpec((tc, D), lambda i: (i, 0))
    return pl.pallas_call(
        functools.partial(_combine_kernel, tc=tc),
        grid=(nt,),
        in_specs=[pl.BlockSpec((None, 1, TOP_K * tc), lambda i: (i, 0, 0),
                               memory_space=pltpu.SMEM),
                  pl.BlockSpec((None, 1, TOP_K * tc), lambda i: (jnp.minimum(i + 1, nt - 1), 0, 0),
                               memory_space=pltpu.SMEM),
                  pl.BlockSpec(memory_space=pl.ANY),
                  pl.BlockSpec((tc, TOP_K), lambda i: (i, 0)),
                  row(), pl.BlockSpec((tc * SLAB, LANES), lambda i: (i, 0)),
                  pl.BlockSpec((D, F), lambda i: (0, 0)),
                  pl.BlockSpec((D, F), lambda i: (0, 0)),
                  pl.BlockSpec((F, D), lambda i: (0, 0)),
                  pl.BlockSpec((None, None, 1, D), lambda i: (i // per_b, 5, 0, 0)),
                  pl.BlockSpec((1, D), lambda i: (0, 0))],
        out_specs=row(),
        out_shape=jax.ShapeDtypeStruct((T, D), F32),
        scratch_shapes=[pltpu.VMEM((2, TOP_K, tc * SLAB, LANES), jnp.uint32),
                        pltpu.VMEM((tc, D), F32), pltpu.SemaphoreType.DMA((2,))],
        compiler_params=_cparams(("arbitrary",)),
        name="combine",
    )(dest_tiles, dest_tiles, ys, w_tok, x1, h2, w1s, w3s, w2s, mod4, g_post.reshape(1, D))


def _lambda_init(layer):
    return 0.8 - 0.6 * math.exp(-0.3 * layer)


def kernel(x, c, w_ada, b_ada, g_pre_mix, w_in, conv_w, conv_b, lru_wa, lru_ba, lru_wx, lru_bx,
           lru_lambda, lam_q1, lam_k1, lam_q2, lam_k2, g_subln, w_proj_rnn, w_proj_att, w_out,
           g_post_mix, g_pre_ffn, w_router, e_bias, w1_e, w3_e, w2_e, w1_s, w3_s, w2_s,
           g_post_ffn):
    B, S, D = x.shape
    T = B * S
    depth = w_ada.shape[0]
    slopes = np.exp2(-8.0 * np.arange(1, N_HEADS + 1, dtype=np.float32) / N_HEADS)
    kbias = _alibi_tables(slopes, S)
    tt = _tile(S, 256)
    bm = _tile(T * TOP_K, 512)
    n_blocks = T * TOP_K // bm

    x2 = x.reshape(T, D)
    for l in range(depth):
        lam_init = _lambda_init(l)
        mod4 = _ada(c, w_ada[l], b_ada[l]).reshape(B, 6, 1, D)

        proj = _inproj(x2, g_pre_mix[l], mod4, w_in[l].astype(BF16), S)
        yr = _lru(proj, conv_w[l], conv_b[l], lru_wa[l].astype(BF16), lru_ba[l],
                  lru_wx[l].astype(BF16), lru_bx[l], lru_lambda[l], B, S)
        lamv = jnp.stack([lam_q1[l], lam_k1[l], lam_q2[l], lam_k2[l]])
        ao = _attn(proj, kbias, lamv, g_subln[l], B, S, lam_init)
        x1, h2, h2p = _mixout(x2, yr, ao, proj, w_proj_rnn[l].astype(BF16),
                         w_proj_att[l].astype(BF16), w_out[l].astype(BF16),
                         g_post_mix[l], g_pre_ffn[l], mod4, S)

        top_e, top_w, rank, cnt = _router(h2, w_router[l].T, e_bias[l])
        counts = cnt[:, 0].astype(jnp.int32)
        offs = jnp.cumsum(counts) - counts
        dest = _slots(offs, top_e, rank)
        dest_tiles = dest.reshape(TOP_K, T // tt, tt).transpose(1, 0, 2).reshape(T // tt, 1,
                                                                                  TOP_K * tt)
        xs = _dispatch(h2p, dest_tiles, tt)
        items = _work_items(counts, bm, n_blocks)
        ys = _gmm(xs, items, w1_e[l], w3_e[l], w2_e[l], bm)
        x2 = _combine(ys, dest_tiles, top_w.T, x1, h2p, w1_s[l].astype(BF16),
                      w3_s[l].astype(BF16), w2_s[l].astype(BF16), mod4, g_post_ffn[l], S, tt)
    return x2.reshape(B, S, D)
```

```python
import functools
import math

import jax
import jax.numpy as jnp
import numpy as np
from jax import lax
from jax.experimental import pallas as pl
from jax.experimental.pallas import tpu as pltpu

F32 = jnp.float32
BF16 = jnp.bfloat16

EPS = 1e-6
N_HEADS = 8
HEAD_DIM = 64
V_DIM = 2 * HEAD_DIM
LRU_BLOCKS = 8
CONV_W = 4
LRU_C = 8.0
N_EXPERTS = 64
TOP_K = 8
N_GROUPS = 8
GROUP_SIZE = N_EXPERTS // N_GROUPS
TOPK_GROUPS = 4
ROUTED_SCALE = 2.5

LANES = 128
SUBLANES = 8
VMEM_LIMIT = 48 * 1024 * 1024


def _cparams(sem):
    return pltpu.CompilerParams(dimension_semantics=sem, vmem_limit_bytes=VMEM_LIMIT)


def _tile(n, pref):
    t = min(n, pref)
    while n % t:
        t //= 2
    return t


def _rms(x, g):
    return x * lax.rsqrt(jnp.mean(x * x, axis=-1, keepdims=True) + EPS) * g


def _ada_kernel(c_ref, w_ref, b_ref, o_ref):
    c = c_ref[...]
    cond = c * jax.nn.sigmoid(c)
    o_ref[...] = jnp.dot(cond, w_ref[...], preferred_element_type=F32) + b_ref[...]


def _ada(c, w, b):
    B, D = c.shape
    N = w.shape[1]
    tn = _tile(N, 1024)
    return pl.pallas_call(
        _ada_kernel,
        grid=(N // tn,),
        in_specs=[pl.BlockSpec((B, D), lambda j: (0, 0)),
                  pl.BlockSpec((D, tn), lambda j: (0, j)),
                  pl.BlockSpec((1, tn), lambda j: (0, j))],
        out_specs=pl.BlockSpec((B, tn), lambda j: (0, j)),
        out_shape=jax.ShapeDtypeStruct((B, N), F32),
        compiler_params=_cparams(("parallel",)),
        name="ada",
    )(c, w, b.reshape(1, N))


LOG2E = 1.4426950408889634
Q_COL_BLOCK = 2
Q_PRESCALE = HEAD_DIM ** -0.5 * LOG2E


def _inproj_kernel(x_ref, g_ref, sh_ref, sc_ref, w_ref, o_ref, h_scr):
    @pl.when(pl.program_id(1) == 0)
    def _():
        h = _rms(x_ref[...], g_ref[...]) * (1.0 + sc_ref[...]) + sh_ref[...]
        h_scr[...] = h.astype(BF16)

    r = jnp.dot(h_scr[...], w_ref[...], preferred_element_type=F32)
    r = r * jnp.where(pl.program_id(1) == Q_COL_BLOCK, Q_PRESCALE, 1.0)
    o_ref[...] = r.astype(o_ref.dtype)


def _inproj(x2, g, mod4, w_bf, S):
    T, D = x2.shape
    N = w_bf.shape[1]
    tm = _tile(S, 2048)
    tn = 1024
    per_b = S // tm
    return pl.pallas_call(
        _inproj_kernel,
        grid=(T // tm, N // tn),
        in_specs=[pl.BlockSpec((tm, D), lambda i, j: (i, 0)),
                  pl.BlockSpec((1, D), lambda i, j: (0, 0)),
                  pl.BlockSpec((None, None, 1, D), lambda i, j: (i // per_b, 0, 0, 0)),
                  pl.BlockSpec((None, None, 1, D), lambda i, j: (i // per_b, 1, 0, 0)),
                  pl.BlockSpec((D, tn), lambda i, j: (0, j))],
        out_specs=pl.BlockSpec((tm, tn), lambda i, j: (i, j)),
        out_shape=jax.ShapeDtypeStruct((T, N), BF16),
        scratch_shapes=[pltpu.VMEM((tm, D), BF16)],
        compiler_params=_cparams(("parallel", "arbitrary")),
        name="inproj",
    )(x2, g.reshape(1, D), mod4, mod4, w_bf)


def _lru_kernel(xr_ref, gr_ref, cw_ref, cb_ref, wa_ref, ba_ref, wx_ref, bx_ref, lam_ref,
                o_ref, xc_scr, prev_scr, h_scr, *, ts):
    s = pl.program_id(1)

    @pl.when(s == 0)
    def _():
        prev_scr[...] = jnp.zeros_like(prev_scr)
        h_scr[...] = jnp.zeros_like(h_scr)

    x = xr_ref[...].astype(F32)
    prev = prev_scr[...]
    row8 = lax.broadcasted_iota(jnp.int32, (SUBLANES, 1), 0)
    acc = cb_ref[...] + cw_ref[CONV_W - 1:CONV_W, :] * x
    xc_scr[...] = acc
    top = cb_ref[...] + cw_ref[CONV_W - 1:CONV_W, :] * x[0:SUBLANES, :]
    for j in range(1, CONV_W):
        wj = cw_ref[CONV_W - 1 - j:CONV_W - j, :]
        rj = pltpu.roll(x, j, axis=0)
        xc_scr[...] += wj * rj
        pj = pltpu.roll(prev, j, axis=0)
        top += wj * jnp.where(row8 < j, pj, rj[0:SUBLANES, :])
    xc_scr[0:SUBLANES, :] = top
    prev_scr[...] = x[ts - SUBLANES:ts, :]

    row = lax.broadcasted_iota(jnp.int32, (ts, 1), 0)
    is_first = jnp.logical_and(row == 0, s == 0)
    sub = lax.broadcasted_iota(jnp.int32, (1, SUBLANES, 1), 1)
    for n in range(LRU_BLOCKS):
        cols = slice(n * LANES, (n + 1) * LANES)
        xc = xc_scr[:, cols]
        xb = xc.astype(BF16)
        r = jax.nn.sigmoid(jnp.dot(xb, wa_ref[n], preferred_element_type=F32) + ba_ref[:, cols])
        i = jax.nn.sigmoid(jnp.dot(xb, wx_ref[n], preferred_element_type=F32) + bx_ref[:, cols])
        lam = lam_ref[:, cols]
        softplus_neg = jnp.maximum(-lam, 0.0) + jnp.log1p(jnp.exp(-jnp.abs(lam)))
        log_a = (-LRU_C * softplus_neg) * r
        a = jnp.exp(log_a)
        m2 = 1.0 - a * a
        mult = jnp.where(m2 > 0.0, m2 * lax.rsqrt(m2), 0.0)
        mult = jnp.where(is_first, 1.0, mult)
        u = mult * (i * xc)
        a = a.reshape(ts // SUBLANES, SUBLANES, LANES)
        u = u.reshape(ts // SUBLANES, SUBLANES, LANES)
        for d in (1, 2, 4):
            keep = sub >= d
            a_sh = jnp.where(keep, pltpu.roll(a, d, axis=1), 1.0)
            u_sh = jnp.where(keep, pltpu.roll(u, d, axis=1), 0.0)
            u = u + a * u_sh
            a = a * a_sh
        a = a.reshape(ts, LANES)
        u = u.reshape(ts, LANES)
        gate = jax.nn.gelu(gr_ref[:, cols].astype(F32))
        carry = h_scr[:, cols]
        step = 2 * SUBLANES
        for g in range(ts // step):
            r0 = g * step
            h0 = u[r0:r0 + SUBLANES, :] + a[r0:r0 + SUBLANES, :] * carry
            carry = h0[SUBLANES - 1:SUBLANES, :]
            h1 = u[r0 + SUBLANES:r0 + step, :] + a[r0 + SUBLANES:r0 + step, :] * carry
            carry = h1[SUBLANES - 1:SUBLANES, :]
            hg = jnp.concatenate([h0, h1], axis=0) * gate[r0:r0 + step, :]
            o_ref[r0:r0 + step, cols] = hg.astype(o_ref.dtype)
        h_scr[:, cols] = carry


def _lru(proj, conv_w, conv_b, wa_bf, ba, wx_bf, bx, lam, B, S):
    C = conv_w.shape[1]
    ts = _tile(S, 256)
    ns = S // ts
    vec = lambda: pl.BlockSpec((1, C), lambda b, s: (0, 0))
    blk = lambda: pl.BlockSpec((LRU_BLOCKS, LANES, LANES), lambda b, s: (0, 0, 0))
    return pl.pallas_call(
        functools.partial(_lru_kernel, ts=ts),
        grid=(B, ns),
        in_specs=[pl.BlockSpec((ts, C), lambda b, s: (b * ns + s, 0)),
                  pl.BlockSpec((ts, C), lambda b, s: (b * ns + s, 1)),
                  pl.BlockSpec((CONV_W, C), lambda b, s: (0, 0)),
                  vec(), blk(), vec(), blk(), vec(), vec()],
        out_specs=pl.BlockSpec((ts, C), lambda b, s: (b * ns + s, 0)),
        out_shape=jax.ShapeDtypeStruct((B * S, C), BF16),
        scratch_shapes=[pltpu.VMEM((ts, C), F32), pltpu.VMEM((SUBLANES, C), F32),
                        pltpu.VMEM((1, C), F32)],
        compiler_params=_cparams(("parallel", "arbitrary")),
        name="lru",
    )(proj, proj, conv_w, conv_b.reshape(1, C), wa_bf, ba.reshape(1, C), wx_bf, bx.reshape(1, C),
      lam.reshape(1, C))


BIAS_LANES = 3
ACC_ROWS = V_DIM + 2 * SUBLANES


def _alibi_tables(slopes, S):
    def top16(x):
        return (x.view(np.uint32) & np.uint32(0xFFFF0000)).view(np.float32)

    pos = np.arange(S, dtype=np.float32)
    b = (slopes.astype(np.float32) * np.float32(LOG2E))[:, None] * pos[None, :]
    hi = top16(b)
    mid = top16(b - hi)
    lo = top16(b - hi - mid)
    half = np.zeros(b.shape + (HEAD_DIM,), np.float32)
    half[..., 0], half[..., 1], half[..., 2] = hi, mid, lo
    zero = np.zeros_like(half)
    table = np.stack([np.concatenate([zero, half], axis=-1),
                      np.concatenate([half, zero], axis=-1)], axis=1)
    return jnp.asarray(table, dtype=BF16)


def _lane_masks():
    lane = lax.broadcasted_iota(jnp.int32, (1, V_DIM), 1)
    own = (jnp.where(lane < HEAD_DIM, 1.0, 0.0), jnp.where(lane >= HEAD_DIM, 1.0, 0.0))
    ones_row = (jnp.where(jnp.logical_and(lane >= HEAD_DIM, lane < HEAD_DIM + BIAS_LANES), 1.0, 0.0),
                jnp.where(lane < BIAS_LANES, 1.0, 0.0))
    return own, ones_row


def _attn_kernel(lamv_ref, gsub_ref, q_ref, k_ref, v_ref, kb_ref, o_ref,
                 vt_scr, kx_scr, *scratch, tq, lam_init):
    tk = tq
    own, _ = _lane_masks()
    for j in range(vt_scr.shape[0]):
        rows = slice(j * tk, (j + 1) * tk)
        vt_scr[j, 0:V_DIM, :] = v_ref[rows, :].astype(F32).T.astype(BF16)
        vt_scr[j, V_DIM:ACC_ROWS, :] = jnp.ones((ACC_ROWS - V_DIM, tk), BF16)
        kf = k_ref[rows, :].astype(F32)
        for mp in range(2):
            kx_scr[mp, rows, :] = (kf * own[mp] + kb_ref[mp, rows, :].astype(F32)).astype(BF16)

    def q_block(qi, carry):
        _attn_q_block(qi, lamv_ref, gsub_ref, q_ref, o_ref, vt_scr, kx_scr, *scratch,
                      tq=tq, lam_init=lam_init)
        return carry

    lax.fori_loop(0, q_ref.shape[0] // tq, q_block, 0)


def _attn_q_block(qi, lamv_ref, gsub_ref, q_ref, o_ref,
                  vt_scr, kx_scr, qx_scr, s_scr, p_scr, a_scr, m_scr, acc_scr, *, tq, lam_init):
    tk = tq
    q_rows = pl.ds(pl.multiple_of(qi * tq, tq), tq)
    own, ones_row = _lane_masks()
    q = q_ref[q_rows, :].astype(F32)
    for mp in range(2):
        qx_scr[mp] = (q * own[mp] + ones_row[mp]).astype(BF16)
    m_scr[...] = jnp.full_like(m_scr, -jnp.inf)
    acc_scr[...] = jnp.zeros_like(acc_scr)
    row8 = lax.broadcasted_iota(jnp.int32, (SUBLANES, LANES), 0)
    col = lax.broadcasted_iota(jnp.int32, (1, LANES), 1)
    neg_inf = jnp.float32(-jnp.inf)

    def tree(op, parts):
        parts = [p for p in parts if p is not None]
        while len(parts) > 1:
            parts = [op(parts[i], parts[i + 1]) if i + 1 < len(parts) else parts[i]
                     for i in range(0, len(parts), 2)]
        return parts[0]

    def scores(j, par):
        start = pl.multiple_of(j * tk, tk)
        for mp in range(2):
            s = lax.dot_general(kx_scr[mp, pl.ds(start, tk), :], qx_scr[mp],
                                (((1,), (1,)), ((), ())), preferred_element_type=F32)
            for c in range(tq // LANES):
                s_scr[par, mp, c] = s[:, c * LANES:(c + 1) * LANES]

    def softmax(par, masked):
        for mp in range(2):
            for c in range(tq // LANES):
                cols = slice(c * LANES, (c + 1) * LANES)
                lo_col, hi_col = c * LANES, (c + 1) * LANES - 1
                s_c = s_scr.at[par, mp, c]
                p_c = p_scr.at[par, mp, c]
                accs = [None] * 4
                for i in range(tk // SUBLANES):
                    r0 = i * SUBLANES
                    if masked and r0 > hi_col:
                        continue
                    t = s_c[r0:r0 + SUBLANES, :]
                    if masked and r0 + SUBLANES - 1 > lo_col:
                        t = jnp.where(row8 + r0 <= col + lo_col, t, neg_inf)
                        s_c[r0:r0 + SUBLANES, :] = t
                    accs[i % 4] = t if accs[i % 4] is None else jnp.maximum(accs[i % 4], t)
                mx = jnp.max(tree(jnp.maximum, accs), axis=0, keepdims=True)
                m_old = m_scr[mp, :, cols]
                m_new = jnp.maximum(m_old, mx)
                a_scr[par, mp, :, cols] = jnp.exp2(m_old - m_new)
                m_scr[mp, :, cols] = m_new
                for i in range(tk // (2 * SUBLANES)):
                    r0 = i * 2 * SUBLANES
                    if masked and r0 > hi_col:
                        p_c[r0:r0 + 2 * SUBLANES, :] = jnp.zeros((2 * SUBLANES, LANES), BF16)
                        continue
                    p = jnp.exp2(s_c[r0:r0 + 2 * SUBLANES, :] - m_new)
                    p_c[r0:r0 + 2 * SUBLANES, :] = p.astype(BF16)

    def values(j, par):
        for mp in range(2):
            p = jnp.concatenate([p_scr[par, mp, c] for c in range(tq // LANES)], axis=1)
            pv = jnp.dot(vt_scr[j], p, preferred_element_type=F32)
            acc_scr[mp] = a_scr[par, mp] * acc_scr[mp] + pv

    p_scr[1] = jnp.zeros(p_scr.shape[1:], BF16)
    a_scr[1] = jnp.ones(a_scr.shape[1:], F32)
    scores(0, 0)

    def stage(j, par):
        scores(j + 1, 1 - par)
        softmax(par, False)
        values(jnp.maximum(j - 1, 0), 1 - par)

    def stage_pair(i, carry):
        stage(2 * i, 0)
        stage(2 * i + 1, 1)
        return carry

    def tail(par):
        softmax(par, True)
        values(jnp.maximum(qi - 1, 0), 1 - par)
        values(qi, par)

    lax.fori_loop(0, qi // 2, stage_pair, 0)

    @pl.when(qi % 2 == 1)
    def _():
        stage(qi - 1, 0)
        tail(1)

    @pl.when(qi % 2 == 0)
    def _():
        tail(0)

    lv = lamv_ref[...]
    lam = (jnp.exp(jnp.sum(lv[0:1, :] * lv[1:2, :], axis=-1, keepdims=True))
           - jnp.exp(jnp.sum(lv[2:3, :] * lv[3:4, :], axis=-1, keepdims=True)) + lam_init)
    o0 = acc_scr[0, 0:V_DIM, :] * (1.0 / acc_scr[0, V_DIM:V_DIM + 1, :])
    o1 = acc_scr[1, 0:V_DIM, :] * (1.0 / acc_scr[1, V_DIM:V_DIM + 1, :])
    o_t = o0 - lam * o1
    o_t = o_t * lax.rsqrt(jnp.mean(o_t * o_t, axis=0, keepdims=True) + EPS) * gsub_ref[...]
    o_ref[q_rows, :] = (o_t * (1.0 - lam_init)).T.astype(o_ref.dtype)


def _attn(proj, kbias, lamv, g_subln, B, S, lam_init):
    tq = _tile(S, 512)
    nq = S // tq
    qc, kc, vc = 2 * 8, 3 * 8, 4 * 8
    return pl.pallas_call(
        functools.partial(_attn_kernel, tq=tq, lam_init=lam_init),
        grid=(B, N_HEADS),
        in_specs=[pl.BlockSpec((4, HEAD_DIM), lambda b, h: (0, 0)),
                  pl.BlockSpec((V_DIM, 1), lambda b, h: (0, 0)),
                  pl.BlockSpec((S, V_DIM), lambda b, h: (b, qc + h)),
                  pl.BlockSpec((S, V_DIM), lambda b, h: (b, kc + h)),
                  pl.BlockSpec((S, V_DIM), lambda b, h: (b, vc + h)),
                  pl.BlockSpec((None, 2, S, V_DIM), lambda b, h: (h, 0, 0, 0))],
        out_specs=pl.BlockSpec((S, V_DIM), lambda b, h: (b, h)),
        out_shape=jax.ShapeDtypeStruct((B * S, N_HEADS * V_DIM), BF16),
        scratch_shapes=[pltpu.VMEM((nq, ACC_ROWS, tq), BF16), pltpu.VMEM((2, S, V_DIM), BF16),
                        pltpu.VMEM((2, tq, V_DIM), BF16),
                        pltpu.VMEM((2, 2, tq // LANES, tq, LANES), F32),
                        pltpu.VMEM((2, 2, tq // LANES, tq, LANES), BF16),
                        pltpu.VMEM((2, 2, 1, tq), F32), pltpu.VMEM((2, 1, tq), F32),
                        pltpu.VMEM((2, ACC_ROWS, tq), F32)],
        compiler_params=_cparams(("parallel", "parallel")),
        name="attn",
    )(lamv, g_subln.reshape(V_DIM, 1), proj, proj, proj, kbias)


def _pack_rows(x):
    half = x.shape[1] // 2
    lo = lax.bitcast_convert_type(x[:, :half].astype(BF16).astype(F32), jnp.uint32)
    hi = lax.bitcast_convert_type(x[:, half:].astype(BF16).astype(F32), jnp.uint32)
    return (hi & jnp.uint32(0xFFFF0000)) | (lo >> 16)


def _unpack_rows(u):
    lo = lax.bitcast_convert_type(u << 16, F32)
    hi = lax.bitcast_convert_type(u & jnp.uint32(0xFFFF0000), F32)
    return lo, hi


SLAB = 4


def _load_slabs(ref, n):
    return jnp.concatenate([ref[pl.ds(c, n, stride=SLAB), :] for c in range(SLAB)], axis=1)


def _store_slabs(ref, u):
    n = u.shape[0]
    for c in range(SLAB):
        ref[pl.ds(c, n, stride=SLAB), :] = u[:, c * LANES:(c + 1) * LANES]


def _mixout_kernel(x_ref, yr_ref, ao_ref, ga_ref, gb_ref, wr_ref, wa_ref, wo_ref,
                   gpost_ref, gt1_ref, gpre_ref, sh2_ref, sc2_ref, x1_ref, h2_ref, h2p_ref):
    ya = jnp.dot(yr_ref[...], wr_ref[...], preferred_element_type=F32)
    yb = jnp.dot(ao_ref[...], wa_ref[...], preferred_element_type=F32)
    merged = (jax.nn.sigmoid(ga_ref[...].astype(F32)) * ya
              + jax.nn.sigmoid(gb_ref[...].astype(F32)) * yb)
    y = jnp.dot(merged.astype(BF16), wo_ref[...], preferred_element_type=F32)
    x1 = x_ref[...] + gt1_ref[...] * _rms(y, gpost_ref[...])
    x1_ref[...] = x1
    h2 = _rms(x1, gpre_ref[...]) * (1.0 + sc2_ref[...]) + sh2_ref[...]
    h2_ref[...] = h2
    _store_slabs(h2p_ref, _pack_rows(h2))


def _mixout(x2, yr, ao, proj, wr_bf, wa_bf, wo_bf, g_post, g_pre, mod4, S):
    T, D = x2.shape
    tm = _tile(S, 512)
    per_b = S // tm
    gac, gbc = 5, 6
    row = lambda: pl.BlockSpec((tm, D), lambda i: (i, 0))
    wsp = lambda: pl.BlockSpec((D, D), lambda i: (0, 0))
    vec = lambda: pl.BlockSpec((1, D), lambda i: (0, 0))
    modv = lambda j: pl.BlockSpec((None, None, 1, D), lambda i: (i // per_b, j, 0, 0))
    return pl.pallas_call(
        _mixout_kernel,
        grid=(T // tm,),
        in_specs=[row(), row(), row(),
                  pl.BlockSpec((tm, D), lambda i: (i, gac)),
                  pl.BlockSpec((tm, D), lambda i: (i, gbc)),
                  wsp(), wsp(), wsp(), vec(), modv(2), vec(), modv(3), modv(4)],
        out_specs=[row(), row(), pl.BlockSpec((tm * SLAB, LANES), lambda i: (i, 0))],
        out_shape=[jax.ShapeDtypeStruct((T, D), F32), jax.ShapeDtypeStruct((T, D), F32),
                   jax.ShapeDtypeStruct((T * SLAB, LANES), jnp.uint32)],
        compiler_params=_cparams(("parallel",)),
        name="mixout",
    )(x2, yr, ao, proj, proj, wr_bf, wa_bf, wo_bf, g_post.reshape(1, D), mod4,
      g_pre.reshape(1, D), mod4, mod4)


def _first_argmax(vals, ids, sentinel):
    m = jnp.max(vals, axis=0, keepdims=True)
    idx = jnp.min(jnp.where(vals == m, ids, sentinel), axis=0, keepdims=True)
    return m, idx


def _router_kernel(h_ref, wr_ref, eb_ref, e_ref, w_ref, r_ref, cnt_ref, cnt_scr, *, tm):
    step = pl.program_id(0)

    @pl.when(step == 0)
    def _():
        cnt_scr[...] = jnp.zeros_like(cnt_scr)

    def split(x):
        hi = x.astype(BF16)
        return hi, (x - hi.astype(F32)).astype(BF16)

    nt = (((1,), (1,)), ((), ()))
    w_hi, w_lo = split(wr_ref[...])
    h_hi, h_lo = split(h_ref[...])
    logits = (lax.dot_general(w_hi, h_hi, nt, preferred_element_type=F32)
              + lax.dot_general(w_hi, h_lo, nt, preferred_element_type=F32)
              + lax.dot_general(w_lo, h_hi, nt, preferred_element_type=F32))
    scores = jax.nn.sigmoid(logits)
    choice = scores + eb_ref[...]
    i8 = lax.broadcasted_iota(jnp.int32, (GROUP_SIZE, tm), 0)
    neg_inf = jnp.float32(-jnp.inf)

    slabs = [choice[g * GROUP_SIZE:(g + 1) * GROUP_SIZE, :] for g in range(N_GROUPS)]
    sc_slabs = [scores[g * GROUP_SIZE:(g + 1) * GROUP_SIZE, :] for g in range(N_GROUPS)]

    gs = jnp.zeros((N_GROUPS, tm), F32)
    for g in range(N_GROUPS):
        m1, idx1 = _first_argmax(slabs[g], i8, GROUP_SIZE)
        m2 = jnp.max(jnp.where(i8 == idx1, neg_inf, slabs[g]), axis=0, keepdims=True)
        gs = jnp.where(i8 == g, m1 + m2, gs)

    sel = jnp.zeros((N_GROUPS, tm), jnp.int32)
    cur = gs
    for _ in range(TOPK_GROUPS):
        _, idx = _first_argmax(cur, i8, N_GROUPS)
        hit = i8 == idx
        sel = jnp.where(hit, 1, sel)
        cur = jnp.where(hit, neg_inf, cur)

    masked = [jnp.where(sel[g:g + 1, :] > 0, slabs[g], neg_inf) for g in range(N_GROUPS)]
    ids = [i8 + g * GROUP_SIZE for g in range(N_GROUPS)]
    onehot = [jnp.zeros((GROUP_SIZE, tm), F32) for _ in range(N_GROUPS)]
    picks = []
    wts = []
    for _ in range(TOP_K):
        m = jnp.max(functools.reduce(jnp.maximum, masked), axis=0, keepdims=True)
        idx = jnp.min(functools.reduce(
            jnp.minimum, [jnp.where(c == m, i, N_EXPERTS) for c, i in zip(masked, ids)]),
            axis=0, keepdims=True)
        w8 = jnp.zeros((GROUP_SIZE, tm), F32)
        for g in range(N_GROUPS):
            hit = ids[g] == idx
            w8 = w8 + jnp.where(hit, sc_slabs[g], 0.0)
            masked[g] = jnp.where(hit, neg_inf, masked[g])
            onehot[g] = jnp.where(hit, 1.0, onehot[g])
        picks.append(idx)
        wts.append(jnp.sum(w8, axis=0, keepdims=True))

    wsum = functools.reduce(lambda a, b: a + b, wts)
    norm = ROUTED_SCALE / (wsum + 1e-20)

    t_row = lax.broadcasted_iota(jnp.int32, (tm, tm), 0)
    t_col = lax.broadcasted_iota(jnp.int32, (tm, tm), 1)
    before = jnp.where(t_row < t_col, 1.0, 0.0).astype(BF16)
    cum = [jnp.dot(onehot[g].astype(BF16), before, preferred_element_type=F32)
           + cnt_scr[g * GROUP_SIZE:(g + 1) * GROUP_SIZE, :] for g in range(N_GROUPS)]

    for kk in range(TOP_K):
        rank8 = jnp.zeros((GROUP_SIZE, tm), F32)
        for g in range(N_GROUPS):
            rank8 = rank8 + jnp.where(ids[g] == picks[kk], cum[g], 0.0)
        rank = jnp.sum(rank8, axis=0, keepdims=True)
        e_ref[kk:kk + 1, :] = picks[kk]
        w_ref[kk:kk + 1, :] = wts[kk] * norm
        r_ref[kk:kk + 1, :] = rank.astype(jnp.int32)

    for g in range(N_GROUPS):
        rows = slice(g * GROUP_SIZE, (g + 1) * GROUP_SIZE)
        cnt_scr[rows, :] = cnt_scr[rows, :] + jnp.sum(onehot[g], axis=1, keepdims=True)
    cnt_ref[...] = jnp.broadcast_to(cnt_scr[...], cnt_ref.shape)


def _router(h2, w_router_t, e_bias):
    T, D = h2.shape
    tm = _tile(T, 512)
    return pl.pallas_call(
        functools.partial(_router_kernel, tm=tm),
        grid=(T // tm,),
        in_specs=[pl.BlockSpec((tm, D), lambda i: (i, 0)),
                  pl.BlockSpec((N_EXPERTS, D), lambda i: (0, 0)),
                  pl.BlockSpec((N_EXPERTS, 1), lambda i: (0, 0))],
        out_specs=[pl.BlockSpec((TOP_K, tm), lambda i: (0, i)),
                   pl.BlockSpec((TOP_K, tm), lambda i: (0, i)),
                   pl.BlockSpec((TOP_K, tm), lambda i: (0, i)),
                   pl.BlockSpec((N_EXPERTS, LANES), lambda i: (0, 0))],
        out_shape=[jax.ShapeDtypeStruct((TOP_K, T), jnp.int32),
                   jax.ShapeDtypeStruct((TOP_K, T), F32),
                   jax.ShapeDtypeStruct((TOP_K, T), jnp.int32),
                   jax.ShapeDtypeStruct((N_EXPERTS, LANES), F32)],
        scratch_shapes=[pltpu.VMEM((N_EXPERTS, 1), F32)],
        compiler_params=_cparams(("arbitrary",)),
        name="router",
    )(h2, w_router_t, e_bias.reshape(N_EXPERTS, 1))


def _slots_kernel(offs_ref, e_ref, r_ref, d_ref):
    e = e_ref[...]
    d = r_ref[...]
    for ex in range(N_EXPERTS):
        d = d + jnp.where(e == ex, offs_ref[ex], 0)
    d_ref[...] = d


def _slots(offs, top_e, rank):
    K, T = top_e.shape
    tm = _tile(T, 4096)
    grid_spec = pltpu.PrefetchScalarGridSpec(
        num_scalar_prefetch=1,
        grid=(T // tm,),
        in_specs=[pl.BlockSpec((K, tm), lambda i, o: (0, i)),
                  pl.BlockSpec((K, tm), lambda i, o: (0, i))],
        out_specs=pl.BlockSpec((K, tm), lambda i, o: (0, i)),
    )
    return pl.pallas_call(
        _slots_kernel,
        grid_spec=grid_spec,
        out_shape=jax.ShapeDtypeStruct((K, T), jnp.int32),
        compiler_params=_cparams(("parallel",)),
        name="slots",
    )(offs, top_e, rank)


def _row_copy(src, src_row, dst, dst_row, sem):
    s0 = pl.multiple_of(src_row * SLAB, SLAB)
    d0 = pl.multiple_of(dst_row * SLAB, SLAB)
    return pltpu.make_async_copy(src.at[pl.ds(s0, SLAB), :], dst.at[pl.ds(d0, SLAB), :], sem)


def _dispatch_kernel(dest_ref, h_ref, xs_ref, sem, *, td):
    def issue(j, carry):
        for kk in range(TOP_K):
            _row_copy(h_ref, j, xs_ref, dest_ref[0, kk * td + j], sem).start(priority=kk % 2)
        return carry

    lax.fori_loop(0, td, issue, 0)
    for _ in range(TOP_K):
        pltpu.make_async_copy(h_ref, xs_ref.at[pl.ds(0, td * SLAB), :], sem).wait()


def _dispatch(h2, dest_tiles, td):
    T = h2.shape[0] // SLAB
    nt = T // td
    return pl.pallas_call(
        functools.partial(_dispatch_kernel, td=td),
        grid=(nt,),
        in_specs=[pl.BlockSpec((None, 1, TOP_K * td), lambda i: (i, 0, 0),
                               memory_space=pltpu.SMEM),
                  pl.BlockSpec((td * SLAB, LANES), lambda i: (i, 0))],
        out_specs=pl.BlockSpec(memory_space=pl.ANY),
        out_shape=jax.ShapeDtypeStruct((T * TOP_K * SLAB, LANES), h2.dtype),
        scratch_shapes=[pltpu.SemaphoreType.DMA],
        compiler_params=_cparams(("arbitrary",)),
        name="dispatch",
    )(dest_tiles, h2)


def _gmm_kernel(blk_ref, exp_ref, lo_ref, hi_ref, first_ref, newe_ref, x_ref, w1_ref, w3_ref,
                w2_ref, o_ref, w1b, w3b, w2b, *, bm):
    it = pl.program_id(0)

    @pl.when(newe_ref[it] == 1)
    def _():
        w1b[...] = w1_ref[...].astype(BF16)
        w3b[...] = w3_ref[...].astype(BF16)
        w2b[...] = w2_ref[...].astype(BF16)

    rows = lax.broadcasted_iota(jnp.int32, (bm, 1), 0)
    valid = jnp.logical_and(rows >= lo_ref[it], rows < hi_ref[it])
    x_lo, x_hi = _unpack_rows(jnp.where(valid, _load_slabs(x_ref, bm), jnp.uint32(0)))
    x_lo = x_lo.astype(BF16)
    x_hi = x_hi.astype(BF16)
    half = x_lo.shape[1]
    h1 = (jnp.dot(x_lo, w1b[:half, :], preferred_element_type=F32)
          + jnp.dot(x_hi, w1b[half:, :], preferred_element_type=F32))
    h3 = (jnp.dot(x_lo, w3b[:half, :], preferred_element_type=F32)
          + jnp.dot(x_hi, w3b[half:, :], preferred_element_type=F32))
    hb = (h1 * jax.nn.sigmoid(h1) * h3).astype(BF16)
    y = jnp.dot(hb, w2b[...], preferred_element_type=F32)

    @pl.when(first_ref[it] == 1)
    def _():
        _store_slabs(o_ref, _pack_rows(y))

    @pl.when(first_ref[it] == 0)
    def _():
        o_lo, o_hi = _unpack_rows(_load_slabs(o_ref, bm))
        _store_slabs(o_ref, _pack_rows(y + jnp.concatenate([o_lo, o_hi], axis=1)))


def _gmm(xs, items, w1, w3, w2, bm):
    A = xs.shape[0] // SLAB
    D = w1.shape[1]
    F = w1.shape[2]
    n_items = items[0].shape[0]
    grid_spec = pltpu.PrefetchScalarGridSpec(
        num_scalar_prefetch=6,
        grid=(n_items,),
        in_specs=[pl.BlockSpec((bm * SLAB, LANES), lambda i, blk, ex, *_: (blk[i], 0)),
                  pl.BlockSpec((None, D, F), lambda i, blk, ex, *_: (ex[i], 0, 0)),
                  pl.BlockSpec((None, D, F), lambda i, blk, ex, *_: (ex[i], 0, 0)),
                  pl.BlockSpec((None, F, D), lambda i, blk, ex, *_: (ex[i], 0, 0))],
        out_specs=pl.BlockSpec((bm * SLAB, LANES), lambda i, blk, ex, *_: (blk[i], 0)),
        scratch_shapes=[pltpu.VMEM((D, F), BF16), pltpu.VMEM((D, F), BF16),
                        pltpu.VMEM((F, D), BF16)],
    )
    return pl.pallas_call(
        functools.partial(_gmm_kernel, bm=bm),
        grid_spec=grid_spec,
        out_shape=jax.ShapeDtypeStruct((A * SLAB, LANES), jnp.uint32),
        compiler_params=_cparams(("arbitrary",)),
        name="gmm",
    )(*items, xs, w1, w3, w2)


def _work_items(counts, bm, n_blocks):
    n_items = n_blocks + N_EXPERTS - 1
    ends = jnp.cumsum(counts)
    starts = ends - counts
    nb = jnp.where(counts > 0, (ends - 1) // bm - starts // bm + 1, 0)
    item_end = jnp.cumsum(nb)
    item_start = item_end - nb
    n_real = item_end[-1]
    i = jnp.arange(n_items, dtype=jnp.int32)
    e = jnp.minimum(jnp.sum(item_end[None, :] <= i[:, None], axis=1), N_EXPERTS - 1).astype(jnp.int32)
    onehot = e[:, None] == jnp.arange(N_EXPERTS, dtype=jnp.int32)[None, :]
    pick = lambda v: jnp.sum(jnp.where(onehot, v[None, :], 0), axis=1)
    start_e, end_e = pick(starts), pick(ends)
    blk = start_e // bm + (i - pick(item_start))
    lo = jnp.clip(start_e - blk * bm, 0, bm)
    hi = jnp.clip(end_e - blk * bm, 0, bm)
    real = i < n_real
    blk = jnp.where(real, blk, n_blocks - 1).astype(jnp.int32)
    lo = jnp.where(real, lo, 0).astype(jnp.int32)
    hi = jnp.where(real, hi, 0).astype(jnp.int32)
    one = jnp.ones((1,), jnp.int32)
    first = jnp.concatenate([one, (blk[1:] != blk[:-1]).astype(jnp.int32)])
    new_e = jnp.concatenate([one, (e[1:] != e[:-1]).astype(jnp.int32)])
    return blk, e, lo, hi, first, new_e


def _combine_kernel(dest_ref, nxt_ref, ys_ref, w_ref, x1_ref, h_ref, w1_ref, w3_ref, w2_ref,
                    gt2_ref, g_ref, o_ref, buf, y_scr, sems, *, tc):
    i = pl.program_id(0)
    n = pl.num_programs(0)
    slot = i % 2

    def gather(idx_ref, s):
        def issue(j, carry):
            for kk in range(TOP_K):
                _row_copy(ys_ref, idx_ref[0, kk * tc + j], buf.at[s, kk], j,
                          sems.at[s]).start(priority=kk % 2)
            return carry

        lax.fori_loop(0, tc, issue, 0, unroll=4)

    @pl.when(i == 0)
    def _():
        gather(dest_ref, 0)

    for s in range(2):
        @pl.when(jnp.logical_and(i + 1 < n, slot == 1 - s))
        def _():
            gather(nxt_ref, s)

    h_lo, h_hi = _unpack_rows(_load_slabs(h_ref, tc))
    h_lo = h_lo.astype(BF16)
    h_hi = h_hi.astype(BF16)
    half = h_lo.shape[1]
    h1 = (jnp.dot(h_lo, w1_ref[:half, :], preferred_element_type=F32)
          + jnp.dot(h_hi, w1_ref[half:, :], preferred_element_type=F32))
    h3 = (jnp.dot(h_lo, w3_ref[:half, :], preferred_element_type=F32)
          + jnp.dot(h_hi, w3_ref[half:, :], preferred_element_type=F32))
    y_scr[...] = jnp.dot((h1 * jax.nn.sigmoid(h1) * h3).astype(BF16), w2_ref[...],
                         preferred_element_type=F32)

    for kk in range(TOP_K):
        pltpu.make_async_copy(ys_ref.at[pl.ds(0, tc * SLAB), :], buf.at[slot, kk],
                              sems.at[slot]).wait()
    rows = 2 * SUBLANES
    for r0 in range(0, tc, rows):
        for kk in range(TOP_K):
            u = jnp.concatenate([buf[slot, kk, pl.ds(r0 * SLAB + c, rows, stride=SLAB), :]
                                 for c in range(SLAB)], axis=1)
            e_lo, e_hi = _unpack_rows(u)
            wk = w_ref[r0:r0 + rows, kk:kk + 1]
            moe_lo = wk * e_lo if kk == 0 else moe_lo + wk * e_lo
            moe_hi = wk * e_hi if kk == 0 else moe_hi + wk * e_hi
        y_scr[r0:r0 + rows, :] += jnp.concatenate([moe_lo, moe_hi], axis=1)
    o_ref[...] = x1_ref[...] + gt2_ref[...] * _rms(y_scr[...], g_ref[...])


def _combine(ys, dest_tiles, w_tok, x1, h2, w1s, w3s, w2s, mod4, g_post, S, tc):
    T, D = x1.shape
    F = w1s.shape[1]
    per_b = S // tc
    nt = T // tc
    row = lambda: pl.BlockSpec((tc, D), lambda i: (i, 0))
    return pl.pallas_call(
        functools.partial(_combine_kernel, tc=tc),
        grid=(nt,),
        in_specs=[pl.BlockSpec((None, 1, TOP_K * tc), lambda i: (i, 0, 0),
                               memory_space=pltpu.SMEM),
                  pl.BlockSpec((None, 1, TOP_K * tc), lambda i: (jnp.minimum(i + 1, nt - 1), 0, 0),
                               memory_space=pltpu.SMEM),
                  pl.BlockSpec(memory_space=pl.ANY),
                  pl.BlockSpec((tc, TOP_K), lambda i: (i, 0)),
                  row(), pl.BlockSpec((tc * SLAB, LANES), lambda i: (i, 0)),
                  pl.BlockSpec((D, F), lambda i: (0, 0)),
                  pl.BlockSpec((D, F), lambda i: (0, 0)),
                  pl.BlockSpec((F, D), lambda i: (0, 0)),
                  pl.BlockSpec((None, None, 1, D), lambda i: (i // per_b, 5, 0, 0)),
                  pl.BlockSpec((1, D), lambda i: (0, 0))],
        out_specs=row(),
        out_shape=jax.ShapeDtypeStruct((T, D), F32),
        scratch_shapes=[pltpu.VMEM((2, TOP_K, tc * SLAB, LANES), jnp.uint32),
                        pltpu.VMEM((tc, D), F32), pltpu.SemaphoreType.DMA((2,))],
        compiler_params=_cparams(("arbitrary",)),
        name="combine",
    )(dest_tiles, dest_tiles, ys, w_tok, x1, h2, w1s, w3s, w2s, mod4, g_post.reshape(1, D))


def _lambda_init(layer):
    return 0.8 - 0.6 * math.exp(-0.3 * layer)


def kernel(x, c, w_ada, b_ada, g_pre_mix, w_in, conv_w, conv_b, lru_wa, lru_ba, lru_wx, lru_bx,
           lru_lambda, lam_q1, lam_k1, lam_q2, lam_k2, g_subln, w_proj_rnn, w_proj_att, w_out,
           g_post_mix, g_pre_ffn, w_router, e_bias, w1_e, w3_e, w2_e, w1_s, w3_s, w2_s,
           g_post_ffn):
    B, S, D = x.shape
    T = B * S
    depth = w_ada.shape[0]
    slopes = np.exp2(-8.0 * np.arange(1, N_HEADS + 1, dtype=np.float32) / N_HEADS)
    kbias = _alibi_tables(slopes, S)
    tt = _tile(S, 256)
    bm = _tile(T * TOP_K, 512)
    n_blocks = T * TOP_K // bm

    x2 = x.reshape(T, D)
    for l in range(depth):
        lam_init = _lambda_init(l)
        mod4 = _ada(c, w_ada[l], b_ada[l]).reshape(B, 6, 1, D)

        proj = _inproj(x2, g_pre_mix[l], mod4, w_in[l].astype(BF16), S)
        yr = _lru(proj, conv_w[l], conv_b[l], lru_wa[l].astype(BF16), lru_ba[l],
                  lru_wx[l].astype(BF16), lru_bx[l], lru_lambda[l], B, S)
        lamv = jnp.stack([lam_q1[l], lam_k1[l], lam_q2[l], lam_k2[l]])
        ao = _attn(proj, kbias, lamv, g_subln[l], B, S, lam_init)
        x1, h2, h2p = _mixout(x2, yr, ao, proj, w_proj_rnn[l].astype(BF16),
                         w_proj_att[l].astype(BF16), w_out[l].astype(BF16),
                         g_post_mix[l], g_pre_ffn[l], mod4, S)

        top_e, top_w, rank, cnt = _router(h2, w_router[l].T, e_bias[l])
        counts = cnt[:, 0].astype(jnp.int32)
        offs = jnp.cumsum(counts) - counts
        dest = _slots(offs, top_e, rank)
        dest_tiles = dest.reshape(TOP_K, T // tt, tt).transpose(1, 0, 2).reshape(T // tt, 1,
                                                                                  TOP_K * tt)
        xs = _dispatch(h2p, dest_tiles, tt)
        items = _work_items(counts, bm, n_blocks)
        ys = _gmm(xs, items, w1_e[l], w3_e[l], w2_e[l], bm)
        x2 = _combine(ys, dest_tiles, top_w.T, x1, h2p, w1_s[l].astype(BF16),
                      w3_s[l].astype(BF16), w2_s[l].astype(BF16), mod4, g_post_ffn[l], S, tt)
    return x2.reshape(B, S, D)
```

```python
import functools
import math

import jax
import jax.numpy as jnp
import numpy as np
from jax import lax
from jax.experimental import pallas as pl
from jax.experimental.pallas import tpu as pltpu

F32 = jnp.float32
BF16 = jnp.bfloat16

EPS = 1e-6
N_HEADS = 8
HEAD_DIM = 64
V_DIM = 2 * HEAD_DIM
LRU_BLOCKS = 8
CONV_W = 4
LRU_C = 8.0
N_EXPERTS = 64
TOP_K = 8
N_GROUPS = 8
GROUP_SIZE = N_EXPERTS // N_GROUPS
TOPK_GROUPS = 4
ROUTED_SCALE = 2.5

LANES = 128
SUBLANES = 8
VMEM_LIMIT = 48 * 1024 * 1024


def _cparams(sem):
    return pltpu.CompilerParams(dimension_semantics=sem, vmem_limit_bytes=VMEM_LIMIT)


def _tile(n, pref):
    t = min(n, pref)
    while n % t:
        t //= 2
    return t


def _rms(x, g):
    return x * lax.rsqrt(jnp.mean(x * x, axis=-1, keepdims=True) + EPS) * g


def _ada_kernel(c_ref, w_ref, b_ref, o_ref):
    c = c_ref[...]
    cond = c * jax.nn.sigmoid(c)
    o_ref[...] = jnp.dot(cond, w_ref[...], preferred_element_type=F32) + b_ref[...]


def _ada(c, w, b):
    B, D = c.shape
    N = w.shape[1]
    tn = _tile(N, 1024)
    return pl.pallas_call(
        _ada_kernel,
        grid=(N // tn,),
        in_specs=[pl.BlockSpec((B, D), lambda j: (0, 0)),
                  pl.BlockSpec((D, tn), lambda j: (0, j)),
                  pl.BlockSpec((1, tn), lambda j: (0, j))],
        out_specs=pl.BlockSpec((B, tn), lambda j: (0, j)),
        out_shape=jax.ShapeDtypeStruct((B, N), F32),
        compiler_params=_cparams(("parallel",)),
        name="ada",
    )(c, w, b.reshape(1, N))


LOG2E = 1.4426950408889634
Q_COL_BLOCK = 2
Q_PRESCALE = HEAD_DIM ** -0.5 * LOG2E


def _inproj_kernel(x_ref, g_ref, sh_ref, sc_ref, w_ref, o_ref, h_scr):
    @pl.when(pl.program_id(1) == 0)
    def _():
        h = _rms(x_ref[...], g_ref[...]) * (1.0 + sc_ref[...]) + sh_ref[...]
        h_scr[...] = h.astype(BF16)

    r = jnp.dot(h_scr[...], w_ref[...], preferred_element_type=F32)
    r = r * jnp.where(pl.program_id(1) == Q_COL_BLOCK, Q_PRESCALE, 1.0)
    o_ref[...] = r.astype(o_ref.dtype)


def _inproj(x2, g, mod4, w_bf, S):
    T, D = x2.shape
    N = w_bf.shape[1]
    tm = _tile(S, 2048)
    tn = 1024
    per_b = S // tm
    return pl.pallas_call(
        _inproj_kernel,
        grid=(T // tm, N // tn),
        in_specs=[pl.BlockSpec((tm, D), lambda i, j: (i, 0)),
                  pl.BlockSpec((1, D), lambda i, j: (0, 0)),
                  pl.BlockSpec((None, None, 1, D), lambda i, j: (i // per_b, 0, 0, 0)),
                  pl.BlockSpec((None, None, 1, D), lambda i, j: (i // per_b, 1, 0, 0)),
                  pl.BlockSpec((D, tn), lambda i, j: (0, j))],
        out_specs=pl.BlockSpec((tm, tn), lambda i, j: (i, j)),
        out_shape=jax.ShapeDtypeStruct((T, N), BF16),
        scratch_shapes=[pltpu.VMEM((tm, D), BF16)],
        compiler_params=_cparams(("parallel", "arbitrary")),
        name="inproj",
    )(x2, g.reshape(1, D), mod4, mod4, w_bf)


def _lru_kernel(xr_ref, gr_ref, cw_ref, cb_ref, wa_ref, ba_ref, wx_ref, bx_ref, lam_ref,
                o_ref, xc_scr, prev_scr, h_scr, *, ts):
    s = pl.program_id(1)

    @pl.when(s == 0)
    def _():
        prev_scr[...] = jnp.zeros_like(prev_scr)
        h_scr[...] = jnp.zeros_like(h_scr)

    x = xr_ref[...].astype(F32)
    prev = prev_scr[...]
    row8 = lax.broadcasted_iota(jnp.int32, (SUBLANES, 1), 0)
    acc = cb_ref[...] + cw_ref[CONV_W - 1:CONV_W, :] * x
    xc_scr[...] = acc
    top = cb_ref[...] + cw_ref[CONV_W - 1:CONV_W, :] * x[0:SUBLANES, :]
    for j in range(1, CONV_W):
        wj = cw_ref[CONV_W - 1 - j:CONV_W - j, :]
        rj = pltpu.roll(x, j, axis=0)
        xc_scr[...] += wj * rj
        pj = pltpu.roll(prev, j, axis=0)
        top += wj * jnp.where(row8 < j, pj, rj[0:SUBLANES, :])
    xc_scr[0:SUBLANES, :] = top
    prev_scr[...] = x[ts - SUBLANES:ts, :]

    row = lax.broadcasted_iota(jnp.int32, (ts, 1), 0)
    is_first = jnp.logical_and(row == 0, s == 0)
    sub = lax.broadcasted_iota(jnp.int32, (1, SUBLANES, 1), 1)
    for n in range(LRU_BLOCKS):
        cols = slice(n * LANES, (n + 1) * LANES)
        xc = xc_scr[:, cols]
        xb = xc.astype(BF16)
        r = jax.nn.sigmoid(jnp.dot(xb, wa_ref[n], preferred_element_type=F32) + ba_ref[:, cols])
        i = jax.nn.sigmoid(jnp.dot(xb, wx_ref[n], preferred_element_type=F32) + bx_ref[:, cols])
        lam = lam_ref[:, cols]
        softplus_neg = jnp.maximum(-lam, 0.0) + jnp.log1p(jnp.exp(-jnp.abs(lam)))
        log_a = (-LRU_C * softplus_neg) * r
        a = jnp.exp(log_a)
        m2 = 1.0 - a * a
        mult = jnp.where(m2 > 0.0, m2 * lax.rsqrt(m2), 0.0)
        mult = jnp.where(is_first, 1.0, mult)
        u = mult * (i * xc)
        a = a.reshape(ts // SUBLANES, SUBLANES, LANES)
        u = u.reshape(ts // SUBLANES, SUBLANES, LANES)
        for d in (1, 2, 4):
            keep = sub >= d
            a_sh = jnp.where(keep, pltpu.roll(a, d, axis=1), 1.0)
            u_sh = jnp.where(keep, pltpu.roll(u, d, axis=1), 0.0)
            u = u + a * u_sh
            a = a * a_sh
        a = a.reshape(ts, LANES)
        u = u.reshape(ts, LANES)
        gate = jax.nn.gelu(gr_ref[:, cols].astype(F32))
        carry = h_scr[:, cols]
        step = 2 * SUBLANES
        for g in range(ts // step):
            r0 = g * step
            h0 = u[r0:r0 + SUBLANES, :] + a[r0:r0 + SUBLANES, :] * carry
            carry = h0[SUBLANES - 1:SUBLANES, :]
            h1 = u[r0 + SUBLANES:r0 + step, :] + a[r0 + SUBLANES:r0 + step, :] * carry
            carry = h1[SUBLANES - 1:SUBLANES, :]
            hg = jnp.concatenate([h0, h1], axis=0) * gate[r0:r0 + step, :]
            o_ref[r0:r0 + step, cols] = hg.astype(o_ref.dtype)
        h_scr[:, cols] = carry


def _lru(proj, conv_w, conv_b, wa_bf, ba, wx_bf, bx, lam, B, S):
    C = conv_w.shape[1]
    ts = _tile(S, 256)
    ns = S // ts
    vec = lambda: pl.BlockSpec((1, C), lambda b, s: (0, 0))
    blk = lambda: pl.BlockSpec((LRU_BLOCKS, LANES, LANES), lambda b, s: (0, 0, 0))
    return pl.pallas_call(
        functools.partial(_lru_kernel, ts=ts),
        grid=(B, ns),
        in_specs=[pl.BlockSpec((ts, C), lambda b, s: (b * ns + s, 0)),
                  pl.BlockSpec((ts, C), lambda b, s: (b * ns + s, 1)),
                  pl.BlockSpec((CONV_W, C), lambda b, s: (0, 0)),
                  vec(), blk(), vec(), blk(), vec(), vec()],
        out_specs=pl.BlockSpec((ts, C), lambda b, s: (b * ns + s, 0)),
        out_shape=jax.ShapeDtypeStruct((B * S, C), BF16),
        scratch_shapes=[pltpu.VMEM((ts, C), F32), pltpu.VMEM((SUBLANES, C), F32),
                        pltpu.VMEM((1, C), F32)],
        compiler_params=_cparams(("parallel", "arbitrary")),
        name="lru",
    )(proj, proj, conv_w, conv_b.reshape(1, C), wa_bf, ba.reshape(1, C), wx_bf, bx.reshape(1, C),
      lam.reshape(1, C))


BIAS_LANES = 3
ACC_ROWS = V_DIM + 2 * SUBLANES


def _alibi_tables(slopes, S):
    def top16(x):
        return (x.view(np.uint32) & np.uint32(0xFFFF0000)).view(np.float32)

    pos = np.arange(S, dtype=np.float32)
    b = (slopes.astype(np.float32) * np.float32(LOG2E))[:, None] * pos[None, :]
    hi = top16(b)
    mid = top16(b - hi)
    lo = top16(b - hi - mid)
    half = np.zeros(b.shape + (HEAD_DIM,), np.float32)
    half[..., 0], half[..., 1], half[..., 2] = hi, mid, lo
    zero = np.zeros_like(half)
    table = np.stack([np.concatenate([zero, half], axis=-1),
                      np.concatenate([half, zero], axis=-1)], axis=1)
    return jnp.asarray(table, dtype=BF16)


def _lane_masks():
    lane = lax.broadcasted_iota(jnp.int32, (1, V_DIM), 1)
    own = (jnp.where(lane < HEAD_DIM, 1.0, 0.0), jnp.where(lane >= HEAD_DIM, 1.0, 0.0))
    ones_row = (jnp.where(jnp.logical_and(lane >= HEAD_DIM, lane < HEAD_DIM + BIAS_LANES), 1.0, 0.0),
                jnp.where(lane < BIAS_LANES, 1.0, 0.0))
    return own, ones_row


def _attn_kernel(lamv_ref, gsub_ref, q_ref, k_ref, v_ref, kb_ref, o_ref,
                 vt_scr, kx_scr, *scratch, tq, lam_init):
    tk = tq
    own, _ = _lane_masks()
    for j in range(vt_scr.shape[0]):
        rows = slice(j * tk, (j + 1) * tk)
        vt_scr[j, 0:V_DIM, :] = v_ref[rows, :].astype(F32).T.astype(BF16)
        vt_scr[j, V_DIM:ACC_ROWS, :] = jnp.ones((ACC_ROWS - V_DIM, tk), BF16)
        kf = k_ref[rows, :].astype(F32)
        for mp in range(2):
            kx_scr[mp, rows, :] = (kf * own[mp] + kb_ref[mp, rows, :].astype(F32)).astype(BF16)

    def q_block(qi, carry):
        _attn_q_block(qi, lamv_ref, gsub_ref, q_ref, o_ref, vt_scr, kx_scr, *scratch,
                      tq=tq, lam_init=lam_init)
        return carry

    lax.fori_loop(0, q_ref.shape[0] // tq, q_block, 0)


def _attn_q_block(qi, lamv_ref, gsub_ref, q_ref, o_ref,
                  vt_scr, kx_scr, qx_scr, s_scr, p_scr, a_scr, m_scr, acc_scr, *, tq, lam_init):
    tk = tq
    q_rows = pl.ds(pl.multiple_of(qi * tq, tq), tq)
    own, ones_row = _lane_masks()
    q = q_ref[q_rows, :].astype(F32)
    for mp in range(2):
        qx_scr[mp] = (q * own[mp] + ones_row[mp]).astype(BF16)
    m_scr[...] = jnp.full_like(m_scr, -jnp.inf)
    acc_scr[...] = jnp.zeros_like(acc_scr)
    row8 = lax.broadcasted_iota(jnp.int32, (SUBLANES, LANES), 0)
    col = lax.broadcasted_iota(jnp.int32, (1, LANES), 1)
    neg_inf = jnp.float32(-jnp.inf)

    def tree(op, parts):
        parts = [p for p in parts if p is not None]
        while len(parts) > 1:
            parts = [op(parts[i], parts[i + 1]) if i + 1 < len(parts) else parts[i]
                     for i in range(0, len(parts), 2)]
        return parts[0]

    def scores(j, par):
        start = pl.multiple_of(j * tk, tk)
        for mp in range(2):
            s = lax.dot_general(kx_scr[mp, pl.ds(start, tk), :], qx_scr[mp],
                                (((1,), (1,)), ((), ())), preferred_element_type=F32)
            for c in range(tq // LANES):
                s_scr[par, mp, c] = s[:, c * LANES:(c + 1) * LANES]

    def softmax(par, masked):
        for mp in range(2):
            for c in range(tq // LANES):
                cols = slice(c * LANES, (c + 1) * LANES)
                lo_col, hi_col = c * LANES, (c + 1) * LANES - 1
                s_c = s_scr.at[par, mp, c]
                p_c = p_scr.at[par, mp, c]
                accs = [None] * 4
                for i in range(tk // SUBLANES):
                    r0 = i * SUBLANES
                    if masked and r0 > hi_col:
                        continue
                    t = s_c[r0:r0 + SUBLANES, :]
                    if masked and r0 + SUBLANES - 1 > lo_col:
                        t = jnp.where(row8 + r0 <= col + lo_col, t, neg_inf)
                        s_c[r0:r0 + SUBLANES, :] = t
                    accs[i % 4] = t if accs[i % 4] is None else jnp.maximum(accs[i % 4], t)
                mx = jnp.max(tree(jnp.maximum, accs), axis=0, keepdims=True)
                m_old = m_scr[mp, :, cols]
                m_new = jnp.maximum(m_old, mx)
                a_scr[par, mp, :, cols] = jnp.exp2(m_old - m_new)
                m_scr[mp, :, cols] = m_new
                for i in range(tk // (2 * SUBLANES)):
                    r0 = i * 2 * SUBLANES
                    if masked and r0 > hi_col:
                        p_c[r0:r0 + 2 * SUBLANES, :] = jnp.zeros((2 * SUBLANES, LANES), BF16)
                        continue
                    p = jnp.exp2(s_c[r0:r0 + 2 * SUBLANES, :] - m_new)
                    p_c[r0:r0 + 2 * SUBLANES, :] = p.astype(BF16)

    def values(j, par):
        for mp in range(2):
            p = jnp.concatenate([p_scr[par, mp, c] for c in range(tq // LANES)], axis=1)
            pv = jnp.dot(vt_scr[j], p, preferred_element_type=F32)
            acc_scr[mp] = a_scr[par, mp] * acc_scr[mp] + pv

    p_scr[1] = jnp.zeros(p_scr.shape[1:], BF16)
    a_scr[1] = jnp.ones(a_scr.shape[1:], F32)
    scores(0, 0)

    def stage(j, par):
        scores(j + 1, 1 - par)
        softmax(par, False)
        values(jnp.maximum(j - 1, 0), 1 - par)

    def stage_pair(i, carry):
        stage(2 * i, 0)
        stage(2 * i + 1, 1)
        return carry

    def tail(par):
        softmax(par, True)
        values(jnp.maximum(qi - 1, 0), 1 - par)
        values(qi, par)

    lax.fori_loop(0, qi // 2, stage_pair, 0)

    @pl.when(qi % 2 == 1)
    def _():
        stage(qi - 1, 0)
        tail(1)

    @pl.when(qi % 2 == 0)
    def _():
        tail(0)

    lv = lamv_ref[...]
    lam = (jnp.exp(jnp.sum(lv[0:1, :] * lv[1:2, :], axis=-1, keepdims=True))
           - jnp.exp(jnp.sum(lv[2:3, :] * lv[3:4, :], axis=-1, keepdims=True)) + lam_init)
    o0 = acc_scr[0, 0:V_DIM, :] * (1.0 / acc_scr[0, V_DIM:V_DIM + 1, :])
    o1 = acc_scr[1, 0:V_DIM, :] * (1.0 / acc_scr[1, V_DIM:V_DIM + 1, :])
    o_t = o0 - lam * o1
    o_t = o_t * lax.rsqrt(jnp.mean(o_t * o_t, axis=0, keepdims=True) + EPS) * gsub_ref[...]
    o_ref[q_rows, :] = (o_t * (1.0 - lam_init)).T.astype(o_ref.dtype)


def _attn(proj, kbias, lamv, g_subln, B, S, lam_init):
    tq = _tile(S, 512)
    nq = S // tq
    qc, kc, vc = 2 * 8, 3 * 8, 4 * 8
    return pl.pallas_call(
        functools.partial(_attn_kernel, tq=tq, lam_init=lam_init),
        grid=(B, N_HEADS),
        in_specs=[pl.BlockSpec((4, HEAD_DIM), lambda b, h: (0, 0)),
                  pl.BlockSpec((V_DIM, 1), lambda b, h: (0, 0)),
                  pl.BlockSpec((S, V_DIM), lambda b, h: (b, qc + h)),
                  pl.BlockSpec((S, V_DIM), lambda b, h: (b, kc + h)),
                  pl.BlockSpec((S, V_DIM), lambda b, h: (b, vc + h)),
                  pl.BlockSpec((None, 2, S, V_DIM), lambda b, h: (h, 0, 0, 0))],
        out_specs=pl.BlockSpec((S, V_DIM), lambda b, h: (b, h)),
        out_shape=jax.ShapeDtypeStruct((B * S, N_HEADS * V_DIM), BF16),
        scratch_shapes=[pltpu.VMEM((nq, ACC_ROWS, tq), BF16), pltpu.VMEM((2, S, V_DIM), BF16),
                        pltpu.VMEM((2, tq, V_DIM), BF16),
                        pltpu.VMEM((2, 2, tq // LANES, tq, LANES), F32),
                        pltpu.VMEM((2, 2, tq // LANES, tq, LANES), BF16),
                        pltpu.VMEM((2, 2, 1, tq), F32), pltpu.VMEM((2, 1, tq), F32),
                        pltpu.VMEM((2, ACC_ROWS, tq), F32)],
        compiler_params=_cparams(("parallel", "parallel")),
        name="attn",
    )(lamv, g_subln.reshape(V_DIM, 1), proj, proj, proj, kbias)


def _pack_rows(x):
    half = x.shape[1] // 2
    lo = lax.bitcast_convert_type(x[:, :half].astype(BF16).astype(F32), jnp.uint32)
    hi = lax.bitcast_convert_type(x[:, half:].astype(BF16).astype(F32), jnp.uint32)
    return (hi & jnp.uint32(0xFFFF0000)) | (lo >> 16)


def _unpack_rows(u):
    lo = lax.bitcast_convert_type(u << 16, F32)
    hi = lax.bitcast_convert_type(u & jnp.uint32(0xFFFF0000), F32)
    return lo, hi


SLAB = 4


def _load_slabs(ref, n):
    return jnp.concatenate([ref[pl.ds(c, n, stride=SLAB), :] for c in range(SLAB)], axis=1)


def _store_slabs(ref, u):
    n = u.shape[0]
    for c in range(SLAB):
        ref[pl.ds(c, n, stride=SLAB), :] = u[:, c * LANES:(c + 1) * LANES]


def _mixout_kernel(x_ref, yr_ref, ao_ref, ga_ref, gb_ref, wr_ref, wa_ref, wo_ref,
                   gpost_ref, gt1_ref, gpre_ref, sh2_ref, sc2_ref, wrt_ref, eb_ref,
                   x1_ref, h2p_ref, e_ref, w_ref, r_ref, cnt_ref, cnt_scr):
    ya = jnp.dot(yr_ref[...], wr_ref[...], preferred_element_type=F32)
    yb = jnp.dot(ao_ref[...], wa_ref[...], preferred_element_type=F32)
    merged = (jax.nn.sigmoid(ga_ref[...].astype(F32)) * ya
              + jax.nn.sigmoid(gb_ref[...].astype(F32)) * yb)
    y = jnp.dot(merged.astype(BF16), wo_ref[...], preferred_element_type=F32)
    x1 = x_ref[...] + gt1_ref[...] * _rms(y, gpost_ref[...])
    x1_ref[...] = x1
    h2 = _rms(x1, gpre_ref[...]) * (1.0 + sc2_ref[...]) + sh2_ref[...]
    _store_slabs(h2p_ref, _pack_rows(h2))
    _route(h2, wrt_ref, eb_ref, e_ref, w_ref, r_ref, cnt_ref, cnt_scr)


def _mixout(x2, yr, ao, proj, wr_bf, wa_bf, wo_bf, g_post, g_pre, mod4, w_router_t, e_bias, S):
    T, D = x2.shape
    tm = _tile(S, 512)
    per_b = S // tm
    gac, gbc = 5, 6
    row = lambda: pl.BlockSpec((tm, D), lambda i: (i, 0))
    wsp = lambda: pl.BlockSpec((D, D), lambda i: (0, 0))
    vec = lambda: pl.BlockSpec((1, D), lambda i: (0, 0))
    modv = lambda j: pl.BlockSpec((None, None, 1, D), lambda i: (i // per_b, j, 0, 0))
    topk = lambda: pl.BlockSpec((TOP_K, tm), lambda i: (0, i))
    return pl.pallas_call(
        _mixout_kernel,
        grid=(T // tm,),
        in_specs=[row(), row(), row(),
                  pl.BlockSpec((tm, D), lambda i: (i, gac)),
                  pl.BlockSpec((tm, D), lambda i: (i, gbc)),
                  wsp(), wsp(), wsp(), vec(), modv(2), vec(), modv(3), modv(4),
                  pl.BlockSpec((N_EXPERTS, D), lambda i: (0, 0)),
                  pl.BlockSpec((N_EXPERTS, 1), lambda i: (0, 0))],
        out_specs=[row(), pl.BlockSpec((tm * SLAB, LANES), lambda i: (i, 0)),
                   topk(), topk(), topk(),
                   pl.BlockSpec((N_EXPERTS, LANES), lambda i: (0, 0))],
        out_shape=[jax.ShapeDtypeStruct((T, D), F32),
                   jax.ShapeDtypeStruct((T * SLAB, LANES), jnp.uint32),
                   jax.ShapeDtypeStruct((TOP_K, T), jnp.int32),
                   jax.ShapeDtypeStruct((TOP_K, T), F32),
                   jax.ShapeDtypeStruct((TOP_K, T), jnp.int32),
                   jax.ShapeDtypeStruct((N_EXPERTS, LANES), F32)],
        scratch_shapes=[pltpu.VMEM((N_EXPERTS, 1), F32)],
        compiler_params=_cparams(("arbitrary",)),
        name="mixout",
    )(x2, yr, ao, proj, proj, wr_bf, wa_bf, wo_bf, g_post.reshape(1, D), mod4,
      g_pre.reshape(1, D), mod4, mod4, w_router_t, e_bias.reshape(N_EXPERTS, 1))


def _first_argmax(vals, ids, sentinel):
    m = jnp.max(vals, axis=0, keepdims=True)
    idx = jnp.min(jnp.where(vals == m, ids, sentinel), axis=0, keepdims=True)
    return m, idx


def _route(h2, wr_ref, eb_ref, e_ref, w_ref, r_ref, cnt_ref, cnt_scr):
    tm = h2.shape[0]
    step = pl.program_id(0)

    @pl.when(step == 0)
    def _():
        cnt_scr[...] = jnp.zeros_like(cnt_scr)

    def split(x):
        hi = x.astype(BF16)
        return hi, (x - hi.astype(F32)).astype(BF16)

    nt = (((1,), (1,)), ((), ()))
    w_hi, w_lo = split(wr_ref[...])
    h_hi, h_lo = split(h2)
    logits = (lax.dot_general(w_hi, h_hi, nt, preferred_element_type=F32)
              + lax.dot_general(w_hi, h_lo, nt, preferred_element_type=F32)
              + lax.dot_general(w_lo, h_hi, nt, preferred_element_type=F32))
    scores = jax.nn.sigmoid(logits)
    choice = scores + eb_ref[...]
    i8 = lax.broadcasted_iota(jnp.int32, (GROUP_SIZE, tm), 0)
    neg_inf = jnp.float32(-jnp.inf)

    slabs = [choice[g * GROUP_SIZE:(g + 1) * GROUP_SIZE, :] for g in range(N_GROUPS)]
    sc_slabs = [scores[g * GROUP_SIZE:(g + 1) * GROUP_SIZE, :] for g in range(N_GROUPS)]

    gs = jnp.zeros((N_GROUPS, tm), F32)
    for g in range(N_GROUPS):
        m1, idx1 = _first_argmax(slabs[g], i8, GROUP_SIZE)
        m2 = jnp.max(jnp.where(i8 == idx1, neg_inf, slabs[g]), axis=0, keepdims=True)
        gs = jnp.where(i8 == g, m1 + m2, gs)

    sel = jnp.zeros((N_GROUPS, tm), jnp.int32)
    cur = gs
    for _ in range(TOPK_GROUPS):
        _, idx = _first_argmax(cur, i8, N_GROUPS)
        hit = i8 == idx
        sel = jnp.where(hit, 1, sel)
        cur = jnp.where(hit, neg_inf, cur)

    masked = [jnp.where(sel[g:g + 1, :] > 0, slabs[g], neg_inf) for g in range(N_GROUPS)]
    ids = [i8 + g * GROUP_SIZE for g in range(N_GROUPS)]
    onehot = [jnp.zeros((GROUP_SIZE, tm), F32) for _ in range(N_GROUPS)]
    picks = []
    wts = []
    for _ in range(TOP_K):
        m = jnp.max(functools.reduce(jnp.maximum, masked), axis=0, keepdims=True)
        idx = jnp.min(functools.reduce(
            jnp.minimum, [jnp.where(c == m, i, N_EXPERTS) for c, i in zip(masked, ids)]),
            axis=0, keepdims=True)
        w8 = jnp.zeros((GROUP_SIZE, tm), F32)
        for g in range(N_GROUPS):
            hit = ids[g] == idx
            w8 = w8 + jnp.where(hit, sc_slabs[g], 0.0)
            masked[g] = jnp.where(hit, neg_inf, masked[g])
            onehot[g] = jnp.where(hit, 1.0, onehot[g])
        picks.append(idx)
        wts.append(jnp.sum(w8, axis=0, keepdims=True))

    wsum = functools.reduce(lambda a, b: a + b, wts)
    norm = ROUTED_SCALE / (wsum + 1e-20)

    t_row = lax.broadcasted_iota(jnp.int32, (tm, tm), 0)
    t_col = lax.broadcasted_iota(jnp.int32, (tm, tm), 1)
    before = jnp.where(t_row < t_col, 1.0, 0.0).astype(BF16)
    cum = [jnp.dot(onehot[g].astype(BF16), before, preferred_element_type=F32)
           + cnt_scr[g * GROUP_SIZE:(g + 1) * GROUP_SIZE, :] for g in range(N_GROUPS)]

    for kk in range(TOP_K):
        rank8 = jnp.zeros((GROUP_SIZE, tm), F32)
        for g in range(N_GROUPS):
            rank8 = rank8 + jnp.where(ids[g] == picks[kk], cum[g], 0.0)
        rank = jnp.sum(rank8, axis=0, keepdims=True)
        e_ref[kk:kk + 1, :] = picks[kk]
        w_ref[kk:kk + 1, :] = wts[kk] * norm
        r_ref[kk:kk + 1, :] = rank.astype(jnp.int32)

    for g in range(N_GROUPS):
        rows = slice(g * GROUP_SIZE, (g + 1) * GROUP_SIZE)
        cnt_scr[rows, :] = cnt_scr[rows, :] + jnp.sum(onehot[g], axis=1, keepdims=True)
    cnt_ref[...] = jnp.broadcast_to(cnt_scr[...], cnt_ref.shape)


def _slots_kernel(offs_ref, e_ref, r_ref, d_ref):
    e = e_ref[...]
    d = r_ref[...]
    for ex in range(N_EXPERTS):
        d = d + jnp.where(e == ex, offs_ref[ex], 0)
    d_ref[...] = d


def _slots(offs, top_e, rank):
    K, T = top_e.shape
    tm = _tile(T, 4096)
    grid_spec = pltpu.PrefetchScalarGridSpec(
        num_scalar_prefetch=1,
        grid=(T // tm,),
        in_specs=[pl.BlockSpec((K, tm), lambda i, o: (0, i)),
                  pl.BlockSpec((K, tm), lambda i, o: (0, i))],
        out_specs=pl.BlockSpec((K, tm), lambda i, o: (0, i)),
    )
    return pl.pallas_call(
        _slots_kernel,
        grid_spec=grid_spec,
        out_shape=jax.ShapeDtypeStruct((K, T), jnp.int32),
        compiler_params=_cparams(("parallel",)),
        name="slots",
    )(offs, top_e, rank)


def _row_copy(src, src_row, dst, dst_row, sem):
    s0 = pl.multiple_of(src_row * SLAB, SLAB)
    d0 = pl.multiple_of(dst_row * SLAB, SLAB)
    return pltpu.make_async_copy(src.at[pl.ds(s0, SLAB), :], dst.at[pl.ds(d0, SLAB), :], sem)


def _dispatch_kernel(dest_ref, h_ref, xs_ref, sem, *, td):
    def issue(j, carry):
        for kk in range(TOP_K):
            _row_copy(h_ref, j, xs_ref, dest_ref[0, kk * td + j], sem).start(priority=kk % 2)
        return carry

    lax.fori_loop(0, td, issue, 0)
    for _ in range(TOP_K):
        pltpu.make_async_copy(h_ref, xs_ref.at[pl.ds(0, td * SLAB), :], sem).wait()


def _dispatch(h2, dest_tiles, td):
    T = h2.shape[0] // SLAB
    nt = T // td
    return pl.pallas_call(
        functools.partial(_dispatch_kernel, td=td),
        grid=(nt,),
        in_specs=[pl.BlockSpec((None, 1, TOP_K * td), lambda i: (i, 0, 0),
                               memory_space=pltpu.SMEM),
                  pl.BlockSpec((td * SLAB, LANES), lambda i: (i, 0))],
        out_specs=pl.BlockSpec(memory_space=pl.ANY),
        out_shape=jax.ShapeDtypeStruct((T * TOP_K * SLAB, LANES), h2.dtype),
        scratch_shapes=[pltpu.SemaphoreType.DMA],
        compiler_params=_cparams(("arbitrary",)),
        name="dispatch",
    )(dest_tiles, h2)


def _gmm_kernel(blk_ref, exp_ref, lo_ref, hi_ref, first_ref, newe_ref, x_ref, w1_ref, w3_ref,
                w2_ref, o_ref, w1b, w3b, w2b, *, bm):
    it = pl.program_id(0)

    @pl.when(newe_ref[it] == 1)
    def _():
        w1b[...] = w1_ref[...].astype(BF16)
        w3b[...] = w3_ref[...].astype(BF16)
        w2b[...] = w2_ref[...].astype(BF16)

    rows = lax.broadcasted_iota(jnp.int32, (bm, 1), 0)
    valid = jnp.logical_and(rows >= lo_ref[it], rows < hi_ref[it])
    x_lo, x_hi = _unpack_rows(jnp.where(valid, _load_slabs(x_ref, bm), jnp.uint32(0)))
    x_lo = x_lo.astype(BF16)
    x_hi = x_hi.astype(BF16)
    half = x_lo.shape[1]
    h1 = (jnp.dot(x_lo, w1b[:half, :], preferred_element_type=F32)
          + jnp.dot(x_hi, w1b[half:, :], preferred_element_type=F32))
    h3 = (jnp.dot(x_lo, w3b[:half, :], preferred_element_type=F32)
          + jnp.dot(x_hi, w3b[half:, :], preferred_element_type=F32))
    hb = (h1 * jax.nn.sigmoid(h1) * h3).astype(BF16)
    y = jnp.dot(hb, w2b[...], preferred_element_type=F32)

    @pl.when(first_ref[it] == 1)
    def _():
        _store_slabs(o_ref, _pack_rows(y))

    @pl.when(first_ref[it] == 0)
    def _():
        o_lo, o_hi = _unpack_rows(_load_slabs(o_ref, bm))
        _store_slabs(o_ref, _pack_rows(y + jnp.concatenate([o_lo, o_hi], axis=1)))


def _gmm(xs, items, w1, w3, w2, bm):
    A = xs.shape[0] // SLAB
    D = w1.shape[1]
    F = w1.shape[2]
    n_items = items[0].shape[0]
    grid_spec = pltpu.PrefetchScalarGridSpec(
        num_scalar_prefetch=6,
        grid=(n_items,),
        in_specs=[pl.BlockSpec((bm * SLAB, LANES), lambda i, blk, ex, *_: (blk[i], 0)),
                  pl.BlockSpec((None, D, F), lambda i, blk, ex, *_: (ex[i], 0, 0)),
                  pl.BlockSpec((None, D, F), lambda i, blk, ex, *_: (ex[i], 0, 0)),
                  pl.BlockSpec((None, F, D), lambda i, blk, ex, *_: (ex[i], 0, 0))],
        out_specs=pl.BlockSpec((bm * SLAB, LANES), lambda i, blk, ex, *_: (blk[i], 0)),
        scratch_shapes=[pltpu.VMEM((D, F), BF16), pltpu.VMEM((D, F), BF16),
                        pltpu.VMEM((F, D), BF16)],
    )
    return pl.pallas_call(
        functools.partial(_gmm_kernel, bm=bm),
        grid_spec=grid_spec,
        out_shape=jax.ShapeDtypeStruct((A * SLAB, LANES), jnp.uint32),
        compiler_params=_cparams(("arbitrary",)),
        name="gmm",
    )(*items, xs, w1, w3, w2)


def _work_items(counts, bm, n_blocks):
    n_items = n_blocks + N_EXPERTS - 1
    ends = jnp.cumsum(counts)
    starts = ends - counts
    nb = jnp.where(counts > 0, (ends - 1) // bm - starts // bm + 1, 0)
    item_end = jnp.cumsum(nb)
    item_start = item_end - nb
    n_real = item_end[-1]
    i = jnp.arange(n_items, dtype=jnp.int32)
    e = jnp.minimum(jnp.sum(item_end[None, :] <= i[:, None], axis=1), N_EXPERTS - 1).astype(jnp.int32)
    onehot = e[:, None] == jnp.arange(N_EXPERTS, dtype=jnp.int32)[None, :]
    pick = lambda v: jnp.sum(jnp.where(onehot, v[None, :], 0), axis=1)
    start_e, end_e = pick(starts), pick(ends)
    blk = start_e // bm + (i - pick(item_start))
    lo = jnp.clip(start_e - blk * bm, 0, bm)
    hi = jnp.clip(end_e - blk * bm, 0, bm)
    real = i < n_real
    blk = jnp.where(real, blk, n_blocks - 1).astype(jnp.int32)
    lo = jnp.where(real, lo, 0).astype(jnp.int32)
    hi = jnp.where(real, hi, 0).astype(jnp.int32)
    one = jnp.ones((1,), jnp.int32)
    first = jnp.concatenate([one, (blk[1:] != blk[:-1]).astype(jnp.int32)])
    new_e = jnp.concatenate([one, (e[1:] != e[:-1]).astype(jnp.int32)])
    return blk, e, lo, hi, first, new_e


def _combine_kernel(dest_ref, nxt_ref, ys_ref, w_ref, x1_ref, h_ref, w1_ref, w3_ref, w2_ref,
                    gt2_ref, g_ref, o_ref, buf, y_scr, sems, *, tc):
    i = pl.program_id(0)
    n = pl.num_programs(0)
    slot = i % 2

    def gather(idx_ref, s):
        def issue(j, carry):
            for kk in range(TOP_K):
                _row_copy(ys_ref, idx_ref[0, kk * tc + j], buf.at[s, kk], j,
                          sems.at[s]).start(priority=kk % 2)
            return carry

        lax.fori_loop(0, tc, issue, 0, unroll=4)

    @pl.when(i == 0)
    def _():
        gather(dest_ref, 0)

    for s in range(2):
        @pl.when(jnp.logical_and(i + 1 < n, slot == 1 - s))
        def _():
            gather(nxt_ref, s)

    h_lo, h_hi = _unpack_rows(_load_slabs(h_ref, tc))
    h_lo = h_lo.astype(BF16)
    h_hi = h_hi.astype(BF16)
    half = h_lo.shape[1]
    h1 = (jnp.dot(h_lo, w1_ref[:half, :], preferred_element_type=F32)
          + jnp.dot(h_hi, w1_ref[half:, :], preferred_element_type=F32))
    h3 = (jnp.dot(h_lo, w3_ref[:half, :], preferred_element_type=F32)
          + jnp.dot(h_hi, w3_ref[half:, :], preferred_element_type=F32))
    y_scr[...] = jnp.dot((h1 * jax.nn.sigmoid(h1) * h3).astype(BF16), w2_ref[...],
                         preferred_element_type=F32)

    for kk in range(TOP_K):
        pltpu.make_async_copy(ys_ref.at[pl.ds(0, tc * SLAB), :], buf.at[slot, kk],
                              sems.at[slot]).wait()
    rows = 2 * SUBLANES
    for r0 in range(0, tc, rows):
        for kk in range(TOP_K):
            u = jnp.concatenate([buf[slot, kk, pl.ds(r0 * SLAB + c, rows, stride=SLAB), :]
                                 for c in range(SLAB)], axis=1)
            e_lo, e_hi = _unpack_rows(u)
            wk = w_ref[r0:r0 + rows, kk:kk + 1]
            moe_lo = wk * e_lo if kk == 0 else moe_lo + wk * e_lo
            moe_hi = wk * e_hi if kk == 0 else moe_hi + wk * e_hi
        y_scr[r0:r0 + rows, :] += jnp.concatenate([moe_lo, moe_hi], axis=1)
    o_ref[...] = x1_ref[...] + gt2_ref[...] * _rms(y_scr[...], g_ref[...])


def _combine(ys, dest_tiles, w_tok, x1, h2, w1s, w3s, w2s, mod4, g_post, S, tc):
    T, D = x1.shape
    F = w1s.shape[1]
    per_b = S // tc
    nt = T // tc
    row = lambda: pl.BlockSpec((tc, D), lambda i: (i, 0))
    return pl.pallas_call(
        functools.partial(_combine_kernel, tc=tc),
        grid=(nt,),
        in_specs=[pl.BlockSpec((None, 1, TOP_K * tc), lambda i: (i, 0, 0),
                               memory_space=pltpu.SMEM),
                  pl.BlockSpec((None, 1, TOP_K * tc), lambda i: (jnp.minimum(i + 1, nt - 1), 0, 0),
                               memory_space=pltpu.SMEM),
                  pl.BlockSpec(memory_space=pl.ANY),
                  pl.BlockSpec((tc, TOP_K), lambda i: (i, 0)),
                  row(), pl.BlockSpec((tc * SLAB, LANES), lambda i: (i, 0)),
                  pl.BlockSpec((D, F), lambda i: (0, 0)),
                  pl.BlockSpec((D, F), lambda i: (0, 0)),
                  pl.BlockSpec((F, D), lambda i: (0, 0)),
                  pl.BlockSpec((None, None, 1, D), lambda i: (i // per_b, 5, 0, 0)),
                  pl.BlockSpec((1, D), lambda i: (0, 0))],
        out_specs=row(),
        out_shape=jax.ShapeDtypeStruct((T, D), F32),
        scratch_shapes=[pltpu.VMEM((2, TOP_K, tc * SLAB, LANES), jnp.uint32),
                        pltpu.VMEM((tc, D), F32), pltpu.SemaphoreType.DMA((2,))],
        compiler_params=_cparams(("arbitrary",)),
        name="combine",
    )(dest_tiles, dest_tiles, ys, w_tok, x1, h2, w1s, w3s, w2s, mod4, g_post.reshape(1, D))


def _lambda_init(layer):
    return 0.8 - 0.6 * math.exp(-0.3 * layer)


def kernel(x, c, w_ada, b_ada, g_pre_mix, w_in, conv_w, conv_b, lru_wa, lru_ba, lru_wx, lru_bx,
           lru_lambda, lam_q1, lam_k1, lam_q2, lam_k2, g_subln, w_proj_rnn, w_proj_att, w_out,
           g_post_mix, g_pre_ffn, w_router, e_bias, w1_e, w3_e, w2_e, w1_s, w3_s, w2_s,
           g_post_ffn):
    B, S, D = x.shape
    T = B * S
    depth = w_ada.shape[0]
    slopes = np.exp2(-8.0 * np.arange(1, N_HEADS + 1, dtype=np.float32) / N_HEADS)
    kbias = _alibi_tables(slopes, S)
    tt = _tile(S, 256)
    bm = _tile(T * TOP_K, 512)
    n_blocks = T * TOP_K // bm

    x2 = x.reshape(T, D)
    for l in range(depth):
        lam_init = _lambda_init(l)
        mod4 = _ada(c, w_ada[l], b_ada[l]).reshape(B, 6, 1, D)

        proj = _inproj(x2, g_pre_mix[l], mod4, w_in[l].astype(BF16), S)
        yr = _lru(proj, conv_w[l], conv_b[l], lru_wa[l].astype(BF16), lru_ba[l],
                  lru_wx[l].astype(BF16), lru_bx[l], lru_lambda[l], B, S)
        lamv = jnp.stack([lam_q1[l], lam_k1[l], lam_q2[l], lam_k2[l]])
        ao = _attn(proj, kbias, lamv, g_subln[l], B, S, lam_init)
        x1, h2p, top_e, top_w, rank, cnt = _mixout(
            x2, yr, ao, proj, w_proj_rnn[l].astype(BF16), w_proj_att[l].astype(BF16),
            w_out[l].astype(BF16), g_post_mix[l], g_pre_ffn[l], mod4, w_router[l].T, e_bias[l], S)
        counts = cnt[:, 0].astype(jnp.int32)
        offs = jnp.cumsum(counts) - counts
        dest = _slots(offs, top_e, rank)
        dest_tiles = dest.reshape(TOP_K, T // tt, tt).transpose(1, 0, 2).reshape(T // tt, 1,
                                                                                  TOP_K * tt)
        xs = _dispatch(h2p, dest_tiles, tt)
        items = _work_items(counts, bm, n_blocks)
        ys = _gmm(xs, items, w1_e[l], w3_e[l], w2_e[l], bm)
        x2 = _combine(ys, dest_tiles, top_w.T, x1, h2p, w1_s[l].astype(BF16),
                      w3_s[l].astype(BF16), w2_s[l].astype(BF16), mod4, g_post_ffn[l], S, tt)
    return x2.reshape(B, S, D)
```

```python
import functools
import math

import jax
import jax.numpy as jnp
import numpy as np
from jax import lax
from jax.experimental import pallas as pl
from jax.experimental.pallas import tpu as pltpu

F32 = jnp.float32
BF16 = jnp.bfloat16

EPS = 1e-6
N_HEADS = 8
HEAD_DIM = 64
V_DIM = 2 * HEAD_DIM
LRU_BLOCKS = 8
CONV_W = 4
LRU_C = 8.0
N_EXPERTS = 64
TOP_K = 8
N_GROUPS = 8
GROUP_SIZE = N_EXPERTS // N_GROUPS
TOPK_GROUPS = 4
ROUTED_SCALE = 2.5

LANES = 128
SUBLANES = 8
VMEM_LIMIT = 48 * 1024 * 1024


def _cparams(sem):
    return pltpu.CompilerParams(dimension_semantics=sem, vmem_limit_bytes=VMEM_LIMIT)


def _tile(n, pref):
    t = min(n, pref)
    while n % t:
        t //= 2
    return t


def _rms(x, g):
    return x * lax.rsqrt(jnp.mean(x * x, axis=-1, keepdims=True) + EPS) * g


def _ada_kernel(c_ref, w_ref, b_ref, o_ref):
    c = c_ref[...]
    cond = c * jax.nn.sigmoid(c)
    o_ref[...] = jnp.dot(cond, w_ref[...], preferred_element_type=F32) + b_ref[...]


def _ada(c, w, b):
    B, D = c.shape
    N = w.shape[1]
    tn = _tile(N, 1024)
    return pl.pallas_call(
        _ada_kernel,
        grid=(N // tn,),
        in_specs=[pl.BlockSpec((B, D), lambda j: (0, 0)),
                  pl.BlockSpec((D, tn), lambda j: (0, j)),
                  pl.BlockSpec((1, tn), lambda j: (0, j))],
        out_specs=pl.BlockSpec((B, tn), lambda j: (0, j)),
        out_shape=jax.ShapeDtypeStruct((B, N), F32),
        compiler_params=_cparams(("parallel",)),
        name="ada",
    )(c, w, b.reshape(1, N))


LOG2E = 1.4426950408889634
Q_COL_BLOCK = 2
Q_PRESCALE = HEAD_DIM ** -0.5 * LOG2E


def _inproj_kernel(x_ref, g_ref, sh_ref, sc_ref, w_ref, o_ref, h_scr):
    @pl.when(pl.program_id(1) == 0)
    def _():
        h = _rms(x_ref[...], g_ref[...]) * (1.0 + sc_ref[...]) + sh_ref[...]
        h_scr[...] = h.astype(BF16)

    r = jnp.dot(h_scr[...], w_ref[...], preferred_element_type=F32)
    r = r * jnp.where(pl.program_id(1) == Q_COL_BLOCK, Q_PRESCALE, 1.0)
    o_ref[...] = r.astype(o_ref.dtype)


def _inproj(x2, g, mod4, w_bf, S):
    T, D = x2.shape
    N = w_bf.shape[1]
    tm = _tile(S, 2048)
    tn = 1024
    per_b = S // tm
    return pl.pallas_call(
        _inproj_kernel,
        grid=(T // tm, N // tn),
        in_specs=[pl.BlockSpec((tm, D), lambda i, j: (i, 0)),
                  pl.BlockSpec((1, D), lambda i, j: (0, 0)),
                  pl.BlockSpec((None, None, 1, D), lambda i, j: (i // per_b, 0, 0, 0)),
                  pl.BlockSpec((None, None, 1, D), lambda i, j: (i // per_b, 1, 0, 0)),
                  pl.BlockSpec((D, tn), lambda i, j: (0, j))],
        out_specs=pl.BlockSpec((tm, tn), lambda i, j: (i, j)),
        out_shape=jax.ShapeDtypeStruct((T, N), BF16),
        scratch_shapes=[pltpu.VMEM((tm, D), BF16)],
        compiler_params=_cparams(("parallel", "arbitrary")),
        name="inproj",
    )(x2, g.reshape(1, D), mod4, mod4, w_bf)


def _lru_kernel(xr_ref, gr_ref, cw_ref, cb_ref, wa_ref, ba_ref, wx_ref, bx_ref, lam_ref,
                o_ref, xc_scr, prev_scr, h_scr, *, ts):
    s = pl.program_id(1)

    @pl.when(s == 0)
    def _():
        prev_scr[...] = jnp.zeros_like(prev_scr)
        h_scr[...] = jnp.zeros_like(h_scr)

    x = xr_ref[...].astype(F32)
    prev = prev_scr[...]
    row8 = lax.broadcasted_iota(jnp.int32, (SUBLANES, 1), 0)
    acc = cb_ref[...] + cw_ref[CONV_W - 1:CONV_W, :] * x
    xc_scr[...] = acc
    top = cb_ref[...] + cw_ref[CONV_W - 1:CONV_W, :] * x[0:SUBLANES, :]
    for j in range(1, CONV_W):
        wj = cw_ref[CONV_W - 1 - j:CONV_W - j, :]
        rj = pltpu.roll(x, j, axis=0)
        xc_scr[...] += wj * rj
        pj = pltpu.roll(prev, j, axis=0)
        top += wj * jnp.where(row8 < j, pj, rj[0:SUBLANES, :])
    xc_scr[0:SUBLANES, :] = top
    prev_scr[...] = x[ts - SUBLANES:ts, :]

    row = lax.broadcasted_iota(jnp.int32, (ts, 1), 0)
    is_first = jnp.logical_and(row == 0, s == 0)
    sub = lax.broadcasted_iota(jnp.int32, (1, SUBLANES, 1), 1)
    for n in range(LRU_BLOCKS):
        cols = slice(n * LANES, (n + 1) * LANES)
        xc = xc_scr[:, cols]
        xb = xc.astype(BF16)
        r = jax.nn.sigmoid(jnp.dot(xb, wa_ref[n], preferred_element_type=F32) + ba_ref[:, cols])
        i = jax.nn.sigmoid(jnp.dot(xb, wx_ref[n], preferred_element_type=F32) + bx_ref[:, cols])
        lam = lam_ref[:, cols]
        softplus_neg = jnp.maximum(-lam, 0.0) + jnp.log1p(jnp.exp(-jnp.abs(lam)))
        log_a = (-LRU_C * softplus_neg) * r
        a = jnp.exp(log_a)
        m2 = 1.0 - a * a
        mult = jnp.where(m2 > 0.0, m2 * lax.rsqrt(m2), 0.0)
        mult = jnp.where(is_first, 1.0, mult)
        u = mult * (i * xc)
        a = a.reshape(ts // SUBLANES, SUBLANES, LANES)
        u = u.reshape(ts // SUBLANES, SUBLANES, LANES)
        for d in (1, 2, 4):
            keep = sub >= d
            a_sh = jnp.where(keep, pltpu.roll(a, d, axis=1), 1.0)
            u_sh = jnp.where(keep, pltpu.roll(u, d, axis=1), 0.0)
            u = u + a * u_sh
            a = a * a_sh
        a = a.reshape(ts, LANES)
        u = u.reshape(ts, LANES)
        gate = jax.nn.gelu(gr_ref[:, cols].astype(F32))
        carry = h_scr[:, cols]
        step = 2 * SUBLANES
        for g in range(ts // step):
            r0 = g * step
            h0 = u[r0:r0 + SUBLANES, :] + a[r0:r0 + SUBLANES, :] * carry
            carry = h0[SUBLANES - 1:SUBLANES, :]
            h1 = u[r0 + SUBLANES:r0 + step, :] + a[r0 + SUBLANES:r0 + step, :] * carry
            carry = h1[SUBLANES - 1:SUBLANES, :]
            hg = jnp.concatenate([h0, h1], axis=0) * gate[r0:r0 + step, :]
            o_ref[r0:r0 + step, cols] = hg.astype(o_ref.dtype)
        h_scr[:, cols] = carry


def _lru(proj, conv_w, conv_b, wa_bf, ba, wx_bf, bx, lam, B, S):
    C = conv_w.shape[1]
    ts = _tile(S, 256)
    ns = S // ts
    vec = lambda: pl.BlockSpec((1, C), lambda b, s: (0, 0))
    blk = lambda: pl.BlockSpec((LRU_BLOCKS, LANES, LANES), lambda b, s: (0, 0, 0))
    return pl.pallas_call(
        functools.partial(_lru_kernel, ts=ts),
        grid=(B, ns),
        in_specs=[pl.BlockSpec((ts, C), lambda b, s: (b * ns + s, 0)),
                  pl.BlockSpec((ts, C), lambda b, s: (b * ns + s, 1)),
                  pl.BlockSpec((CONV_W, C), lambda b, s: (0, 0)),
                  vec(), blk(), vec(), blk(), vec(), vec()],
        out_specs=pl.BlockSpec((ts, C), lambda b, s: (b * ns + s, 0)),
        out_shape=jax.ShapeDtypeStruct((B * S, C), BF16),
        scratch_shapes=[pltpu.VMEM((ts, C), F32), pltpu.VMEM((SUBLANES, C), F32),
                        pltpu.VMEM((1, C), F32)],
        compiler_params=_cparams(("parallel", "arbitrary")),
        name="lru",
    )(proj, proj, conv_w, conv_b.reshape(1, C), wa_bf, ba.reshape(1, C), wx_bf, bx.reshape(1, C),
      lam.reshape(1, C))


BIAS_LANES = 3
ACC_ROWS = V_DIM + 2 * SUBLANES


def _alibi_tables(slopes, S):
    def top16(x):
        return (x.view(np.uint32) & np.uint32(0xFFFF0000)).view(np.float32)

    pos = np.arange(S, dtype=np.float32)
    b = (slopes.astype(np.float32) * np.float32(LOG2E))[:, None] * pos[None, :]
    hi = top16(b)
    mid = top16(b - hi)
    lo = top16(b - hi - mid)
    half = np.zeros(b.shape + (HEAD_DIM,), np.float32)
    half[..., 0], half[..., 1], half[..., 2] = hi, mid, lo
    zero = np.zeros_like(half)
    table = np.stack([np.concatenate([zero, half], axis=-1),
                      np.concatenate([half, zero], axis=-1)], axis=1)
    return jnp.asarray(table, dtype=BF16)


def _lane_masks():
    lane = lax.broadcasted_iota(jnp.int32, (1, V_DIM), 1)
    own = (jnp.where(lane < HEAD_DIM, 1.0, 0.0), jnp.where(lane >= HEAD_DIM, 1.0, 0.0))
    ones_row = (jnp.where(jnp.logical_and(lane >= HEAD_DIM, lane < HEAD_DIM + BIAS_LANES), 1.0, 0.0),
                jnp.where(lane < BIAS_LANES, 1.0, 0.0))
    return own, ones_row


def _attn_kernel(lamv_ref, gsub_ref, q_ref, k_ref, v_ref, kb_ref, o_ref,
                 vt_scr, kx_scr, *scratch, tq, lam_init):
    tk = tq
    own, _ = _lane_masks()
    for j in range(vt_scr.shape[0]):
        rows = slice(j * tk, (j + 1) * tk)
        vt_scr[j, 0:V_DIM, :] = v_ref[rows, :].astype(F32).T.astype(BF16)
        vt_scr[j, V_DIM:ACC_ROWS, :] = jnp.ones((ACC_ROWS - V_DIM, tk), BF16)
        kf = k_ref[rows, :].astype(F32)
        for mp in range(2):
            kx_scr[mp, rows, :] = (kf * own[mp] + kb_ref[mp, rows, :].astype(F32)).astype(BF16)

    def q_block(qi, carry):
        _attn_q_block(qi, lamv_ref, gsub_ref, q_ref, o_ref, vt_scr, kx_scr, *scratch,
                      tq=tq, lam_init=lam_init)
        return carry

    lax.fori_loop(0, q_ref.shape[0] // tq, q_block, 0)


def _attn_q_block(qi, lamv_ref, gsub_ref, q_ref, o_ref,
                  vt_scr, kx_scr, qx_scr, s_scr, p_scr, a_scr, mx_scr, m_scr, acc_scr, *, tq,
                  lam_init):
    tk = tq
    q_rows = pl.ds(pl.multiple_of(qi * tq, tq), tq)
    own, ones_row = _lane_masks()
    q = q_ref[q_rows, :].astype(F32)
    for mp in range(2):
        qx_scr[mp] = (q * own[mp] + ones_row[mp]).astype(BF16)
    m_scr[...] = jnp.full_like(m_scr, -jnp.inf)
    acc_scr[...] = jnp.zeros_like(acc_scr)
    row8 = lax.broadcasted_iota(jnp.int32, (SUBLANES, LANES), 0)
    col = lax.broadcasted_iota(jnp.int32, (1, LANES), 1)
    neg_inf = jnp.float32(-jnp.inf)

    def tree(op, parts):
        parts = [p for p in parts if p is not None]
        while len(parts) > 1:
            parts = [op(parts[i], parts[i + 1]) if i + 1 < len(parts) else parts[i]
                     for i in range(0, len(parts), 2)]
        return parts[0]

    def scores(j, par):
        start = pl.multiple_of(j * tk, tk)
        for mp in range(2):
            s = lax.dot_general(kx_scr[mp, pl.ds(start, tk), :], qx_scr[mp],
                                (((1,), (1,)), ((), ())), preferred_element_type=F32)
            for c in range(tq // LANES):
                cols = slice(c * LANES, (c + 1) * LANES)
                s_scr[par, mp, c] = s[:, cols]
                accs = [None] * 4
                for i in range(tk // SUBLANES):
                    t = s[i * SUBLANES:(i + 1) * SUBLANES, cols]
                    accs[i % 4] = t if accs[i % 4] is None else jnp.maximum(accs[i % 4], t)
                mx_scr[par, mp, :, cols] = jnp.max(tree(jnp.maximum, accs), axis=0, keepdims=True)

    def softmax(par, masked):
        for mp in range(2):
            for c in range(tq // LANES):
                cols = slice(c * LANES, (c + 1) * LANES)
                lo_col, hi_col = c * LANES, (c + 1) * LANES - 1
                s_c = s_scr.at[par, mp, c]
                p_c = p_scr.at[par, mp, c]
                if masked:
                    accs = [None] * 4
                    for i in range(tk // SUBLANES):
                        r0 = i * SUBLANES
                        if r0 > hi_col:
                            continue
                        t = s_c[r0:r0 + SUBLANES, :]
                        if r0 + SUBLANES - 1 > lo_col:
                            t = jnp.where(row8 + r0 <= col + lo_col, t, neg_inf)
                            s_c[r0:r0 + SUBLANES, :] = t
                        accs[i % 4] = t if accs[i % 4] is None else jnp.maximum(accs[i % 4], t)
                    mx = jnp.max(tree(jnp.maximum, accs), axis=0, keepdims=True)
                else:
                    mx = mx_scr[par, mp, :, cols]
                m_old = m_scr[mp, :, cols]
                m_new = jnp.maximum(m_old, mx)
                a_scr[par, mp, :, cols] = jnp.exp2(m_old - m_new)
                m_scr[mp, :, cols] = m_new
                for i in range(tk // (2 * SUBLANES)):
                    r0 = i * 2 * SUBLANES
                    if masked and r0 > hi_col:
                        p_c[r0:r0 + 2 * SUBLANES, :] = jnp.zeros((2 * SUBLANES, LANES), BF16)
                        continue
                    p = jnp.exp2(s_c[r0:r0 + 2 * SUBLANES, :] - m_new)
                    p_c[r0:r0 + 2 * SUBLANES, :] = p.astype(BF16)

    def values(j, par):
        for mp in range(2):
            p = jnp.concatenate([p_scr[par, mp, c] for c in range(tq // LANES)], axis=1)
            pv = jnp.dot(vt_scr[j], p, preferred_element_type=F32)
            acc_scr[mp] = a_scr[par, mp] * acc_scr[mp] + pv

    p_scr[1] = jnp.zeros(p_scr.shape[1:], BF16)
    a_scr[1] = jnp.ones(a_scr.shape[1:], F32)
    scores(0, 0)

    def stage(j, par):
        scores(j + 1, 1 - par)
        softmax(par, False)
        values(jnp.maximum(j - 1, 0), 1 - par)

    def stage_pair(i, carry):
        stage(2 * i, 0)
        stage(2 * i + 1, 1)
        return carry

    def tail(par):
        softmax(par, True)
        values(jnp.maximum(qi - 1, 0), 1 - par)
        values(qi, par)

    lax.fori_loop(0, qi // 2, stage_pair, 0)

    @pl.when(qi % 2 == 1)
    def _():
        stage(qi - 1, 0)
        tail(1)

    @pl.when(qi % 2 == 0)
    def _():
        tail(0)

    lv = lamv_ref[...]
    lam = (jnp.exp(jnp.sum(lv[0:1, :] * lv[1:2, :], axis=-1, keepdims=True))
           - jnp.exp(jnp.sum(lv[2:3, :] * lv[3:4, :], axis=-1, keepdims=True)) + lam_init)
    o0 = acc_scr[0, 0:V_DIM, :] * (1.0 / acc_scr[0, V_DIM:V_DIM + 1, :])
    o1 = acc_scr[1, 0:V_DIM, :] * (1.0 / acc_scr[1, V_DIM:V_DIM + 1, :])
    o_t = o0 - lam * o1
    o_t = o_t * lax.rsqrt(jnp.mean(o_t * o_t, axis=0, keepdims=True) + EPS) * gsub_ref[...]
    o_ref[q_rows, :] = (o_t * (1.0 - lam_init)).T.astype(o_ref.dtype)


def _attn(proj, kbias, lamv, g_subln, B, S, lam_init):
    tq = _tile(S, 512)
    nq = S // tq
    qc, kc, vc = 2 * 8, 3 * 8, 4 * 8
    return pl.pallas_call(
        functools.partial(_attn_kernel, tq=tq, lam_init=lam_init),
        grid=(B, N_HEADS),
        in_specs=[pl.BlockSpec((4, HEAD_DIM), lambda b, h: (0, 0)),
                  pl.BlockSpec((V_DIM, 1), lambda b, h: (0, 0)),
                  pl.BlockSpec((S, V_DIM), lambda b, h: (b, qc + h)),
                  pl.BlockSpec((S, V_DIM), lambda b, h: (b, kc + h)),
                  pl.BlockSpec((S, V_DIM), lambda b, h: (b, vc + h)),
                  pl.BlockSpec((None, 2, S, V_DIM), lambda b, h: (h, 0, 0, 0))],
        out_specs=pl.BlockSpec((S, V_DIM), lambda b, h: (b, h)),
        out_shape=jax.ShapeDtypeStruct((B * S, N_HEADS * V_DIM), BF16),
        scratch_shapes=[pltpu.VMEM((nq, ACC_ROWS, tq), BF16), pltpu.VMEM((2, S, V_DIM), BF16),
                        pltpu.VMEM((2, tq, V_DIM), BF16),
                        pltpu.VMEM((2, 2, tq // LANES, tq, LANES), F32),
                        pltpu.VMEM((2, 2, tq // LANES, tq, LANES), BF16),
                        pltpu.VMEM((2, 2, 1, tq), F32), pltpu.VMEM((2, 2, 1, tq), F32),
                        pltpu.VMEM((2, 1, tq), F32),
                        pltpu.VMEM((2, ACC_ROWS, tq), F32)],
        compiler_params=_cparams(("parallel", "parallel")),
        name="attn",
    )(lamv, g_subln.reshape(V_DIM, 1), proj, proj, proj, kbias)


def _pack_rows(x):
    half = x.shape[1] // 2
    lo = lax.bitcast_convert_type(x[:, :half].astype(BF16).astype(F32), jnp.uint32)
    hi = lax.bitcast_convert_type(x[:, half:].astype(BF16).astype(F32), jnp.uint32)
    return (hi & jnp.uint32(0xFFFF0000)) | (lo >> 16)


def _unpack_rows(u):
    lo = lax.bitcast_convert_type(u << 16, F32)
    hi = lax.bitcast_convert_type(u & jnp.uint32(0xFFFF0000), F32)
    return lo, hi


SLAB = 4


def _load_slabs(ref, n):
    return jnp.concatenate([ref[pl.ds(c, n, stride=SLAB), :] for c in range(SLAB)], axis=1)


def _store_slabs(ref, u):
    n = u.shape[0]
    for c in range(SLAB):
        ref[pl.ds(c, n, stride=SLAB), :] = u[:, c * LANES:(c + 1) * LANES]


def _mixout_kernel(x_ref, yr_ref, ao_ref, ga_ref, gb_ref, wr_ref, wa_ref, wo_ref,
                   gpost_ref, gt1_ref, gpre_ref, sh2_ref, sc2_ref, wrt_ref, eb_ref,
                   x1_ref, h2p_ref, e_ref, w_ref, r_ref, cnt_ref, cnt_scr):
    ya = jnp.dot(yr_ref[...], wr_ref[...], preferred_element_type=F32)
    yb = jnp.dot(ao_ref[...], wa_ref[...], preferred_element_type=F32)
    merged = (jax.nn.sigmoid(ga_ref[...].astype(F32)) * ya
              + jax.nn.sigmoid(gb_ref[...].astype(F32)) * yb)
    y = jnp.dot(merged.astype(BF16), wo_ref[...], preferred_element_type=F32)
    x1 = x_ref[...] + gt1_ref[...] * _rms(y, gpost_ref[...])
    x1_ref[...] = x1
    h2 = _rms(x1, gpre_ref[...]) * (1.0 + sc2_ref[...]) + sh2_ref[...]
    _store_slabs(h2p_ref, _pack_rows(h2))
    _route(h2, wrt_ref, eb_ref, e_ref, w_ref, r_ref, cnt_ref, cnt_scr)


def _mixout(x2, yr, ao, proj, wr_bf, wa_bf, wo_bf, g_post, g_pre, mod4, w_router_t, e_bias, S):
    T, D = x2.shape
    tm = _tile(S, 512)
    per_b = S // tm
    gac, gbc = 5, 6
    row = lambda: pl.BlockSpec((tm, D), lambda i: (i, 0))
    wsp = lambda: pl.BlockSpec((D, D), lambda i: (0, 0))
    vec = lambda: pl.BlockSpec((1, D), lambda i: (0, 0))
    modv = lambda j: pl.BlockSpec((None, None, 1, D), lambda i: (i // per_b, j, 0, 0))
    topk = lambda: pl.BlockSpec((TOP_K, tm), lambda i: (0, i))
    return pl.pallas_call(
        _mixout_kernel,
        grid=(T // tm,),
        in_specs=[row(), row(), row(),
                  pl.BlockSpec((tm, D), lambda i: (i, gac)),
                  pl.BlockSpec((tm, D), lambda i: (i, gbc)),
                  wsp(), wsp(), wsp(), vec(), modv(2), vec(), modv(3), modv(4),
                  pl.BlockSpec((N_EXPERTS, D), lambda i: (0, 0)),
                  pl.BlockSpec((N_EXPERTS, 1), lambda i: (0, 0))],
        out_specs=[row(), pl.BlockSpec((tm * SLAB, LANES), lambda i: (i, 0)),
                   topk(), topk(), topk(),
                   pl.BlockSpec((N_EXPERTS, LANES), lambda i: (0, 0))],
        out_shape=[jax.ShapeDtypeStruct((T, D), F32),
                   jax.ShapeDtypeStruct((T * SLAB, LANES), jnp.uint32),
                   jax.ShapeDtypeStruct((TOP_K, T), jnp.int32),
                   jax.ShapeDtypeStruct((TOP_K, T), F32),
                   jax.ShapeDtypeStruct((TOP_K, T), jnp.int32),
                   jax.ShapeDtypeStruct((N_EXPERTS, LANES), F32)],
        scratch_shapes=[pltpu.VMEM((N_EXPERTS, 1), F32)],
        compiler_params=_cparams(("arbitrary",)),
        name="mixout",
    )(x2, yr, ao, proj, proj, wr_bf, wa_bf, wo_bf, g_post.reshape(1, D), mod4,
      g_pre.reshape(1, D), mod4, mod4, w_router_t, e_bias.reshape(N_EXPERTS, 1))


def _first_argmax(vals, ids, sentinel):
    m = jnp.max(vals, axis=0, keepdims=True)
    idx = jnp.min(jnp.where(vals == m, ids, sentinel), axis=0, keepdims=True)
    return m, idx


def _route(h2, wr_ref, eb_ref, e_ref, w_ref, r_ref, cnt_ref, cnt_scr):
    tm = h2.shape[0]
    step = pl.program_id(0)

    @pl.when(step == 0)
    def _():
        cnt_scr[...] = jnp.zeros_like(cnt_scr)

    def split(x):
        hi = x.astype(BF16)
        return hi, (x - hi.astype(F32)).astype(BF16)

    nt = (((1,), (1,)), ((), ()))
    w_hi, w_lo = split(wr_ref[...])
    h_hi, h_lo = split(h2)
    logits = (lax.dot_general(w_hi, h_hi, nt, preferred_element_type=F32)
              + lax.dot_general(w_hi, h_lo, nt, preferred_element_type=F32)
              + lax.dot_general(w_lo, h_hi, nt, preferred_element_type=F32))
    scores = jax.nn.sigmoid(logits)
    choice = scores + eb_ref[...]
    i8 = lax.broadcasted_iota(jnp.int32, (GROUP_SIZE, tm), 0)
    neg_inf = jnp.float32(-jnp.inf)

    slabs = [choice[g * GROUP_SIZE:(g + 1) * GROUP_SIZE, :] for g in range(N_GROUPS)]
    sc_slabs = [scores[g * GROUP_SIZE:(g + 1) * GROUP_SIZE, :] for g in range(N_GROUPS)]

    gs = jnp.zeros((N_GROUPS, tm), F32)
    for g in range(N_GROUPS):
        m1, idx1 = _first_argmax(slabs[g], i8, GROUP_SIZE)
        m2 = jnp.max(jnp.where(i8 == idx1, neg_inf, slabs[g]), axis=0, keepdims=True)
        gs = jnp.where(i8 == g, m1 + m2, gs)

    sel = jnp.zeros((N_GROUPS, tm), jnp.int32)
    cur = gs
    for _ in range(TOPK_GROUPS):
        _, idx = _first_argmax(cur, i8, N_GROUPS)
        hit = i8 == idx
        sel = jnp.where(hit, 1, sel)
        cur = jnp.where(hit, neg_inf, cur)

    masked = [jnp.where(sel[g:g + 1, :] > 0, slabs[g], neg_inf) for g in range(N_GROUPS)]
    ids = [i8 + g * GROUP_SIZE for g in range(N_GROUPS)]
    onehot = [jnp.zeros((GROUP_SIZE, tm), F32) for _ in range(N_GROUPS)]
    picks = []
    wts = []
    for _ in range(TOP_K):
        m = jnp.max(functools.reduce(jnp.maximum, masked), axis=0, keepdims=True)
        idx = jnp.min(functools.reduce(
            jnp.minimum, [jnp.where(c == m, i, N_EXPERTS) for c, i in zip(masked, ids)]),
            axis=0, keepdims=True)
        w8 = jnp.zeros((GROUP_SIZE, tm), F32)
        for g in range(N_GROUPS):
            hit = ids[g] == idx
            w8 = w8 + jnp.where(hit, sc_slabs[g], 0.0)
            masked[g] = jnp.where(hit, neg_inf, masked[g])
            onehot[g] = jnp.where(hit, 1.0, onehot[g])
        picks.append(idx)
        wts.append(jnp.sum(w8, axis=0, keepdims=True))

    wsum = functools.reduce(lambda a, b: a + b, wts)
    norm = ROUTED_SCALE / (wsum + 1e-20)

    t_row = lax.broadcasted_iota(jnp.int32, (tm, tm), 0)
    t_col = lax.broadcasted_iota(jnp.int32, (tm, tm), 1)
    before = jnp.where(t_row < t_col, 1.0, 0.0).astype(BF16)
    cum = [jnp.dot(onehot[g].astype(BF16), before, preferred_element_type=F32)
           + cnt_scr[g * GROUP_SIZE:(g + 1) * GROUP_SIZE, :] for g in range(N_GROUPS)]

    for kk in range(TOP_K):
        rank8 = jnp.zeros((GROUP_SIZE, tm), F32)
        for g in range(N_GROUPS):
            rank8 = rank8 + jnp.where(ids[g] == picks[kk], cum[g], 0.0)
        rank = jnp.sum(rank8, axis=0, keepdims=True)
        e_ref[kk:kk + 1, :] = picks[kk]
        w_ref[kk:kk + 1, :] = wts[kk] * norm
        r_ref[kk:kk + 1, :] = rank.astype(jnp.int32)

    for g in range(N_GROUPS):
        rows = slice(g * GROUP_SIZE, (g + 1) * GROUP_SIZE)
        cnt_scr[rows, :] = cnt_scr[rows, :] + jnp.sum(onehot[g], axis=1, keepdims=True)
    cnt_ref[...] = jnp.broadcast_to(cnt_scr[...], cnt_ref.shape)


def _slots_kernel(offs_ref, e_ref, r_ref, d_ref):
    e = e_ref[...]
    d = r_ref[...]
    for ex in range(N_EXPERTS):
        d = d + jnp.where(e == ex, offs_ref[ex], 0)
    d_ref[...] = d


def _slots(offs, top_e, rank):
    K, T = top_e.shape
    tm = _tile(T, 4096)
    grid_spec = pltpu.PrefetchScalarGridSpec(
        num_scalar_prefetch=1,
        grid=(T // tm,),
        in_specs=[pl.BlockSpec((K, tm), lambda i, o: (0, i)),
                  pl.BlockSpec((K, tm), lambda i, o: (0, i))],
        out_specs=pl.BlockSpec((K, tm), lambda i, o: (0, i)),
    )
    return pl.pallas_call(
        _slots_kernel,
        grid_spec=grid_spec,
        out_shape=jax.ShapeDtypeStruct((K, T), jnp.int32),
        compiler_params=_cparams(("parallel",)),
        name="slots",
    )(offs, top_e, rank)


def _row_copy(src, src_row, dst, dst_row, sem):
    s0 = pl.multiple_of(src_row * SLAB, SLAB)
    d0 = pl.multiple_of(dst_row * SLAB, SLAB)
    return pltpu.make_async_copy(src.at[pl.ds(s0, SLAB), :], dst.at[pl.ds(d0, SLAB), :], sem)


def _dispatch_kernel(dest_ref, h_ref, xs_ref, sem, *, td):
    def issue(j, carry):
        for kk in range(TOP_K):
            _row_copy(h_ref, j, xs_ref, dest_ref[0, kk * td + j], sem).start(priority=kk % 2)
        return carry

    lax.fori_loop(0, td, issue, 0)
    for _ in range(TOP_K):
        pltpu.make_async_copy(h_ref, xs_ref.at[pl.ds(0, td * SLAB), :], sem).wait()


def _dispatch(h2, dest_tiles, td):
    T = h2.shape[0] // SLAB
    nt = T // td
    return pl.pallas_call(
        functools.partial(_dispatch_kernel, td=td),
        grid=(nt,),
        in_specs=[pl.BlockSpec((None, 1, TOP_K * td), lambda i: (i, 0, 0),
                               memory_space=pltpu.SMEM),
                  pl.BlockSpec((td * SLAB, LANES), lambda i: (i, 0))],
        out_specs=pl.BlockSpec(memory_space=pl.ANY),
        out_shape=jax.ShapeDtypeStruct((T * TOP_K * SLAB, LANES), h2.dtype),
        scratch_shapes=[pltpu.SemaphoreType.DMA],
        compiler_params=_cparams(("arbitrary",)),
        name="dispatch",
    )(dest_tiles, h2)


def _gmm_kernel(blk_ref, exp_ref, lo_ref, hi_ref, first_ref, newe_ref, x_ref, w1_ref, w3_ref,
                w2_ref, o_ref, w1b, w3b, w2b, *, bm):
    it = pl.program_id(0)

    @pl.when(newe_ref[it] == 1)
    def _():
        w1b[...] = w1_ref[...].astype(BF16)
        w3b[...] = w3_ref[...].astype(BF16)
        w2b[...] = w2_ref[...].astype(BF16)

    rows = lax.broadcasted_iota(jnp.int32, (bm, 1), 0)
    valid = jnp.logical_and(rows >= lo_ref[it], rows < hi_ref[it])
    x_lo, x_hi = _unpack_rows(jnp.where(valid, _load_slabs(x_ref, bm), jnp.uint32(0)))
    x_lo = x_lo.astype(BF16)
    x_hi = x_hi.astype(BF16)
    half = x_lo.shape[1]
    h1 = (jnp.dot(x_lo, w1b[:half, :], preferred_element_type=F32)
          + jnp.dot(x_hi, w1b[half:, :], preferred_element_type=F32))
    h3 = (jnp.dot(x_lo, w3b[:half, :], preferred_element_type=F32)
          + jnp.dot(x_hi, w3b[half:, :], preferred_element_type=F32))
    hb = (h1 * jax.nn.sigmoid(h1) * h3).astype(BF16)
    y = jnp.dot(hb, w2b[...], preferred_element_type=F32)

    @pl.when(first_ref[it] == 1)
    def _():
        _store_slabs(o_ref, _pack_rows(y))

    @pl.when(first_ref[it] == 0)
    def _():
        o_lo, o_hi = _unpack_rows(_load_slabs(o_ref, bm))
        _store_slabs(o_ref, _pack_rows(y + jnp.concatenate([o_lo, o_hi], axis=1)))


def _gmm(xs, items, w1, w3, w2, bm):
    A = xs.shape[0] // SLAB
    D = w1.shape[1]
    F = w1.shape[2]
    n_items = items[0].shape[0]
    grid_spec = pltpu.PrefetchScalarGridSpec(
        num_scalar_prefetch=6,
        grid=(n_items,),
        in_specs=[pl.BlockSpec((bm * SLAB, LANES), lambda i, blk, ex, *_: (blk[i], 0)),
                  pl.BlockSpec((None, D, F), lambda i, blk, ex, *_: (ex[i], 0, 0)),
                  pl.BlockSpec((None, D, F), lambda i, blk, ex, *_: (ex[i], 0, 0)),
                  pl.BlockSpec((None, F, D), lambda i, blk, ex, *_: (ex[i], 0, 0))],
        out_specs=pl.BlockSpec((bm * SLAB, LANES), lambda i, blk, ex, *_: (blk[i], 0)),
        scratch_shapes=[pltpu.VMEM((D, F), BF16), pltpu.VMEM((D, F), BF16),
                        pltpu.VMEM((F, D), BF16)],
    )
    return pl.pallas_call(
        functools.partial(_gmm_kernel, bm=bm),
        grid_spec=grid_spec,
        out_shape=jax.ShapeDtypeStruct((A * SLAB, LANES), jnp.uint32),
        compiler_params=_cparams(("arbitrary",)),
        name="gmm",
    )(*items, xs, w1, w3, w2)


def _work_items(counts, bm, n_blocks):
    n_items = n_blocks + N_EXPERTS - 1
    ends = jnp.cumsum(counts)
    starts = ends - counts
    nb = jnp.where(counts > 0, (ends - 1) // bm - starts // bm + 1, 0)
    item_end = jnp.cumsum(nb)
    item_start = item_end - nb
    n_real = item_end[-1]
    i = jnp.arange(n_items, dtype=jnp.int32)
    e = jnp.minimum(jnp.sum(item_end[None, :] <= i[:, None], axis=1), N_EXPERTS - 1).astype(jnp.int32)
    onehot = e[:, None] == jnp.arange(N_EXPERTS, dtype=jnp.int32)[None, :]
    pick = lambda v: jnp.sum(jnp.where(onehot, v[None, :], 0), axis=1)
    start_e, end_e = pick(starts), pick(ends)
    blk = start_e // bm + (i - pick(item_start))
    lo = jnp.clip(start_e - blk * bm, 0, bm)
    hi = jnp.clip(end_e - blk * bm, 0, bm)
    real = i < n_real
    blk = jnp.where(real, blk, n_blocks - 1).astype(jnp.int32)
    lo = jnp.where(real, lo, 0).astype(jnp.int32)
    hi = jnp.where(real, hi, 0).astype(jnp.int32)
    one = jnp.ones((1,), jnp.int32)
    first = jnp.concatenate([one, (blk[1:] != blk[:-1]).astype(jnp.int32)])
    new_e = jnp.concatenate([one, (e[1:] != e[:-1]).astype(jnp.int32)])
    return blk, e, lo, hi, first, new_e


def _combine_kernel(dest_ref, nxt_ref, ys_ref, w_ref, x1_ref, h_ref, w1_ref, w3_ref, w2_ref,
                    gt2_ref, g_ref, o_ref, buf, y_scr, sems, *, tc):
    i = pl.program_id(0)
    n = pl.num_programs(0)
    slot = i % 2

    def gather(idx_ref, s):
        def issue(j, carry):
            for kk in range(TOP_K):
                _row_copy(ys_ref, idx_ref[0, kk * tc + j], buf.at[s, kk], j,
                          sems.at[s]).start(priority=kk % 2)
            return carry

        lax.fori_loop(0, tc, issue, 0, unroll=4)

    @pl.when(i == 0)
    def _():
        gather(dest_ref, 0)

    for s in range(2):
        @pl.when(jnp.logical_and(i + 1 < n, slot == 1 - s))
        def _():
            gather(nxt_ref, s)

    h_lo, h_hi = _unpack_rows(_load_slabs(h_ref, tc))
    h_lo = h_lo.astype(BF16)
    h_hi = h_hi.astype(BF16)
    half = h_lo.shape[1]
    h1 = (jnp.dot(h_lo, w1_ref[:half, :], preferred_element_type=F32)
          + jnp.dot(h_hi, w1_ref[half:, :], preferred_element_type=F32))
    h3 = (jnp.dot(h_lo, w3_ref[:half, :], preferred_element_type=F32)
          + jnp.dot(h_hi, w3_ref[half:, :], preferred_element_type=F32))
    y_scr[...] = jnp.dot((h1 * jax.nn.sigmoid(h1) * h3).astype(BF16), w2_ref[...],
                         preferred_element_type=F32)

    for kk in range(TOP_K):
        pltpu.make_async_copy(ys_ref.at[pl.ds(0, tc * SLAB), :], buf.at[slot, kk],
                              sems.at[slot]).wait()
    rows = 2 * SUBLANES
    for r0 in range(0, tc, rows):
        for kk in range(TOP_K):
            u = jnp.concatenate([buf[slot, kk, pl.ds(r0 * SLAB + c, rows, stride=SLAB), :]
                                 for c in range(SLAB)], axis=1)
            e_lo, e_hi = _unpack_rows(u)
            wk = w_ref[r0:r0 + rows, kk:kk + 1]
            moe_lo = wk * e_lo if kk == 0 else moe_lo + wk * e_lo
            moe_hi = wk * e_hi if kk == 0 else moe_hi + wk * e_hi
        y_scr[r0:r0 + rows, :] += jnp.concatenate([moe_lo, moe_hi], axis=1)
    o_ref[...] = x1_ref[...] + gt2_ref[...] * _rms(y_scr[...], g_ref[...])


def _combine(ys, dest_tiles, w_tok, x1, h2, w1s, w3s, w2s, mod4, g_post, S, tc):
    T, D = x1.shape
    F = w1s.shape[1]
    per_b = S // tc
    nt = T // tc
    row = lambda: pl.BlockSpec((tc, D), lambda i: (i, 0))
    return pl.pallas_call(
        functools.partial(_combine_kernel, tc=tc),
        grid=(nt,),
        in_specs=[pl.BlockSpec((None, 1, TOP_K * tc), lambda i: (i, 0, 0),
                               memory_space=pltpu.SMEM),
                  pl.BlockSpec((None, 1, TOP_K * tc), lambda i: (jnp.minimum(i + 1, nt - 1), 0, 0),
                               memory_space=pltpu.SMEM),
                  pl.BlockSpec(memory_space=pl.ANY),
                  pl.BlockSpec((tc, TOP_K), lambda i: (i, 0)),
                  row(), pl.BlockSpec((tc * SLAB, LANES), lambda i: (i, 0)),
                  pl.BlockSpec((D, F), lambda i: (0, 0)),
                  pl.BlockSpec((D, F), lambda i: (0, 0)),
                  pl.BlockSpec((F, D), lambda i: (0, 0)),
                  pl.BlockSpec((None, None, 1, D), lambda i: (i // per_b, 5, 0, 0)),
                  pl.BlockSpec((1, D), lambda i: (0, 0))],
        out_specs=row(),
        out_shape=jax.ShapeDtypeStruct((T, D), F32),
        scratch_shapes=[pltpu.VMEM((2, TOP_K, tc * SLAB, LANES), jnp.uint32),
                        pltpu.VMEM((tc, D), F32), pltpu.SemaphoreType.DMA((2,))],
        compiler_params=_cparams(("arbitrary",)),
        name="combine",
    )(dest_tiles, dest_tiles, ys, w_tok, x1, h2, w1s, w3s, w2s, mod4, g_post.reshape(1, D))


def _lambda_init(layer):
    return 0.8 - 0.6 * math.exp(-0.3 * layer)


def kernel(x, c, w_ada, b_ada, g_pre_mix, w_in, conv_w, conv_b, lru_wa, lru_ba, lru_wx, lru_bx,
           lru_lambda, lam_q1, lam_k1, lam_q2, lam_k2, g_subln, w_proj_rnn, w_proj_att, w_out,
           g_post_mix, g_pre_ffn, w_router, e_bias, w1_e, w3_e, w2_e, w1_s, w3_s, w2_s,
           g_post_ffn):
    B, S, D = x.shape
    T = B * S
    depth = w_ada.shape[0]
    slopes = np.exp2(-8.0 * np.arange(1, N_HEADS + 1, dtype=np.float32) / N_HEADS)
    kbias = _alibi_tables(slopes, S)
    tt = _tile(S, 256)
    bm = _tile(T * TOP_K, 512)
    n_blocks = T * TOP_K // bm

    x2 = x.reshape(T, D)
    for l in range(depth):
        lam_init = _lambda_init(l)
        mod4 = _ada(c, w_ada[l], b_ada[l]).reshape(B, 6, 1, D)

        proj = _inproj(x2, g_pre_mix[l], mod4, w_in[l].astype(BF16), S)
        yr = _lru(proj, conv_w[l], conv_b[l], lru_wa[l].astype(BF16), lru_ba[l],
                  lru_wx[l].astype(BF16), lru_bx[l], lru_lambda[l], B, S)
        lamv = jnp.stack([lam_q1[l], lam_k1[l], lam_q2[l], lam_k2[l]])
        ao = _attn(proj, kbias, lamv, g_subln[l], B, S, lam_init)
        x1, h2p, top_e, top_w, rank, cnt = _mixout(
            x2, yr, ao, proj, w_proj_rnn[l].astype(BF16), w_proj_att[l].astype(BF16),
            w_out[l].astype(BF16), g_post_mix[l], g_pre_ffn[l], mod4, w_router[l].T, e_bias[l], S)
        counts = cnt[:, 0].astype(jnp.int32)
        offs = jnp.cumsum(counts) - counts
        dest = _slots(offs, top_e, rank)
        dest_tiles = dest.reshape(TOP_K, T // tt, tt).transpose(1, 0, 2).reshape(T // tt, 1,
                                                                                  TOP_K * tt)
        xs = _dispatch(h2p, dest_tiles, tt)
        items = _work_items(counts, bm, n_blocks)
        ys = _gmm(xs, items, w1_e[l], w3_e[l], w2_e[l], bm)
        x2 = _combine(ys, dest_tiles, top_w.T, x1, h2p, w1_s[l].astype(BF16),
                      w3_s[l].astype(BF16), w2_s[l].astype(BF16), mod4, g_post_ffn[l], S, tt)
    return x2.reshape(B, S, D)
```

```python
import functools
import math

import jax
import jax.numpy as jnp
import numpy as np
from jax import lax
from jax.experimental import pallas as pl
from jax.experimental.pallas import tpu as pltpu

F32 = jnp.float32
BF16 = jnp.bfloat16

EPS = 1e-6
N_HEADS = 8
HEAD_DIM = 64
V_DIM = 2 * HEAD_DIM
LRU_BLOCKS = 8
CONV_W = 4
LRU_C = 8.0
N_EXPERTS = 64
TOP_K = 8
N_GROUPS = 8
GROUP_SIZE = N_EXPERTS // N_GROUPS
TOPK_GROUPS = 4
ROUTED_SCALE = 2.5

LANES = 128
SUBLANES = 8
VMEM_LIMIT = 48 * 1024 * 1024


def _cparams(sem):
    return pltpu.CompilerParams(dimension_semantics=sem, vmem_limit_bytes=VMEM_LIMIT)


def _tile(n, pref):
    t = min(n, pref)
    while n % t:
        t //= 2
    return t


def _rms(x, g):
    return x * lax.rsqrt(jnp.mean(x * x, axis=-1, keepdims=True) + EPS) * g


def _ada_kernel(c_ref, w_ref, b_ref, o_ref):
    c = c_ref[...]
    cond = c * jax.nn.sigmoid(c)
    o_ref[...] = jnp.dot(cond, w_ref[...], preferred_element_type=F32) + b_ref[...]


def _ada(c, w, b):
    B, D = c.shape
    N = w.shape[1]
    tn = _tile(N, 1024)
    return pl.pallas_call(
        _ada_kernel,
        grid=(N // tn,),
        in_specs=[pl.BlockSpec((B, D), lambda j: (0, 0)),
                  pl.BlockSpec((D, tn), lambda j: (0, j)),
                  pl.BlockSpec((1, tn), lambda j: (0, j))],
        out_specs=pl.BlockSpec((B, tn), lambda j: (0, j)),
        out_shape=jax.ShapeDtypeStruct((B, N), F32),
        compiler_params=_cparams(("parallel",)),
        name="ada",
    )(c, w, b.reshape(1, N))


LOG2E = 1.4426950408889634
Q_COL_BLOCK = 2
Q_PRESCALE = HEAD_DIM ** -0.5 * LOG2E


def _inproj_kernel(x_ref, g_ref, sh_ref, sc_ref, w_ref, o_ref, h_scr):
    @pl.when(pl.program_id(1) == 0)
    def _():
        h = _rms(x_ref[...], g_ref[...]) * (1.0 + sc_ref[...]) + sh_ref[...]
        h_scr[...] = h.astype(BF16)

    r = jnp.dot(h_scr[...], w_ref[...], preferred_element_type=F32)
    r = r * jnp.where(pl.program_id(1) == Q_COL_BLOCK, Q_PRESCALE, 1.0)
    o_ref[...] = r.astype(o_ref.dtype)


def _inproj(x2, g, mod4, w_bf, S):
    T, D = x2.shape
    N = w_bf.shape[1]
    tm = _tile(S, 2048)
    tn = 1024
    per_b = S // tm
    return pl.pallas_call(
        _inproj_kernel,
        grid=(T // tm, N // tn),
        in_specs=[pl.BlockSpec((tm, D), lambda i, j: (i, 0)),
                  pl.BlockSpec((1, D), lambda i, j: (0, 0)),
                  pl.BlockSpec((None, None, 1, D), lambda i, j: (i // per_b, 0, 0, 0)),
                  pl.BlockSpec((None, None, 1, D), lambda i, j: (i // per_b, 1, 0, 0)),
                  pl.BlockSpec((D, tn), lambda i, j: (0, j))],
        out_specs=pl.BlockSpec((tm, tn), lambda i, j: (i, j)),
        out_shape=jax.ShapeDtypeStruct((T, N), BF16),
        scratch_shapes=[pltpu.VMEM((tm, D), BF16)],
        compiler_params=_cparams(("parallel", "arbitrary")),
        name="inproj",
    )(x2, g.reshape(1, D), mod4, mod4, w_bf)


def _lru_kernel(xr_ref, gr_ref, cw_ref, cb_ref, wa_ref, ba_ref, wx_ref, bx_ref, lam_ref,
                o_ref, xc_scr, prev_scr, h_scr, *, ts):
    s = pl.program_id(1)

    @pl.when(s == 0)
    def _():
        prev_scr[...] = jnp.zeros_like(prev_scr)
        h_scr[...] = jnp.zeros_like(h_scr)

    x = xr_ref[...].astype(F32)
    prev = prev_scr[...]
    row8 = lax.broadcasted_iota(jnp.int32, (SUBLANES, 1), 0)
    acc = cb_ref[...] + cw_ref[CONV_W - 1:CONV_W, :] * x
    xc_scr[...] = acc
    top = cb_ref[...] + cw_ref[CONV_W - 1:CONV_W, :] * x[0:SUBLANES, :]
    for j in range(1, CONV_W):
        wj = cw_ref[CONV_W - 1 - j:CONV_W - j, :]
        rj = pltpu.roll(x, j, axis=0)
        xc_scr[...] += wj * rj
        pj = pltpu.roll(prev, j, axis=0)
        top += wj * jnp.where(row8 < j, pj, rj[0:SUBLANES, :])
    xc_scr[0:SUBLANES, :] = top
    prev_scr[...] = x[ts - SUBLANES:ts, :]

    row = lax.broadcasted_iota(jnp.int32, (ts, 1), 0)
    is_first = jnp.logical_and(row == 0, s == 0)
    sub = lax.broadcasted_iota(jnp.int32, (1, SUBLANES, 1), 1)
    for n in range(LRU_BLOCKS):
        cols = slice(n * LANES, (n + 1) * LANES)
        xc = xc_scr[:, cols]
        xb = xc.astype(BF16)
        r = jax.nn.sigmoid(jnp.dot(xb, wa_ref[n], preferred_element_type=F32) + ba_ref[:, cols])
        i = jax.nn.sigmoid(jnp.dot(xb, wx_ref[n], preferred_element_type=F32) + bx_ref[:, cols])
        lam = lam_ref[:, cols]
        softplus_neg = jnp.maximum(-lam, 0.0) + jnp.log1p(jnp.exp(-jnp.abs(lam)))
        log_a = (-LRU_C * softplus_neg) * r
        a = jnp.exp(log_a)
        m2 = 1.0 - a * a
        mult = jnp.where(m2 > 0.0, m2 * lax.rsqrt(m2), 0.0)
        mult = jnp.where(is_first, 1.0, mult)
        u = mult * (i * xc)
        a = a.reshape(ts // SUBLANES, SUBLANES, LANES)
        u = u.reshape(ts // SUBLANES, SUBLANES, LANES)
        for d in (1, 2, 4):
            keep = sub >= d
            a_sh = jnp.where(keep, pltpu.roll(a, d, axis=1), 1.0)
            u_sh = jnp.where(keep, pltpu.roll(u, d, axis=1), 0.0)
            u = u + a * u_sh
            a = a * a_sh
        a = a.reshape(ts, LANES)
        u = u.reshape(ts, LANES)
        gate = jax.nn.gelu(gr_ref[:, cols].astype(F32))
        carry = h_scr[:, cols]
        step = 2 * SUBLANES
        for g in range(ts // step):
            r0 = g * step
            h0 = u[r0:r0 + SUBLANES, :] + a[r0:r0 + SUBLANES, :] * carry
            carry = h0[SUBLANES - 1:SUBLANES, :]
            h1 = u[r0 + SUBLANES:r0 + step, :] + a[r0 + SUBLANES:r0 + step, :] * carry
            carry = h1[SUBLANES - 1:SUBLANES, :]
            hg = jnp.concatenate([h0, h1], axis=0) * gate[r0:r0 + step, :]
            o_ref[r0:r0 + step, cols] = hg.astype(o_ref.dtype)
        h_scr[:, cols] = carry


def _lru(proj, conv_w, conv_b, wa_bf, ba, wx_bf, bx, lam, B, S):
    C = conv_w.shape[1]
    ts = _tile(S, 256)
    ns = S // ts
    vec = lambda: pl.BlockSpec((1, C), lambda b, s: (0, 0))
    blk = lambda: pl.BlockSpec((LRU_BLOCKS, LANES, LANES), lambda b, s: (0, 0, 0))
    return pl.pallas_call(
        functools.partial(_lru_kernel, ts=ts),
        grid=(B, ns),
        in_specs=[pl.BlockSpec((ts, C), lambda b, s: (b * ns + s, 0)),
                  pl.BlockSpec((ts, C), lambda b, s: (b * ns + s, 1)),
                  pl.BlockSpec((CONV_W, C), lambda b, s: (0, 0)),
                  vec(), blk(), vec(), blk(), vec(), vec()],
        out_specs=pl.BlockSpec((ts, C), lambda b, s: (b * ns + s, 0)),
        out_shape=jax.ShapeDtypeStruct((B * S, C), BF16),
        scratch_shapes=[pltpu.VMEM((ts, C), F32), pltpu.VMEM((SUBLANES, C), F32),
                        pltpu.VMEM((1, C), F32)],
        compiler_params=_cparams(("parallel", "arbitrary")),
        name="lru",
    )(proj, proj, conv_w, conv_b.reshape(1, C), wa_bf, ba.reshape(1, C), wx_bf, bx.reshape(1, C),
      lam.reshape(1, C))


BIAS_LANES = 3
ACC_ROWS = V_DIM + 2 * SUBLANES


def _alibi_tables(slopes, S):
    def top16(x):
        return (x.view(np.uint32) & np.uint32(0xFFFF0000)).view(np.float32)

    pos = np.arange(S, dtype=np.float32)
    b = (slopes.astype(np.float32) * np.float32(LOG2E))[:, None] * pos[None, :]
    hi = top16(b)
    mid = top16(b - hi)
    lo = top16(b - hi - mid)
    half = np.zeros(b.shape + (HEAD_DIM,), np.float32)
    half[..., 0], half[..., 1], half[..., 2] = hi, mid, lo
    zero = np.zeros_like(half)
    table = np.stack([np.concatenate([zero, half], axis=-1),
                      np.concatenate([half, zero], axis=-1)], axis=1)
    return jnp.asarray(table, dtype=BF16)


def _lane_masks():
    lane = lax.broadcasted_iota(jnp.int32, (1, V_DIM), 1)
    own = (jnp.where(lane < HEAD_DIM, 1.0, 0.0), jnp.where(lane >= HEAD_DIM, 1.0, 0.0))
    ones_row = (jnp.where(jnp.logical_and(lane >= HEAD_DIM, lane < HEAD_DIM + BIAS_LANES), 1.0, 0.0),
                jnp.where(lane < BIAS_LANES, 1.0, 0.0))
    return own, ones_row


def _attn_kernel(lamv_ref, gsub_ref, q_ref, k_ref, v_ref, kb_ref, o_ref,
                 vt_scr, kx_scr, *scratch, tq, lam_init):
    tk = tq
    own, _ = _lane_masks()
    p_scr, acc_scr = scratch[2], scratch[6]
    nk = vt_scr.shape[0] - 1
    vt_scr[nk] = jnp.zeros(vt_scr.shape[1:], BF16)
    p_scr[...] = jnp.zeros_like(p_scr)
    acc_scr[...] = jnp.zeros_like(acc_scr)
    for j in range(nk):
        rows = slice(j * tk, (j + 1) * tk)
        vt_scr[j, 0:V_DIM, :] = v_ref[rows, :].astype(F32).T.astype(BF16)
        vt_scr[j, V_DIM:ACC_ROWS, :] = jnp.ones((ACC_ROWS - V_DIM, tk), BF16)
        kf = k_ref[rows, :].astype(F32)
        for mp in range(2):
            kx_scr[mp, rows, :] = (kf * own[mp] + kb_ref[mp, rows, :].astype(F32)).astype(BF16)

    def q_block(qi, carry):
        _attn_q_block(qi, lamv_ref, gsub_ref, q_ref, o_ref, vt_scr, kx_scr, *scratch,
                      tq=tq, lam_init=lam_init)
        return carry

    lax.fori_loop(0, q_ref.shape[0] // tq, q_block, 0)


def _attn_q_block(qi, lamv_ref, gsub_ref, q_ref, o_ref,
                  vt_scr, kx_scr, qx_scr, s_scr, p_scr, a_scr, mx_scr, m_scr, acc_scr, *, tq,
                  lam_init):
    tk = tq
    q_rows = pl.ds(pl.multiple_of(qi * tq, tq), tq)
    own, ones_row = _lane_masks()
    q = q_ref[q_rows, :].astype(F32)
    for mp in range(2):
        qx_scr[mp] = (q * own[mp] + ones_row[mp]).astype(BF16)
    m_scr[...] = jnp.full_like(m_scr, -jnp.inf)
    zero_blk = vt_scr.shape[0] - 1
    row8 =lax.broadcasted_iota(jnp.int32, (SUBLANES, LANES), 0)
    col = lax.broadcasted_iota(jnp.int32, (1, LANES), 1)
    neg_inf = jnp.float32(-jnp.inf)

    def tree(op, parts):
        parts = [p for p in parts if p is not None]
        while len(parts) > 1:
            parts = [op(parts[i], parts[i + 1]) if i + 1 < len(parts) else parts[i]
                     for i in range(0, len(parts), 2)]
        return parts[0]

    def scores(j, par):
        start = pl.multiple_of(j * tk, tk)
        for mp in range(2):
            s = lax.dot_general(kx_scr[mp, pl.ds(start, tk), :], qx_scr[mp],
                                (((1,), (1,)), ((), ())), preferred_element_type=F32)
            for c in range(tq // LANES):
                cols = slice(c * LANES, (c + 1) * LANES)
                s_scr[par, mp, c] = s[:, cols]
                accs = [None] * 4
                for i in range(tk // SUBLANES):
                    t = s[i * SUBLANES:(i + 1) * SUBLANES, cols]
                    accs[i % 4] = t if accs[i % 4] is None else jnp.maximum(accs[i % 4], t)
                mx_scr[par, mp, :, cols] = jnp.max(tree(jnp.maximum, accs), axis=0, keepdims=True)

    def softmax(par, masked):
        for mp in range(2):
            for c in range(tq // LANES):
                cols = slice(c * LANES, (c + 1) * LANES)
                lo_col, hi_col = c * LANES, (c + 1) * LANES - 1
                s_c = s_scr.at[par, mp, c]
                p_c = p_scr.at[par, mp, c]
                if masked:
                    accs = [None] * 4
                    for i in range(tk // SUBLANES):
                        r0 = i * SUBLANES
                        if r0 > hi_col:
                            continue
                        t = s_c[r0:r0 + SUBLANES, :]
                        if r0 + SUBLANES - 1 > lo_col:
                            t = jnp.where(row8 + r0 <= col + lo_col, t, neg_inf)
                            s_c[r0:r0 + SUBLANES, :] = t
                        accs[i % 4] = t if accs[i % 4] is None else jnp.maximum(accs[i % 4], t)
                    mx = jnp.max(tree(jnp.maximum, accs), axis=0, keepdims=True)
                else:
                    mx = mx_scr[par, mp, :, cols]
                m_old = m_scr[mp, :, cols]
                m_new = jnp.maximum(m_old, mx)
                a_scr[par, mp, :, cols] = jnp.exp2(m_old - m_new)
                m_scr[mp, :, cols] = m_new
                for i in range(tk // (2 * SUBLANES)):
                    r0 = i * 2 * SUBLANES
                    if masked and r0 > hi_col:
                        p_c[r0:r0 + 2 * SUBLANES, :] = jnp.zeros((2 * SUBLANES, LANES), BF16)
                        continue
                    p = jnp.exp2(s_c[r0:r0 + 2 * SUBLANES, :] - m_new)
                    p_c[r0:r0 + 2 * SUBLANES, :] = p.astype(BF16)

    def values(j, par):
        for mp in range(2):
            p = jnp.concatenate([p_scr[par, mp, c] for c in range(tq // LANES)], axis=1)
            pv = jnp.dot(vt_scr[j], p, preferred_element_type=F32)
            acc_scr[mp] = a_scr[par, mp] * acc_scr[mp] + pv

    a_scr[1] = jnp.ones(a_scr.shape[1:], F32)
    scores(0, 0)

    def stage(j, par):
        scores(j + 1, 1 - par)
        softmax(par, False)
        values(jnp.where(j >= 1, j - 1, zero_blk), 1 - par)

    def stage_pair(i, carry):
        stage(2 * i, 0)
        stage(2 * i + 1, 1)
        return carry

    def tail(par):
        softmax(par, True)
        values(jnp.where(qi >= 1, qi - 1, zero_blk), 1 - par)
        values(qi, par)

    lax.fori_loop(0, qi // 2, stage_pair, 0)

    @pl.when(qi % 2 == 1)
    def _():
        stage(qi - 1, 0)
        tail(1)

    @pl.when(qi % 2 == 0)
    def _():
        tail(0)

    lv = lamv_ref[...]
    lam = (jnp.exp(jnp.sum(lv[0:1, :] * lv[1:2, :], axis=-1, keepdims=True))
           - jnp.exp(jnp.sum(lv[2:3, :] * lv[3:4, :], axis=-1, keepdims=True)) + lam_init)
    o0 = acc_scr[0, 0:V_DIM, :] * (1.0 / acc_scr[0, V_DIM:V_DIM + 1, :])
    o1 = acc_scr[1, 0:V_DIM, :] * (1.0 / acc_scr[1, V_DIM:V_DIM + 1, :])
    o_t = o0 - lam * o1
    o_t = o_t * lax.rsqrt(jnp.mean(o_t * o_t, axis=0, keepdims=True) + EPS) * gsub_ref[...]
    o_ref[q_rows, :] = (o_t * (1.0 - lam_init)).T.astype(o_ref.dtype)


def _attn(proj, kbias, lamv, g_subln, B, S, lam_init):
    tq = _tile(S, 512)
    nq = S // tq
    qc, kc, vc = 2 * 8, 3 * 8, 4 * 8
    return pl.pallas_call(
        functools.partial(_attn_kernel, tq=tq, lam_init=lam_init),
        grid=(B, N_HEADS),
        in_specs=[pl.BlockSpec((4, HEAD_DIM), lambda b, h: (0, 0)),
                  pl.BlockSpec((V_DIM, 1), lambda b, h: (0, 0)),
                  pl.BlockSpec((S, V_DIM), lambda b, h: (b, qc + h)),
                  pl.BlockSpec((S, V_DIM), lambda b, h: (b, kc + h)),
                  pl.BlockSpec((S, V_DIM), lambda b, h: (b, vc + h)),
                  pl.BlockSpec((None, 2, S, V_DIM), lambda b, h: (h, 0, 0, 0))],
        out_specs=pl.BlockSpec((S, V_DIM), lambda b, h: (b, h)),
        out_shape=jax.ShapeDtypeStruct((B * S, N_HEADS * V_DIM), BF16),
        scratch_shapes=[pltpu.VMEM((nq + 1, ACC_ROWS, tq), BF16), pltpu.VMEM((2, S, V_DIM), BF16),
                        pltpu.VMEM((2, tq, V_DIM), BF16),
                        pltpu.VMEM((2, 2, tq // LANES, tq, LANES), F32),
                        pltpu.VMEM((2, 2, tq // LANES, tq, LANES), BF16),
                        pltpu.VMEM((2, 2, 1, tq), F32), pltpu.VMEM((2, 2, 1, tq), F32),
                        pltpu.VMEM((2, 1, tq), F32),
                        pltpu.VMEM((2, ACC_ROWS, tq), F32)],
        compiler_params=_cparams(("parallel", "parallel")),
        name="attn",
    )(lamv, g_subln.reshape(V_DIM, 1), proj, proj, proj, kbias)


def _pack_rows(x):
    half = x.shape[1] // 2
    lo = lax.bitcast_convert_type(x[:, :half].astype(BF16).astype(F32), jnp.uint32)
    hi = lax.bitcast_convert_type(x[:, half:].astype(BF16).astype(F32), jnp.uint32)
    return (hi & jnp.uint32(0xFFFF0000)) | (lo >> 16)


def _unpack_rows(u):
    lo = lax.bitcast_convert_type(u << 16, F32)
    hi = lax.bitcast_convert_type(u & jnp.uint32(0xFFFF0000), F32)
    return lo, hi


SLAB = 4


def _load_slabs(ref, n):
    return jnp.concatenate([ref[pl.ds(c, n, stride=SLAB), :] for c in range(SLAB)], axis=1)


def _store_slabs(ref, u):
    n = u.shape[0]
    for c in range(SLAB):
        ref[pl.ds(c, n, stride=SLAB), :] = u[:, c * LANES:(c + 1) * LANES]


def _mixout_kernel(x_ref, yr_ref, ao_ref, ga_ref, gb_ref, wr_ref, wa_ref, wo_ref,
                   gpost_ref, gt1_ref, gpre_ref, sh2_ref, sc2_ref, wrt_ref, eb_ref,
                   x1_ref, h2p_ref, e_ref, w_ref, r_ref, cnt_ref, cnt_scr):
    ya = jnp.dot(yr_ref[...], wr_ref[...], preferred_element_type=F32)
    yb = jnp.dot(ao_ref[...], wa_ref[...], preferred_element_type=F32)
    merged = (jax.nn.sigmoid(ga_ref[...].astype(F32)) * ya
              + jax.nn.sigmoid(gb_ref[...].astype(F32)) * yb)
    y = jnp.dot(merged.astype(BF16), wo_ref[...], preferred_element_type=F32)
    x1 = x_ref[...] + gt1_ref[...] * _rms(y, gpost_ref[...])
    x1_ref[...] = x1
    h2 = _rms(x1, gpre_ref[...]) * (1.0 + sc2_ref[...]) + sh2_ref[...]
    _store_slabs(h2p_ref, _pack_rows(h2))
    _route(h2, wrt_ref, eb_ref, e_ref, w_ref, r_ref, cnt_ref, cnt_scr)


def _mixout(x2, yr, ao, proj, wr_bf, wa_bf, wo_bf, g_post, g_pre, mod4, w_router_t, e_bias, S):
    T, D = x2.shape
    tm = _tile(S, 512)
    per_b = S // tm
    gac, gbc = 5, 6
    row = lambda: pl.BlockSpec((tm, D), lambda i: (i, 0))
    wsp = lambda: pl.BlockSpec((D, D), lambda i: (0, 0))
    vec = lambda: pl.BlockSpec((1, D), lambda i: (0, 0))
    modv = lambda j: pl.BlockSpec((None, None, 1, D), lambda i: (i // per_b, j, 0, 0))
    topk = lambda: pl.BlockSpec((TOP_K, tm), lambda i: (0, i))
    return pl.pallas_call(
        _mixout_kernel,
        grid=(T // tm,),
        in_specs=[row(), row(), row(),
                  pl.BlockSpec((tm, D), lambda i: (i, gac)),
                  pl.BlockSpec((tm, D), lambda i: (i, gbc)),
                  wsp(), wsp(), wsp(), vec(), modv(2), vec(), modv(3), modv(4),
                  pl.BlockSpec((N_EXPERTS, D), lambda i: (0, 0)),
                  pl.BlockSpec((N_EXPERTS, 1), lambda i: (0, 0))],
        out_specs=[row(), pl.BlockSpec((tm * SLAB, LANES), lambda i: (i, 0)),
                   topk(), topk(), topk(),
                   pl.BlockSpec((N_EXPERTS, LANES), lambda i: (0, 0))],
        out_shape=[jax.ShapeDtypeStruct((T, D), F32),
                   jax.ShapeDtypeStruct((T * SLAB, LANES), jnp.uint32),
                   jax.ShapeDtypeStruct((TOP_K, T), jnp.int32),
                   jax.ShapeDtypeStruct((TOP_K, T), F32),
                   jax.ShapeDtypeStruct((TOP_K, T), jnp.int32),
                   jax.ShapeDtypeStruct((N_EXPERTS, LANES), F32)],
        scratch_shapes=[pltpu.VMEM((N_EXPERTS, 1), F32)],
        compiler_params=_cparams(("arbitrary",)),
        name="mixout",
    )(x2, yr, ao, proj, proj, wr_bf, wa_bf, wo_bf, g_post.reshape(1, D), mod4,
      g_pre.reshape(1, D), mod4, mod4, w_router_t, e_bias.reshape(N_EXPERTS, 1))


def _first_argmax(vals, ids, sentinel):
    m = jnp.max(vals, axis=0, keepdims=True)
    idx = jnp.min(jnp.where(vals == m, ids, sentinel), axis=0, keepdims=True)
    return m, idx


def _route(h2, wr_ref, eb_ref, e_ref, w_ref, r_ref, cnt_ref, cnt_scr):
    tm = h2.shape[0]
    step = pl.program_id(0)

    @pl.when(step == 0)
    def _():
        cnt_scr[...] = jnp.zeros_like(cnt_scr)

    def split(x):
        hi = x.astype(BF16)
        return hi, (x - hi.astype(F32)).astype(BF16)

    nt = (((1,), (1,)), ((), ()))
    w_hi, w_lo = split(wr_ref[...])
    h_hi, h_lo = split(h2)
    logits = (lax.dot_general(w_hi, h_hi, nt, preferred_element_type=F32)
              + lax.dot_general(w_hi, h_lo, nt, preferred_element_type=F32)
              + lax.dot_general(w_lo, h_hi, nt, preferred_element_type=F32))
    scores = jax.nn.sigmoid(logits)
    choice = scores + eb_ref[...]
    i8 = lax.broadcasted_iota(jnp.int32, (GROUP_SIZE, tm), 0)
    neg_inf = jnp.float32(-jnp.inf)

    slabs = [choice[g * GROUP_SIZE:(g + 1) * GROUP_SIZE, :] for g in range(N_GROUPS)]
    sc_slabs = [scores[g * GROUP_SIZE:(g + 1) * GROUP_SIZE, :] for g in range(N_GROUPS)]

    gs = jnp.zeros((N_GROUPS, tm), F32)
    for g in range(N_GROUPS):
        m1, idx1 = _first_argmax(slabs[g], i8, GROUP_SIZE)
        m2 = jnp.max(jnp.where(i8 == idx1, neg_inf, slabs[g]), axis=0, keepdims=True)
        gs = jnp.where(i8 == g, m1 + m2, gs)

    sel = jnp.zeros((N_GROUPS, tm), jnp.int32)
    cur = gs
    for _ in range(TOPK_GROUPS):
        _, idx = _first_argmax(cur, i8, N_GROUPS)
        hit = i8 == idx
        sel = jnp.where(hit, 1, sel)
        cur = jnp.where(hit, neg_inf, cur)

    masked = [jnp.where(sel[g:g + 1, :] > 0, slabs[g], neg_inf) for g in range(N_GROUPS)]
    ids = [i8 + g * GROUP_SIZE for g in range(N_GROUPS)]
    onehot = [jnp.zeros((GROUP_SIZE, tm), F32) for _ in range(N_GROUPS)]
    picks = []
    wts = []
    for _ in range(TOP_K):
        m = jnp.max(functools.reduce(jnp.maximum, masked), axis=0, keepdims=True)
        idx = jnp.min(functools.reduce(
            jnp.minimum, [jnp.where(c == m, i, N_EXPERTS) for c, i in zip(masked, ids)]),
            axis=0, keepdims=True)
        w8 = jnp.zeros((GROUP_SIZE, tm), F32)
        for g in range(N_GROUPS):
            hit = ids[g] == idx
            w8 = w8 + jnp.where(hit, sc_slabs[g], 0.0)
            masked[g] = jnp.where(hit, neg_inf, masked[g])
            onehot[g] = jnp.where(hit, 1.0, onehot[g])
        picks.append(idx)
        wts.append(jnp.sum(w8, axis=0, keepdims=True))

    wsum = functools.reduce(lambda a, b: a + b, wts)
    norm = ROUTED_SCALE / (wsum + 1e-20)

    t_row = lax.broadcasted_iota(jnp.int32, (tm, tm), 0)
    t_col = lax.broadcasted_iota(jnp.int32, (tm, tm), 1)
    before = jnp.where(t_row < t_col, 1.0, 0.0).astype(BF16)
    cum = [jnp.dot(onehot[g].astype(BF16), before, preferred_element_type=F32)
           + cnt_scr[g * GROUP_SIZE:(g + 1) * GROUP_SIZE, :] for g in range(N_GROUPS)]

    for kk in range(TOP_K):
        rank8 = jnp.zeros((GROUP_SIZE, tm), F32)
        for g in range(N_GROUPS):
            rank8 = rank8 + jnp.where(ids[g] == picks[kk], cum[g], 0.0)
        rank = jnp.sum(rank8, axis=0, keepdims=True)
        e_ref[kk:kk + 1, :] = picks[kk]
        w_ref[kk:kk + 1, :] = wts[kk] * norm
        r_ref[kk:kk + 1, :] = rank.astype(jnp.int32)

    for g in range(N_GROUPS):
        rows = slice(g * GROUP_SIZE, (g + 1) * GROUP_SIZE)
        cnt_scr[rows, :] = cnt_scr[rows, :] + jnp.sum(onehot[g], axis=1, keepdims=True)
    cnt_ref[...] = jnp.broadcast_to(cnt_scr[...], cnt_ref.shape)


def _slots_kernel(offs_ref, e_ref, r_ref, d_ref):
    e = e_ref[...]
    d = r_ref[...]
    for ex in range(N_EXPERTS):
        d = d + jnp.where(e == ex, offs_ref[ex], 0)
    d_ref[...] = d


def _slots(offs, top_e, rank):
    K, T = top_e.shape
    tm = _tile(T, 4096)
    grid_spec = pltpu.PrefetchScalarGridSpec(
        num_scalar_prefetch=1,
        grid=(T // tm,),
        in_specs=[pl.BlockSpec((K, tm), lambda i, o: (0, i)),
                  pl.BlockSpec((K, tm), lambda i, o: (0, i))],
        out_specs=pl.BlockSpec((K, tm), lambda i, o: (0, i)),
    )
    return pl.pallas_call(
        _slots_kernel,
        grid_spec=grid_spec,
        out_shape=jax.ShapeDtypeStruct((K, T), jnp.int32),
        compiler_params=_cparams(("parallel",)),
        name="slots",
    )(offs, top_e, rank)


def _row_copy(src, src_row, dst, dst_row, sem):
    s0 = pl.multiple_of(src_row * SLAB, SLAB)
    d0 = pl.multiple_of(dst_row * SLAB, SLAB)
    return pltpu.make_async_copy(src.at[pl.ds(s0, SLAB), :], dst.at[pl.ds(d0, SLAB), :], sem)


def _dispatch_kernel(dest_ref, h_ref, xs_ref, sem, *, td):
    def issue(j, carry):
        for kk in range(TOP_K):
            _row_copy(h_ref, j, xs_ref, dest_ref[0, kk * td + j], sem).start(priority=kk % 2)
        return carry

    lax.fori_loop(0, td, issue, 0)
    for _ in range(TOP_K):
        pltpu.make_async_copy(h_ref, xs_ref.at[pl.ds(0, td * SLAB), :], sem).wait()


def _dispatch(h2, dest_tiles, td):
    T = h2.shape[0] // SLAB
    nt = T // td
    return pl.pallas_call(
        functools.partial(_dispatch_kernel, td=td),
        grid=(nt,),
        in_specs=[pl.BlockSpec((None, 1, TOP_K * td), lambda i: (i, 0, 0),
                               memory_space=pltpu.SMEM),
                  pl.BlockSpec((td * SLAB, LANES), lambda i: (i, 0))],
        out_specs=pl.BlockSpec(memory_space=pl.ANY),
        out_shape=jax.ShapeDtypeStruct((T * TOP_K * SLAB, LANES), h2.dtype),
        scratch_shapes=[pltpu.SemaphoreType.DMA],
        compiler_params=_cparams(("arbitrary",)),
        name="dispatch",
    )(dest_tiles, h2)


def _gmm_kernel(blk_ref, exp_ref, lo_ref, hi_ref, first_ref, newe_ref, x_ref, w1_ref, w3_ref,
                w2_ref, o_ref, w1b, w3b, w2b, *, bm):
    it = pl.program_id(0)

    @pl.when(newe_ref[it] == 1)
    def _():
        w1b[...] = w1_ref[...].astype(BF16)
        w3b[...] = w3_ref[...].astype(BF16)
        w2b[...] = w2_ref[...].astype(BF16)

    rows = lax.broadcasted_iota(jnp.int32, (bm, 1), 0)
    valid = jnp.logical_and(rows >= lo_ref[it], rows < hi_ref[it])
    x_lo, x_hi = _unpack_rows(jnp.where(valid, _load_slabs(x_ref, bm), jnp.uint32(0)))
    x_lo = x_lo.astype(BF16)
    x_hi = x_hi.astype(BF16)
    half = x_lo.shape[1]
    h1 = (jnp.dot(x_lo, w1b[:half, :], preferred_element_type=F32)
          + jnp.dot(x_hi, w1b[half:, :], preferred_element_type=F32))
    h3 = (jnp.dot(x_lo, w3b[:half, :], preferred_element_type=F32)
          + jnp.dot(x_hi, w3b[half:, :], preferred_element_type=F32))
    hb = (h1 * jax.nn.sigmoid(h1) * h3).astype(BF16)
    y = jnp.dot(hb, w2b[...], preferred_element_type=F32)

    @pl.when(first_ref[it] == 1)
    def _():
        _store_slabs(o_ref, _pack_rows(y))

    @pl.when(first_ref[it] == 0)
    def _():
        o_lo, o_hi = _unpack_rows(_load_slabs(o_ref, bm))
        _store_slabs(o_ref, _pack_rows(y + jnp.concatenate([o_lo, o_hi], axis=1)))


def _gmm(xs, items, w1, w3, w2, bm):
    A = xs.shape[0] // SLAB
    D = w1.shape[1]
    F = w1.shape[2]
    n_items = items[0].shape[0]
    grid_spec = pltpu.PrefetchScalarGridSpec(
        num_scalar_prefetch=6,
        grid=(n_items,),
        in_specs=[pl.BlockSpec((bm * SLAB, LANES), lambda i, blk, ex, *_: (blk[i], 0)),
                  pl.BlockSpec((None, D, F), lambda i, blk, ex, *_: (ex[i], 0, 0)),
                  pl.BlockSpec((None, D, F), lambda i, blk, ex, *_: (ex[i], 0, 0)),
                  pl.BlockSpec((None, F, D), lambda i, blk, ex, *_: (ex[i], 0, 0))],
        out_specs=pl.BlockSpec((bm * SLAB, LANES), lambda i, blk, ex, *_: (blk[i], 0)),
        scratch_shapes=[pltpu.VMEM((D, F), BF16), pltpu.VMEM((D, F), BF16),
                        pltpu.VMEM((F, D), BF16)],
    )
    return pl.pallas_call(
        functools.partial(_gmm_kernel, bm=bm),
        grid_spec=grid_spec,
        out_shape=jax.ShapeDtypeStruct((A * SLAB, LANES), jnp.uint32),
        compiler_params=_cparams(("arbitrary",)),
        name="gmm",
    )(*items, xs, w1, w3, w2)


def _work_items(counts, bm, n_blocks):
    n_items = n_blocks + N_EXPERTS - 1
    ends = jnp.cumsum(counts)
    starts = ends - counts
    nb = jnp.where(counts > 0, (ends - 1) // bm - starts // bm + 1, 0)
    item_end = jnp.cumsum(nb)
    item_start = item_end - nb
    n_real = item_end[-1]
    i = jnp.arange(n_items, dtype=jnp.int32)
    e = jnp.minimum(jnp.sum(item_end[None, :] <= i[:, None], axis=1), N_EXPERTS - 1).astype(jnp.int32)
    onehot = e[:, None] == jnp.arange(N_EXPERTS, dtype=jnp.int32)[None, :]
    pick = lambda v: jnp.sum(jnp.where(onehot, v[None, :], 0), axis=1)
    start_e, end_e = pick(starts), pick(ends)
    blk = start_e // bm + (i - pick(item_start))
    lo = jnp.clip(start_e - blk * bm, 0, bm)
    hi = jnp.clip(end_e - blk * bm, 0, bm)
    real = i < n_real
    blk = jnp.where(real, blk, n_blocks - 1).astype(jnp.int32)
    lo = jnp.where(real, lo, 0).astype(jnp.int32)
    hi = jnp.where(real, hi, 0).astype(jnp.int32)
    one = jnp.ones((1,), jnp.int32)
    first = jnp.concatenate([one, (blk[1:] != blk[:-1]).astype(jnp.int32)])
    new_e = jnp.concatenate([one, (e[1:] != e[:-1]).astype(jnp.int32)])
    return blk, e, lo, hi, first, new_e


def _combine_kernel(dest_ref, nxt_ref, ys_ref, w_ref, x1_ref, h_ref, w1_ref, w3_ref, w2_ref,
                    gt2_ref, g_ref, o_ref, buf, y_scr, sems, *, tc):
    i = pl.program_id(0)
    n = pl.num_programs(0)
    slot = i % 2

    def gather(idx_ref, s):
        def issue(j, carry):
            for kk in range(TOP_K):
                _row_copy(ys_ref, idx_ref[0, kk * tc + j], buf.at[s, kk], j,
                          sems.at[s]).start(priority=kk % 2)
            return carry

        lax.fori_loop(0, tc, issue, 0, unroll=4)

    @pl.when(i == 0)
    def _():
        gather(dest_ref, 0)

    for s in range(2):
        @pl.when(jnp.logical_and(i + 1 < n, slot == 1 - s))
        def _():
            gather(nxt_ref, s)

    h_lo, h_hi = _unpack_rows(_load_slabs(h_ref, tc))
    h_lo = h_lo.astype(BF16)
    h_hi = h_hi.astype(BF16)
    half = h_lo.shape[1]
    h1 = (jnp.dot(h_lo, w1_ref[:half, :], preferred_element_type=F32)
          + jnp.dot(h_hi, w1_ref[half:, :], preferred_element_type=F32))
    h3 = (jnp.dot(h_lo, w3_ref[:half, :], preferred_element_type=F32)
          + jnp.dot(h_hi, w3_ref[half:, :], preferred_element_type=F32))
    y_scr[...] = jnp.dot((h1 * jax.nn.sigmoid(h1) * h3).astype(BF16), w2_ref[...],
                         preferred_element_type=F32)

    for kk in range(TOP_K):
        pltpu.make_async_copy(ys_ref.at[pl.ds(0, tc * SLAB), :], buf.at[slot, kk],
                              sems.at[slot]).wait()
    rows = 2 * SUBLANES
    for r0 in range(0, tc, rows):
        for kk in range(TOP_K):
            u = jnp.concatenate([buf[slot, kk, pl.ds(r0 * SLAB + c, rows, stride=SLAB), :]
                                 for c in range(SLAB)], axis=1)
            e_lo, e_hi = _unpack_rows(u)
            wk = w_ref[r0:r0 + rows, kk:kk + 1]
            moe_lo = wk * e_lo if kk == 0 else moe_lo + wk * e_lo
            moe_hi = wk * e_hi if kk == 0 else moe_hi + wk * e_hi
        y_scr[r0:r0 + rows, :] += jnp.concatenate([moe_lo, moe_hi], axis=1)
    o_ref[...] = x1_ref[...] + gt2_ref[...] * _rms(y_scr[...], g_ref[...])


def _combine(ys, dest_tiles, w_tok, x1, h2, w1s, w3s, w2s, mod4, g_post, S, tc):
    T, D = x1.shape
    F = w1s.shape[1]
    per_b = S // tc
    nt = T // tc
    row = lambda: pl.BlockSpec((tc, D), lambda i: (i, 0))
    return pl.pallas_call(
        functools.partial(_combine_kernel, tc=tc),
        grid=(nt,),
        in_specs=[pl.BlockSpec((None, 1, TOP_K * tc), lambda i: (i, 0, 0),
                               memory_space=pltpu.SMEM),
                  pl.BlockSpec((None, 1, TOP_K * tc), lambda i: (jnp.minimum(i + 1, nt - 1), 0, 0),
                               memory_space=pltpu.SMEM),
                  pl.BlockSpec(memory_space=pl.ANY),
                  pl.BlockSpec((tc, TOP_K), lambda i: (i, 0)),
                  row(), pl.BlockSpec((tc * SLAB, LANES), lambda i: (i, 0)),
                  pl.BlockSpec((D, F), lambda i: (0, 0)),
                  pl.BlockSpec((D, F), lambda i: (0, 0)),
                  pl.BlockSpec((F, D), lambda i: (0, 0)),
                  pl.BlockSpec((None, None, 1, D), lambda i: (i // per_b, 5, 0, 0)),
                  pl.BlockSpec((1, D), lambda i: (0, 0))],
        out_specs=row(),
        out_shape=jax.ShapeDtypeStruct((T, D), F32),
        scratch_shapes=[pltpu.VMEM((2, TOP_K, tc * SLAB, LANES), jnp.uint32),
                        pltpu.VMEM((tc, D), F32), pltpu.SemaphoreType.DMA((2,))],
        compiler_params=_cparams(("arbitrary",)),
        name="combine",
    )(dest_tiles, dest_tiles, ys, w_tok, x1, h2, w1s, w3s, w2s, mod4, g_post.reshape(1, D))


def _lambda_init(layer):
    return 0.8 - 0.6 * math.exp(-0.3 * layer)


def kernel(x, c, w_ada, b_ada, g_pre_mix, w_in, conv_w, conv_b, lru_wa, lru_ba, lru_wx, lru_bx,
           lru_lambda, lam_q1, lam_k1, lam_q2, lam_k2, g_subln, w_proj_rnn, w_proj_att, w_out,
           g_post_mix, g_pre_ffn, w_router, e_bias, w1_e, w3_e, w2_e, w1_s, w3_s, w2_s,
           g_post_ffn):
    B, S, D = x.shape
    T = B * S
    depth = w_ada.shape[0]
    slopes = np.exp2(-8.0 * np.arange(1, N_HEADS + 1, dtype=np.float32) / N_HEADS)
    kbias = _alibi_tables(slopes, S)
    tt = _tile(S, 256)
    bm = _tile(T * TOP_K, 512)
    n_blocks = T * TOP_K // bm

    x2 = x.reshape(T, D)
    for l in range(depth):
        lam_init = _lambda_init(l)
        mod4 = _ada(c, w_ada[l], b_ada[l]).reshape(B, 6, 1, D)

        proj = _inproj(x2, g_pre_mix[l], mod4, w_in[l].astype(BF16), S)
        yr = _lru(proj, conv_w[l], conv_b[l], lru_wa[l].astype(BF16), lru_ba[l],
                  lru_wx[l].astype(BF16), lru_bx[l], lru_lambda[l], B, S)
        lamv = jnp.stack([lam_q1[l], lam_k1[l], lam_q2[l], lam_k2[l]])
        ao = _attn(proj, kbias, lamv, g_subln[l], B, S, lam_init)
        x1, h2p, top_e, top_w, rank, cnt = _mixout(
            x2, yr, ao, proj, w_proj_rnn[l].astype(BF16), w_proj_att[l].astype(BF16),
            w_out[l].astype(BF16), g_post_mix[l], g_pre_ffn[l], mod4, w_router[l].T, e_bias[l], S)
        counts = cnt[:, 0].astype(jnp.int32)
        offs = jnp.cumsum(counts) - counts
        dest = _slots(offs, top_e, rank)
        dest_tiles = dest.reshape(TOP_K, T // tt, tt).transpose(1, 0, 2).reshape(T // tt, 1,
                                                                                  TOP_K * tt)
        xs = _dispatch(h2p, dest_tiles, tt)
        items = _work_items(counts, bm, n_blocks)
        ys = _gmm(xs, items, w1_e[l], w3_e[l], w2_e[l], bm)
        x2 = _combine(ys, dest_tiles, top_w.T, x1, h2p, w1_s[l].astype(BF16),
                      w3_s[l].astype(BF16), w2_s[l].astype(BF16), mod4, g_post_ffn[l], S, tt)
    return x2.reshape(B, S, D)
```

```python
import functools
import math

import jax
import jax.numpy as jnp
import numpy as np
from jax import lax
from jax.experimental import pallas as pl
from jax.experimental.pallas import tpu as pltpu

F32 = jnp.float32
BF16 = jnp.bfloat16

EPS = 1e-6
N_HEADS = 8
HEAD_DIM = 64
V_DIM = 2 * HEAD_DIM
LRU_BLOCKS = 8
CONV_W = 4
LRU_C = 8.0
N_EXPERTS = 64
TOP_K = 8
N_GROUPS = 8
GROUP_SIZE = N_EXPERTS // N_GROUPS
TOPK_GROUPS = 4
ROUTED_SCALE = 2.5

LANES = 128
SUBLANES = 8
VMEM_LIMIT = 48 * 1024 * 1024


def _cparams(sem):
    return pltpu.CompilerParams(dimension_semantics=sem, vmem_limit_bytes=VMEM_LIMIT)


def _tile(n, pref):
    t = min(n, pref)
    while n % t:
        t //= 2
    return t


def _rms(x, g):
    return x * lax.rsqrt(jnp.mean(x * x, axis=-1, keepdims=True) + EPS) * g


def _ada_kernel(c_ref, w_ref, b_ref, o_ref):
    c = c_ref[...]
    cond = c * jax.nn.sigmoid(c)
    o_ref[...] = jnp.dot(cond, w_ref[...], preferred_element_type=F32) + b_ref[...]


def _ada(c, w, b):
    B, D = c.shape
    N = w.shape[1]
    tn = _tile(N, 1024)
    return pl.pallas_call(
        _ada_kernel,
        grid=(N // tn,),
        in_specs=[pl.BlockSpec((B, D), lambda j: (0, 0)),
                  pl.BlockSpec((D, tn), lambda j: (0, j)),
                  pl.BlockSpec((1, tn), lambda j: (0, j))],
        out_specs=pl.BlockSpec((B, tn), lambda j: (0, j)),
        out_shape=jax.ShapeDtypeStruct((B, N), F32),
        compiler_params=_cparams(("parallel",)),
        name="ada",
    )(c, w, b.reshape(1, N))


LOG2E = 1.4426950408889634
Q_COL_BLOCK = 2
Q_PRESCALE = HEAD_DIM ** -0.5 * LOG2E


def _inproj_kernel(x_ref, g_ref, sh_ref, sc_ref, w_ref, o_ref, h_scr):
    @pl.when(pl.program_id(1) == 0)
    def _():
        h = _rms(x_ref[...], g_ref[...]) * (1.0 + sc_ref[...]) + sh_ref[...]
        h_scr[...] = h.astype(BF16)

    r = jnp.dot(h_scr[...], w_ref[...], preferred_element_type=F32)
    r = r * jnp.where(pl.program_id(1) == Q_COL_BLOCK, Q_PRESCALE, 1.0)
    o_ref[...] = r.astype(o_ref.dtype)


def _inproj(x2, g, mod4, w_bf, S):
    T, D = x2.shape
    N = w_bf.shape[1]
    tm = _tile(S, 2048)
    tn = 1024
    per_b = S // tm
    return pl.pallas_call(
        _inproj_kernel,
        grid=(T // tm, N // tn),
        in_specs=[pl.BlockSpec((tm, D), lambda i, j: (i, 0)),
                  pl.BlockSpec((1, D), lambda i, j: (0, 0)),
                  pl.BlockSpec((None, None, 1, D), lambda i, j: (i // per_b, 0, 0, 0)),
                  pl.BlockSpec((None, None, 1, D), lambda i, j: (i // per_b, 1, 0, 0)),
                  pl.BlockSpec((D, tn), lambda i, j: (0, j))],
        out_specs=pl.BlockSpec((tm, tn), lambda i, j: (i, j)),
        out_shape=jax.ShapeDtypeStruct((T, N), BF16),
        scratch_shapes=[pltpu.VMEM((tm, D), BF16)],
        compiler_params=_cparams(("parallel", "arbitrary")),
        name="inproj",
    )(x2, g.reshape(1, D), mod4, mod4, w_bf)


def _lru_kernel(xr_ref, gr_ref, cw_ref, cb_ref, wa_ref, ba_ref, wx_ref, bx_ref, lam_ref,
                o_ref, xc_scr, prev_scr, h_scr, *, ts):
    s = pl.program_id(1)

    @pl.when(s == 0)
    def _():
        prev_scr[...] = jnp.zeros_like(prev_scr)
        h_scr[...] = jnp.zeros_like(h_scr)

    x = xr_ref[...].astype(F32)
    prev = prev_scr[...]
    row8 = lax.broadcasted_iota(jnp.int32, (SUBLANES, 1), 0)
    acc = cb_ref[...] + cw_ref[CONV_W - 1:CONV_W, :] * x
    xc_scr[...] = acc
    top = cb_ref[...] + cw_ref[CONV_W - 1:CONV_W, :] * x[0:SUBLANES, :]
    for j in range(1, CONV_W):
        wj = cw_ref[CONV_W - 1 - j:CONV_W - j, :]
        rj = pltpu.roll(x, j, axis=0)
        xc_scr[...] += wj * rj
        pj = pltpu.roll(prev, j, axis=0)
        top += wj * jnp.where(row8 < j, pj, rj[0:SUBLANES, :])
    xc_scr[0:SUBLANES, :] = top
    prev_scr[...] = x[ts - SUBLANES:ts, :]

    row = lax.broadcasted_iota(jnp.int32, (ts, 1), 0)
    is_first = jnp.logical_and(row == 0, s == 0)
    sub = lax.broadcasted_iota(jnp.int32, (1, SUBLANES, 1), 1)
    for n in range(LRU_BLOCKS):
        cols = slice(n * LANES, (n + 1) * LANES)
        xc = xc_scr[:, cols]
        xb = xc.astype(BF16)
        r = jax.nn.sigmoid(jnp.dot(xb, wa_ref[n], preferred_element_type=F32) + ba_ref[:, cols])
        i = jax.nn.sigmoid(jnp.dot(xb, wx_ref[n], preferred_element_type=F32) + bx_ref[:, cols])
        lam = lam_ref[:, cols]
        softplus_neg = jnp.maximum(-lam, 0.0) + jnp.log1p(jnp.exp(-jnp.abs(lam)))
        log_a = (-LRU_C * softplus_neg) * r
        a = jnp.exp(log_a)
        m2 = 1.0 - a * a
        mult = jnp.where(m2 > 0.0, m2 * lax.rsqrt(m2), 0.0)
        mult = jnp.where(is_first, 1.0, mult)
        u = mult * (i * xc)
        a = a.reshape(ts // SUBLANES, SUBLANES, LANES)
        u = u.reshape(ts // SUBLANES, SUBLANES, LANES)
        for d in (1, 2, 4):
            keep = sub >= d
            a_sh = jnp.where(keep, pltpu.roll(a, d, axis=1), 1.0)
            u_sh = jnp.where(keep, pltpu.roll(u, d, axis=1), 0.0)
            u = u + a * u_sh
            a = a * a_sh
        a = a.reshape(ts, LANES)
        u = u.reshape(ts, LANES)
        gate = jax.nn.gelu(gr_ref[:, cols].astype(F32))
        carry = h_scr[:, cols]
        step = 2 * SUBLANES
        for g in range(ts // step):
            r0 = g * step
            h0 = u[r0:r0 + SUBLANES, :] + a[r0:r0 + SUBLANES, :] * carry
            carry = h0[SUBLANES - 1:SUBLANES, :]
            h1 = u[r0 + SUBLANES:r0 + step, :] + a[r0 + SUBLANES:r0 + step, :] * carry
            carry = h1[SUBLANES - 1:SUBLANES, :]
            hg = jnp.concatenate([h0, h1], axis=0) * gate[r0:r0 + step, :]
            o_ref[r0:r0 + step, cols] = hg.astype(o_ref.dtype)
        h_scr[:, cols] = carry


def _lru(proj, conv_w, conv_b, wa_bf, ba, wx_bf, bx, lam, B, S):
    C = conv_w.shape[1]
    ts = _tile(S, 256)
    ns = S // ts
    vec = lambda: pl.BlockSpec((1, C), lambda b, s: (0, 0))
    blk = lambda: pl.BlockSpec((LRU_BLOCKS, LANES, LANES), lambda b, s: (0, 0, 0))
    return pl.pallas_call(
        functools.partial(_lru_kernel, ts=ts),
        grid=(B, ns),
        in_specs=[pl.BlockSpec((ts, C), lambda b, s: (b * ns + s, 0)),
                  pl.BlockSpec((ts, C), lambda b, s: (b * ns + s, 1)),
                  pl.BlockSpec((CONV_W, C), lambda b, s: (0, 0)),
                  vec(), blk(), vec(), blk(), vec(), vec()],
        out_specs=pl.BlockSpec((ts, C), lambda b, s: (b * ns + s, 0)),
        out_shape=jax.ShapeDtypeStruct((B * S, C), BF16),
        scratch_shapes=[pltpu.VMEM((ts, C), F32), pltpu.VMEM((SUBLANES, C), F32),
                        pltpu.VMEM((1, C), F32)],
        compiler_params=_cparams(("parallel", "arbitrary")),
        name="lru",
    )(proj, proj, conv_w, conv_b.reshape(1, C), wa_bf, ba.reshape(1, C), wx_bf, bx.reshape(1, C),
      lam.reshape(1, C))


BIAS_LANES = 3
ACC_ROWS = V_DIM + 2 * SUBLANES


def _alibi_tables(slopes, S):
    def top16(x):
        return (x.view(np.uint32) & np.uint32(0xFFFF0000)).view(np.float32)

    pos = np.arange(S, dtype=np.float32)
    b = (slopes.astype(np.float32) * np.float32(LOG2E))[:, None] * pos[None, :]
    hi = top16(b)
    mid = top16(b - hi)
    lo = top16(b - hi - mid)
    half = np.zeros(b.shape + (HEAD_DIM,), np.float32)
    half[..., 0], half[..., 1], half[..., 2] = hi, mid, lo
    zero = np.zeros_like(half)
    table = np.stack([np.concatenate([zero, half], axis=-1),
                      np.concatenate([half, zero], axis=-1)], axis=1)
    return jnp.asarray(table, dtype=BF16)


def _lane_masks():
    lane = lax.broadcasted_iota(jnp.int32, (1, V_DIM), 1)
    own = (jnp.where(lane < HEAD_DIM, 1.0, 0.0), jnp.where(lane >= HEAD_DIM, 1.0, 0.0))
    ones_row = (jnp.where(jnp.logical_and(lane >= HEAD_DIM, lane < HEAD_DIM + BIAS_LANES), 1.0, 0.0),
                jnp.where(lane < BIAS_LANES, 1.0, 0.0))
    return own, ones_row


def _attn_kernel(lamv_ref, gsub_ref, q_ref, k_ref, v_ref, kb_ref, o_ref,
                 vt_scr, kx_scr, *scratch, tq, lam_init):
    tk = tq
    own, _ = _lane_masks()
    acc_scr = scratch[6]
    acc_scr[...] = jnp.zeros_like(acc_scr)
    for j in range(vt_scr.shape[0]):
        rows = slice(j * tk, (j + 1) * tk)
        vt_scr[j, 0:V_DIM, :] = v_ref[rows, :].astype(F32).T.astype(BF16)
        vt_scr[j, V_DIM:ACC_ROWS, :] = jnp.ones((ACC_ROWS - V_DIM, tk), BF16)
        kf = k_ref[rows, :].astype(F32)
        for mp in range(2):
            kx_scr[mp, rows, :] = (kf * own[mp] + kb_ref[mp, rows, :].astype(F32)).astype(BF16)

    def q_block(qi, carry):
        _attn_q_block(qi, lamv_ref, gsub_ref, q_ref, o_ref, vt_scr, kx_scr, *scratch,
                      tq=tq, lam_init=lam_init)
        return carry

    lax.fori_loop(0, q_ref.shape[0] // tq, q_block, 0)


def _attn_q_block(qi, lamv_ref, gsub_ref, q_ref, o_ref,
                  vt_scr, kx_scr, qx_scr, s_scr, p_scr, a_scr, mx_scr, m_scr, acc_scr, *, tq,
                  lam_init):
    tk = tq
    q_rows = pl.ds(pl.multiple_of(qi * tq, tq), tq)
    own, ones_row = _lane_masks()
    q = q_ref[q_rows, :].astype(F32)
    for mp in range(2):
        qx_scr[mp] = (q * own[mp] + ones_row[mp]).astype(BF16)
    m_scr[...] = jnp.full_like(m_scr, -jnp.inf)
    row8 =lax.broadcasted_iota(jnp.int32, (SUBLANES, LANES), 0)
    col = lax.broadcasted_iota(jnp.int32, (1, LANES), 1)
    neg_inf = jnp.float32(-jnp.inf)

    def tree(op, parts):
        parts = [p for p in parts if p is not None]
        while len(parts) > 1:
            parts = [op(parts[i], parts[i + 1]) if i + 1 < len(parts) else parts[i]
                     for i in range(0, len(parts), 2)]
        return parts[0]

    def scores(j, par):
        start = pl.multiple_of(j * tk, tk)
        for mp in range(2):
            s = lax.dot_general(kx_scr[mp, pl.ds(start, tk), :], qx_scr[mp],
                                (((1,), (1,)), ((), ())), preferred_element_type=F32)
            for c in range(tq // LANES):
                cols = slice(c * LANES, (c + 1) * LANES)
                s_scr[par, mp, c] = s[:, cols]
                accs = [None] * 4
                for i in range(tk // SUBLANES):
                    t = s[i * SUBLANES:(i + 1) * SUBLANES, cols]
                    accs[i % 4] = t if accs[i % 4] is None else jnp.maximum(accs[i % 4], t)
                mx_scr[par, mp, :, cols] = jnp.max(tree(jnp.maximum, accs), axis=0, keepdims=True)

    def softmax(par, masked):
        for mp in range(2):
            for c in range(tq // LANES):
                cols = slice(c * LANES, (c + 1) * LANES)
                lo_col, hi_col = c * LANES, (c + 1) * LANES - 1
                s_c = s_scr.at[par, mp, c]
                p_c = p_scr.at[par, mp, c]
                if masked:
                    accs = [None] * 4
                    for i in range(tk // SUBLANES):
                        r0 = i * SUBLANES
                        if r0 > hi_col:
                            continue
                        t = s_c[r0:r0 + SUBLANES, :]
                        if r0 + SUBLANES - 1 > lo_col:
                            t = jnp.where(row8 + r0 <= col + lo_col, t, neg_inf)
                            s_c[r0:r0 + SUBLANES, :] = t
                        accs[i % 4] = t if accs[i % 4] is None else jnp.maximum(accs[i % 4], t)
                    mx = jnp.max(tree(jnp.maximum, accs), axis=0, keepdims=True)
                else:
                    mx = mx_scr[par, mp, :, cols]
                m_old = m_scr[mp, :, cols]
                m_new = jnp.maximum(m_old, mx)
                a_scr[par, mp, :, cols] = jnp.exp2(m_old - m_new)
                m_scr[mp, :, cols] = m_new
                for i in range(tk // (2 * SUBLANES)):
                    r0 = i * 2 * SUBLANES
                    if masked and r0 > hi_col:
                        p_c[r0:r0 + 2 * SUBLANES, :] = jnp.zeros((2 * SUBLANES, LANES), BF16)
                        continue
                    p = jnp.exp2(s_c[r0:r0 + 2 * SUBLANES, :] - m_new)
                    p_c[r0:r0 + 2 * SUBLANES, :] = p.astype(BF16)

    def values(j, par):
        for mp in range(2):
            p = jnp.concatenate([p_scr[par, mp, c] for c in range(tq // LANES)], axis=1)
            pv = jnp.dot(vt_scr[j], p, preferred_element_type=F32)
            acc_scr[mp] = a_scr[par, mp] * acc_scr[mp] + pv

    scores(0, 0)

    def stage(j, par, first=False):
        scores(j + 1, 1 - par)
        softmax(par, False)
        if not first:
            values(j - 1, 1 - par)

    def stage_pairs(count):
        def body(i, carry):
            stage(2 * i + 1, 1)
            stage(2 * i + 2, 0)
            return carry

        lax.fori_loop(0, count, body, 0)

    @pl.when(qi == 0)
    def _():
        softmax(0, True)
        values(0, 0)

    @pl.when(qi % 2 == 1)
    def _():
        stage(0, 0, first=True)
        stage_pairs((qi - 1) // 2)
        softmax(1, True)
        values(qi - 1, 0)
        values(qi, 1)

    @pl.when(jnp.logical_and(qi % 2 == 0, qi >= 2))
    def _():
        stage(0, 0, first=True)
        stage_pairs((qi - 2) // 2)
        stage(qi - 1, 1)
        softmax(0, True)
        values(qi - 1, 1)
        values(qi, 0)

    lv = lamv_ref[...]
    lam = (jnp.exp(jnp.sum(lv[0:1, :] * lv[1:2, :], axis=-1, keepdims=True))
           - jnp.exp(jnp.sum(lv[2:3, :] * lv[3:4, :], axis=-1, keepdims=True)) + lam_init)
    o0 = acc_scr[0, 0:V_DIM, :] * (1.0 / acc_scr[0, V_DIM:V_DIM + 1, :])
    o1 = acc_scr[1, 0:V_DIM, :] * (1.0 / acc_scr[1, V_DIM:V_DIM + 1, :])
    o_t = o0 - lam * o1
    o_t = o_t * lax.rsqrt(jnp.mean(o_t * o_t, axis=0, keepdims=True) + EPS) * gsub_ref[...]
    o_ref[q_rows, :] = (o_t * (1.0 - lam_init)).T.astype(o_ref.dtype)


def _attn(proj, kbias, lamv, g_subln, B, S, lam_init):
    tq = _tile(S, 512)
    nq = S // tq
    qc, kc, vc = 2 * 8, 3 * 8, 4 * 8
    return pl.pallas_call(
        functools.partial(_attn_kernel, tq=tq, lam_init=lam_init),
        grid=(B, N_HEADS),
        in_specs=[pl.BlockSpec((4, HEAD_DIM), lambda b, h: (0, 0)),
                  pl.BlockSpec((V_DIM, 1), lambda b, h: (0, 0)),
                  pl.BlockSpec((S, V_DIM), lambda b, h: (b, qc + h)),
                  pl.BlockSpec((S, V_DIM), lambda b, h: (b, kc + h)),
                  pl.BlockSpec((S, V_DIM), lambda b, h: (b, vc + h)),
                  pl.BlockSpec((None, 2, S, V_DIM), lambda b, h: (h, 0, 0, 0))],
        out_specs=pl.BlockSpec((S, V_DIM), lambda b, h: (b, h)),
        out_shape=jax.ShapeDtypeStruct((B * S, N_HEADS * V_DIM), BF16),
        scratch_shapes=[pltpu.VMEM((nq, ACC_ROWS, tq), BF16), pltpu.VMEM((2, S, V_DIM), BF16),
                        pltpu.VMEM((2, tq, V_DIM), BF16),
                        pltpu.VMEM((2, 2, tq // LANES, tq, LANES), F32),
                        pltpu.VMEM((2, 2, tq // LANES, tq, LANES), BF16),
                        pltpu.VMEM((2, 2, 1, tq), F32), pltpu.VMEM((2, 2, 1, tq), F32),
                        pltpu.VMEM((2, 1, tq), F32),
                        pltpu.VMEM((2, ACC_ROWS, tq), F32)],
        compiler_params=_cparams(("parallel", "parallel")),
        name="attn",
    )(lamv, g_subln.reshape(V_DIM, 1), proj, proj, proj, kbias)


def _pack_rows(x):
    half = x.shape[1] // 2
    lo = lax.bitcast_convert_type(x[:, :half].astype(BF16).astype(F32), jnp.uint32)
    hi = lax.bitcast_convert_type(x[:, half:].astype(BF16).astype(F32), jnp.uint32)
    return (hi & jnp.uint32(0xFFFF0000)) | (lo >> 16)


def _unpack_rows(u):
    lo = lax.bitcast_convert_type(u << 16, F32)
    hi = lax.bitcast_convert_type(u & jnp.uint32(0xFFFF0000), F32)
    return lo, hi


SLAB = 4


def _load_slabs(ref, n):
    return jnp.concatenate([ref[pl.ds(c, n, stride=SLAB), :] for c in range(SLAB)], axis=1)


def _store_slabs(ref, u):
    n = u.shape[0]
    for c in range(SLAB):
        ref[pl.ds(c, n, stride=SLAB), :] = u[:, c * LANES:(c + 1) * LANES]


def _mixout_kernel(x_ref, yr_ref, ao_ref, ga_ref, gb_ref, wr_ref, wa_ref, wo_ref,
                   gpost_ref, gt1_ref, gpre_ref, sh2_ref, sc2_ref, wrt_ref, eb_ref,
                   x1_ref, h2p_ref, e_ref, w_ref, r_ref, cnt_ref, cnt_scr):
    ya = jnp.dot(yr_ref[...], wr_ref[...], preferred_element_type=F32)
    yb = jnp.dot(ao_ref[...], wa_ref[...], preferred_element_type=F32)
    merged = (jax.nn.sigmoid(ga_ref[...].astype(F32)) * ya
              + jax.nn.sigmoid(gb_ref[...].astype(F32)) * yb)
    y = jnp.dot(merged.astype(BF16), wo_ref[...], preferred_element_type=F32)
    x1 = x_ref[...] + gt1_ref[...] * _rms(y, gpost_ref[...])
    x1_ref[...] = x1
    h2 = _rms(x1, gpre_ref[...]) * (1.0 + sc2_ref[...]) + sh2_ref[...]
    _store_slabs(h2p_ref, _pack_rows(h2))
    _route(h2, wrt_ref, eb_ref, e_ref, w_ref, r_ref, cnt_ref, cnt_scr)


def _mixout(x2, yr, ao, proj, wr_bf, wa_bf, wo_bf, g_post, g_pre, mod4, w_router_t, e_bias, S):
    T, D = x2.shape
    tm = _tile(S, 512)
    per_b = S // tm
    gac, gbc = 5, 6
    row = lambda: pl.BlockSpec((tm, D), lambda i: (i, 0))
    wsp = lambda: pl.BlockSpec((D, D), lambda i: (0, 0))
    vec = lambda: pl.BlockSpec((1, D), lambda i: (0, 0))
    modv = lambda j: pl.BlockSpec((None, None, 1, D), lambda i: (i // per_b, j, 0, 0))
    topk = lambda: pl.BlockSpec((TOP_K, tm), lambda i: (0, i))
    return pl.pallas_call(
        _mixout_kernel,
        grid=(T // tm,),
        in_specs=[row(), row(), row(),
                  pl.BlockSpec((tm, D), lambda i: (i, gac)),
                  pl.BlockSpec((tm, D), lambda i: (i, gbc)),
                  wsp(), wsp(), wsp(), vec(), modv(2), vec(), modv(3), modv(4),
                  pl.BlockSpec((N_EXPERTS, D), lambda i: (0, 0)),
                  pl.BlockSpec((N_EXPERTS, 1), lambda i: (0, 0))],
        out_specs=[row(), pl.BlockSpec((tm * SLAB, LANES), lambda i: (i, 0)),
                   topk(), topk(), topk(),
                   pl.BlockSpec((N_EXPERTS, LANES), lambda i: (0, 0))],
        out_shape=[jax.ShapeDtypeStruct((T, D), F32),
                   jax.ShapeDtypeStruct((T * SLAB, LANES), jnp.uint32),
                   jax.ShapeDtypeStruct((TOP_K, T), jnp.int32),
                   jax.ShapeDtypeStruct((TOP_K, T), F32),
                   jax.ShapeDtypeStruct((TOP_K, T), jnp.int32),
                   jax.ShapeDtypeStruct((N_EXPERTS, LANES), F32)],
        scratch_shapes=[pltpu.VMEM((N_EXPERTS, 1), F32)],
        compiler_params=_cparams(("arbitrary",)),
        name="mixout",
    )(x2, yr, ao, proj, proj, wr_bf, wa_bf, wo_bf, g_post.reshape(1, D), mod4,
      g_pre.reshape(1, D), mod4, mod4, w_router_t, e_bias.reshape(N_EXPERTS, 1))


def _first_argmax(vals, ids, sentinel):
    m = jnp.max(vals, axis=0, keepdims=True)
    idx = jnp.min(jnp.where(vals == m, ids, sentinel), axis=0, keepdims=True)
    return m, idx


def _route(h2, wr_ref, eb_ref, e_ref, w_ref, r_ref, cnt_ref, cnt_scr):
    tm = h2.shape[0]
    step = pl.program_id(0)

    @pl.when(step == 0)
    def _():
        cnt_scr[...] = jnp.zeros_like(cnt_scr)

    def split(x):
        hi = x.astype(BF16)
        return hi, (x - hi.astype(F32)).astype(BF16)

    nt = (((1,), (1,)), ((), ()))
    w_hi, w_lo = split(wr_ref[...])
    h_hi, h_lo = split(h2)
    logits = (lax.dot_general(w_hi, h_hi, nt, preferred_element_type=F32)
              + lax.dot_general(w_hi, h_lo, nt, preferred_element_type=F32)
              + lax.dot_general(w_lo, h_hi, nt, preferred_element_type=F32))
    scores = jax.nn.sigmoid(logits)
    choice = scores + eb_ref[...]
    i8 = lax.broadcasted_iota(jnp.int32, (GROUP_SIZE, tm), 0)
    neg_inf = jnp.float32(-jnp.inf)

    slabs = [choice[g * GROUP_SIZE:(g + 1) * GROUP_SIZE, :] for g in range(N_GROUPS)]
    sc_slabs = [scores[g * GROUP_SIZE:(g + 1) * GROUP_SIZE, :] for g in range(N_GROUPS)]

    gs = jnp.zeros((N_GROUPS, tm), F32)
    for g in range(N_GROUPS):
        m1, idx1 = _first_argmax(slabs[g], i8, GROUP_SIZE)
        m2 = jnp.max(jnp.where(i8 == idx1, neg_inf, slabs[g]), axis=0, keepdims=True)
        gs = jnp.where(i8 == g, m1 + m2, gs)

    sel = jnp.zeros((N_GROUPS, tm), jnp.int32)
    cur = gs
    for _ in range(TOPK_GROUPS):
        _, idx = _first_argmax(cur, i8, N_GROUPS)
        hit = i8 == idx
        sel = jnp.where(hit, 1, sel)
        cur = jnp.where(hit, neg_inf, cur)

    masked = [jnp.where(sel[g:g + 1, :] > 0, slabs[g], neg_inf) for g in range(N_GROUPS)]
    ids = [i8 + g * GROUP_SIZE for g in range(N_GROUPS)]
    onehot = [jnp.zeros((GROUP_SIZE, tm), F32) for _ in range(N_GROUPS)]
    picks = []
    wts = []
    for _ in range(TOP_K):
        m = jnp.max(functools.reduce(jnp.maximum, masked), axis=0, keepdims=True)
        idx = jnp.min(functools.reduce(
            jnp.minimum, [jnp.where(c == m, i, N_EXPERTS) for c, i in zip(masked, ids)]),
            axis=0, keepdims=True)
        w8 = jnp.zeros((GROUP_SIZE, tm), F32)
        for g in range(N_GROUPS):
            hit = ids[g] == idx
            w8 = w8 + jnp.where(hit, sc_slabs[g], 0.0)
            masked[g] = jnp.where(hit, neg_inf, masked[g])
            onehot[g] = jnp.where(hit, 1.0, onehot[g])
        picks.append(idx)
        wts.append(jnp.sum(w8, axis=0, keepdims=True))

    wsum = functools.reduce(lambda a, b: a + b, wts)
    norm = ROUTED_SCALE / (wsum + 1e-20)

    t_row = lax.broadcasted_iota(jnp.int32, (tm, tm), 0)
    t_col = lax.broadcasted_iota(jnp.int32, (tm, tm), 1)
    before = jnp.where(t_row < t_col, 1.0, 0.0).astype(BF16)
    cum = [jnp.dot(onehot[g].astype(BF16), before, preferred_element_type=F32)
           + cnt_scr[g * GROUP_SIZE:(g + 1) * GROUP_SIZE, :] for g in range(N_GROUPS)]

    for kk in range(TOP_K):
        rank8 = jnp.zeros((GROUP_SIZE, tm), F32)
        for g in range(N_GROUPS):
            rank8 = rank8 + jnp.where(ids[g] == picks[kk], cum[g], 0.0)
        rank = jnp.sum(rank8, axis=0, keepdims=True)
        e_ref[kk:kk + 1, :] = picks[kk]
        w_ref[kk:kk + 1, :] = wts[kk] * norm
        r_ref[kk:kk + 1, :] = rank.astype(jnp.int32)

    for g in range(N_GROUPS):
        rows = slice(g * GROUP_SIZE, (g + 1) * GROUP_SIZE)
        cnt_scr[rows, :] = cnt_scr[rows, :] + jnp.sum(onehot[g], axis=1, keepdims=True)
    cnt_ref[...] = jnp.broadcast_to(cnt_scr[...], cnt_ref.shape)


def _slots_kernel(offs_ref, e_ref, r_ref, d_ref):
    e = e_ref[...]
    d = r_ref[...]
    for ex in range(N_EXPERTS):
        d = d + jnp.where(e == ex, offs_ref[ex], 0)
    d_ref[...] = d


def _slots(offs, top_e, rank):
    K, T = top_e.shape
    tm = _tile(T, 4096)
    grid_spec = pltpu.PrefetchScalarGridSpec(
        num_scalar_prefetch=1,
        grid=(T // tm,),
        in_specs=[pl.BlockSpec((K, tm), lambda i, o: (0, i)),
                  pl.BlockSpec((K, tm), lambda i, o: (0, i))],
        out_specs=pl.BlockSpec((K, tm), lambda i, o: (0, i)),
    )
    return pl.pallas_call(
        _slots_kernel,
        grid_spec=grid_spec,
        out_shape=jax.ShapeDtypeStruct((K, T), jnp.int32),
        compiler_params=_cparams(("parallel",)),
        name="slots",
    )(offs, top_e, rank)


def _row_copy(src, src_row, dst, dst_row, sem):
    s0 = pl.multiple_of(src_row * SLAB, SLAB)
    d0 = pl.multiple_of(dst_row * SLAB, SLAB)
    return pltpu.make_async_copy(src.at[pl.ds(s0, SLAB), :], dst.at[pl.ds(d0, SLAB), :], sem)


def _dispatch_kernel(dest_ref, h_ref, xs_ref, sem, *, td):
    def issue(j, carry):
        for kk in range(TOP_K):
            _row_copy(h_ref, j, xs_ref, dest_ref[0, kk * td + j], sem).start(priority=kk % 2)
        return carry

    lax.fori_loop(0, td, issue, 0)
    for _ in range(TOP_K):
        pltpu.make_async_copy(h_ref, xs_ref.at[pl.ds(0, td * SLAB), :], sem).wait()


def _dispatch(h2, dest_tiles, td):
    T = h2.shape[0] // SLAB
    nt = T // td
    return pl.pallas_call(
        functools.partial(_dispatch_kernel, td=td),
        grid=(nt,),
        in_specs=[pl.BlockSpec((None, 1, TOP_K * td), lambda i: (i, 0, 0),
                               memory_space=pltpu.SMEM),
                  pl.BlockSpec((td * SLAB, LANES), lambda i: (i, 0))],
        out_specs=pl.BlockSpec(memory_space=pl.ANY),
        out_shape=jax.ShapeDtypeStruct((T * TOP_K * SLAB, LANES), h2.dtype),
        scratch_shapes=[pltpu.SemaphoreType.DMA],
        compiler_params=_cparams(("arbitrary",)),
        name="dispatch",
    )(dest_tiles, h2)


def _gmm_kernel(blk_ref, exp_ref, lo_ref, hi_ref, first_ref, newe_ref, x_ref, w1_ref, w3_ref,
                w2_ref, o_ref, w1b, w3b, w2b, *, bm):
    it = pl.program_id(0)

    @pl.when(newe_ref[it] == 1)
    def _():
        w1b[...] = w1_ref[...].astype(BF16)
        w3b[...] = w3_ref[...].astype(BF16)
        w2b[...] = w2_ref[...].astype(BF16)

    rows = lax.broadcasted_iota(jnp.int32, (bm, 1), 0)
    valid = jnp.logical_and(rows >= lo_ref[it], rows < hi_ref[it])
    x_lo, x_hi = _unpack_rows(jnp.where(valid, _load_slabs(x_ref, bm), jnp.uint32(0)))
    x_lo = x_lo.astype(BF16)
    x_hi = x_hi.astype(BF16)
    half = x_lo.shape[1]
    h1 = (jnp.dot(x_lo, w1b[:half, :], preferred_element_type=F32)
          + jnp.dot(x_hi, w1b[half:, :], preferred_element_type=F32))
    h3 = (jnp.dot(x_lo, w3b[:half, :], preferred_element_type=F32)
          + jnp.dot(x_hi, w3b[half:, :], preferred_element_type=F32))
    hb = (h1 * jax.nn.sigmoid(h1) * h3).astype(BF16)
    y = jnp.dot(hb, w2b[...], preferred_element_type=F32)

    @pl.when(first_ref[it] == 1)
    def _():
        _store_slabs(o_ref, _pack_rows(y))

    @pl.when(first_ref[it] == 0)
    def _():
        o_lo, o_hi = _unpack_rows(_load_slabs(o_ref, bm))
        _store_slabs(o_ref, _pack_rows(y + jnp.concatenate([o_lo, o_hi], axis=1)))


def _gmm(xs, items, w1, w3, w2, bm):
    A = xs.shape[0] // SLAB
    D = w1.shape[1]
    F = w1.shape[2]
    n_items = items[0].shape[0]
    grid_spec = pltpu.PrefetchScalarGridSpec(
        num_scalar_prefetch=6,
        grid=(n_items,),
        in_specs=[pl.BlockSpec((bm * SLAB, LANES), lambda i, blk, ex, *_: (blk[i], 0)),
                  pl.BlockSpec((None, D, F), lambda i, blk, ex, *_: (ex[i], 0, 0)),
                  pl.BlockSpec((None, D, F), lambda i, blk, ex, *_: (ex[i], 0, 0)),
                  pl.BlockSpec((None, F, D), lambda i, blk, ex, *_: (ex[i], 0, 0))],
        out_specs=pl.BlockSpec((bm * SLAB, LANES), lambda i, blk, ex, *_: (blk[i], 0)),
        scratch_shapes=[pltpu.VMEM((D, F), BF16), pltpu.VMEM((D, F), BF16),
                        pltpu.VMEM((F, D), BF16)],
    )
    return pl.pallas_call(
        functools.partial(_gmm_kernel, bm=bm),
        grid_spec=grid_spec,
        out_shape=jax.ShapeDtypeStruct((A * SLAB, LANES), jnp.uint32),
        compiler_params=_cparams(("arbitrary",)),
        name="gmm",
    )(*items, xs, w1, w3, w2)


def _work_items(counts, bm, n_blocks):
    n_items = n_blocks + N_EXPERTS - 1
    ends = jnp.cumsum(counts)
    starts = ends - counts
    nb = jnp.where(counts > 0, (ends - 1) // bm - starts // bm + 1, 0)
    item_end = jnp.cumsum(nb)
    item_start = item_end - nb
    n_real = item_end[-1]
    i = jnp.arange(n_items, dtype=jnp.int32)
    e = jnp.minimum(jnp.sum(item_end[None, :] <= i[:, None], axis=1), N_EXPERTS - 1).astype(jnp.int32)
    onehot = e[:, None] == jnp.arange(N_EXPERTS, dtype=jnp.int32)[None, :]
    pick = lambda v: jnp.sum(jnp.where(onehot, v[None, :], 0), axis=1)
    start_e, end_e = pick(starts), pick(ends)
    blk = start_e // bm + (i - pick(item_start))
    lo = jnp.clip(start_e - blk * bm, 0, bm)
    hi = jnp.clip(end_e - blk * bm, 0, bm)
    real = i < n_real
    blk = jnp.where(real, blk, n_blocks - 1).astype(jnp.int32)
    lo = jnp.where(real, lo, 0).astype(jnp.int32)
    hi = jnp.where(real, hi, 0).astype(jnp.int32)
    one = jnp.ones((1,), jnp.int32)
    first = jnp.concatenate([one, (blk[1:] != blk[:-1]).astype(jnp.int32)])
    new_e = jnp.concatenate([one, (e[1:] != e[:-1]).astype(jnp.int32)])
    return blk, e, lo, hi, first, new_e


def _combine_kernel(dest_ref, nxt_ref, ys_ref, w_ref, x1_ref, h_ref, w1_ref, w3_ref, w2_ref,
                    gt2_ref, g_ref, o_ref, buf, y_scr, sems, *, tc):
    i = pl.program_id(0)
    n = pl.num_programs(0)
    slot = i % 2

    def gather(idx_ref, s):
        def issue(j, carry):
            for kk in range(TOP_K):
                _row_copy(ys_ref, idx_ref[0, kk * tc + j], buf.at[s, kk], j,
                          sems.at[s]).start(priority=kk % 2)
            return carry

        lax.fori_loop(0, tc, issue, 0, unroll=4)

    @pl.when(i == 0)
    def _():
        gather(dest_ref, 0)

    for s in range(2):
        @pl.when(jnp.logical_and(i + 1 < n, slot == 1 - s))
        def _():
            gather(nxt_ref, s)

    h_lo, h_hi = _unpack_rows(_load_slabs(h_ref, tc))
    h_lo = h_lo.astype(BF16)
    h_hi = h_hi.astype(BF16)
    half = h_lo.shape[1]
    h1 = (jnp.dot(h_lo, w1_ref[:half, :], preferred_element_type=F32)
          + jnp.dot(h_hi, w1_ref[half:, :], preferred_element_type=F32))
    h3 = (jnp.dot(h_lo, w3_ref[:half, :], preferred_element_type=F32)
          + jnp.dot(h_hi, w3_ref[half:, :], preferred_element_type=F32))
    y_scr[...] = jnp.dot((h1 * jax.nn.sigmoid(h1) * h3).astype(BF16), w2_ref[...],
                         preferred_element_type=F32)

    for kk in range(TOP_K):
        pltpu.make_async_copy(ys_ref.at[pl.ds(0, tc * SLAB), :], buf.at[slot, kk],
                              sems.at[slot]).wait()
    rows = 2 * SUBLANES
    for r0 in range(0, tc, rows):
        for kk in range(TOP_K):
            u = jnp.concatenate([buf[slot, kk, pl.ds(r0 * SLAB + c, rows, stride=SLAB), :]
                                 for c in range(SLAB)], axis=1)
            e_lo, e_hi = _unpack_rows(u)
            wk = w_ref[r0:r0 + rows, kk:kk + 1]
            moe_lo = wk * e_lo if kk == 0 else moe_lo + wk * e_lo
            moe_hi = wk * e_hi if kk == 0 else moe_hi + wk * e_hi
        y_scr[r0:r0 + rows, :] += jnp.concatenate([moe_lo, moe_hi], axis=1)
    o_ref[...] = x1_ref[...] + gt2_ref[...] * _rms(y_scr[...], g_ref[...])


def _combine(ys, dest_tiles, w_tok, x1, h2, w1s, w3s, w2s, mod4, g_post, S, tc):
    T, D = x1.shape
    F = w1s.shape[1]
    per_b = S // tc
    nt = T // tc
    row = lambda: pl.BlockSpec((tc, D), lambda i: (i, 0))
    return pl.pallas_call(
        functools.partial(_combine_kernel, tc=tc),
        grid=(nt,),
        in_specs=[pl.BlockSpec((None, 1, TOP_K * tc), lambda i: (i, 0, 0),
                               memory_space=pltpu.SMEM),
                  pl.BlockSpec((None, 1, TOP_K * tc), lambda i: (jnp.minimum(i + 1, nt - 1), 0, 0),
                               memory_space=pltpu.SMEM),
                  pl.BlockSpec(memory_space=pl.ANY),
                  pl.BlockSpec((tc, TOP_K), lambda i: (i, 0)),
                  row(), pl.BlockSpec((tc * SLAB, LANES), lambda i: (i, 0)),
                  pl.BlockSpec((D, F), lambda i: (0, 0)),
                  pl.BlockSpec((D, F), lambda i: (0, 0)),
                  pl.BlockSpec((F, D), lambda i: (0, 0)),
                  pl.BlockSpec((None, None, 1, D), lambda i: (i // per_b, 5, 0, 0)),
                  pl.BlockSpec((1, D), lambda i: (0, 0))],
        out_specs=row(),
        out_shape=jax.ShapeDtypeStruct((T, D), F32),
        scratch_shapes=[pltpu.VMEM((2, TOP_K, tc * SLAB, LANES), jnp.uint32),
                        pltpu.VMEM((tc, D), F32), pltpu.SemaphoreType.DMA((2,))],
        compiler_params=_cparams(("arbitrary",)),
        name="combine",
    )(dest_tiles, dest_tiles, ys, w_tok, x1, h2, w1s, w3s, w2s, mod4, g_post.reshape(1, D))


def _lambda_init(layer):
    return 0.8 - 0.6 * math.exp(-0.3 * layer)


def kernel(x, c, w_ada, b_ada, g_pre_mix, w_in, conv_w, conv_b, lru_wa, lru_ba, lru_wx, lru_bx,
           lru_lambda, lam_q1, lam_k1, lam_q2, lam_k2, g_subln, w_proj_rnn, w_proj_att, w_out,
           g_post_mix, g_pre_ffn, w_router, e_bias, w1_e, w3_e, w2_e, w1_s, w3_s, w2_s,
           g_post_ffn):
    B, S, D = x.shape
    T = B * S
    depth = w_ada.shape[0]
    slopes = np.exp2(-8.0 * np.arange(1, N_HEADS + 1, dtype=np.float32) / N_HEADS)
    kbias = _alibi_tables(slopes, S)
    tt = _tile(S, 256)
    bm = _tile(T * TOP_K, 512)
    n_blocks = T * TOP_K // bm

    x2 = x.reshape(T, D)
    for l in range(depth):
        lam_init = _lambda_init(l)
        mod4 = _ada(c, w_ada[l], b_ada[l]).reshape(B, 6, 1, D)

        proj = _inproj(x2, g_pre_mix[l], mod4, w_in[l].astype(BF16), S)
        yr = _lru(proj, conv_w[l], conv_b[l], lru_wa[l].astype(BF16), lru_ba[l],
                  lru_wx[l].astype(BF16), lru_bx[l], lru_lambda[l], B, S)
        lamv = jnp.stack([lam_q1[l], lam_k1[l], lam_q2[l], lam_k2[l]])
        ao = _attn(proj, kbias, lamv, g_subln[l], B, S, lam_init)
        x1, h2p, top_e, top_w, rank, cnt = _mixout(
            x2, yr, ao, proj, w_proj_rnn[l].astype(BF16), w_proj_att[l].astype(BF16),
            w_out[l].astype(BF16), g_post_mix[l], g_pre_ffn[l], mod4, w_router[l].T, e_bias[l], S)
        counts = cnt[:, 0].astype(jnp.int32)
        offs = jnp.cumsum(counts) - counts
        dest = _slots(offs, top_e, rank)
        dest_tiles = dest.reshape(TOP_K, T // tt, tt).transpose(1, 0, 2).reshape(T // tt, 1,
                                                                                  TOP_K * tt)
        xs = _dispatch(h2p, dest_tiles, tt)
        items = _work_items(counts, bm, n_blocks)
        ys = _gmm(xs, items, w1_e[l], w3_e[l], w2_e[l], bm)
        x2 = _combine(ys, dest_tiles, top_w.T, x1, h2p, w1_s[l].astype(BF16),
                      w3_s[l].astype(BF16), w2_s[l].astype(BF16), mod4, g_post_ffn[l], S, tt)
    return x2.reshape(B, S, D)
```
